```python
import math
import jax
import jax.numpy as jnp
from jax import lax
import numpy as np

D_MODEL = 1024
BATCH = 2
SEQ = 16384
DEPTH = 1
DEC_BATCH = 32
DEC_SEQ = 32
PAST_LEN = 2048

CHUNK = 64
WINDOW = 128
HEAD_DIM = 64
MIX_W = D_MODEL // 2
N_HEADS = MIX_W // HEAD_DIM
N_KV_HEADS = N_HEADS // 4
KV_REP = N_HEADS // N_KV_HEADS
ATTN_W = N_HEADS * HEAD_DIM
KV_W = N_KV_HEADS * HEAD_DIM
SSM_GROUP = 16
SSM_W = MIX_W
SSM_GROUPS = SSM_W // SSM_GROUP
SSM_STATE = 64
MEM_LEN = 256
MEM_HEADS = 4
MEM_HEAD_DIM = MIX_W // MEM_HEADS
MEM_W = MEM_HEADS * MEM_HEAD_DIM
N_BRANCH = 3
IN_COLS = ATTN_W + 2 * KV_W + SSM_W + MEM_W + N_BRANCH * D_MODEL
D_FF = 128 * ((8 * D_MODEL // 3 + 127) // 128)
N_BUCKETS = 32
MAX_DISTANCE = 128
RMS_EPS = 1e-6
NEG_INF = -1e30
DT_MIN = 1e-3
DT_MAX = 1e-1

kernel_name = 'hybrid_streaming_encoder_step'


def rmsnorm(x, g):
    x32 = x.astype(jnp.float32)
    y = x32 * lax.rsqrt(jnp.mean(x32 * x32, axis=-1, keepdims=True) + RMS_EPS)
    return (y * g.astype(jnp.float32)).astype(x.dtype)


def swiglu(h, w_in, w_out):
    g, u = jnp.split(h @ w_in, 2, axis=-1)
    return (jax.nn.silu(g) * u) @ w_out


def t5_bucket(rel):
    half = N_BUCKETS // 2
    max_exact = half // 2
    ret = (rel > 0).astype(np.int32) * half
    n = np.abs(rel)
    large = max_exact + (np.log(np.maximum(n, 1) / max_exact) / math.log(MAX_DISTANCE / max_exact) * (half - max_exact)).astype(np.int32)
    large = np.minimum(large, half - 1)
    return ret + np.where(n < max_exact, n, large)


def band_bias(rel_table, n_q, n_back, n_k):
    i = np.arange(n_q)[:, None]
    j = np.arange(n_k)[None, :]
    bucket = t5_bucket((j - n_back) - i)
    b = rel_table[bucket].astype(jnp.float32)
    return jnp.transpose(b, (2, 0, 1)).reshape(N_KV_HEADS, KV_REP, n_q, n_k)


def sink_attention(q, k, v, bias, sink, key_valid=None):
    logits = jnp.einsum('...qgrd,...kgd->...grqk', q, k, preferred_element_type=jnp.float32) * (HEAD_DIM ** -0.5) + bias
    if key_valid is not None:
        logits = jnp.where(key_valid, logits, NEG_INF)
    sink_l = sink.astype(jnp.float32).reshape(N_KV_HEADS, KV_REP)[:, :, None, None]
    m = jnp.maximum(jnp.max(logits, axis=-1, keepdims=True), sink_l)
    e = jnp.exp(logits - m)
    p = e / (jnp.sum(e, axis=-1, keepdims=True) + jnp.exp(sink_l - m))
    return jnp.einsum('...grqk,...kgd->...qgrd', p.astype(v.dtype), v)


def swa_prompt(q, k, v, rel_table, sink):
    b, L = q.shape[0], q.shape[1]
    nc = L // CHUNK
    nb = WINDOW // CHUNK
    band = (nb + 1) * CHUNK
    qc = q.reshape(b, nc, CHUNK, N_KV_HEADS, KV_REP, HEAD_DIM)
    pad = ((0, 0), (WINDOW, 0), (0, 0), (0, 0))
    kc = jnp.pad(k, pad).reshape(b, nc + nb, CHUNK, N_KV_HEADS, HEAD_DIM)
    vc = jnp.pad(v, pad).reshape(b, nc + nb, CHUNK, N_KV_HEADS, HEAD_DIM)
    kb = jnp.concatenate([kc[:, s:s + nc] for s in range(nb + 1)], axis=2)
    vb = jnp.concatenate([vc[:, s:s + nc] for s in range(nb + 1)], axis=2)
    key_pos = np.arange(nc)[:, None] * CHUNK - WINDOW + np.arange(band)[None, :]
    valid = jnp.asarray(key_pos >= 0)[:, None, None, None, :]
    bias = band_bias(rel_table, CHUNK, WINDOW, band)
    out = sink_attention(qc, kb, vb, bias, sink, valid)
    return out.reshape(b, L, ATTN_W)


def swa_sample(q, k, v, k_cache, v_cache, rel_table, sink):
    b, s = q.shape[0], q.shape[1]
    n_back = k_cache.shape[1]
    kk = jnp.concatenate([k_cache.astype(k.dtype), k], axis=1)
    vv = jnp.concatenate([v_cache.astype(v.dtype), v], axis=1)
    bias = band_bias(rel_table, s, n_back, n_back + s)
    out = sink_attention(q, kk, vv, bias, sink)
    return out.reshape(b, s, ATTN_W)


def ssm_discretise(lam_re, lam_im, log_dt, b_re, b_im):
    f32 = jnp.float32
    dt = jnp.exp(log_dt.astype(f32))[:, None]
    lr, li = lam_re.astype(f32), lam_im.astype(f32)
    mag = jnp.exp(lr * dt)
    a_re, a_im = mag * jnp.cos(li * dt), mag * jnp.sin(li * dt)
    den = lr * lr + li * li
    coef_re = ((a_re - 1.0) * lr + a_im * li) / den
    coef_im = (a_im * lr - (a_re - 1.0) * li) / den
    br, bi = b_re.astype(f32), b_im.astype(f32)
    cr, ci = coef_re[..., None], coef_im[..., None]
    return a_re, a_im, cr * br - ci * bi, cr * bi + ci * br


def complex_affine_combine(e1, e2):
    a1r, a1i, b1r, b1i = e1
    a2r, a2i, b2r, b2i = e2
    return (a2r * a1r - a2i * a1i, a2r * a1i + a2i * a1r,
            a2r * b1r - a2i * b1i + b2r, a2r * b1i + a2i * b1r + b2i)


def ssm_branch(u, state, w):
    b, L, _ = u.shape
    f32 = jnp.float32
    ug = u.reshape(b, L, SSM_GROUPS, SSM_GROUP).astype(f32)
    a_re, a_im, bb_re, bb_im = ssm_discretise(w['ssm_lambda_re'], w['ssm_lambda_im'], w['ssm_log_dt'], w['ssm_b_re'], w['ssm_b_im'])
    bu_re = jnp.einsum('blgc,gpc->blgp', ug, bb_re)
    bu_im = jnp.einsum('blgc,gpc->blgp', ug, bb_im)
    shape = bu_re.shape
    elems = (jnp.broadcast_to(a_re, shape), jnp.broadcast_to(a_im, shape), bu_re, bu_im)
    ac_re, ac_im, s_re, s_im = lax.associative_scan(complex_affine_combine, elems, axis=1)
    if state is not None:
        s0_re = state[0].astype(f32)[:, None]
        s0_im = state[1].astype(f32)[:, None]
        s_re, s_im = (s_re + ac_re * s0_re - ac_im * s0_im, s_im + ac_re * s0_im + ac_im * s0_re)
    c_re, c_im = w['ssm_c_re'].astype(f32), w['ssm_c_im'].astype(f32)
    y = (jnp.einsum('blgp,gcp->blgc', s_re, c_re) - jnp.einsum('blgp,gcp->blgc', s_im, c_im)
         + w['ssm_d'].astype(f32) * ug)
    y = y.reshape(b, L, SSM_W).astype(u.dtype)
    ya, yb = jnp.split(y @ w['w_ssm_glu'], 2, axis=-1)
    return ya * jax.nn.sigmoid(yb), s_re[:, -1], s_im[:, -1]


def memory_kv(mem, g, w_kv):
    mk, mv = jnp.split(rmsnorm(mem, g) @ w_kv, 2, axis=-1)
    shape = mem.shape[:2] + (MEM_HEADS, MEM_HEAD_DIM)
    return mk.reshape(shape), mv.reshape(shape)


def cross_attention(q, mk, mv):
    logits = jnp.einsum('blhd,bmhd->bhlm', q, mk, preferred_element_type=jnp.float32) * (MEM_HEAD_DIM ** -0.5)
    p = jax.nn.softmax(logits, axis=-1).astype(mv.dtype)
    return jnp.einsum('bhlm,bmhd->blhd', p, mv)


def layer(x, mem_k, mem_v, swa_cache, ssm_state, rel_table, w):
    b, L, _ = x.shape
    x = x + 0.5 * rmsnorm(swiglu(rmsnorm(x, w['ff1_pre_g']), w['w_ff1_in'], w['w_ff1_out']), w['ff1_post_g'])
    h = rmsnorm(x, w['mix_pre_g'])
    cuts = np.cumsum([ATTN_W, KV_W, KV_W, SSM_W, MEM_W]).tolist()
    q, k, v, u, qm, gate_logits = jnp.split(h @ w['w_in'], cuts, axis=-1)
    q = q.reshape(b, L, N_KV_HEADS, KV_REP, HEAD_DIM)
    k = k.reshape(b, L, N_KV_HEADS, HEAD_DIM)
    v = v.reshape(b, L, N_KV_HEADS, HEAD_DIM)
    if swa_cache is None:
        attn = swa_prompt(q, k, v, rel_table, w['attn_sink'])
    else:
        attn = swa_sample(q, k, v, swa_cache[0], swa_cache[1], rel_table, w['attn_sink'])
    ssm_out, s_re, s_im = ssm_branch(u, ssm_state, w)
    mem_out = cross_attention(qm.reshape(b, L, MEM_HEADS, MEM_HEAD_DIM), mem_k.astype(x.dtype), mem_v.astype(x.dtype)).reshape(b, L, MEM_W)
    gates = jax.nn.sigmoid(gate_logits).reshape(b, L, N_BRANCH, D_MODEL)
    merged = (gates[:, :, 0] * (attn @ w['w_attn_br']) + gates[:, :, 1] * ssm_out
              + gates[:, :, 2] * (mem_out @ w['w_mem_br']))
    x = x + rmsnorm(merged @ w['w_out'], w['mix_post_g'])
    x = x + 0.5 * rmsnorm(swiglu(rmsnorm(x, w['ff2_pre_g']), w['w_ff2_in'], w['w_ff2_out']), w['ff2_post_g'])
    return x, k, v, s_re, s_im


def setup_inputs(seed: int = 0) -> dict:
    key = jax.random.key(seed)
    ks = iter(jax.random.split(key, 48))
    f32 = jnp.float32
    nrm = lambda shape, scale: scale * jax.random.normal(next(ks), shape, f32)
    gain = lambda shape: 1.0 + 0.02 * jax.random.normal(next(ks), shape, f32)
    n_back = min(WINDOW, PAST_LEN)
    lam_im_base = jnp.pi * jnp.arange(SSM_STATE, dtype=f32)
    return {
        'x_prompt': nrm((BATCH, SEQ, D_MODEL), 1.0),
        'x_sample': nrm((DEC_BATCH, DEC_SEQ, D_MODEL), 1.0),
        'cache_swa_k': nrm((DEPTH, DEC_BATCH, n_back, N_KV_HEADS, HEAD_DIM), 1.0),
        'cache_swa_v': nrm((DEPTH, DEC_BATCH, n_back, N_KV_HEADS, HEAD_DIM), 1.0),
        'cache_mem_k': nrm((DEPTH, DEC_BATCH, MEM_LEN, MEM_HEADS, MEM_HEAD_DIM), 1.0),
        'cache_mem_v': nrm((DEPTH, DEC_BATCH, MEM_LEN, MEM_HEADS, MEM_HEAD_DIM), 1.0),
        'state_ssm_re': nrm((DEPTH, DEC_BATCH, SSM_GROUPS, SSM_STATE), 0.1),
        'state_ssm_im': nrm((DEPTH, DEC_BATCH, SSM_GROUPS, SSM_STATE), 0.1),
        'mem_prompt': nrm((BATCH, MEM_LEN, D_MODEL), 1.0),
        'rel_bias_table': nrm((N_BUCKETS, N_HEADS), 0.2),
        'ff1_pre_g': gain((DEPTH, D_MODEL)),
        'ff1_post_g': gain((DEPTH, D_MODEL)),
        'w_ff1_in': nrm((DEPTH, D_MODEL, 2 * D_FF), D_MODEL ** -0.5),
        'w_ff1_out': nrm((DEPTH, D_FF, D_MODEL), D_FF ** -0.5),
        'mix_pre_g': gain((DEPTH, D_MODEL)),
        'mix_post_g': gain((DEPTH, D_MODEL)),
        'w_in': nrm((DEPTH, D_MODEL, IN_COLS), D_MODEL ** -0.5),
        'mem_norm_g': gain((DEPTH, D_MODEL)),
        'w_mem_kv': nrm((DEPTH, D_MODEL, 2 * MEM_W), D_MODEL ** -0.5),
        'attn_sink': nrm((DEPTH, N_HEADS), 0.5),
        'ssm_lambda_re': -0.5 + nrm((DEPTH, SSM_GROUPS, SSM_STATE), 0.01),
        'ssm_lambda_im': lam_im_base + nrm((DEPTH, SSM_GROUPS, SSM_STATE), 0.01),
        'ssm_log_dt': jax.random.uniform(next(ks), (DEPTH, SSM_GROUPS), f32, math.log(DT_MIN), math.log(DT_MAX)),
        'ssm_b_re': nrm((DEPTH, SSM_GROUPS, SSM_STATE, SSM_GROUP), (2 * SSM_GROUP) ** -0.5),
        'ssm_b_im': nrm((DEPTH, SSM_GROUPS, SSM_STATE, SSM_GROUP), (2 * SSM_GROUP) ** -0.5),
        'ssm_c_re': nrm((DEPTH, SSM_GROUPS, SSM_GROUP, SSM_STATE), SSM_STATE ** -0.5),
        'ssm_c_im': nrm((DEPTH, SSM_GROUPS, SSM_GROUP, SSM_STATE), SSM_STATE ** -0.5),
        'ssm_d': nrm((DEPTH, SSM_GROUPS, SSM_GROUP), 1.0),
        'w_ssm_glu': nrm((DEPTH, SSM_W, 2 * D_MODEL), SSM_W ** -0.5),
        'w_attn_br': nrm((DEPTH, ATTN_W, D_MODEL), ATTN_W ** -0.5),
        'w_mem_br': nrm((DEPTH, MEM_W, D_MODEL), MEM_W ** -0.5),
        'w_out': nrm((DEPTH, D_MODEL, D_MODEL), D_MODEL ** -0.5),
        'ff2_pre_g': gain((DEPTH, D_MODEL)),
        'ff2_post_g': gain((DEPTH, D_MODEL)),
        'w_ff2_in': nrm((DEPTH, D_MODEL, 2 * D_FF), D_MODEL ** -0.5),
        'w_ff2_out': nrm((DEPTH, D_FF, D_MODEL), D_FF ** -0.5),
    }


def reference(x_prompt, x_sample, cache_swa_k, cache_swa_v, cache_mem_k, cache_mem_v,
              state_ssm_re, state_ssm_im, mem_prompt, rel_bias_table,
              ff1_pre_g, ff1_post_g, w_ff1_in, w_ff1_out, mix_pre_g, mix_post_g, w_in,
              mem_norm_g, w_mem_kv, attn_sink, ssm_lambda_re, ssm_lambda_im, ssm_log_dt,
              ssm_b_re, ssm_b_im, ssm_c_re, ssm_c_im, ssm_d, w_ssm_glu, w_attn_br, w_mem_br,
              w_out, ff2_pre_g, ff2_post_g, w_ff2_in, w_ff2_out):
    yp, ys = x_prompt, x_sample
    pk_l, pv_l, pmk_l, pmv_l, pre_l, pim_l = [], [], [], [], [], []
    sk_l, sv_l, sre_l, sim_l = [], [], [], []
    for l in range(DEPTH):
        w = dict(ff1_pre_g=ff1_pre_g[l], ff1_post_g=ff1_post_g[l], w_ff1_in=w_ff1_in[l], w_ff1_out=w_ff1_out[l],
                 mix_pre_g=mix_pre_g[l], mix_post_g=mix_post_g[l], w_in=w_in[l], attn_sink=attn_sink[l],
                 ssm_lambda_re=ssm_lambda_re[l], ssm_lambda_im=ssm_lambda_im[l], ssm_log_dt=ssm_log_dt[l],
                 ssm_b_re=ssm_b_re[l], ssm_b_im=ssm_b_im[l], ssm_c_re=ssm_c_re[l], ssm_c_im=ssm_c_im[l],
                 ssm_d=ssm_d[l], w_ssm_glu=w_ssm_glu[l], w_attn_br=w_attn_br[l], w_mem_br=w_mem_br[l],
                 w_out=w_out[l], ff2_pre_g=ff2_pre_g[l], ff2_post_g=ff2_post_g[l],
                 w_ff2_in=w_ff2_in[l], w_ff2_out=w_ff2_out[l])
        mk, mv = memory_kv(mem_prompt, mem_norm_g[l], w_mem_kv[l])
        yp, pk, pv, pre, pim = layer(yp, mk, mv, None, None, rel_bias_table, w)
        ys, sk, sv, sre, sim = layer(ys, cache_mem_k[l], cache_mem_v[l], (cache_swa_k[l], cache_swa_v[l]),
                                     (state_ssm_re[l], state_ssm_im[l]), rel_bias_table, w)
        n_keep = min(WINDOW, pk.shape[1])
        pk_l.append(pk[:, -n_keep:])
        pv_l.append(pv[:, -n_keep:])
        pmk_l.append(mk)
        pmv_l.append(mv)
        pre_l.append(pre)
        pim_l.append(pim)
        sk_l.append(sk)
        sv_l.append(sv)
        sre_l.append(sre)
        sim_l.append(sim)
    return (yp, ys, jnp.stack(pk_l), jnp.stack(pv_l), jnp.stack(pmk_l), jnp.stack(pmv_l),
            jnp.stack(pre_l), jnp.stack(pim_l), jnp.stack(sk_l), jnp.stack(sv_l),
            jnp.stack(sre_l), jnp.stack(sim_l))
```

```python
import functools
import math

import numpy as np
import jax
import jax.numpy as jnp
from jax import lax
from jax.experimental import pallas as pl
from jax.experimental.pallas import tpu as pltpu

D_MODEL = 1024
CHUNK = 64
WINDOW = 128
HEAD_DIM = 64
MIX_W = D_MODEL // 2
N_HEADS = MIX_W // HEAD_DIM
N_KV_HEADS = N_HEADS // 4
KV_REP = N_HEADS // N_KV_HEADS
ATTN_W = N_HEADS * HEAD_DIM
KV_W = N_KV_HEADS * HEAD_DIM
SSM_GROUP = 16
SSM_W = MIX_W
SSM_GROUPS = SSM_W // SSM_GROUP
SSM_STATE = 64
MEM_LEN = 256
MEM_HEADS = 4
MEM_HEAD_DIM = MIX_W // MEM_HEADS
MEM_W = MEM_HEADS * MEM_HEAD_DIM
N_BRANCH = 3
PROJ_COLS = ATTN_W + 2 * KV_W + SSM_W + MEM_W
D_FF = 128 * ((8 * D_MODEL // 3 + 127) // 128)
N_BUCKETS = 32
MAX_DISTANCE = 128
RMS_EPS = 1e-6
NEG_INF = -1e30

SSM_T = 16
SSM_K = SSM_T * SSM_GROUP
SSM_S2 = 2 * SSM_STATE
SSM_WIDTH = SSM_GROUPS * SSM_S2
SCAN_ROWS = 8
N_SCAN_TABLES = 10
SCAN_LANES = 1024
SSM_CHUNKS_PER_STEP = 64

LANES = 128
ROW_TILE = 256
VMEM_LIMIT = 56 * 1024 * 1024

F32 = jnp.float32
BF16 = jnp.bfloat16


def _params(*sem):
    return pltpu.CompilerParams(dimension_semantics=sem, vmem_limit_bytes=VMEM_LIMIT)


def _resident(shape):
    zeros = (0,) * len(shape)
    return pl.BlockSpec(shape, lambda *_: zeros, pipeline_mode=pl.Buffered(1))


def _rms(x, g):
    return x * lax.rsqrt(jnp.mean(x * x, axis=-1, keepdims=True) + RMS_EPS) * g


def _mm(a, b):
    return jnp.dot(a, b, preferred_element_type=F32)


def _ffn(x, gpre, gpost, w_in_ref, w_out_ref):
    h = _rms(x, gpre).astype(BF16)
    gu = _mm(h, w_in_ref[...])
    g, u = gu[:, :D_FF], gu[:, D_FF:]
    act = (g * jax.nn.sigmoid(g) * u).astype(BF16)
    return x + 0.5 * _rms(_mm(act, w_out_ref[...]), gpost)


def _ffn1_proj_body(x_ref, gpre_ref, gpost_ref, wfi_ref, wfo_ref, gmix_ref, wp_ref,
                    x1_ref, q_ref, k_ref, v_ref, u_ref, qm_ref):
    x1 = _ffn(x_ref[...], gpre_ref[...], gpost_ref[...], wfi_ref, wfo_ref)
    x1_ref[...] = x1
    p = _mm(_rms(x1, gmix_ref[...]).astype(BF16), wp_ref[...])
    c0, c1, c2, c3 = ATTN_W, ATTN_W + KV_W, ATTN_W + 2 * KV_W, ATTN_W + 2 * KV_W + SSM_W
    q_ref[...] = p[:, :c0].astype(BF16)
    k_ref[...] = p[:, c0:c1]
    v_ref[...] = p[:, c1:c2]
    u_ref[...] = p[:, c2:c3]
    qm_ref[...] = p[:, c3:].astype(BF16)


def _ffn1_proj(x, gpre, gpost, wfi, wfo, gmix, wp):
    n = x.shape[0]
    row = lambda w: pl.BlockSpec((ROW_TILE, w), lambda i: (i, 0))
    return pl.pallas_call(
        _ffn1_proj_body,
        grid=(n // ROW_TILE,),
        in_specs=[row(D_MODEL), _resident((1, D_MODEL)), _resident((1, D_MODEL)),
                  _resident(wfi.shape), _resident(wfo.shape), _resident((1, D_MODEL)),
                  _resident(wp.shape)],
        out_specs=[row(D_MODEL), row(ATTN_W), row(KV_W), row(KV_W), row(SSM_W), row(MEM_W)],
        out_shape=[jax.ShapeDtypeStruct((n, D_MODEL), F32),
                   jax.ShapeDtypeStruct((n, ATTN_W), BF16),
                   jax.ShapeDtypeStruct((n, KV_W), F32),
                   jax.ShapeDtypeStruct((n, KV_W), F32),
                   jax.ShapeDtypeStruct((n, SSM_W), F32),
                   jax.ShapeDtypeStruct((n, MEM_W), BF16)],
        compiler_params=_params("parallel"),
        name="ffn1_proj",
    )(x, gpre, gpost, wfi, wfo, gmix, wp)


def _mem_kv_body(m_ref, g_ref, w_ref, k_ref, v_ref):
    kv = _mm(_rms(m_ref[...], g_ref[...]).astype(BF16), w_ref[...])
    k_ref[...] = kv[:, :MEM_W]
    v_ref[...] = kv[:, MEM_W:]


def _mem_kv(mem, g, w):
    n = mem.shape[0]
    row = lambda wd: pl.BlockSpec((ROW_TILE, wd), lambda i: (i, 0))
    return pl.pallas_call(
        _mem_kv_body,
        grid=(n // ROW_TILE,),
        in_specs=[row(D_MODEL), _resident((1, D_MODEL)), _resident(w.shape)],
        out_specs=[row(MEM_W), row(MEM_W)],
        out_shape=[jax.ShapeDtypeStruct((n, MEM_W), F32)] * 2,
        compiler_params=_params("parallel"),
        name="mem_kv",
    )(mem, g, w)


def _sink_attention(qs, kb, vb, bias, sink, valid):
    s = lax.dot_general(qs, kb, (((1,), (1,)), ((), ())), preferred_element_type=F32)
    s = s * (HEAD_DIM ** -0.5) + bias
    if valid is not None:
        s = jnp.where(valid, s, NEG_INF)
    m = jnp.maximum(jnp.max(s, axis=-1, keepdims=True), sink)
    e = jnp.exp(s - m)
    p = e / (jnp.sum(e, axis=-1, keepdims=True) + jnp.exp(sink - m))
    return _mm(p.astype(BF16), vb)


def _gqa_chunk(q, kb, vb, bias, sink, valid):
    nq = q.shape[0]
    low = lax.broadcasted_iota(jnp.int32, (nq, LANES), 1) < HEAD_DIM
    zero = jnp.zeros((nq, LANES), BF16)
    pairs = []
    for g in range(N_KV_HEADS):
        rows = []
        for r in range(KV_REP):
            h = g * KV_REP + r
            q2 = q[:, (h // 2) * LANES:(h // 2 + 1) * LANES]
            rows.append(jnp.where(low, q2, zero) if h % 2 == 0 else jnp.where(low, zero, q2))
        o = _sink_attention(jnp.concatenate(rows, axis=0), kb[g], vb[g], bias[g], sink[g], valid)
        for r in range(0, KV_REP, 2):
            pairs.append(jnp.where(low, o[r * nq:(r + 1) * nq], o[(r + 1) * nq:(r + 2) * nq]))
    return jnp.concatenate(pairs, axis=1)


def _mem_attention(qm, mk_ref, mv_ref):
    outs = []
    for h in range(MEM_HEADS):
        sl = slice(h * MEM_HEAD_DIM, (h + 1) * MEM_HEAD_DIM)
        s = lax.dot_general(qm[:, sl], mk_ref[:, sl].astype(BF16), (((1,), (1,)), ((), ())),
                            preferred_element_type=F32) * (MEM_HEAD_DIM ** -0.5)
        e = jnp.exp(s - jnp.max(s, axis=-1, keepdims=True))
        p = e / jnp.sum(e, axis=-1, keepdims=True)
        outs.append(_mm(p.astype(BF16), mv_ref[:, sl].astype(BF16)))
    return jnp.concatenate(outs, axis=1)


def _attn_prompt_body(q_ref, k_ref, kh_ref, v_ref, vh_ref, bias_ref, sink_ref, qm_ref, mk_ref, mv_ref,
                      o_ref, mo_ref):
    i = pl.program_id(1)
    tq = q_ref.shape[1]
    kf = [jnp.concatenate([kh_ref[0, g], k_ref[0, g]], axis=0) for g in range(N_KV_HEADS)]
    vf = [jnp.concatenate([vh_ref[0, g], v_ref[0, g]], axis=0) for g in range(N_KV_HEADS)]
    band = WINDOW + CHUNK
    bias = [bias_ref[g] for g in range(N_KV_HEADS)]
    sink = [sink_ref[g] for g in range(N_KV_HEADS)]
    for c in range(tq // CHUNK):
        lo = c * CHUNK
        kb = [kf[g][lo:lo + band] for g in range(N_KV_HEADS)]
        vb = [vf[g][lo:lo + band] for g in range(N_KV_HEADS)]
        valid = None
        if lo < WINDOW:
            key_pos = lax.broadcasted_iota(jnp.int32, (1, band), 1) + (i * tq + lo - WINDOW)
            valid = key_pos >= 0
        o = _gqa_chunk(q_ref[0, lo:lo + CHUNK, :], kb, vb, bias, sink, valid)
        o_ref[0, lo:lo + CHUNK, :] = o.astype(BF16)
    mo_ref[0] = _mem_attention(qm_ref[0], mk_ref.at[0], mv_ref.at[0]).astype(BF16)


def _attn_prompt(q, k2, v2, bias, sink, qm, mk, mv):
    b, l, _ = q.shape
    tq = ROW_TILE
    hb = tq // WINDOW
    cur = lambda w: pl.BlockSpec((1, tq, w), lambda bi, i: (bi, i, 0))
    kcur = pl.BlockSpec((1, N_KV_HEADS, tq, LANES), lambda bi, i: (bi, 0, i, 0))
    khalo = pl.BlockSpec((1, N_KV_HEADS, WINDOW, LANES),
                         lambda bi, i: (bi, 0, jnp.maximum(i * hb - 1, 0), 0))
    mem = pl.BlockSpec((1, MEM_LEN, MEM_W), lambda bi, i: (bi, 0, 0))
    return pl.pallas_call(
        _attn_prompt_body,
        grid=(b, l // tq),
        in_specs=[cur(ATTN_W), kcur, khalo, kcur, khalo, _resident(bias.shape), _resident(sink.shape),
                  cur(MEM_W), mem, mem],
        out_specs=[cur(ATTN_W), cur(MEM_W)],
        out_shape=[jax.ShapeDtypeStruct((b, l, ATTN_W), BF16), jax.ShapeDtypeStruct((b, l, MEM_W), BF16)],
        compiler_params=_params("parallel", "parallel"),
        name="attn_prompt",
    )(q, k2, k2, v2, v2, bias, sink, qm, mk, mv)


def _attn_sample_body(q_ref, k_ref, v_ref, bias_ref, sink_ref, qm_ref, mk_ref, mv_ref, o_ref, mo_ref):
    bias = [bias_ref[g] for g in range(N_KV_HEADS)]
    sink = [sink_ref[g] for g in range(N_KV_HEADS)]
    for b in range(q_ref.shape[0]):
        kb = [k_ref[b, g] for g in range(N_KV_HEADS)]
        vb = [v_ref[b, g] for g in range(N_KV_HEADS)]
        o_ref[b] = _gqa_chunk(q_ref[b], kb, vb, bias, sink, None).astype(BF16)
        mo_ref[b] = _mem_attention(qm_ref[b], mk_ref.at[b], mv_ref.at[b]).astype(BF16)


def _attn_sample(q, kk2, vv2, bias, sink, qm, mk, mv):
    b, s, _ = q.shape
    nb = 8
    nk = kk2.shape[2]
    blk = lambda *shape: pl.BlockSpec((nb,) + shape, lambda i: (i,) + (0,) * len(shape))
    return pl.pallas_call(
        _attn_sample_body,
        grid=(b // nb,),
        in_specs=[blk(s, ATTN_W), blk(N_KV_HEADS, nk, LANES), blk(N_KV_HEADS, nk, LANES),
                  _resident(bias.shape), _resident(sink.shape),
                  blk(s, MEM_W), blk(MEM_LEN, MEM_W), blk(MEM_LEN, MEM_W)],
        out_specs=[blk(s, ATTN_W), blk(s, MEM_W)],
        out_shape=[jax.ShapeDtypeStruct((b, s, ATTN_W), BF16), jax.ShapeDtypeStruct((b, s, MEM_W), BF16)],
        compiler_params=_params("parallel"),
        name="attn_sample",
    )(q, kk2, vv2, bias, sink, qm, mk, mv)


def _ssm_weights_body(lam_re_row, lam_im_row, lam_re_col, lam_im_col, log_dt, b_re_t, b_im_t,
                      b_re_rows, b_im_rows, c_re_lanes, c_im_lanes,
                      wconv_ref, wstate_ref, wout_ref, a_tab_ref):
    dt = jnp.exp(log_dt[0])

    def zoh_coef(lr, li):
        mag = jnp.exp(lr * dt)
        a_re, a_im = mag * jnp.cos(li * dt), mag * jnp.sin(li * dt)
        den = lr * lr + li * li
        return ((a_re - 1.0) * lr + a_im * li) / den, (a_im * lr - (a_re - 1.0) * li) / den

    def a_power(lr, li, n):
        mag = jnp.exp(lr * dt * n)
        return mag * jnp.cos(li * dt * n), mag * jnp.sin(li * dt * n)

    lr, li = lam_re_row[0], lam_im_row[0]
    cr, ci = zoh_coef(lr, li)
    t_row = (lax.broadcasted_iota(jnp.int32, (SSM_K, 1), 0) // SSM_GROUP).astype(F32)
    pr, pi = a_power(lr, li, (SSM_T - 1) - t_row)
    zr, zi = pr * cr - pi * ci, pr * ci + pi * cr
    br, bi = b_re_rows[0], b_im_rows[0]
    wstate_ref[0] = jnp.concatenate([zr * br - zi * bi, zr * bi + zi * br], axis=1).astype(BF16)

    lrc, lic = lam_re_col[0], lam_im_col[0]
    t_lane = (lax.broadcasted_iota(jnp.int32, (1, SSM_K), 1) // SSM_GROUP).astype(F32)
    qr, qi = a_power(lrc, lic, t_lane + 1.0)
    ccr, cci = c_re_lanes[0], c_im_lanes[0]
    wout_ref[0] = jnp.concatenate([ccr * qr - cci * qi, -(ccr * qi + cci * qr)], axis=0).astype(BF16)

    gr, gi = a_power(lrc, lic, t_lane)
    g_re, g_im = ccr * gr - cci * gi, ccr * gi + cci * gr
    btr, bti = b_re_t[0], b_im_t[0]
    bbr, bbi = cr * btr - ci * bti, cr * bti + ci * btr
    hi = lax.Precision.HIGHEST
    krow = (jnp.dot(bbr, g_re, precision=hi, preferred_element_type=F32)
            - jnp.dot(bbi, g_im, precision=hi, preferred_element_type=F32))
    lane = lax.broadcasted_iota(jnp.int32, (SSM_GROUP, SSM_K), 1)
    for t in range(SSM_T):
        blk = krow if t == 0 else jnp.where(lane >= t * SSM_GROUP, pltpu.roll(krow, t * SSM_GROUP, 1), 0.0)
        wconv_ref[0, t * SSM_GROUP:(t + 1) * SSM_GROUP, :] = blk.astype(BF16)

    idx = lax.broadcasted_iota(jnp.int32, (N_SCAN_TABLES * SCAN_ROWS, SSM_STATE), 0)
    tab, r = idx // SCAN_ROWS, idx % SCAN_ROWS
    stride = jnp.where(tab < 2, 1, jnp.where(tab < 4, 2, 4))
    n = jnp.where(tab < 6, stride, jnp.where(tab < 8, r, SCAN_ROWS))
    keep = jnp.logical_or(tab >= 6, r >= stride)
    er, ei = a_power(lr, li, (n * SSM_T).astype(F32))
    er, ei = jnp.where(keep, er, 0.0), jnp.where(keep, ei, 0.0)
    odd = tab % 2 == 1
    a_tab_ref[0] = jnp.concatenate([jnp.where(odd, -ei, er), jnp.where(odd, ei, er)], axis=1)


def _ssm_weights(lam_re, lam_im, log_dt, b_re, b_im, c_re, c_im):
    g, p, c = b_re.shape
    row3 = lambda x: x.reshape(g, 1, p)
    col3 = lambda x: x.reshape(g, p, 1)
    b_t = lambda x: jnp.transpose(x, (0, 2, 1))
    b_rows = lambda x: jnp.tile(b_t(x), (1, SSM_T, 1))
    c_lanes = lambda x: jnp.tile(jnp.transpose(x, (0, 2, 1)), (1, 1, SSM_T))
    args = (row3(lam_re), row3(lam_im), col3(lam_re), col3(lam_im), log_dt.reshape(g, 1, 1),
            b_t(b_re), b_t(b_im), b_rows(b_re), b_rows(b_im), c_lanes(c_re), c_lanes(c_im))
    spec = lambda x: pl.BlockSpec((1,) + x.shape[1:], lambda i: (i, 0, 0))
    out_shapes = [((g, SSM_K, SSM_K), BF16), ((g, SSM_K, SSM_S2), BF16), ((g, SSM_S2, SSM_K), BF16),
                  ((g, N_SCAN_TABLES * SCAN_ROWS, SSM_S2), F32)]
    return pl.pallas_call(
        _ssm_weights_body,
        grid=(g,),
        in_specs=[spec(a) for a in args],
        out_specs=[pl.BlockSpec((1,) + s[1:], lambda i: (i, 0, 0)) for s, _ in out_shapes],
        out_shape=[jax.ShapeDtypeStruct(s, d) for s, d in out_shapes],
        compiler_params=_params("parallel"),
        name="ssm_weights",
    )(*args)


def _swap_halves(s):
    return jnp.concatenate([pltpu.roll(s[:, l:l + SSM_S2], SSM_STATE, 1)
                            for l in range(0, s.shape[1], SSM_S2)], axis=1)


def _ssm_body(slab_rows, u_ref, wconv_ref, wstate_ref, wout_ref, d_ref, a_tab_ref, s0_ref,
              y_ref, s_ref, v_scr, s_scr):
    n_sets, rows = u_ref.shape[1], u_ref.shape[2]

    @pl.when(pl.program_id(0) == 0)
    def _():
        s_scr[...] = s0_ref[...]

    for g in range(SSM_GROUPS):
        for j in range(n_sets):
            v_scr[j * rows:(j + 1) * rows, g * SSM_S2:(g + 1) * SSM_S2] = _mm(
                u_ref[g, j].astype(BF16), wstate_ref[g])

    if slab_rows:
        a_mul, a_swap = a_tab_ref[6, 1:2, :], a_tab_ref[7, 1:2, :]

        def step(c, s):
            r0 = pl.multiple_of(c * slab_rows, SCAN_ROWS)
            inc = v_scr[pl.ds(r0, slab_rows), :]
            v_scr[pl.ds(r0, slab_rows), :] = s
            return a_mul * s + a_swap * _swap_halves(s) + inc

        s_scr[...] = lax.fori_loop(0, rows // slab_rows, step, s_scr[...])
    else:
        not_first = lax.broadcasted_iota(jnp.int32, (SCAN_ROWS, 1), 0) >= 1
        for j in range(n_sets):
            for l0 in range(0, SSM_WIDTH, SCAN_LANES):
                lanes = slice(l0, l0 + SCAN_LANES)
                carry_rows = slice(j * SCAN_ROWS, (j + 1) * SCAN_ROWS)

                def block(t, _, j=j, lanes=lanes, carry_rows=carry_rows):
                    r0 = pl.multiple_of(j * rows + t * SCAN_ROWS, SCAN_ROWS)
                    x = v_scr[pl.ds(r0, SCAN_ROWS), lanes]
                    for k in range(3):
                        sh = pltpu.roll(x, 1 << k, 0)
                        x = x + a_tab_ref[2 * k, :, lanes] * sh + a_tab_ref[2 * k + 1, :, lanes] * _swap_halves(sh)
                    carry = s_scr[carry_rows, lanes]
                    carry_sw = _swap_halves(carry)
                    enter = (jnp.where(not_first, pltpu.roll(x, 1, 0), 0.0)
                             + a_tab_ref[6, :, lanes] * carry + a_tab_ref[7, :, lanes] * carry_sw)
                    v_scr[pl.ds(r0, SCAN_ROWS), lanes] = enter
                    s_scr[carry_rows, lanes] = (jnp.broadcast_to(x[SCAN_ROWS - 1:, :], x.shape)
                                                + a_tab_ref[8, :, lanes] * carry + a_tab_ref[9, :, lanes] * carry_sw)
                    return 0

                lax.fori_loop(0, rows // SCAN_ROWS, block, 0)
    s_ref[...] = s_scr[...]

    for g in range(SSM_GROUPS):
        for j in range(n_sets):
            u = u_ref[g, j]
            y = _mm(u.astype(BF16), wconv_ref[g])
            y = y + _mm(v_scr[j * rows:(j + 1) * rows, g * SSM_S2:(g + 1) * SSM_S2].astype(BF16), wout_ref[g])
            y_ref[g, j] = y + u * d_ref[g]


def _ssm(u_t, slab_rows, wconv, wstate, wout, d_lanes, a_tab, s0, rows_per_step):
    g, ns, r, k = u_t.shape
    blk = pl.BlockSpec((g, ns, rows_per_step, k), lambda i: (0, 0, i, 0))
    return pl.pallas_call(
        functools.partial(_ssm_body, slab_rows),
        grid=(r // rows_per_step,),
        in_specs=[blk, _resident(wconv.shape), _resident(wstate.shape), _resident(wout.shape),
                  _resident(d_lanes.shape), _resident(a_tab.shape), _resident(s0.shape)],
        out_specs=[blk, pl.BlockSpec(s0.shape, lambda i: (0, 0))],
        out_shape=[jax.ShapeDtypeStruct(u_t.shape, F32), jax.ShapeDtypeStruct(s0.shape, F32)],
        scratch_shapes=[pltpu.VMEM((ns * rows_per_step, SSM_WIDTH), F32), pltpu.VMEM(s0.shape, F32)],
        compiler_params=_params("arbitrary"),
        name="ssm_scan",
    )(u_t, wconv, wstate, wout, d_lanes, a_tab, s0)


def _ssm_branch(u, s0, ssm_w):
    b, l, _ = u.shape
    nc = l // SSM_T
    cps = min(nc, SSM_CHUNKS_PER_STEP)
    u5 = u.reshape(b, nc, SSM_T, SSM_GROUPS, SSM_GROUP)
    if b % SCAN_ROWS == 0:
        u_t = u5.transpose(3, 1, 0, 2, 4).reshape(SSM_GROUPS, 1, nc * b, SSM_K)
        y_t, s_last = _ssm(u_t, b, *ssm_w, s0, cps * b)
        y5 = y_t.reshape(SSM_GROUPS, nc, b, SSM_T, SSM_GROUP).transpose(2, 1, 3, 0, 4)
    else:
        u_t = u5.transpose(3, 0, 1, 2, 4).reshape(SSM_GROUPS, b, nc, SSM_K)
        y_t, s_last = _ssm(u_t, 0, *ssm_w, jnp.repeat(s0, SCAN_ROWS, axis=0), cps)
        y5 = y_t.reshape(SSM_GROUPS, b, nc, SSM_T, SSM_GROUP).transpose(1, 2, 3, 0, 4)
        s_last = s_last[::SCAN_ROWS]
    s_last = s_last.reshape(b, SSM_GROUPS, 2, SSM_STATE)
    return y5.reshape(b, l, SSM_W), s_last[:, :, 0], s_last[:, :, 1]


def _merge_ffn2_body(x_ref, a_ref, m_ref, y_ref, gmix_ref, wg_ref, wab_ref, wglu_ref, wmb_ref, wo_ref,
                     gpost_ref, g2pre_ref, g2post_ref, wfi_ref, wfo_ref, o_ref):
    x1 = x_ref[...]
    gates = jax.nn.sigmoid(_mm(_rms(x1, gmix_ref[...]).astype(BF16), wg_ref[...]))
    yy = _mm(y_ref[...].astype(BF16), wglu_ref[...])
    ssm_out = yy[:, :D_MODEL] * jax.nn.sigmoid(yy[:, D_MODEL:])
    merged = (gates[:, :D_MODEL] * _mm(a_ref[...], wab_ref[...])
              + gates[:, D_MODEL:2 * D_MODEL] * ssm_out
              + gates[:, 2 * D_MODEL:] * _mm(m_ref[...], wmb_ref[...]))
    x2 = x1 + _rms(_mm(merged.astype(BF16), wo_ref[...]), gpost_ref[...])
    o_ref[...] = _ffn(x2, g2pre_ref[...], g2post_ref[...], wfi_ref, wfo_ref)


def _merge_ffn2(x1, attn, memo, yssm, gmix, wg, wab, wglu, wmb, wo, gpost, g2pre, g2post, wfi, wfo):
    n = x1.shape[0]
    row = lambda w: pl.BlockSpec((ROW_TILE, w), lambda i: (i, 0))
    vec = _resident((1, D_MODEL))
    return pl.pallas_call(
        _merge_ffn2_body,
        grid=(n // ROW_TILE,),
        in_specs=[row(D_MODEL), row(ATTN_W), row(MEM_W), row(SSM_W), vec, _resident(wg.shape),
                  _resident(wab.shape), _resident(wglu.shape), _resident(wmb.shape), _resident(wo.shape),
                  vec, vec, vec, _resident(wfi.shape), _resident(wfo.shape)],
        out_specs=row(D_MODEL),
        out_shape=jax.ShapeDtypeStruct((n, D_MODEL), F32),
        compiler_params=_params("parallel"),
        name="merge_ffn2",
    )(x1, attn, memo, yssm, gmix, wg, wab, wglu, wmb, wo, gpost, g2pre, g2post, wfi, wfo)


def _t5_bucket(rel):
    half = N_BUCKETS // 2
    max_exact = half // 2
    ret = (rel > 0).astype(np.int32) * half
    n = np.abs(rel)
    large = max_exact + (np.log(np.maximum(n, 1) / max_exact) / math.log(MAX_DISTANCE / max_exact)
                         * (half - max_exact)).astype(np.int32)
    large = np.minimum(large, half - 1)
    return ret + np.where(n < max_exact, n, large)


def _band_bias(rel_table, n_q, n_back, n_k):
    i = np.arange(n_q)[:, None]
    j = np.arange(n_k)[None, :]
    b = rel_table[_t5_bucket((j - n_back) - i)].astype(F32)
    return jnp.transpose(b, (2, 0, 1)).reshape(N_KV_HEADS, KV_REP * n_q, n_k)


def _sink_rows(sink, n_q):
    return jnp.repeat(sink.astype(F32).reshape(N_KV_HEADS, KV_REP), n_q, axis=1)[:, :, None]


def _kv_lanes(x):
    b, n, _ = x.shape
    xh = x.astype(BF16).reshape(b, n, N_KV_HEADS, HEAD_DIM).transpose(0, 2, 1, 3)
    return jnp.concatenate([xh, xh], axis=-1)


def kernel(x_prompt, x_sample, cache_swa_k, cache_swa_v, cache_mem_k, cache_mem_v, state_ssm_re, state_ssm_im, mem_prompt, rel_bias_table, ff1_pre_g, ff1_post_g, w_ff1_in, w_ff1_out, mix_pre_g, mix_post_g, w_in, mem_norm_g, w_mem_kv, attn_sink, ssm_lambda_re, ssm_lambda_im, ssm_log_dt, ssm_b_re, ssm_b_im, ssm_c_re, ssm_c_im, ssm_d, w_ssm_glu, w_attn_br, w_mem_br, w_out, ff2_pre_g, ff2_post_g, w_ff2_in, w_ff2_out):
    assert x_prompt.shape[0] and ff1_pre_g.shape[0] == 1, "single-layer step"
    bp, lp, _ = x_prompt.shape
    bs, ls, _ = x_sample.shape
    vec = lambda g: g[0].reshape(1, D_MODEL).astype(F32)
    w16 = lambda w: w[0].astype(BF16)

    wfi1, wfo1, wfi2, wfo2 = w16(w_ff1_in), w16(w_ff1_out), w16(w_ff2_in), w16(w_ff2_out)
    w_in16 = w16(w_in)
    wp, wg = w_in16[:, :PROJ_COLS], w_in16[:, PROJ_COLS:]
    wab, wglu, wmb, wo = w16(w_attn_br), w16(w_ssm_glu), w16(w_mem_br), w16(w_out)

    wconv, wstate, wout_s, a_tab = _ssm_weights(
        ssm_lambda_re[0], ssm_lambda_im[0], ssm_log_dt[0], ssm_b_re[0], ssm_b_im[0], ssm_c_re[0], ssm_c_im[0])
    width = SSM_WIDTH
    a_tab = a_tab.reshape(SSM_GROUPS, N_SCAN_TABLES, SCAN_ROWS, SSM_S2).transpose(1, 2, 0, 3)
    ssm_w = (wconv, wstate, wout_s, jnp.tile(ssm_d[0].astype(F32), (1, SSM_T)).reshape(SSM_GROUPS, 1, SSM_K),
             a_tab.reshape(N_SCAN_TABLES, SCAN_ROWS, width))

    mk_p, mv_p = _mem_kv(mem_prompt.reshape(bp * MEM_LEN, D_MODEL), vec(mem_norm_g), w16(w_mem_kv))

    def group(x, attend, s0, mem_k, mem_v):
        b, l, _ = x.shape
        x1, q, k, v, u, qm = _ffn1_proj(x.reshape(b * l, D_MODEL), vec(ff1_pre_g), vec(ff1_post_g),
                                        wfi1, wfo1, vec(mix_pre_g), wp)
        k3, v3 = k.reshape(b, l, KV_W), v.reshape(b, l, KV_W)
        attn, memo = attend(q.reshape(b, l, ATTN_W), k3, v3, qm.reshape(b, l, MEM_W), mem_k, mem_v)
        y_ssm, s_re, s_im = _ssm_branch(u.reshape(b, l, SSM_W), s0, ssm_w)
        y = _merge_ffn2(x1, attn.reshape(b * l, ATTN_W), memo.reshape(b * l, MEM_W),
                        y_ssm.reshape(b * l, SSM_W), vec(mix_pre_g), wg, wab, wglu, wmb, wo,
                        vec(mix_post_g), vec(ff2_pre_g), vec(ff2_post_g), wfi2, wfo2)
        return y.reshape(b, l, D_MODEL), k3, v3, s_re, s_im

    def attend_prompt(q, k, v, qm, mem_k, mem_v):
        bias = _band_bias(rel_bias_table, CHUNK, WINDOW, WINDOW + CHUNK)
        return _attn_prompt(q, _kv_lanes(k), _kv_lanes(v), bias, _sink_rows(attn_sink[0], CHUNK), qm, mem_k, mem_v)

    def attend_sample(q, k, v, qm, mem_k, mem_v):
        n_back = cache_swa_k.shape[2]
        kk = jnp.concatenate([cache_swa_k[0].reshape(bs, n_back, KV_W), k], axis=1)
        vv = jnp.concatenate([cache_swa_v[0].reshape(bs, n_back, KV_W), v], axis=1)
        bias = _band_bias(rel_bias_table, ls, n_back, n_back + ls)
        return _attn_sample(q, _kv_lanes(kk), _kv_lanes(vv), bias, _sink_rows(attn_sink[0], ls), qm, mem_k, mem_v)

    yp, pk, pv, pre, pim = group(x_prompt, attend_prompt, jnp.zeros((bp, width), F32),
                                 mk_p.reshape(bp, MEM_LEN, MEM_W), mv_p.reshape(bp, MEM_LEN, MEM_W))
    s0 = jnp.stack([state_ssm_re[0], state_ssm_im[0]], axis=2).reshape(bs, width).astype(F32)
    ys, sk, sv, sre, sim = group(x_sample, attend_sample, s0,
                                 cache_mem_k[0].reshape(bs, MEM_LEN, MEM_W),
                                 cache_mem_v[0].reshape(bs, MEM_LEN, MEM_W))

    n_keep = min(WINDOW, lp)
    heads = lambda t: t.reshape(t.shape[0], t.shape[1], N_KV_HEADS, HEAD_DIM)[None]
    mem_heads = lambda t: t.reshape(bp, MEM_LEN, MEM_HEADS, MEM_HEAD_DIM)[None]
    return (yp, ys, heads(pk[:, -n_keep:]), heads(pv[:, -n_keep:]), mem_heads(mk_p), mem_heads(mv_p),
            pre[None], pim[None], heads(sk), heads(sv), sre[None], sim[None])
```

```python
import functools
import math

import numpy as np
import jax
import jax.numpy as jnp
from jax import lax
from jax.experimental import pallas as pl
from jax.experimental.pallas import tpu as pltpu

D_MODEL = 1024
CHUNK = 64
WINDOW = 128
HEAD_DIM = 64
MIX_W = D_MODEL // 2
N_HEADS = MIX_W // HEAD_DIM
N_KV_HEADS = N_HEADS // 4
KV_REP = N_HEADS // N_KV_HEADS
ATTN_W = N_HEADS * HEAD_DIM
KV_W = N_KV_HEADS * HEAD_DIM
SSM_GROUP = 16
SSM_W = MIX_W
SSM_GROUPS = SSM_W // SSM_GROUP
SSM_STATE = 64
MEM_LEN = 256
MEM_HEADS = 4
MEM_HEAD_DIM = MIX_W // MEM_HEADS
MEM_W = MEM_HEADS * MEM_HEAD_DIM
D_FF = 128 * ((8 * D_MODEL // 3 + 127) // 128)
N_BUCKETS = 32
MAX_DISTANCE = 128
RMS_EPS = 1e-6
NEG_INF = -1e30

LANES = 128
MXU_TILE = 256
ROW_TILE = 256
VMEM_LIMIT = 56 * 1024 * 1024

KV2_W = N_KV_HEADS * LANES

SSM_T = 16
SSM_K = SSM_T * SSM_GROUP
SSM_S2 = 2 * SSM_STATE
SSM_WIDTH = SSM_GROUPS * SSM_S2
SSM_GB = LANES // SSM_GROUP
SSM_NGB = SSM_GROUPS // SSM_GB
SSM_XW = SSM_T * LANES
SSM_SW = SSM_GB * SSM_S2
SSM_TPT = MXU_TILE // LANES
SSM_NT = SSM_T // SSM_TPT
SCAN_ROWS = 8
N_SCAN_TABLES = 10
SSM_CHUNKS_PER_STEP = 128

F32 = jnp.float32
BF16 = jnp.bfloat16


def _params(*sem):
    return pltpu.CompilerParams(dimension_semantics=sem, vmem_limit_bytes=VMEM_LIMIT)


def _resident(shape):
    zeros = (0,) * len(shape)
    return pl.BlockSpec(shape, lambda *_: zeros, pipeline_mode=pl.Buffered(1))


def _rms(x, g):
    return x * lax.rsqrt(jnp.mean(x * x, axis=-1, keepdims=True) + RMS_EPS) * g


def _mm(a, b):
    return jnp.dot(a, b, preferred_element_type=F32)


def _ffn(x, gpre, gpost, w_in_ref, w_out_ref):
    h = _rms(x, gpre).astype(BF16)
    gu = _mm(h, w_in_ref[...])
    g, u = gu[:, :D_FF], gu[:, D_FF:]
    act = (g * jax.nn.sigmoid(g) * u).astype(BF16)
    return x + 0.5 * _rms(_mm(act, w_out_ref[...]), gpost)


PROJ_SPLIT = (("q", ATTN_W, BF16), ("k", KV_W, F32), ("v", KV_W, F32), ("u", SSM_W, F32),
              ("qm", MEM_W, BF16), ("k2", KV2_W, BF16), ("v2", KV2_W, BF16))


def _ffn1_proj_body(x_ref, gpre_ref, gpost_ref, wfi_ref, wfo_ref, gmix_ref, wp_ref, x1_ref, *out_refs):
    x1 = _ffn(x_ref[...], gpre_ref[...], gpost_ref[...], wfi_ref, wfo_ref)
    x1_ref[...] = x1
    p = _mm(_rms(x1, gmix_ref[...]).astype(BF16), wp_ref[...])
    col = 0
    for ref, (_, width, dtype) in zip(out_refs, PROJ_SPLIT):
        ref[...] = p[:, col:col + width].astype(dtype)
        col += width


def _ffn1_proj(x, gpre, gpost, wfi, wfo, gmix, wp):
    n = x.shape[0]
    row = lambda w: pl.BlockSpec((ROW_TILE, w), lambda i: (i, 0))
    return pl.pallas_call(
        _ffn1_proj_body,
        grid=(n // ROW_TILE,),
        in_specs=[row(D_MODEL), _resident((1, D_MODEL)), _resident((1, D_MODEL)),
                  _resident(wfi.shape), _resident(wfo.shape), _resident((1, D_MODEL)),
                  _resident(wp.shape)],
        out_specs=[row(D_MODEL)] + [row(w) for _, w, _ in PROJ_SPLIT],
        out_shape=[jax.ShapeDtypeStruct((n, D_MODEL), F32)]
        + [jax.ShapeDtypeStruct((n, w), d) for _, w, d in PROJ_SPLIT],
        compiler_params=_params("parallel"),
        name="ffn1_proj",
    )(x, gpre, gpost, wfi, wfo, gmix, wp)


def _mem_kv_body(m_ref, g_ref, w_ref, k_ref, v_ref):
    kv = _mm(_rms(m_ref[...], g_ref[...]).astype(BF16), w_ref[...])
    k_ref[...] = kv[:, :MEM_W]
    v_ref[...] = kv[:, MEM_W:]


def _mem_kv(mem, g, w):
    n = mem.shape[0]
    row = lambda wd: pl.BlockSpec((ROW_TILE, wd), lambda i: (i, 0))
    return pl.pallas_call(
        _mem_kv_body,
        grid=(n // ROW_TILE,),
        in_specs=[row(D_MODEL), _resident((1, D_MODEL)), _resident(w.shape)],
        out_specs=[row(MEM_W), row(MEM_W)],
        out_shape=[jax.ShapeDtypeStruct((n, MEM_W), F32)] * 2,
        compiler_params=_params("parallel"),
        name="mem_kv",
    )(mem, g, w)


def _sink_attention(qs, kb, vb, bias, sink, valid):
    s = lax.dot_general(qs, kb, (((1,), (1,)), ((), ())), preferred_element_type=F32)
    s = s * (HEAD_DIM ** -0.5) + bias
    if valid is not None:
        s = jnp.where(valid, s, NEG_INF)
    m = jnp.maximum(jnp.max(s, axis=-1, keepdims=True), sink)
    e = jnp.exp(s - m)
    p = e / (jnp.sum(e, axis=-1, keepdims=True) + jnp.exp(sink - m))
    return _mm(p.astype(BF16), vb)


def _gqa_chunk(q, kb, vb, bias, sink, valid):
    nq = q.shape[0]
    low = lax.broadcasted_iota(jnp.int32, (nq, LANES), 1) < HEAD_DIM
    zero = jnp.zeros((nq, LANES), BF16)
    pairs = []
    for g in range(N_KV_HEADS):
        rows = []
        for r in range(KV_REP):
            h = g * KV_REP + r
            q2 = q[:, (h // 2) * LANES:(h // 2 + 1) * LANES]
            rows.append(jnp.where(low, q2, zero) if h % 2 == 0 else jnp.where(low, zero, q2))
        o = _sink_attention(jnp.concatenate(rows, axis=0), kb[g], vb[g], bias[g], sink[g], valid)
        for r in range(0, KV_REP, 2):
            pairs.append(jnp.where(low, o[r * nq:(r + 1) * nq], o[(r + 1) * nq:(r + 2) * nq]))
    return jnp.concatenate(pairs, axis=1)


def _mem_attention(qm, mk_ref, mv_ref):
    outs = []
    for h in range(MEM_HEADS):
        sl = slice(h * MEM_HEAD_DIM, (h + 1) * MEM_HEAD_DIM)
        s = lax.dot_general(qm[:, sl], mk_ref[:, sl].astype(BF16), (((1,), (1,)), ((), ())),
                            preferred_element_type=F32) * (MEM_HEAD_DIM ** -0.5)
        e = jnp.exp(s - jnp.max(s, axis=-1, keepdims=True))
        p = e / jnp.sum(e, axis=-1, keepdims=True)
        outs.append(_mm(p.astype(BF16), mv_ref[:, sl].astype(BF16)))
    return jnp.concatenate(outs, axis=1)


def _kv_heads(x):
    return [x[:, g * LANES:(g + 1) * LANES] for g in range(N_KV_HEADS)]


def _attn_prompt_body(q_ref, k_ref, kh_ref, v_ref, vh_ref, bias_ref, sink_ref, qm_ref, mk_ref, mv_ref,
                      o_ref, mo_ref):
    i = pl.program_id(1)
    tq = q_ref.shape[1]
    kf = jnp.concatenate([kh_ref[0], k_ref[0]], axis=0)
    vf = jnp.concatenate([vh_ref[0], v_ref[0]], axis=0)
    band = WINDOW + CHUNK
    bias = [bias_ref[g] for g in range(N_KV_HEADS)]
    sink = [sink_ref[g] for g in range(N_KV_HEADS)]
    for c in range(tq // CHUNK):
        lo = c * CHUNK
        valid = None
        if lo < WINDOW:
            key_pos = lax.broadcasted_iota(jnp.int32, (1, band), 1) + (i * tq + lo - WINDOW)
            valid = key_pos >= 0
        o = _gqa_chunk(q_ref[0, lo:lo + CHUNK, :], _kv_heads(kf[lo:lo + band]), _kv_heads(vf[lo:lo + band]),
                       bias, sink, valid)
        o_ref[0, lo:lo + CHUNK, :] = o.astype(BF16)
    mo_ref[0] = _mem_attention(qm_ref[0], mk_ref.at[0], mv_ref.at[0]).astype(BF16)


def _attn_prompt(q, k2, v2, bias, sink, qm, mk, mv):
    b, l, _ = q.shape
    tq = ROW_TILE
    hb = tq // WINDOW
    cur = lambda w: pl.BlockSpec((1, tq, w), lambda bi, i: (bi, i, 0))
    halo = pl.BlockSpec((1, WINDOW, KV2_W), lambda bi, i: (bi, jnp.maximum(i * hb - 1, 0), 0))
    mem = pl.BlockSpec((1, MEM_LEN, MEM_W), lambda bi, i: (bi, 0, 0))
    return pl.pallas_call(
        _attn_prompt_body,
        grid=(b, l // tq),
        in_specs=[cur(ATTN_W), cur(KV2_W), halo, cur(KV2_W), halo, _resident(bias.shape),
                  _resident(sink.shape), cur(MEM_W), mem, mem],
        out_specs=[cur(ATTN_W), cur(MEM_W)],
        out_shape=[jax.ShapeDtypeStruct((b, l, ATTN_W), BF16), jax.ShapeDtypeStruct((b, l, MEM_W), BF16)],
        compiler_params=_params("parallel", "parallel"),
        name="attn_prompt",
    )(q, k2, k2, v2, v2, bias, sink, qm, mk, mv)


def _attn_sample_body(q_ref, k_ref, v_ref, bias_ref, sink_ref, qm_ref, mk_ref, mv_ref, o_ref, mo_ref):
    bias = [bias_ref[g] for g in range(N_KV_HEADS)]
    sink = [sink_ref[g] for g in range(N_KV_HEADS)]
    for b in range(q_ref.shape[0]):
        o_ref[b] = _gqa_chunk(q_ref[b], _kv_heads(k_ref[b]), _kv_heads(v_ref[b]), bias, sink, None).astype(BF16)
        mo_ref[b] = _mem_attention(qm_ref[b], mk_ref.at[b], mv_ref.at[b]).astype(BF16)


def _attn_sample(q, kk2, vv2, bias, sink, qm, mk, mv):
    b, s, _ = q.shape
    nb = 8
    nk = kk2.shape[1]
    blk = lambda *shape: pl.BlockSpec((nb,) + shape, lambda i: (i,) + (0,) * len(shape))
    return pl.pallas_call(
        _attn_sample_body,
        grid=(b // nb,),
        in_specs=[blk(s, ATTN_W), blk(nk, KV2_W), blk(nk, KV2_W),
                  _resident(bias.shape), _resident(sink.shape),
                  blk(s, MEM_W), blk(MEM_LEN, MEM_W), blk(MEM_LEN, MEM_W)],
        out_specs=[blk(s, ATTN_W), blk(s, MEM_W)],
        out_shape=[jax.ShapeDtypeStruct((b, s, ATTN_W), BF16), jax.ShapeDtypeStruct((b, s, MEM_W), BF16)],
        compiler_params=_params("parallel"),
        name="attn_sample",
    )(q, kk2, vv2, bias, sink, qm, mk, mv)


def _ssm_weights_body(lam_re_row, lam_im_row, lam_re_col, lam_im_col, log_dt, b_re_t, b_im_t,
                      b_re_rows, b_im_rows, c_re_lanes, c_im_lanes,
                      krow_ref, wstate_ref, wout_ref, a_tab_ref):
    dt = jnp.exp(log_dt[0])

    def zoh_coef(lr, li):
        mag = jnp.exp(lr * dt)
        a_re, a_im = mag * jnp.cos(li * dt), mag * jnp.sin(li * dt)
        den = lr * lr + li * li
        return ((a_re - 1.0) * lr + a_im * li) / den, (a_im * lr - (a_re - 1.0) * li) / den

    def a_power(lr, li, n):
        mag = jnp.exp(lr * dt * n)
        return mag * jnp.cos(li * dt * n), mag * jnp.sin(li * dt * n)

    lr, li = lam_re_row[0], lam_im_row[0]
    cr, ci = zoh_coef(lr, li)
    t_row = (lax.broadcasted_iota(jnp.int32, (SSM_K, 1), 0) // SSM_GROUP).astype(F32)
    pr, pi = a_power(lr, li, (SSM_T - 1) - t_row)
    zr, zi = pr * cr - pi * ci, pr * ci + pi * cr
    br, bi = b_re_rows[0], b_im_rows[0]
    wstate_ref[0] = jnp.concatenate([zr * br - zi * bi, zr * bi + zi * br], axis=1)

    lrc, lic = lam_re_col[0], lam_im_col[0]
    t_lane = (lax.broadcasted_iota(jnp.int32, (1, SSM_K), 1) // SSM_GROUP).astype(F32)
    qr, qi = a_power(lrc, lic, t_lane + 1.0)
    ccr, cci = c_re_lanes[0], c_im_lanes[0]
    wout_ref[0] = jnp.concatenate([ccr * qr - cci * qi, -(ccr * qi + cci * qr)], axis=0)

    gr, gi = a_power(lrc, lic, t_lane)
    g_re, g_im = ccr * gr - cci * gi, ccr * gi + cci * gr
    btr, bti = b_re_t[0], b_im_t[0]
    bbr, bbi = cr * btr - ci * bti, cr * bti + ci * btr
    hi = lax.Precision.HIGHEST
    krow_ref[0] = (jnp.dot(bbr, g_re, precision=hi, preferred_element_type=F32)
                   - jnp.dot(bbi, g_im, precision=hi, preferred_element_type=F32))

    idx = lax.broadcasted_iota(jnp.int32, (N_SCAN_TABLES * SCAN_ROWS, SSM_STATE), 0)
    tab, r = idx // SCAN_ROWS, idx % SCAN_ROWS
    stride = jnp.where(tab < 2, 1, jnp.where(tab < 4, 2, 4))
    n = jnp.where(tab < 6, stride, jnp.where(tab < 8, r, SCAN_ROWS))
    keep = jnp.logical_or(tab >= 6, r >= stride)
    er, ei = a_power(lr, li, (n * SSM_T).astype(F32))
    er, ei = jnp.where(keep, er, 0.0), jnp.where(keep, ei, 0.0)
    odd = tab % 2 == 1
    a_tab_ref[0] = jnp.concatenate([jnp.where(odd, -ei, er), jnp.where(odd, ei, er)], axis=1)


def _ssm_weights(lam_re, lam_im, log_dt, b_re, b_im, c_re, c_im):
    g, p, c = b_re.shape
    row3 = lambda x: x.reshape(g, 1, p)
    col3 = lambda x: x.reshape(g, p, 1)
    b_t = lambda x: jnp.transpose(x, (0, 2, 1))
    b_rows = lambda x: jnp.tile(b_t(x), (1, SSM_T, 1))
    c_lanes = lambda x: jnp.tile(jnp.transpose(x, (0, 2, 1)), (1, 1, SSM_T))
    args = (row3(lam_re), row3(lam_im), col3(lam_re), col3(lam_im), log_dt.reshape(g, 1, 1),
            b_t(b_re), b_t(b_im), b_rows(b_re), b_rows(b_im), c_lanes(c_re), c_lanes(c_im))
    spec = lambda x: pl.BlockSpec((1,) + x.shape[1:], lambda i: (i, 0, 0))
    out_shapes = [(g, SSM_GROUP, SSM_K), (g, SSM_K, SSM_S2), (g, SSM_S2, SSM_K),
                  (g, N_SCAN_TABLES * SCAN_ROWS, SSM_S2)]
    return pl.pallas_call(
        _ssm_weights_body,
        grid=(g,),
        in_specs=[spec(a) for a in args],
        out_specs=[pl.BlockSpec((1,) + s[1:], lambda i: (i, 0, 0)) for s in out_shapes],
        out_shape=[jax.ShapeDtypeStruct(s, F32) for s in out_shapes],
        compiler_params=_params("parallel"),
        name="ssm_weights",
    )(*args)


def _ssm_block_layout(krow, wstate, wout):
    eye = jnp.eye(SSM_GB, dtype=F32)
    k4 = krow.reshape(SSM_GROUPS, SSM_GROUP, SSM_T, SSM_GROUP)
    k4 = jnp.concatenate([jnp.zeros_like(k4[:, :, :1]), k4], axis=2)
    d = np.arange(SSM_NT)[::-1, None, None]
    lag = SSM_TPT * d + np.arange(SSM_TPT)[None, None, :] - np.arange(SSM_TPT)[None, :, None] + 1
    tiles = k4[:, :, lag, :]
    tiles = tiles.reshape(SSM_NGB, SSM_GB, SSM_GROUP, SSM_NT, SSM_TPT, SSM_TPT, SSM_GROUP)
    wconv = jnp.einsum('xgcdioe,gh->xdigcohe', tiles, eye).reshape(SSM_NGB, SSM_NT * MXU_TILE, MXU_TILE)
    ws = wstate.reshape(SSM_NGB, SSM_GB, SSM_T, SSM_GROUP, SSM_S2)
    wst = jnp.einsum('xgtcs,gh->xtgchs', ws, eye).reshape(SSM_NGB, SSM_XW, SSM_SW)
    wo = wout.reshape(SSM_NGB, SSM_GB, SSM_S2, SSM_T, SSM_GROUP)
    wo = jnp.einsum('xgstc,gh->xgsthc', wo, eye).reshape(SSM_NGB, SSM_SW, SSM_XW)
    return wconv.astype(BF16), wst.astype(BF16), wo.astype(BF16)


def _swap_halves(s):
    return jnp.concatenate([pltpu.roll(s[:, l:l + SSM_S2], SSM_STATE, 1)
                            for l in range(0, s.shape[1], SSM_S2)], axis=1)


def _ssm_body(row_sets, slab_rows, u_ref, wconv_ref, wst_ref, wo_ref, d_ref, a_tab_ref, s0_ref,
              y_ref, s_ref, v_scr, s_scr):
    def piece(ref_set, first, n, stride, t):
        return ref_set, pl.ds(first + t, n, stride=stride)

    @pl.when(pl.program_id(1) == 0)
    def _():
        s_scr[...] = s0_ref[0]

    x = jnp.concatenate(
        [jnp.concatenate([u_ref[piece(*rs, t)] for t in range(SSM_T)], axis=1) for rs in row_sets],
        axis=0).astype(BF16)
    v_scr[...] = _mm(x, wst_ref[0])

    n_rows = v_scr.shape[0]
    if slab_rows:
        a_mul, a_swap = a_tab_ref[6, 1:2, :], a_tab_ref[7, 1:2, :]
        s = s_scr[...]
        for c in range(n_rows // slab_rows):
            rows = slice(c * slab_rows, (c + 1) * slab_rows)
            inc = v_scr[rows, :]
            v_scr[rows, :] = s
            s = a_mul * s + a_swap * _swap_halves(s) + inc
        s_scr[...] = s
    else:
        not_first = lax.broadcasted_iota(jnp.int32, (SCAN_ROWS, 1), 0) >= 1
        r0_set = 0
        for j, (_, _, n, _) in enumerate(row_sets):
            carry_rows = slice(j * SCAN_ROWS, (j + 1) * SCAN_ROWS)

            def block(t, _, r0_set=r0_set, carry_rows=carry_rows):
                r0 = pl.multiple_of(r0_set + t * SCAN_ROWS, SCAN_ROWS)
                xs = v_scr[pl.ds(r0, SCAN_ROWS), :]
                for k in range(3):
                    sh = pltpu.roll(xs, 1 << k, 0)
                    xs = xs + a_tab_ref[2 * k] * sh + a_tab_ref[2 * k + 1] * _swap_halves(sh)
                carry = s_scr[carry_rows, :]
                carry_sw = _swap_halves(carry)
                enter = (jnp.where(not_first, pltpu.roll(xs, 1, 0), 0.0)
                         + a_tab_ref[6] * carry + a_tab_ref[7] * carry_sw)
                v_scr[pl.ds(r0, SCAN_ROWS), :] = enter
                s_scr[carry_rows, :] = (jnp.broadcast_to(xs[SCAN_ROWS - 1:, :], xs.shape)
                                        + a_tab_ref[8] * carry + a_tab_ref[9] * carry_sw)
                return 0

            lax.fori_loop(0, n // SCAN_ROWS, block, 0)
            r0_set += n
    s_ref[0] = s_scr[...]

    enter = v_scr[...].astype(BF16)
    d = d_ref[...]
    for jo in range(SSM_NT):
        k_hi = (jo + 1) * MXU_TILE
        yt = _mm(x[:, :k_hi], wconv_ref[0, (SSM_NT - 1 - jo) * MXU_TILE:, :])
        yt = yt + _mm(enter, wo_ref[0, :, jo * MXU_TILE:(jo + 1) * MXU_TILE])
        r0 = 0
        for rs in row_sets:
            for tl in range(SSM_TPT):
                idx = piece(*rs, jo * SSM_TPT + tl)
                y_ref[idx] = yt[r0:r0 + rs[2], tl * LANES:(tl + 1) * LANES] + u_ref[idx] * d
            r0 += rs[2]


def _ssm(u, row_sets, slab_rows, block_rows, wconv, wst, wo, d_lanes, a_tab, s0):
    ns, r, _ = u.shape
    chunk_rows = sum(rs[2] for rs in row_sets)
    carry_rows = s0.shape[1]
    blk = pl.BlockSpec((ns, block_rows, LANES), lambda gb, i: (0, i, gb))
    per_gb = lambda x: pl.BlockSpec((1,) + x.shape[1:], lambda gb, i: (gb, 0, 0))
    return pl.pallas_call(
        functools.partial(_ssm_body, row_sets, slab_rows),
        grid=(SSM_NGB, r // block_rows),
        in_specs=[blk, per_gb(wconv), per_gb(wst), per_gb(wo),
                  pl.BlockSpec((1, LANES), lambda gb, i: (0, gb)),
                  pl.BlockSpec((N_SCAN_TABLES, SCAN_ROWS, SSM_SW), lambda gb, i: (0, 0, gb)),
                  per_gb(s0)],
        out_specs=[blk, per_gb(s0)],
        out_shape=[jax.ShapeDtypeStruct(u.shape, F32), jax.ShapeDtypeStruct(s0.shape, F32)],
        scratch_shapes=[pltpu.VMEM((chunk_rows, SSM_SW), F32), pltpu.VMEM((carry_rows, SSM_SW), F32)],
        compiler_params=_params("parallel", "arbitrary"),
        name="ssm_scan",
    )(u, wconv, wst, wo, d_lanes, a_tab, s0)


def _ssm_branch(u, s0, ssm_w):
    b, l, _ = u.shape
    nc = l // SSM_T
    by_gb = lambda s: s.reshape(s.shape[0], SSM_NGB, SSM_SW).transpose(1, 0, 2)
    if b % SCAN_ROWS == 0:
        row_sets = tuple((0, c * SSM_T, b, l) for c in range(nc))
        y, s_last = _ssm(u.reshape(1, b * l, SSM_W), row_sets, b, b * l, *ssm_w, by_gb(s0))
    else:
        cps = min(nc, SSM_CHUNKS_PER_STEP)
        row_sets = tuple((j, 0, cps, SSM_T) for j in range(b))
        y, s_last = _ssm(u, row_sets, 0, cps * SSM_T, *ssm_w, by_gb(jnp.repeat(s0, SCAN_ROWS, axis=0)))
        s_last = s_last[:, ::SCAN_ROWS]
    s_last = s_last.transpose(1, 0, 2).reshape(b, SSM_GROUPS, 2, SSM_STATE)
    return y.reshape(b, l, SSM_W), s_last[:, :, 0], s_last[:, :, 1]


def _merge_ffn2_body(x_ref, a_ref, m_ref, y_ref, gmix_ref, wg_ref, wab_ref, wglu_ref, wmb_ref, wo_ref,
                     gpost_ref, g2pre_ref, g2post_ref, wfi_ref, wfo_ref, o_ref):
    x1 = x_ref[...]
    gates = jax.nn.sigmoid(_mm(_rms(x1, gmix_ref[...]).astype(BF16), wg_ref[...]))
    yy = _mm(y_ref[...].astype(BF16), wglu_ref[...])
    ssm_out = yy[:, :D_MODEL] * jax.nn.sigmoid(yy[:, D_MODEL:])
    merged = (gates[:, :D_MODEL] * _mm(a_ref[...], wab_ref[...])
              + gates[:, D_MODEL:2 * D_MODEL] * ssm_out
              + gates[:, 2 * D_MODEL:] * _mm(m_ref[...], wmb_ref[...]))
    x2 = x1 + _rms(_mm(merged.astype(BF16), wo_ref[...]), gpost_ref[...])
    o_ref[...] = _ffn(x2, g2pre_ref[...], g2post_ref[...], wfi_ref, wfo_ref)


def _merge_ffn2(x1, attn, memo, yssm, gmix, wg, wab, wglu, wmb, wo, gpost, g2pre, g2post, wfi, wfo):
    n = x1.shape[0]
    row = lambda w: pl.BlockSpec((ROW_TILE, w), lambda i: (i, 0))
    vec = _resident((1, D_MODEL))
    return pl.pallas_call(
        _merge_ffn2_body,
        grid=(n // ROW_TILE,),
        in_specs=[row(D_MODEL), row(ATTN_W), row(MEM_W), row(SSM_W), vec, _resident(wg.shape),
                  _resident(wab.shape), _resident(wglu.shape), _resident(wmb.shape), _resident(wo.shape),
                  vec, vec, vec, _resident(wfi.shape), _resident(wfo.shape)],
        out_specs=row(D_MODEL),
        out_shape=jax.ShapeDtypeStruct((n, D_MODEL), F32),
        compiler_params=_params("parallel"),
        name="merge_ffn2",
    )(x1, attn, memo, yssm, gmix, wg, wab, wglu, wmb, wo, gpost, g2pre, g2post, wfi, wfo)


def _t5_bucket(rel):
    half = N_BUCKETS // 2
    max_exact = half // 2
    ret = (rel > 0).astype(np.int32) * half
    n = np.abs(rel)
    large = max_exact + (np.log(np.maximum(n, 1) / max_exact) / math.log(MAX_DISTANCE / max_exact)
                         * (half - max_exact)).astype(np.int32)
    large = np.minimum(large, half - 1)
    return ret + np.where(n < max_exact, n, large)


def _band_bias(rel_table, n_q, n_back, n_k):
    i = np.arange(n_q)[:, None]
    j = np.arange(n_k)[None, :]
    b = rel_table[_t5_bucket((j - n_back) - i)].astype(F32)
    return jnp.transpose(b, (2, 0, 1)).reshape(N_KV_HEADS, KV_REP * n_q, n_k)


def _sink_rows(sink, n_q):
    return jnp.repeat(sink.astype(F32).reshape(N_KV_HEADS, KV_REP), n_q, axis=1)[:, :, None]


def _twice_per_head(x, axis):
    shape = x.shape
    x = x.reshape(shape[:axis] + (N_KV_HEADS, 1, HEAD_DIM) + shape[axis + 1:])
    x = jnp.concatenate([x, x], axis=axis + 1)
    return x.reshape(shape[:axis] + (KV2_W,) + shape[axis + 1:])


def kernel(x_prompt, x_sample, cache_swa_k, cache_swa_v, cache_mem_k, cache_mem_v, state_ssm_re, state_ssm_im, mem_prompt, rel_bias_table, ff1_pre_g, ff1_post_g, w_ff1_in, w_ff1_out, mix_pre_g, mix_post_g, w_in, mem_norm_g, w_mem_kv, attn_sink, ssm_lambda_re, ssm_lambda_im, ssm_log_dt, ssm_b_re, ssm_b_im, ssm_c_re, ssm_c_im, ssm_d, w_ssm_glu, w_attn_br, w_mem_br, w_out, ff2_pre_g, ff2_post_g, w_ff2_in, w_ff2_out):
    assert ff1_pre_g.shape[0] == 1, "single-layer step"
    bp, lp, _ = x_prompt.shape
    bs, ls, _ = x_sample.shape
    vec = lambda g: g[0].reshape(1, D_MODEL).astype(F32)
    w16 = lambda w: w[0].astype(BF16)

    wfi1, wfo1, wfi2, wfo2 = w16(w_ff1_in), w16(w_ff1_out), w16(w_ff2_in), w16(w_ff2_out)
    w_in16 = w16(w_in)
    c_k, c_v, c_u = ATTN_W, ATTN_W + KV_W, ATTN_W + 2 * KV_W
    c_g = c_u + SSM_W + MEM_W
    wp = jnp.concatenate([w_in16[:, :c_g], _twice_per_head(w_in16[:, c_k:c_v], 1),
                          _twice_per_head(w_in16[:, c_v:c_u], 1)], axis=1)
    wg = w_in16[:, c_g:]
    wab, wglu, wmb, wo = w16(w_attn_br), w16(w_ssm_glu), w16(w_mem_br), w16(w_out)

    krow, wstate, wout_s, a_tab = _ssm_weights(
        ssm_lambda_re[0], ssm_lambda_im[0], ssm_log_dt[0], ssm_b_re[0], ssm_b_im[0], ssm_c_re[0], ssm_c_im[0])
    a_tab = a_tab.reshape(SSM_GROUPS, N_SCAN_TABLES, SCAN_ROWS, SSM_S2).transpose(1, 2, 0, 3)
    ssm_w = _ssm_block_layout(krow, wstate, wout_s) + (
        ssm_d[0].astype(F32).reshape(1, SSM_W), a_tab.reshape(N_SCAN_TABLES, SCAN_ROWS, SSM_WIDTH))

    mk_p, mv_p = _mem_kv(mem_prompt.reshape(bp * MEM_LEN, D_MODEL), vec(mem_norm_g), w16(w_mem_kv))

    def group(x, attend, s0, mem_k, mem_v):
        b, l, _ = x.shape
        r3 = lambda t: t.reshape(b, l, t.shape[-1])
        x1, q, k, v, u, qm, k2, v2 = _ffn1_proj(x.reshape(b * l, D_MODEL), vec(ff1_pre_g), vec(ff1_post_g),
                                                wfi1, wfo1, vec(mix_pre_g), wp)
        attn, memo = attend(r3(q), r3(k2), r3(v2), r3(qm), mem_k, mem_v)
        y_ssm, s_re, s_im = _ssm_branch(r3(u), s0, ssm_w)
        y = _merge_ffn2(x1, attn.reshape(b * l, ATTN_W), memo.reshape(b * l, MEM_W),
                        y_ssm.reshape(b * l, SSM_W), vec(mix_pre_g), wg, wab, wglu, wmb, wo,
                        vec(mix_post_g), vec(ff2_pre_g), vec(ff2_post_g), wfi2, wfo2)
        return r3(y), r3(k), r3(v), s_re, s_im

    def attend_prompt(q, k2, v2, qm, mem_k, mem_v):
        bias = _band_bias(rel_bias_table, CHUNK, WINDOW, WINDOW + CHUNK)
        return _attn_prompt(q, k2, v2, bias, _sink_rows(attn_sink[0], CHUNK), qm, mem_k, mem_v)

    def attend_sample(q, k2, v2, qm, mem_k, mem_v):
        n_back = cache_swa_k.shape[2]
        cache2 = lambda c: _twice_per_head(c[0].reshape(bs, n_back, KV_W).astype(BF16), 2)
        kk = jnp.concatenate([cache2(cache_swa_k), k2], axis=1)
        vv = jnp.concatenate([cache2(cache_swa_v), v2], axis=1)
        bias = _band_bias(rel_bias_table, ls, n_back, n_back + ls)
        return _attn_sample(q, kk, vv, bias, _sink_rows(attn_sink[0], ls), qm, mem_k, mem_v)

    yp, pk, pv, pre, pim = group(x_prompt, attend_prompt, jnp.zeros((bp, SSM_WIDTH), F32),
                                 mk_p.reshape(bp, MEM_LEN, MEM_W), mv_p.reshape(bp, MEM_LEN, MEM_W))
    s0 = jnp.stack([state_ssm_re[0], state_ssm_im[0]], axis=2).reshape(bs, SSM_WIDTH).astype(F32)
    ys, sk, sv, sre, sim = group(x_sample, attend_sample, s0,
                                 cache_mem_k[0].reshape(bs, MEM_LEN, MEM_W),
                                 cache_mem_v[0].reshape(bs, MEM_LEN, MEM_W))

    n_keep = min(WINDOW, lp)
    heads = lambda t: t.reshape(t.shape[0], t.shape[1], N_KV_HEADS, HEAD_DIM)[None]
    mem_heads = lambda t: t.reshape(bp, MEM_LEN, MEM_HEADS, MEM_HEAD_DIM)[None]
    return (yp, ys, heads(pk[:, -n_keep:]), heads(pv[:, -n_keep:]), mem_heads(mk_p), mem_heads(mv_p),
            pre[None], pim[None], heads(sk), heads(sv), sre[None], sim[None])
```

```python
import functools
import math

import numpy as np
import jax
import jax.numpy as jnp
from jax import lax
from jax.experimental import pallas as pl
from jax.experimental.pallas import tpu as pltpu

D_MODEL = 1024
CHUNK = 64
WINDOW = 128
HEAD_DIM = 64
MIX_W = D_MODEL // 2
N_HEADS = MIX_W // HEAD_DIM
N_KV_HEADS = N_HEADS // 4
KV_REP = N_HEADS // N_KV_HEADS
ATTN_W = N_HEADS * HEAD_DIM
KV_W = N_KV_HEADS * HEAD_DIM
SSM_GROUP = 16
SSM_W = MIX_W
SSM_GROUPS = SSM_W // SSM_GROUP
SSM_STATE = 64
MEM_LEN = 256
MEM_HEADS = 4
MEM_HEAD_DIM = MIX_W // MEM_HEADS
MEM_W = MEM_HEADS * MEM_HEAD_DIM
D_FF = 128 * ((8 * D_MODEL // 3 + 127) // 128)
N_BUCKETS = 32
MAX_DISTANCE = 128
RMS_EPS = 1e-6
NEG_INF = -1e30

LANES = 128
MXU_TILE = 256
ROW_TILE = 256
VMEM_LIMIT = 56 * 1024 * 1024

KV2_W = N_KV_HEADS * LANES

SSM_T = 16
SSM_K = SSM_T * SSM_GROUP
SSM_S2 = 2 * SSM_STATE
SSM_WIDTH = SSM_GROUPS * SSM_S2
SSM_GB = LANES // SSM_GROUP
SSM_NGB = SSM_GROUPS // SSM_GB
SSM_XW = SSM_T * LANES
SSM_SW = SSM_GB * SSM_S2
SSM_TPT = MXU_TILE // LANES
SSM_NT = SSM_T // SSM_TPT
SCAN_ROWS = 8
N_SCAN_TABLES = 10
SSM_CHUNKS_PER_STEP = 128

F32 = jnp.float32
BF16 = jnp.bfloat16


def _params(*sem):
    return pltpu.CompilerParams(dimension_semantics=sem, vmem_limit_bytes=VMEM_LIMIT)


def _resident(shape):
    zeros = (0,) * len(shape)
    return pl.BlockSpec(shape, lambda *_: zeros, pipeline_mode=pl.Buffered(1))


def _rms(x, g):
    return x * lax.rsqrt(jnp.mean(x * x, axis=-1, keepdims=True) + RMS_EPS) * g


def _mm(a, b):
    return jnp.dot(a, b, preferred_element_type=F32)


def _ffn(x, gpre, gpost, w_in_ref, w_out_ref):
    h = _rms(x, gpre).astype(BF16)
    gu = _mm(h, w_in_ref[...])
    g, u = gu[:, :D_FF], gu[:, D_FF:]
    act = (g * jax.nn.sigmoid(g) * u).astype(BF16)
    return x + 0.5 * _rms(_mm(act, w_out_ref[...]), gpost)


PROJ_SPLIT = (("q", ATTN_W, BF16), ("k", KV_W, F32), ("v", KV_W, F32), ("u", SSM_W, F32),
              ("qm", MEM_W, BF16), ("k2", KV2_W, BF16), ("v2", KV2_W, BF16))


def _ffn1_proj_body(x_ref, gpre_ref, gpost_ref, wfi_ref, wfo_ref, gmix_ref, wp_ref, x1_ref, *out_refs):
    x1 = _ffn(x_ref[...], gpre_ref[...], gpost_ref[...], wfi_ref, wfo_ref)
    x1_ref[...] = x1
    p = _mm(_rms(x1, gmix_ref[...]).astype(BF16), wp_ref[...])
    col = 0
    for ref, (_, width, dtype) in zip(out_refs, PROJ_SPLIT):
        ref[...] = p[:, col:col + width].astype(dtype)
        col += width


def _ffn1_proj(x, gpre, gpost, wfi, wfo, gmix, wp):
    n = x.shape[0]
    row = lambda w: pl.BlockSpec((ROW_TILE, w), lambda i: (i, 0))
    return pl.pallas_call(
        _ffn1_proj_body,
        grid=(n // ROW_TILE,),
        in_specs=[row(D_MODEL), _resident((1, D_MODEL)), _resident((1, D_MODEL)),
                  _resident(wfi.shape), _resident(wfo.shape), _resident((1, D_MODEL)),
                  _resident(wp.shape)],
        out_specs=[row(D_MODEL)] + [row(w) for _, w, _ in PROJ_SPLIT],
        out_shape=[jax.ShapeDtypeStruct((n, D_MODEL), F32)]
        + [jax.ShapeDtypeStruct((n, w), d) for _, w, d in PROJ_SPLIT],
        compiler_params=_params("parallel"),
        name="ffn1_proj",
    )(x, gpre, gpost, wfi, wfo, gmix, wp)


def _mem_kv_body(m_ref, g_ref, w_ref, k_ref, v_ref):
    kv = _mm(_rms(m_ref[...], g_ref[...]).astype(BF16), w_ref[...])
    k_ref[...] = kv[:, :MEM_W]
    v_ref[...] = kv[:, MEM_W:]


def _mem_kv(mem, g, w):
    n = mem.shape[0]
    row = lambda wd: pl.BlockSpec((ROW_TILE, wd), lambda i: (i, 0))
    return pl.pallas_call(
        _mem_kv_body,
        grid=(n // ROW_TILE,),
        in_specs=[row(D_MODEL), _resident((1, D_MODEL)), _resident(w.shape)],
        out_specs=[row(MEM_W), row(MEM_W)],
        out_shape=[jax.ShapeDtypeStruct((n, MEM_W), F32)] * 2,
        compiler_params=_params("parallel"),
        name="mem_kv",
    )(mem, g, w)


def _qk(q, k):
    return lax.dot_general(q, k, (((1,), (1,)), ((), ())), preferred_element_type=F32)


def _softmax_pv(scores, values, sinks=None):
    s = jnp.concatenate(scores, axis=0)
    m = jnp.max(s, axis=-1, keepdims=True)
    if sinks is not None:
        sink = jnp.concatenate(sinks, axis=0)
        m = jnp.maximum(m, sink)
    e = jnp.exp(s - m)
    den = jnp.sum(e, axis=-1, keepdims=True)
    if sinks is not None:
        den = den + jnp.exp(sink - m)
    p = (e / den).astype(BF16)
    outs, r0 = [], 0
    for sc, v in zip(scores, values):
        outs.append(_mm(p[r0:r0 + sc.shape[0]], v))
        r0 += sc.shape[0]
    return outs


def _gqa_queries(q):
    nq = q.shape[0]
    low = lax.broadcasted_iota(jnp.int32, (nq, LANES), 1) < HEAD_DIM
    zero = jnp.zeros((nq, LANES), BF16)
    stacks = []
    for g in range(N_KV_HEADS):
        rows = []
        for r in range(KV_REP):
            h = g * KV_REP + r
            q2 = q[:, (h // 2) * LANES:(h // 2 + 1) * LANES]
            rows.append(jnp.where(low, q2, zero) if h % 2 == 0 else jnp.where(low, zero, q2))
        stacks.append(jnp.concatenate(rows, axis=0))
    return stacks


def _gqa_outputs(outs, nq):
    low = lax.broadcasted_iota(jnp.int32, (nq, LANES), 1) < HEAD_DIM
    pairs = []
    for o in outs:
        for r in range(0, KV_REP, 2):
            pairs.append(jnp.where(low, o[r * nq:(r + 1) * nq], o[(r + 1) * nq:(r + 2) * nq]))
    return jnp.concatenate(pairs, axis=1)


def _mem_scores(qm, mk_ref):
    return [_qk(qm[:, h * MEM_HEAD_DIM:(h + 1) * MEM_HEAD_DIM],
                mk_ref[:, h * MEM_HEAD_DIM:(h + 1) * MEM_HEAD_DIM].astype(BF16)) * (MEM_HEAD_DIM ** -0.5)
            for h in range(MEM_HEADS)]


def _mem_values(mv_ref):
    return [mv_ref[:, h * MEM_HEAD_DIM:(h + 1) * MEM_HEAD_DIM].astype(BF16) for h in range(MEM_HEADS)]


def _kv_heads(x):
    return [x[:, g * LANES:(g + 1) * LANES] for g in range(N_KV_HEADS)]


def _attn_prompt_body(q_ref, k_ref, kh_ref, v_ref, vh_ref, bias_ref, sink_ref, qm_ref, mk_ref, mv_ref,
                      o_ref, mo_ref):
    i = pl.program_id(1)
    tq = q_ref.shape[1]
    kf = jnp.concatenate([kh_ref[0], k_ref[0]], axis=0)
    vf = jnp.concatenate([vh_ref[0], v_ref[0]], axis=0)
    band = WINDOW + CHUNK
    scores, values, sinks = [], [], []
    for c in range(tq // CHUNK):
        lo = c * CHUNK
        valid = None
        if lo < WINDOW:
            key_pos = lax.broadcasted_iota(jnp.int32, (1, band), 1) + (i * tq + lo - WINDOW)
            valid = key_pos >= 0
        kb, vb = _kv_heads(kf[lo:lo + band]), _kv_heads(vf[lo:lo + band])
        for g, qs in enumerate(_gqa_queries(q_ref[0, lo:lo + CHUNK, :])):
            s = _qk(qs, kb[g]) + bias_ref[g]
            scores.append(s if valid is None else jnp.where(valid, s, NEG_INF))
            values.append(vb[g])
            sinks.append(sink_ref[g])
    outs = _softmax_pv(scores, values, sinks)
    for c in range(tq // CHUNK):
        o = _gqa_outputs(outs[c * N_KV_HEADS:(c + 1) * N_KV_HEADS], CHUNK)
        o_ref[0, c * CHUNK:(c + 1) * CHUNK, :] = o.astype(BF16)
    mem = _softmax_pv(_mem_scores(qm_ref[0], mk_ref.at[0]), _mem_values(mv_ref.at[0]))
    mo_ref[0] = jnp.concatenate(mem, axis=1).astype(BF16)


def _attn_prompt(q, k2, v2, bias, sink, qm, mk, mv):
    b, l, _ = q.shape
    tq = ROW_TILE
    hb = tq // WINDOW
    cur = lambda w: pl.BlockSpec((1, tq, w), lambda bi, i: (bi, i, 0))
    halo = pl.BlockSpec((1, WINDOW, KV2_W), lambda bi, i: (bi, jnp.maximum(i * hb - 1, 0), 0))
    mem = pl.BlockSpec((1, MEM_LEN, MEM_W), lambda bi, i: (bi, 0, 0))
    return pl.pallas_call(
        _attn_prompt_body,
        grid=(b, l // tq),
        in_specs=[cur(ATTN_W), cur(KV2_W), halo, cur(KV2_W), halo, _resident(bias.shape),
                  _resident(sink.shape), cur(MEM_W), mem, mem],
        out_specs=[cur(ATTN_W), cur(MEM_W)],
        out_shape=[jax.ShapeDtypeStruct((b, l, ATTN_W), BF16), jax.ShapeDtypeStruct((b, l, MEM_W), BF16)],
        compiler_params=_params("parallel", "parallel"),
        name="attn_prompt",
    )(q, k2, k2, v2, v2, bias, sink, qm, mk, mv)


def _attn_sample_body(q_ref, k_ref, v_ref, bias_ref, sink_ref, qm_ref, mk_ref, mv_ref, o_ref, mo_ref):
    nb, nq = q_ref.shape[0], q_ref.shape[1]
    scores, values, sinks, mem_scores, mem_values = [], [], [], [], []
    for b in range(nb):
        kb, vb = _kv_heads(k_ref[b]), _kv_heads(v_ref[b])
        for g, qs in enumerate(_gqa_queries(q_ref[b])):
            scores.append(_qk(qs, kb[g]) + bias_ref[g])
            values.append(vb[g])
            sinks.append(sink_ref[g])
        mem_scores += _mem_scores(qm_ref[b], mk_ref.at[b])
        mem_values += _mem_values(mv_ref.at[b])
    outs = _softmax_pv(scores, values, sinks)
    mem = _softmax_pv(mem_scores, mem_values)
    for b in range(nb):
        o_ref[b] = _gqa_outputs(outs[b * N_KV_HEADS:(b + 1) * N_KV_HEADS], nq).astype(BF16)
        mo_ref[b] = jnp.concatenate(mem[b * MEM_HEADS:(b + 1) * MEM_HEADS], axis=1).astype(BF16)


def _attn_sample(q, kk2, vv2, bias, sink, qm, mk, mv):
    b, s, _ = q.shape
    nb = 8
    nk = kk2.shape[1]
    blk = lambda *shape: pl.BlockSpec((nb,) + shape, lambda i: (i,) + (0,) * len(shape))
    return pl.pallas_call(
        _attn_sample_body,
        grid=(b // nb,),
        in_specs=[blk(s, ATTN_W), blk(nk, KV2_W), blk(nk, KV2_W),
                  _resident(bias.shape), _resident(sink.shape),
                  blk(s, MEM_W), blk(MEM_LEN, MEM_W), blk(MEM_LEN, MEM_W)],
        out_specs=[blk(s, ATTN_W), blk(s, MEM_W)],
        out_shape=[jax.ShapeDtypeStruct((b, s, ATTN_W), BF16), jax.ShapeDtypeStruct((b, s, MEM_W), BF16)],
        compiler_params=_params("parallel"),
        name="attn_sample",
    )(q, kk2, vv2, bias, sink, qm, mk, mv)


def _ssm_weights_body(lam_re_row, lam_im_row, lam_re_col, lam_im_col, log_dt, b_re_t, b_im_t,
                      b_re_rows, b_im_rows, c_re_lanes, c_im_lanes,
                      krow_ref, wstate_ref, wout_ref, a_tab_ref):
    dt = jnp.exp(log_dt[0])

    def zoh_coef(lr, li):
        mag = jnp.exp(lr * dt)
        a_re, a_im = mag * jnp.cos(li * dt), mag * jnp.sin(li * dt)
        den = lr * lr + li * li
        return ((a_re - 1.0) * lr + a_im * li) / den, (a_im * lr - (a_re - 1.0) * li) / den

    def a_power(lr, li, n):
        mag = jnp.exp(lr * dt * n)
        return mag * jnp.cos(li * dt * n), mag * jnp.sin(li * dt * n)

    lr, li = lam_re_row[0], lam_im_row[0]
    cr, ci = zoh_coef(lr, li)
    t_row = (lax.broadcasted_iota(jnp.int32, (SSM_K, 1), 0) // SSM_GROUP).astype(F32)
    pr, pi = a_power(lr, li, (SSM_T - 1) - t_row)
    zr, zi = pr * cr - pi * ci, pr * ci + pi * cr
    br, bi = b_re_rows[0], b_im_rows[0]
    wstate_ref[0] = jnp.concatenate([zr * br - zi * bi, zr * bi + zi * br], axis=1)

    lrc, lic = lam_re_col[0], lam_im_col[0]
    t_lane = (lax.broadcasted_iota(jnp.int32, (1, SSM_K), 1) // SSM_GROUP).astype(F32)
    qr, qi = a_power(lrc, lic, t_lane + 1.0)
    ccr, cci = c_re_lanes[0], c_im_lanes[0]
    wout_ref[0] = jnp.concatenate([ccr * qr - cci * qi, -(ccr * qi + cci * qr)], axis=0)

    gr, gi = a_power(lrc, lic, t_lane)
    g_re, g_im = ccr * gr - cci * gi, ccr * gi + cci * gr
    btr, bti = b_re_t[0], b_im_t[0]
    bbr, bbi = cr * btr - ci * bti, cr * bti + ci * btr
    hi = lax.Precision.HIGHEST
    krow_ref[0] = (jnp.dot(bbr, g_re, precision=hi, preferred_element_type=F32)
                   - jnp.dot(bbi, g_im, precision=hi, preferred_element_type=F32))

    idx = lax.broadcasted_iota(jnp.int32, (N_SCAN_TABLES * SCAN_ROWS, SSM_STATE), 0)
    tab, r = idx // SCAN_ROWS, idx % SCAN_ROWS
    stride = jnp.where(tab < 2, 1, jnp.where(tab < 4, 2, 4))
    n = jnp.where(tab < 6, stride, jnp.where(tab < 8, r, SCAN_ROWS))
    keep = jnp.logical_or(tab >= 6, r >= stride)
    er, ei = a_power(lr, li, (n * SSM_T).astype(F32))
    er, ei = jnp.where(keep, er, 0.0), jnp.where(keep, ei, 0.0)
    odd = tab % 2 == 1
    a_tab_ref[0] = jnp.concatenate([jnp.where(odd, -ei, er), jnp.where(odd, ei, er)], axis=1)


def _ssm_weights(lam_re, lam_im, log_dt, b_re, b_im, c_re, c_im):
    g, p, c = b_re.shape
    row3 = lambda x: x.reshape(g, 1, p)
    col3 = lambda x: x.reshape(g, p, 1)
    b_t = lambda x: jnp.transpose(x, (0, 2, 1))
    b_rows = lambda x: jnp.tile(b_t(x), (1, SSM_T, 1))
    c_lanes = lambda x: jnp.tile(jnp.transpose(x, (0, 2, 1)), (1, 1, SSM_T))
    args = (row3(lam_re), row3(lam_im), col3(lam_re), col3(lam_im), log_dt.reshape(g, 1, 1),
            b_t(b_re), b_t(b_im), b_rows(b_re), b_rows(b_im), c_lanes(c_re), c_lanes(c_im))
    spec = lambda x: pl.BlockSpec((1,) + x.shape[1:], lambda i: (i, 0, 0))
    out_shapes = [(g, SSM_GROUP, SSM_K), (g, SSM_K, SSM_S2), (g, SSM_S2, SSM_K),
                  (g, N_SCAN_TABLES * SCAN_ROWS, SSM_S2)]
    return pl.pallas_call(
        _ssm_weights_body,
        grid=(g,),
        in_specs=[spec(a) for a in args],
        out_specs=[pl.BlockSpec((1,) + s[1:], lambda i: (i, 0, 0)) for s in out_shapes],
        out_shape=[jax.ShapeDtypeStruct(s, F32) for s in out_shapes],
        compiler_params=_params("parallel"),
        name="ssm_weights",
    )(*args)


def _ssm_block_layout(krow, wstate, wout):
    eye = jnp.eye(SSM_GB, dtype=F32)
    k4 = krow.reshape(SSM_GROUPS, SSM_GROUP, SSM_T, SSM_GROUP)
    k4 = jnp.concatenate([jnp.zeros_like(k4[:, :, :1]), k4], axis=2)
    d = np.arange(SSM_NT)[::-1, None, None]
    lag = SSM_TPT * d + np.arange(SSM_TPT)[None, None, :] - np.arange(SSM_TPT)[None, :, None] + 1
    tiles = k4[:, :, lag, :]
    tiles = tiles.reshape(SSM_NGB, SSM_GB, SSM_GROUP, SSM_NT, SSM_TPT, SSM_TPT, SSM_GROUP)
    wconv = jnp.einsum('xgcdioe,gh->xdigcohe', tiles, eye).reshape(SSM_NGB, SSM_NT * MXU_TILE, MXU_TILE)
    ws = wstate.reshape(SSM_NGB, SSM_GB, SSM_T, SSM_GROUP, SSM_S2)
    wst = jnp.einsum('xgtcs,gh->xtgchs', ws, eye).reshape(SSM_NGB, SSM_XW, SSM_SW)
    wo = wout.reshape(SSM_NGB, SSM_GB, SSM_S2, SSM_T, SSM_GROUP)
    wo = jnp.einsum('xgstc,gh->xgsthc', wo, eye).reshape(SSM_NGB, SSM_SW, SSM_XW)
    return wconv.astype(BF16), wst.astype(BF16), wo.astype(BF16)


def _swap_halves(s):
    ax = s.ndim - 1
    return jnp.concatenate([pltpu.roll(s[..., l:l + SSM_S2], SSM_STATE, ax)
                            for l in range(0, s.shape[ax], SSM_S2)], axis=ax)


def _ssm_body(row_sets, slab_rows, u_ref, wconv_ref, wst_ref, wo_ref, d_ref, a_tab_ref, s0_ref,
              y_ref, s_ref, v_scr, s_scr):
    def piece(ref_set, first, n, stride, t):
        return ref_set, pl.ds(first + t, n, stride=stride)

    @pl.when(pl.program_id(1) == 0)
    def _():
        s_scr[...] = s0_ref[0]

    x = jnp.concatenate(
        [jnp.concatenate([u_ref[piece(*rs, t)] for t in range(SSM_T)], axis=1) for rs in row_sets],
        axis=0).astype(BF16)
    v_scr[...] = _mm(x, wst_ref[0])

    n_rows = v_scr.shape[0]
    if slab_rows:
        a_mul, a_swap = a_tab_ref[6, 1:2, :], a_tab_ref[7, 1:2, :]
        s = s_scr[...]
        for c in range(n_rows // slab_rows):
            rows = slice(c * slab_rows, (c + 1) * slab_rows)
            inc = v_scr[rows, :]
            v_scr[rows, :] = s
            s = a_mul * s + a_swap * _swap_halves(s) + inc
        s_scr[...] = s
    else:
        n_blocks = n_rows // SCAN_ROWS
        xs = v_scr[...].reshape(n_blocks, SCAN_ROWS, SSM_SW)
        for k in range(3):
            sh = pltpu.roll(xs, 1 << k, 1)
            xs = xs + a_tab_ref[2 * k] * sh + a_tab_ref[2 * k + 1] * _swap_halves(sh)
        carries, blk = [], 0
        for j, (_, _, n, _) in enumerate(row_sets):
            carry = s_scr[j * SCAN_ROWS:(j + 1) * SCAN_ROWS, :]
            for _ in range(n // SCAN_ROWS):
                carries.append(carry)
                last = jnp.broadcast_to(xs[blk, SCAN_ROWS - 1:, :], carry.shape)
                carry = last + a_tab_ref[8] * carry + a_tab_ref[9] * _swap_halves(carry)
                blk += 1
            s_scr[j * SCAN_ROWS:(j + 1) * SCAN_ROWS, :] = carry
        carries = jnp.stack(carries, axis=0)
        not_first = lax.broadcasted_iota(jnp.int32, (1, SCAN_ROWS, 1), 1) >= 1
        enter = (jnp.where(not_first, pltpu.roll(xs, 1, 1), 0.0)
                 + a_tab_ref[6] * carries + a_tab_ref[7] * _swap_halves(carries))
        v_scr[...] = enter.reshape(n_rows, SSM_SW)
    s_ref[0] = s_scr[...]

    enter = v_scr[...].astype(BF16)
    d = d_ref[...]
    for jo in range(SSM_NT):
        k_hi = (jo + 1) * MXU_TILE
        yt = _mm(x[:, :k_hi], wconv_ref[0, (SSM_NT - 1 - jo) * MXU_TILE:, :])
        yt = yt + _mm(enter, wo_ref[0, :, jo * MXU_TILE:(jo + 1) * MXU_TILE])
        r0 = 0
        for rs in row_sets:
            for tl in range(SSM_TPT):
                idx = piece(*rs, jo * SSM_TPT + tl)
                y_ref[idx] = yt[r0:r0 + rs[2], tl * LANES:(tl + 1) * LANES] + u_ref[idx] * d
            r0 += rs[2]


def _ssm(u, row_sets, slab_rows, block_rows, wconv, wst, wo, d_lanes, a_tab, s0):
    ns, r, _ = u.shape
    chunk_rows = sum(rs[2] for rs in row_sets)
    carry_rows = s0.shape[1]
    blk = pl.BlockSpec((ns, block_rows, LANES), lambda gb, i: (0, i, gb))
    per_gb = lambda x: pl.BlockSpec((1,) + x.shape[1:], lambda gb, i: (gb, 0, 0))
    return pl.pallas_call(
        functools.partial(_ssm_body, row_sets, slab_rows),
        grid=(SSM_NGB, r // block_rows),
        in_specs=[blk, per_gb(wconv), per_gb(wst), per_gb(wo),
                  pl.BlockSpec((1, LANES), lambda gb, i: (0, gb)),
                  pl.BlockSpec((N_SCAN_TABLES, SCAN_ROWS, SSM_SW), lambda gb, i: (0, 0, gb)),
                  per_gb(s0)],
        out_specs=[blk, per_gb(s0)],
        out_shape=[jax.ShapeDtypeStruct(u.shape, F32), jax.ShapeDtypeStruct(s0.shape, F32)],
        scratch_shapes=[pltpu.VMEM((chunk_rows, SSM_SW), F32), pltpu.VMEM((carry_rows, SSM_SW), F32)],
        compiler_params=_params("parallel", "arbitrary"),
        name="ssm_scan",
    )(u, wconv, wst, wo, d_lanes, a_tab, s0)


def _ssm_branch(u, s0, ssm_w):
    b, l, _ = u.shape
    nc = l // SSM_T
    by_gb = lambda s: s.reshape(s.shape[0], SSM_NGB, SSM_SW).transpose(1, 0, 2)
    if b % SCAN_ROWS == 0:
        row_sets = tuple((0, c * SSM_T, b, l) for c in range(nc))
        y, s_last = _ssm(u.reshape(1, b * l, SSM_W), row_sets, b, b * l, *ssm_w, by_gb(s0))
    else:
        cps = min(nc, SSM_CHUNKS_PER_STEP)
        row_sets = tuple((j, 0, cps, SSM_T) for j in range(b))
        y, s_last = _ssm(u, row_sets, 0, cps * SSM_T, *ssm_w, by_gb(jnp.repeat(s0, SCAN_ROWS, axis=0)))
        s_last = s_last[:, ::SCAN_ROWS]
    s_last = s_last.transpose(1, 0, 2).reshape(b, SSM_GROUPS, 2, SSM_STATE)
    return y.reshape(b, l, SSM_W), s_last[:, :, 0], s_last[:, :, 1]


def _merge_ffn2_body(x_ref, a_ref, m_ref, y_ref, gmix_ref, wg_ref, wab_ref, wglu_ref, wmb_ref, wo_ref,
                     gpost_ref, g2pre_ref, g2post_ref, wfi_ref, wfo_ref, o_ref):
    x1 = x_ref[...]
    gates = jax.nn.sigmoid(_mm(_rms(x1, gmix_ref[...]).astype(BF16), wg_ref[...]))
    yy = _mm(y_ref[...].astype(BF16), wglu_ref[...])
    ssm_out = yy[:, :D_MODEL] * jax.nn.sigmoid(yy[:, D_MODEL:])
    merged = (gates[:, :D_MODEL] * _mm(a_ref[...], wab_ref[...])
              + gates[:, D_MODEL:2 * D_MODEL] * ssm_out
              + gates[:, 2 * D_MODEL:] * _mm(m_ref[...], wmb_ref[...]))
    x2 = x1 + _rms(_mm(merged.astype(BF16), wo_ref[...]), gpost_ref[...])
    o_ref[...] = _ffn(x2, g2pre_ref[...], g2post_ref[...], wfi_ref, wfo_ref)


def _merge_ffn2(x1, attn, memo, yssm, gmix, wg, wab, wglu, wmb, wo, gpost, g2pre, g2post, wfi, wfo):
    n = x1.shape[0]
    row = lambda w: pl.BlockSpec((ROW_TILE, w), lambda i: (i, 0))
    vec = _resident((1, D_MODEL))
    return pl.pallas_call(
        _merge_ffn2_body,
        grid=(n // ROW_TILE,),
        in_specs=[row(D_MODEL), row(ATTN_W), row(MEM_W), row(SSM_W), vec, _resident(wg.shape),
                  _resident(wab.shape), _resident(wglu.shape), _resident(wmb.shape), _resident(wo.shape),
                  vec, vec, vec, _resident(wfi.shape), _resident(wfo.shape)],
        out_specs=row(D_MODEL),
        out_shape=jax.ShapeDtypeStruct((n, D_MODEL), F32),
        compiler_params=_params("parallel"),
        name="merge_ffn2",
    )(x1, attn, memo, yssm, gmix, wg, wab, wglu, wmb, wo, gpost, g2pre, g2post, wfi, wfo)


def _t5_bucket(rel):
    half = N_BUCKETS // 2
    max_exact = half // 2
    ret = (rel > 0).astype(np.int32) * half
    n = np.abs(rel)
    large = max_exact + (np.log(np.maximum(n, 1) / max_exact) / math.log(MAX_DISTANCE / max_exact)
                         * (half - max_exact)).astype(np.int32)
    large = np.minimum(large, half - 1)
    return ret + np.where(n < max_exact, n, large)


def _band_bias(rel_table, n_q, n_back, n_k):
    i = np.arange(n_q)[:, None]
    j = np.arange(n_k)[None, :]
    b = rel_table[_t5_bucket((j - n_back) - i)].astype(F32)
    return jnp.transpose(b, (2, 0, 1)).reshape(N_KV_HEADS, KV_REP * n_q, n_k)


def _sink_rows(sink, n_q):
    return jnp.repeat(sink.astype(F32).reshape(N_KV_HEADS, KV_REP), n_q, axis=1)[:, :, None]


def _twice_per_head(x, axis):
    shape = x.shape
    x = x.reshape(shape[:axis] + (N_KV_HEADS, 1, HEAD_DIM) + shape[axis + 1:])
    x = jnp.concatenate([x, x], axis=axis + 1)
    return x.reshape(shape[:axis] + (KV2_W,) + shape[axis + 1:])


def kernel(x_prompt, x_sample, cache_swa_k, cache_swa_v, cache_mem_k, cache_mem_v, state_ssm_re, state_ssm_im, mem_prompt, rel_bias_table, ff1_pre_g, ff1_post_g, w_ff1_in, w_ff1_out, mix_pre_g, mix_post_g, w_in, mem_norm_g, w_mem_kv, attn_sink, ssm_lambda_re, ssm_lambda_im, ssm_log_dt, ssm_b_re, ssm_b_im, ssm_c_re, ssm_c_im, ssm_d, w_ssm_glu, w_attn_br, w_mem_br, w_out, ff2_pre_g, ff2_post_g, w_ff2_in, w_ff2_out):
    assert ff1_pre_g.shape[0] == 1, "single-layer step"
    bp, lp, _ = x_prompt.shape
    bs, ls, _ = x_sample.shape
    vec = lambda g: g[0].reshape(1, D_MODEL).astype(F32)
    w16 = lambda w: w[0].astype(BF16)

    wfi1, wfo1, wfi2, wfo2 = w16(w_ff1_in), w16(w_ff1_out), w16(w_ff2_in), w16(w_ff2_out)
    w_in16 = w16(w_in)
    c_k, c_v, c_u = ATTN_W, ATTN_W + KV_W, ATTN_W + 2 * KV_W
    c_g = c_u + SSM_W + MEM_W
    q_scale = HEAD_DIM ** -0.5
    assert math.frexp(q_scale)[0] == 0.5, "power-of-two scale: folding it into the q columns is exact"
    wp = jnp.concatenate([w_in16[:, :c_k] * jnp.asarray(q_scale, BF16), w_in16[:, c_k:c_g],
                          _twice_per_head(w_in16[:, c_k:c_v], 1), _twice_per_head(w_in16[:, c_v:c_u], 1)], axis=1)
    wg = w_in16[:, c_g:]
    wab, wglu, wmb, wo = w16(w_attn_br), w16(w_ssm_glu), w16(w_mem_br), w16(w_out)

    krow, wstate, wout_s, a_tab = _ssm_weights(
        ssm_lambda_re[0], ssm_lambda_im[0], ssm_log_dt[0], ssm_b_re[0], ssm_b_im[0], ssm_c_re[0], ssm_c_im[0])
    a_tab = a_tab.reshape(SSM_GROUPS, N_SCAN_TABLES, SCAN_ROWS, SSM_S2).transpose(1, 2, 0, 3)
    ssm_w = _ssm_block_layout(krow, wstate, wout_s) + (
        ssm_d[0].astype(F32).reshape(1, SSM_W), a_tab.reshape(N_SCAN_TABLES, SCAN_ROWS, SSM_WIDTH))

    mk_p, mv_p = _mem_kv(mem_prompt.reshape(bp * MEM_LEN, D_MODEL), vec(mem_norm_g), w16(w_mem_kv))

    def group(x, attend, s0, mem_k, mem_v):
        b, l, _ = x.shape
        r3 = lambda t: t.reshape(b, l, t.shape[-1])
        x1, q, k, v, u, qm, k2, v2 = _ffn1_proj(x.reshape(b * l, D_MODEL), vec(ff1_pre_g), vec(ff1_post_g),
                                                wfi1, wfo1, vec(mix_pre_g), wp)
        attn, memo = attend(r3(q), r3(k2), r3(v2), r3(qm), mem_k, mem_v)
        y_ssm, s_re, s_im = _ssm_branch(r3(u), s0, ssm_w)
        y = _merge_ffn2(x1, attn.reshape(b * l, ATTN_W), memo.reshape(b * l, MEM_W),
                        y_ssm.reshape(b * l, SSM_W), vec(mix_pre_g), wg, wab, wglu, wmb, wo,
                        vec(mix_post_g), vec(ff2_pre_g), vec(ff2_post_g), wfi2, wfo2)
        return r3(y), r3(k), r3(v), s_re, s_im

    def attend_prompt(q, k2, v2, qm, mem_k, mem_v):
        bias = _band_bias(rel_bias_table, CHUNK, WINDOW, WINDOW + CHUNK)
        return _attn_prompt(q, k2, v2, bias, _sink_rows(attn_sink[0], CHUNK), qm, mem_k, mem_v)

    def attend_sample(q, k2, v2, qm, mem_k, mem_v):
        n_back = cache_swa_k.shape[2]
        cache2 = lambda c: _twice_per_head(c[0].reshape(bs, n_back, KV_W).astype(BF16), 2)
        kk = jnp.concatenate([cache2(cache_swa_k), k2], axis=1)
        vv = jnp.concatenate([cache2(cache_swa_v), v2], axis=1)
        bias = _band_bias(rel_bias_table, ls, n_back, n_back + ls)
        return _attn_sample(q, kk, vv, bias, _sink_rows(attn_sink[0], ls), qm, mem_k, mem_v)

    yp, pk, pv, pre, pim = group(x_prompt, attend_prompt, jnp.zeros((bp, SSM_WIDTH), F32),
                                 mk_p.reshape(bp, MEM_LEN, MEM_W), mv_p.reshape(bp, MEM_LEN, MEM_W))
    s0 = jnp.stack([state_ssm_re[0], state_ssm_im[0]], axis=2).reshape(bs, SSM_WIDTH).astype(F32)
    ys, sk, sv, sre, sim = group(x_sample, attend_sample, s0,
                                 cache_mem_k[0].reshape(bs, MEM_LEN, MEM_W),
                                 cache_mem_v[0].reshape(bs, MEM_LEN, MEM_W))

    n_keep = min(WINDOW, lp)
    heads = lambda t: t.reshape(t.shape[0], t.shape[1], N_KV_HEADS, HEAD_DIM)[None]
    mem_heads = lambda t: t.reshape(bp, MEM_LEN, MEM_HEADS, MEM_HEAD_DIM)[None]
    return (yp, ys, heads(pk[:, -n_keep:]), heads(pv[:, -n_keep:]), mem_heads(mk_p), mem_heads(mv_p),
            pre[None], pim[None], heads(sk), heads(sv), sre[None], sim[None])
```

```python
import functools
import math

import numpy as np
import jax
import jax.numpy as jnp
from jax import lax
from jax.experimental import pallas as pl
from jax.experimental.pallas import tpu as pltpu

D_MODEL = 1024
CHUNK = 64
WINDOW = 128
HEAD_DIM = 64
MIX_W = D_MODEL // 2
N_HEADS = MIX_W // HEAD_DIM
N_KV_HEADS = N_HEADS // 4
KV_REP = N_HEADS // N_KV_HEADS
ATTN_W = N_HEADS * HEAD_DIM
KV_W = N_KV_HEADS * HEAD_DIM
SSM_GROUP = 16
SSM_W = MIX_W
SSM_GROUPS = SSM_W // SSM_GROUP
SSM_STATE = 64
MEM_LEN = 256
MEM_HEADS = 4
MEM_HEAD_DIM = MIX_W // MEM_HEADS
MEM_W = MEM_HEADS * MEM_HEAD_DIM
D_FF = 128 * ((8 * D_MODEL // 3 + 127) // 128)
N_BUCKETS = 32
MAX_DISTANCE = 128
RMS_EPS = 1e-6
NEG_INF = -1e30

LANES = 128
MXU_TILE = 256
ROW_TILE = 256
FFN_ROW_TILE = 512
FFN_CHUNK = 2 * MXU_TILE
VMEM_LIMIT = 60 * 1024 * 1024

KV2_W = N_KV_HEADS * LANES

SSM_T = 16
SSM_K = SSM_T * SSM_GROUP
SSM_S2 = 2 * SSM_STATE
SSM_WIDTH = SSM_GROUPS * SSM_S2
SSM_GB = LANES // SSM_GROUP
SSM_NGB = SSM_GROUPS // SSM_GB
SSM_XW = SSM_T * LANES
SSM_SW = SSM_GB * SSM_S2
SSM_TPT = MXU_TILE // LANES
SSM_NT = SSM_T // SSM_TPT
SCAN_ROWS = 8
N_SCAN_TABLES = 10
SSM_CHUNKS_PER_STEP = 128

F32 = jnp.float32
BF16 = jnp.bfloat16


def _params(*sem):
    return pltpu.CompilerParams(dimension_semantics=sem, vmem_limit_bytes=VMEM_LIMIT)


def _resident(shape):
    zeros = (0,) * len(shape)
    return pl.BlockSpec(shape, lambda *_: zeros, pipeline_mode=pl.Buffered(1))


def _rms(x, g):
    return x * lax.rsqrt(jnp.mean(x * x, axis=-1, keepdims=True) + RMS_EPS) * g


def _mm(a, b):
    return jnp.dot(a, b, preferred_element_type=F32)


def _ffn(x, gpre, gpost, w_in_ref, w_out_ref):
    h = _rms(x, gpre).astype(BF16)
    acts = []
    for c0 in range(0, D_FF, FFN_CHUNK):
        c1 = min(c0 + FFN_CHUNK, D_FF)
        g = _mm(h, w_in_ref[:, c0:c1])
        u = _mm(h, w_in_ref[:, D_FF + c0:D_FF + c1])
        acts.append((g * jax.nn.sigmoid(g) * u).astype(BF16))
    return x + 0.5 * _rms(_mm(jnp.concatenate(acts, axis=1), w_out_ref[...]), gpost)


PROJ_SPLIT = (("q", ATTN_W, BF16), ("k", KV_W, F32), ("v", KV_W, F32), ("u", SSM_W, F32),
              ("qm", MEM_W, BF16), ("k2", KV2_W, BF16), ("v2", KV2_W, BF16))


def _ffn1_proj_body(x_ref, gpre_ref, gpost_ref, wfi_ref, wfo_ref, gmix_ref, wp_ref, x1_ref, *out_refs):
    x1 = _ffn(x_ref[...], gpre_ref[...], gpost_ref[...], wfi_ref, wfo_ref)
    x1_ref[...] = x1
    p = _mm(_rms(x1, gmix_ref[...]).astype(BF16), wp_ref[...])
    col = 0
    for ref, (_, width, dtype) in zip(out_refs, PROJ_SPLIT):
        ref[...] = p[:, col:col + width].astype(dtype)
        col += width


def _ffn1_proj(x, gpre, gpost, wfi, wfo, gmix, wp):
    n = x.shape[0]
    row = lambda w: pl.BlockSpec((FFN_ROW_TILE, w), lambda i: (i, 0))
    return pl.pallas_call(
        _ffn1_proj_body,
        grid=(n // FFN_ROW_TILE,),
        in_specs=[row(D_MODEL), _resident((1, D_MODEL)), _resident((1, D_MODEL)),
                  _resident(wfi.shape), _resident(wfo.shape), _resident((1, D_MODEL)),
                  _resident(wp.shape)],
        out_specs=[row(D_MODEL)] + [row(w) for _, w, _ in PROJ_SPLIT],
        out_shape=[jax.ShapeDtypeStruct((n, D_MODEL), F32)]
        + [jax.ShapeDtypeStruct((n, w), d) for _, w, d in PROJ_SPLIT],
        compiler_params=_params("parallel"),
        name="ffn1_proj",
    )(x, gpre, gpost, wfi, wfo, gmix, wp)


def _mem_kv_body(m_ref, g_ref, w_ref, k_ref, v_ref):
    kv = _mm(_rms(m_ref[...], g_ref[...]).astype(BF16), w_ref[...])
    k_ref[...] = kv[:, :MEM_W]
    v_ref[...] = kv[:, MEM_W:]


def _mem_kv(mem, g, w):
    n = mem.shape[0]
    row = lambda wd: pl.BlockSpec((ROW_TILE, wd), lambda i: (i, 0))
    return pl.pallas_call(
        _mem_kv_body,
        grid=(n // ROW_TILE,),
        in_specs=[row(D_MODEL), _resident((1, D_MODEL)), _resident(w.shape)],
        out_specs=[row(MEM_W), row(MEM_W)],
        out_shape=[jax.ShapeDtypeStruct((n, MEM_W), F32)] * 2,
        compiler_params=_params("parallel"),
        name="mem_kv",
    )(mem, g, w)


def _qk(q, k):
    return lax.dot_general(q, k, (((1,), (1,)), ((), ())), preferred_element_type=F32)


def _softmax_pv(scores, values, sinks=None):
    s = jnp.concatenate(scores, axis=0)
    m = jnp.max(s, axis=-1, keepdims=True)
    if sinks is not None:
        sink = jnp.concatenate(sinks, axis=0)
        m = jnp.maximum(m, sink)
    e = jnp.exp(s - m)
    den = jnp.sum(e, axis=-1, keepdims=True)
    if sinks is not None:
        den = den + jnp.exp(sink - m)
    p = (e / den).astype(BF16)
    outs, r0 = [], 0
    for sc, v in zip(scores, values):
        outs.append(_mm(p[r0:r0 + sc.shape[0]], v))
        r0 += sc.shape[0]
    return outs


def _gqa_queries(q):
    nq = q.shape[0]
    low = lax.broadcasted_iota(jnp.int32, (nq, LANES), 1) < HEAD_DIM
    zero = jnp.zeros((nq, LANES), BF16)
    stacks = []
    for g in range(N_KV_HEADS):
        rows = []
        for r in range(KV_REP):
            h = g * KV_REP + r
            q2 = q[:, (h // 2) * LANES:(h // 2 + 1) * LANES]
            rows.append(jnp.where(low, q2, zero) if h % 2 == 0 else jnp.where(low, zero, q2))
        stacks.append(jnp.concatenate(rows, axis=0))
    return stacks


def _gqa_outputs(outs, nq):
    low = lax.broadcasted_iota(jnp.int32, (nq, LANES), 1) < HEAD_DIM
    pairs = []
    for o in outs:
        for r in range(0, KV_REP, 2):
            pairs.append(jnp.where(low, o[r * nq:(r + 1) * nq], o[(r + 1) * nq:(r + 2) * nq]))
    return jnp.concatenate(pairs, axis=1)


def _mem_scores(qm, mk_ref):
    return [_qk(qm[:, h * MEM_HEAD_DIM:(h + 1) * MEM_HEAD_DIM],
                mk_ref[:, h * MEM_HEAD_DIM:(h + 1) * MEM_HEAD_DIM].astype(BF16)) * (MEM_HEAD_DIM ** -0.5)
            for h in range(MEM_HEADS)]


def _mem_values(mv_ref):
    return [mv_ref[:, h * MEM_HEAD_DIM:(h + 1) * MEM_HEAD_DIM].astype(BF16) for h in range(MEM_HEADS)]


def _kv_heads(x):
    return [x[:, g * LANES:(g + 1) * LANES] for g in range(N_KV_HEADS)]


def _attn_prompt_body(q_ref, k_ref, kh_ref, v_ref, vh_ref, bias_ref, sink_ref, qm_ref, mk_ref, mv_ref,
                      o_ref, mo_ref):
    i = pl.program_id(1)
    tq = q_ref.shape[1]
    kf = jnp.concatenate([kh_ref[0], k_ref[0]], axis=0)
    vf = jnp.concatenate([vh_ref[0], v_ref[0]], axis=0)
    band = WINDOW + CHUNK
    scores, values, sinks = [], [], []
    for c in range(tq // CHUNK):
        lo = c * CHUNK
        valid = None
        if lo < WINDOW:
            key_pos = lax.broadcasted_iota(jnp.int32, (1, band), 1) + (i * tq + lo - WINDOW)
            valid = key_pos >= 0
        kb, vb = _kv_heads(kf[lo:lo + band]), _kv_heads(vf[lo:lo + band])
        for g, qs in enumerate(_gqa_queries(q_ref[0, lo:lo + CHUNK, :])):
            s = _qk(qs, kb[g]) + bias_ref[g]
            scores.append(s if valid is None else jnp.where(valid, s, NEG_INF))
            values.append(vb[g])
            sinks.append(sink_ref[g])
    outs = _softmax_pv(scores, values, sinks)
    for c in range(tq // CHUNK):
        o = _gqa_outputs(outs[c * N_KV_HEADS:(c + 1) * N_KV_HEADS], CHUNK)
        o_ref[0, c * CHUNK:(c + 1) * CHUNK, :] = o.astype(BF16)
    mem = _softmax_pv(_mem_scores(qm_ref[0], mk_ref.at[0]), _mem_values(mv_ref.at[0]))
    mo_ref[0] = jnp.concatenate(mem, axis=1).astype(BF16)


def _attn_prompt(q, k2, v2, bias, sink, qm, mk, mv):
    b, l, _ = q.shape
    tq = ROW_TILE
    hb = tq // WINDOW
    cur = lambda w: pl.BlockSpec((1, tq, w), lambda bi, i: (bi, i, 0))
    halo = pl.BlockSpec((1, WINDOW, KV2_W), lambda bi, i: (bi, jnp.maximum(i * hb - 1, 0), 0))
    mem = pl.BlockSpec((1, MEM_LEN, MEM_W), lambda bi, i: (bi, 0, 0))
    return pl.pallas_call(
        _attn_prompt_body,
        grid=(b, l // tq),
        in_specs=[cur(ATTN_W), cur(KV2_W), halo, cur(KV2_W), halo, _resident(bias.shape),
                  _resident(sink.shape), cur(MEM_W), mem, mem],
        out_specs=[cur(ATTN_W), cur(MEM_W)],
        out_shape=[jax.ShapeDtypeStruct((b, l, ATTN_W), BF16), jax.ShapeDtypeStruct((b, l, MEM_W), BF16)],
        compiler_params=_params("parallel", "parallel"),
        name="attn_prompt",
    )(q, k2, k2, v2, v2, bias, sink, qm, mk, mv)


def _attn_sample_body(q_ref, k_ref, v_ref, bias_ref, sink_ref, qm_ref, mk_ref, mv_ref, o_ref, mo_ref):
    nb, nq = q_ref.shape[0], q_ref.shape[1]
    scores, values, sinks, mem_scores, mem_values = [], [], [], [], []
    for b in range(nb):
        kb, vb = _kv_heads(k_ref[b]), _kv_heads(v_ref[b])
        for g, qs in enumerate(_gqa_queries(q_ref[b])):
            scores.append(_qk(qs, kb[g]) + bias_ref[g])
            values.append(vb[g])
            sinks.append(sink_ref[g])
        mem_scores += _mem_scores(qm_ref[b], mk_ref.at[b])
        mem_values += _mem_values(mv_ref.at[b])
    outs = _softmax_pv(scores, values, sinks)
    mem = _softmax_pv(mem_scores, mem_values)
    for b in range(nb):
        o_ref[b] = _gqa_outputs(outs[b * N_KV_HEADS:(b + 1) * N_KV_HEADS], nq).astype(BF16)
        mo_ref[b] = jnp.concatenate(mem[b * MEM_HEADS:(b + 1) * MEM_HEADS], axis=1).astype(BF16)


def _attn_sample(q, kk2, vv2, bias, sink, qm, mk, mv):
    b, s, _ = q.shape
    nb = 8
    nk = kk2.shape[1]
    blk = lambda *shape: pl.BlockSpec((nb,) + shape, lambda i: (i,) + (0,) * len(shape))
    return pl.pallas_call(
        _attn_sample_body,
        grid=(b // nb,),
        in_specs=[blk(s, ATTN_W), blk(nk, KV2_W), blk(nk, KV2_W),
                  _resident(bias.shape), _resident(sink.shape),
                  blk(s, MEM_W), blk(MEM_LEN, MEM_W), blk(MEM_LEN, MEM_W)],
        out_specs=[blk(s, ATTN_W), blk(s, MEM_W)],
        out_shape=[jax.ShapeDtypeStruct((b, s, ATTN_W), BF16), jax.ShapeDtypeStruct((b, s, MEM_W), BF16)],
        compiler_params=_params("parallel"),
        name="attn_sample",
    )(q, kk2, vv2, bias, sink, qm, mk, mv)


def _ssm_weights_body(lam_re_row, lam_im_row, lam_re_col, lam_im_col, log_dt, b_re_t, b_im_t,
                      b_re_rows, b_im_rows, c_re_rows, c_im_rows, c_re_lanes, c_im_lanes, sel_ref,
                      wconv_ref, wst_ref, wot_ref, a_tab_ref):
    wst_ref[0] = jnp.zeros(wst_ref.shape[1:], BF16)
    wot_ref[0] = jnp.zeros(wot_ref.shape[1:], BF16)
    krows = []
    for gl in range(SSM_GB):
        krows.append(_ssm_group_weights(
            gl, *(r[gl] for r in (lam_re_row, lam_im_row, lam_re_col, lam_im_col, log_dt, b_re_t, b_im_t,
                                  b_re_rows, b_im_rows, c_re_rows, c_im_rows, c_re_lanes, c_im_lanes)),
            wst_ref, wot_ref, a_tab_ref))
    kstack = jnp.concatenate(krows, axis=0).astype(BF16)
    for d in range(SSM_NT):
        for ti in range(SSM_TPT):
            blk = _mm(kstack, sel_ref[d * SSM_TPT + ti])
            r0 = (SSM_NT - 1 - d) * MXU_TILE + ti * LANES
            for gl in range(SSM_GB):
                piece = blk[gl * SSM_GROUP:(gl + 1) * SSM_GROUP]
                piece = piece if gl == 0 else pltpu.roll(piece, gl * SSM_GROUP, 1)
                wconv_ref[0, r0 + gl * SSM_GROUP:r0 + (gl + 1) * SSM_GROUP, :] = piece.astype(BF16)


def _ssm_group_weights(gl, lam_re_row, lam_im_row, lam_re_col, lam_im_col, log_dt, b_re_t, b_im_t,
                       b_re_rows, b_im_rows, c_re_rows, c_im_rows, c_re_lanes, c_im_lanes,
                       wst_ref, wot_ref, a_tab_ref):
    dt = jnp.exp(log_dt)
    rows = lambda t: slice(t * LANES + gl * SSM_GROUP, t * LANES + (gl + 1) * SSM_GROUP)
    lanes = slice(gl * SSM_S2, (gl + 1) * SSM_S2)

    def zoh_coef(lr, li):
        mag = jnp.exp(lr * dt)
        a_re, a_im = mag * jnp.cos(li * dt), mag * jnp.sin(li * dt)
        den = lr * lr + li * li
        return ((a_re - 1.0) * lr + a_im * li) / den, (a_im * lr - (a_re - 1.0) * li) / den

    def a_power(lr, li, n):
        mag = jnp.exp(lr * dt * n)
        return mag * jnp.cos(li * dt * n), mag * jnp.sin(li * dt * n)

    lr, li = lam_re_row, lam_im_row
    cr, ci = zoh_coef(lr, li)
    t_row = (lax.broadcasted_iota(jnp.int32, (SSM_K, 1), 0) // SSM_GROUP).astype(F32)
    pr, pi = a_power(lr, li, (SSM_T - 1) - t_row)
    zr, zi = pr * cr - pi * ci, pr * ci + pi * cr
    br, bi = b_re_rows, b_im_rows
    wstate = jnp.concatenate([zr * br - zi * bi, zr * bi + zi * br], axis=1).astype(BF16)

    qr, qi = a_power(lr, li, t_row + 1.0)
    ccr, cci = c_re_rows, c_im_rows
    wout_t = jnp.concatenate([ccr * qr - cci * qi, -(ccr * qi + cci * qr)], axis=1).astype(BF16)
    for t in range(SSM_T):
        wst_ref[0, rows(t), lanes] = wstate[t * SSM_GROUP:(t + 1) * SSM_GROUP]
        wot_ref[0, rows(t), lanes] = wout_t[t * SSM_GROUP:(t + 1) * SSM_GROUP]

    lrc, lic = lam_re_col, lam_im_col
    t_lane = (lax.broadcasted_iota(jnp.int32, (1, SSM_K), 1) // SSM_GROUP).astype(F32)
    gr, gi = a_power(lrc, lic, t_lane)
    clr, cli = c_re_lanes, c_im_lanes
    g_re, g_im = clr * gr - cli * gi, clr * gi + cli * gr
    btr, bti = b_re_t, b_im_t
    bbr, bbi = cr * btr - ci * bti, cr * bti + ci * btr
    hi = lax.Precision.HIGHEST
    krow = (jnp.dot(bbr, g_re, precision=hi, preferred_element_type=F32)
            - jnp.dot(bbi, g_im, precision=hi, preferred_element_type=F32))

    idx = lax.broadcasted_iota(jnp.int32, (N_SCAN_TABLES * SCAN_ROWS, SSM_STATE), 0)
    tab, r = idx // SCAN_ROWS, idx % SCAN_ROWS
    stride = jnp.where(tab < 2, 1, jnp.where(tab < 4, 2, 4))
    n = jnp.where(tab < 6, stride, jnp.where(tab < 8, r, SCAN_ROWS))
    keep = jnp.logical_or(tab >= 6, r >= stride)
    er, ei = a_power(lr, li, (n * SSM_T).astype(F32))
    er, ei = jnp.where(keep, er, 0.0), jnp.where(keep, ei, 0.0)
    odd = tab % 2 == 1
    tabs = jnp.concatenate([jnp.where(odd, -ei, er), jnp.where(odd, ei, er)], axis=1)
    a_tab_ref[:, :, lanes] = tabs.reshape(N_SCAN_TABLES, SCAN_ROWS, SSM_S2)
    return krow


def _lag_selectors():
    sel = np.zeros((SSM_NT, SSM_TPT, SSM_K, MXU_TILE), np.float32)
    ch = np.arange(SSM_GROUP)
    for d in range(SSM_NT):
        for ti in range(SSM_TPT):
            for to in range(SSM_TPT):
                lag = SSM_TPT * d + to - ti
                if lag >= 0:
                    sel[d, ti, lag * SSM_GROUP + ch, to * LANES + ch] = 1.0
    return jnp.asarray(sel.reshape(SSM_NT * SSM_TPT, SSM_K, MXU_TILE), BF16)


def _ssm_weights(lam_re, lam_im, log_dt, b_re, b_im, c_re, c_im):
    g, p, c = b_re.shape
    row3 = lambda x: x.reshape(g, 1, p)
    col3 = lambda x: x.reshape(g, p, 1)
    t3 = lambda x: jnp.transpose(x, (0, 2, 1))
    b_rows = lambda x: jnp.tile(t3(x), (1, SSM_T, 1))
    c_rows = lambda x: jnp.tile(x, (1, SSM_T, 1))
    c_lanes = lambda x: jnp.tile(t3(x), (1, 1, SSM_T))
    args = (row3(lam_re), row3(lam_im), col3(lam_re), col3(lam_im), log_dt.reshape(g, 1, 1),
            t3(b_re), t3(b_im), b_rows(b_re), b_rows(b_im), c_rows(c_re), c_rows(c_im),
            c_lanes(c_re), c_lanes(c_im))
    sel = _lag_selectors()
    spec = lambda x: pl.BlockSpec((SSM_GB,) + x.shape[1:], lambda i: (i, 0, 0))
    per_gb = lambda *shape: pl.BlockSpec((1,) + shape, lambda i: (i, 0, 0))
    return pl.pallas_call(
        _ssm_weights_body,
        grid=(SSM_NGB,),
        in_specs=[spec(a) for a in args] + [_resident(sel.shape)],
        out_specs=[per_gb(SSM_NT * MXU_TILE, MXU_TILE), per_gb(SSM_XW, SSM_SW), per_gb(SSM_XW, SSM_SW),
                   pl.BlockSpec((N_SCAN_TABLES, SCAN_ROWS, SSM_SW), lambda i: (0, 0, i))],
        out_shape=[jax.ShapeDtypeStruct((SSM_NGB, SSM_NT * MXU_TILE, MXU_TILE), BF16),
                   jax.ShapeDtypeStruct((SSM_NGB, SSM_XW, SSM_SW), BF16),
                   jax.ShapeDtypeStruct((SSM_NGB, SSM_XW, SSM_SW), BF16),
                   jax.ShapeDtypeStruct((N_SCAN_TABLES, SCAN_ROWS, SSM_WIDTH), F32)],
        compiler_params=_params("parallel"),
        name="ssm_weights",
    )(*args, sel)


def _swap_halves(s):
    ax = s.ndim - 1
    return jnp.concatenate([pltpu.roll(s[..., l:l + SSM_S2], SSM_STATE, ax)
                            for l in range(0, s.shape[ax], SSM_S2)], axis=ax)


def _ssm_body(row_sets, slab_rows, u_ref, wconv_ref, wst_ref, wot_ref, d_ref, a_tab_ref, s0_ref,
              y_ref, s_ref, v_scr, s_scr):
    def piece(ref_set, first, n, stride, t):
        return ref_set, pl.ds(first + t, n, stride=stride)

    @pl.when(pl.program_id(1) == 0)
    def _():
        s_scr[...] = s0_ref[0]

    x = jnp.concatenate(
        [jnp.concatenate([u_ref[piece(*rs, t)] for t in range(SSM_T)], axis=1) for rs in row_sets],
        axis=0).astype(BF16)
    v_scr[...] = _mm(x, wst_ref[0])

    n_rows = v_scr.shape[0]
    if slab_rows:
        a_mul, a_swap = a_tab_ref[6, 1:2, :], a_tab_ref[7, 1:2, :]
        s = s_scr[...]
        for c in range(n_rows // slab_rows):
            rows = slice(c * slab_rows, (c + 1) * slab_rows)
            inc = v_scr[rows, :]
            v_scr[rows, :] = s
            s = a_mul * s + a_swap * _swap_halves(s) + inc
        s_scr[...] = s
    else:
        n_blocks = n_rows // SCAN_ROWS
        xs = v_scr[...].reshape(n_blocks, SCAN_ROWS, SSM_SW)
        for k in range(3):
            sh = pltpu.roll(xs, 1 << k, 1)
            xs = xs + a_tab_ref[2 * k] * sh + a_tab_ref[2 * k + 1] * _swap_halves(sh)
        carries, blk = [], 0
        for j, (_, _, n, _) in enumerate(row_sets):
            carry = s_scr[j * SCAN_ROWS:(j + 1) * SCAN_ROWS, :]
            for _ in range(n // SCAN_ROWS):
                carries.append(carry)
                last = jnp.broadcast_to(xs[blk, SCAN_ROWS - 1:, :], carry.shape)
                carry = last + a_tab_ref[8] * carry + a_tab_ref[9] * _swap_halves(carry)
                blk += 1
            s_scr[j * SCAN_ROWS:(j + 1) * SCAN_ROWS, :] = carry
        carries = jnp.stack(carries, axis=0)
        not_first = lax.broadcasted_iota(jnp.int32, (1, SCAN_ROWS, 1), 1) >= 1
        enter = (jnp.where(not_first, pltpu.roll(xs, 1, 1), 0.0)
                 + a_tab_ref[6] * carries + a_tab_ref[7] * _swap_halves(carries))
        v_scr[...] = enter.reshape(n_rows, SSM_SW)
    s_ref[0] = s_scr[...]

    enter = v_scr[...].astype(BF16)
    d = d_ref[...]
    for jo in range(SSM_NT):
        k_hi = (jo + 1) * MXU_TILE
        yt = _mm(x[:, :k_hi], wconv_ref[0, (SSM_NT - 1 - jo) * MXU_TILE:, :])
        yt = yt + _qk(enter, wot_ref[0, jo * MXU_TILE:(jo + 1) * MXU_TILE, :])
        r0 = 0
        for rs in row_sets:
            for tl in range(SSM_TPT):
                idx = piece(*rs, jo * SSM_TPT + tl)
                y_ref[idx] = yt[r0:r0 + rs[2], tl * LANES:(tl + 1) * LANES] + u_ref[idx] * d
            r0 += rs[2]


def _ssm(u, row_sets, slab_rows, block_rows, wconv, wst, wo, d_lanes, a_tab, s0):
    ns, r, _ = u.shape
    chunk_rows = sum(rs[2] for rs in row_sets)
    carry_rows = s0.shape[1]
    blk = pl.BlockSpec((ns, block_rows, LANES), lambda gb, i: (0, i, gb))
    per_gb = lambda x: pl.BlockSpec((1,) + x.shape[1:], lambda gb, i: (gb, 0, 0))
    return pl.pallas_call(
        functools.partial(_ssm_body, row_sets, slab_rows),
        grid=(SSM_NGB, r // block_rows),
        in_specs=[blk, per_gb(wconv), per_gb(wst), per_gb(wo),
                  pl.BlockSpec((1, LANES), lambda gb, i: (0, gb)),
                  pl.BlockSpec((N_SCAN_TABLES, SCAN_ROWS, SSM_SW), lambda gb, i: (0, 0, gb)),
                  per_gb(s0)],
        out_specs=[blk, per_gb(s0)],
        out_shape=[jax.ShapeDtypeStruct(u.shape, F32), jax.ShapeDtypeStruct(s0.shape, F32)],
        scratch_shapes=[pltpu.VMEM((chunk_rows, SSM_SW), F32), pltpu.VMEM((carry_rows, SSM_SW), F32)],
        compiler_params=_params("parallel", "arbitrary"),
        name="ssm_scan",
    )(u, wconv, wst, wo, d_lanes, a_tab, s0)


def _ssm_branch(u, s0, ssm_w):
    b, l, _ = u.shape
    nc = l // SSM_T
    by_gb = lambda s: s.reshape(s.shape[0], SSM_NGB, SSM_SW).transpose(1, 0, 2)
    if b % SCAN_ROWS == 0:
        row_sets = tuple((0, c * SSM_T, b, l) for c in range(nc))
        y, s_last = _ssm(u.reshape(1, b * l, SSM_W), row_sets, b, b * l, *ssm_w, by_gb(s0))
    else:
        cps = min(nc, SSM_CHUNKS_PER_STEP)
        row_sets = tuple((j, 0, cps, SSM_T) for j in range(b))
        y, s_last = _ssm(u, row_sets, 0, cps * SSM_T, *ssm_w, by_gb(jnp.repeat(s0, SCAN_ROWS, axis=0)))
        s_last = s_last[:, ::SCAN_ROWS]
    s_last = s_last.transpose(1, 0, 2).reshape(b, SSM_GROUPS, 2, SSM_STATE)
    return y.reshape(b, l, SSM_W), s_last[:, :, 0], s_last[:, :, 1]


def _merge_ffn2_body(x_ref, a_ref, m_ref, y_ref, gmix_ref, wg_ref, wab_ref, wglu_ref, wmb_ref, wo_ref,
                     gpost_ref, g2pre_ref, g2post_ref, wfi_ref, wfo_ref, o_ref):
    x1 = x_ref[...]
    h = _rms(x1, gmix_ref[...]).astype(BF16)
    gate = lambda j: jax.nn.sigmoid(_mm(h, wg_ref[:, j * D_MODEL:(j + 1) * D_MODEL]))
    y = y_ref[...].astype(BF16)
    merged = gate(0) * _mm(a_ref[...], wab_ref[...])
    merged = merged + gate(1) * (_mm(y, wglu_ref[:, :D_MODEL]) * jax.nn.sigmoid(_mm(y, wglu_ref[:, D_MODEL:])))
    merged = merged + gate(2) * _mm(m_ref[...], wmb_ref[...])
    x2 = x1 + _rms(_mm(merged.astype(BF16), wo_ref[...]), gpost_ref[...])
    o_ref[...] = _ffn(x2, g2pre_ref[...], g2post_ref[...], wfi_ref, wfo_ref)


def _merge_ffn2(x1, attn, memo, yssm, gmix, wg, wab, wglu, wmb, wo, gpost, g2pre, g2post, wfi, wfo):
    n = x1.shape[0]
    row = lambda w: pl.BlockSpec((FFN_ROW_TILE, w), lambda i: (i, 0))
    vec = _resident((1, D_MODEL))
    return pl.pallas_call(
        _merge_ffn2_body,
        grid=(n // FFN_ROW_TILE,),
        in_specs=[row(D_MODEL), row(ATTN_W), row(MEM_W), row(SSM_W), vec, _resident(wg.shape),
                  _resident(wab.shape), _resident(wglu.shape), _resident(wmb.shape), _resident(wo.shape),
                  vec, vec, vec, _resident(wfi.shape), _resident(wfo.shape)],
        out_specs=row(D_MODEL),
        out_shape=jax.ShapeDtypeStruct((n, D_MODEL), F32),
        compiler_params=_params("parallel"),
        name="merge_ffn2",
    )(x1, attn, memo, yssm, gmix, wg, wab, wglu, wmb, wo, gpost, g2pre, g2post, wfi, wfo)


def _t5_bucket(rel):
    half = N_BUCKETS // 2
    max_exact = half // 2
    ret = (rel > 0).astype(np.int32) * half
    n = np.abs(rel)
    large = max_exact + (np.log(np.maximum(n, 1) / max_exact) / math.log(MAX_DISTANCE / max_exact)
                         * (half - max_exact)).astype(np.int32)
    large = np.minimum(large, half - 1)
    return ret + np.where(n < max_exact, n, large)


def _band_bias(rel_table, n_q, n_back, n_k):
    i = np.arange(n_q)[:, None]
    j = np.arange(n_k)[None, :]
    bucket = _t5_bucket((j - n_back) - i).reshape(-1)
    onehot = np.zeros((N_BUCKETS, bucket.size), np.float32)
    onehot[bucket, np.arange(bucket.size)] = 1.0
    b = jnp.dot(rel_table.astype(F32).T, jnp.asarray(onehot), precision=lax.Precision.HIGHEST)
    return b.reshape(N_KV_HEADS, KV_REP * n_q, n_k)


def _sink_rows(sink, n_q):
    return jnp.repeat(sink.astype(F32).reshape(N_KV_HEADS, KV_REP), n_q, axis=1)[:, :, None]


def _twice_per_head(x, axis):
    shape = x.shape
    x = x.reshape(shape[:axis] + (N_KV_HEADS, 1, HEAD_DIM) + shape[axis + 1:])
    x = jnp.concatenate([x, x], axis=axis + 1)
    return x.reshape(shape[:axis] + (KV2_W,) + shape[axis + 1:])


def kernel(x_prompt, x_sample, cache_swa_k, cache_swa_v, cache_mem_k, cache_mem_v, state_ssm_re, state_ssm_im, mem_prompt, rel_bias_table, ff1_pre_g, ff1_post_g, w_ff1_in, w_ff1_out, mix_pre_g, mix_post_g, w_in, mem_norm_g, w_mem_kv, attn_sink, ssm_lambda_re, ssm_lambda_im, ssm_log_dt, ssm_b_re, ssm_b_im, ssm_c_re, ssm_c_im, ssm_d, w_ssm_glu, w_attn_br, w_mem_br, w_out, ff2_pre_g, ff2_post_g, w_ff2_in, w_ff2_out):
    assert ff1_pre_g.shape[0] == 1, "single-layer step"
    bp, lp, _ = x_prompt.shape
    bs, ls, _ = x_sample.shape
    vec = lambda g: g[0].reshape(1, D_MODEL).astype(F32)
    w16 = lambda w: w[0].astype(BF16)

    wfi1, wfo1, wfi2, wfo2 = w16(w_ff1_in), w16(w_ff1_out), w16(w_ff2_in), w16(w_ff2_out)
    w_in16 = w16(w_in)
    c_k, c_v, c_u = ATTN_W, ATTN_W + KV_W, ATTN_W + 2 * KV_W
    c_g = c_u + SSM_W + MEM_W
    q_scale = HEAD_DIM ** -0.5
    assert math.frexp(q_scale)[0] == 0.5, "power-of-two scale: folding it into the q columns is exact"
    wp = jnp.concatenate([w_in16[:, :c_k] * jnp.asarray(q_scale, BF16), w_in16[:, c_k:c_g],
                          _twice_per_head(w_in16[:, c_k:c_v], 1), _twice_per_head(w_in16[:, c_v:c_u], 1)], axis=1)
    wg = w_in16[:, c_g:]
    wab, wglu, wmb, wo = w16(w_attn_br), w16(w_ssm_glu), w16(w_mem_br), w16(w_out)

    wconv, wst, wot, a_tab = _ssm_weights(
        ssm_lambda_re[0], ssm_lambda_im[0], ssm_log_dt[0], ssm_b_re[0], ssm_b_im[0], ssm_c_re[0], ssm_c_im[0])
    ssm_w = (wconv, wst, wot, ssm_d[0].astype(F32).reshape(1, SSM_W), a_tab)

    mk_p, mv_p = _mem_kv(mem_prompt.reshape(bp * MEM_LEN, D_MODEL), vec(mem_norm_g), w16(w_mem_kv))

    def group(x, attend, s0, mem_k, mem_v):
        b, l, _ = x.shape
        r3 = lambda t: t.reshape(b, l, t.shape[-1])
        x1, q, k, v, u, qm, k2, v2 = _ffn1_proj(x.reshape(b * l, D_MODEL), vec(ff1_pre_g), vec(ff1_post_g),
                                                wfi1, wfo1, vec(mix_pre_g), wp)
        attn, memo = attend(r3(q), r3(k2), r3(v2), r3(qm), mem_k, mem_v)
        y_ssm, s_re, s_im = _ssm_branch(r3(u), s0, ssm_w)
        y = _merge_ffn2(x1, attn.reshape(b * l, ATTN_W), memo.reshape(b * l, MEM_W),
                        y_ssm.reshape(b * l, SSM_W), vec(mix_pre_g), wg, wab, wglu, wmb, wo,
                        vec(mix_post_g), vec(ff2_pre_g), vec(ff2_post_g), wfi2, wfo2)
        return r3(y), r3(k), r3(v), s_re, s_im

    def attend_prompt(q, k2, v2, qm, mem_k, mem_v):
        bias = _band_bias(rel_bias_table, CHUNK, WINDOW, WINDOW + CHUNK)
        return _attn_prompt(q, k2, v2, bias, _sink_rows(attn_sink[0], CHUNK), qm, mem_k, mem_v)

    def attend_sample(q, k2, v2, qm, mem_k, mem_v):
        n_back = cache_swa_k.shape[2]
        cache2 = lambda c: _twice_per_head(c[0].reshape(bs, n_back, KV_W).astype(BF16), 2)
        kk = jnp.concatenate([cache2(cache_swa_k), k2], axis=1)
        vv = jnp.concatenate([cache2(cache_swa_v), v2], axis=1)
        bias = _band_bias(rel_bias_table, ls, n_back, n_back + ls)
        return _attn_sample(q, kk, vv, bias, _sink_rows(attn_sink[0], ls), qm, mem_k, mem_v)

    yp, pk, pv, pre, pim = group(x_prompt, attend_prompt, jnp.zeros((bp, SSM_WIDTH), F32),
                                 mk_p.reshape(bp, MEM_LEN, MEM_W), mv_p.reshape(bp, MEM_LEN, MEM_W))
    s0 = jnp.stack([state_ssm_re[0], state_ssm_im[0]], axis=2).reshape(bs, SSM_WIDTH).astype(F32)
    ys, sk, sv, sre, sim = group(x_sample, attend_sample, s0,
                                 cache_mem_k[0].reshape(bs, MEM_LEN, MEM_W),
                                 cache_mem_v[0].reshape(bs, MEM_LEN, MEM_W))

    n_keep = min(WINDOW, lp)
    heads = lambda t: t.reshape(t.shape[0], t.shape[1], N_KV_HEADS, HEAD_DIM)[None]
    mem_heads = lambda t: t.reshape(bp, MEM_LEN, MEM_HEADS, MEM_HEAD_DIM)[None]
    return (yp, ys, heads(pk[:, -n_keep:]), heads(pv[:, -n_keep:]), mem_heads(mk_p), mem_heads(mv_p),
            pre[None], pim[None], heads(sk), heads(sv), sre[None], sim[None])
```

```python
import functools
import math

import numpy as np
import jax
import jax.numpy as jnp
from jax import lax
from jax.experimental import pallas as pl
from jax.experimental.pallas import tpu as pltpu

D_MODEL = 1024
CHUNK = 64
WINDOW = 128
HEAD_DIM = 64
MIX_W = D_MODEL // 2
N_HEADS = MIX_W // HEAD_DIM
N_KV_HEADS = N_HEADS // 4
KV_REP = N_HEADS // N_KV_HEADS
ATTN_W = N_HEADS * HEAD_DIM
KV_W = N_KV_HEADS * HEAD_DIM
SSM_GROUP = 16
SSM_W = MIX_W
SSM_GROUPS = SSM_W // SSM_GROUP
SSM_STATE = 64
MEM_LEN = 256
MEM_HEADS = 4
MEM_HEAD_DIM = MIX_W // MEM_HEADS
MEM_W = MEM_HEADS * MEM_HEAD_DIM
D_FF = 128 * ((8 * D_MODEL // 3 + 127) // 128)
N_BUCKETS = 32
MAX_DISTANCE = 128
RMS_EPS = 1e-6
NEG_INF = -1e30

LANES = 128
MXU_TILE = 256
ROW_TILE = 256
FFN_ROW_TILE = 512
FFN_CHUNK = 2 * MXU_TILE
VMEM_LIMIT = 60 * 1024 * 1024

KV2_W = N_KV_HEADS * LANES

SSM_T = 16
SSM_K = SSM_T * SSM_GROUP
SSM_S2 = 2 * SSM_STATE
SSM_WIDTH = SSM_GROUPS * SSM_S2
SSM_GB = LANES // SSM_GROUP
SSM_NGB = SSM_GROUPS // SSM_GB
SSM_XW = SSM_T * LANES
SSM_SW = SSM_GB * SSM_S2
SSM_TPT = MXU_TILE // LANES
SSM_NT = SSM_T // SSM_TPT
SCAN_ROWS = 8
N_SCAN_TABLES = 10
SSM_CHUNKS_PER_STEP = 128

F32 = jnp.float32
BF16 = jnp.bfloat16


def _params(*sem):
    return pltpu.CompilerParams(dimension_semantics=sem, vmem_limit_bytes=VMEM_LIMIT)


def _resident(shape):
    zeros = (0,) * len(shape)
    return pl.BlockSpec(shape, lambda *_: zeros, pipeline_mode=pl.Buffered(1))


def _rms(x, g):
    return x * lax.rsqrt(jnp.mean(x * x, axis=-1, keepdims=True) + RMS_EPS) * g


def _mm(a, b):
    return jnp.dot(a, b, preferred_element_type=F32)


def _ffn(x, gpre, gpost, w_in_ref, w_out_ref):
    h = _rms(x, gpre).astype(BF16)
    acts = []
    for c0 in range(0, D_FF, FFN_CHUNK):
        c1 = min(c0 + FFN_CHUNK, D_FF)
        g = _mm(h, w_in_ref[:, c0:c1])
        u = _mm(h, w_in_ref[:, D_FF + c0:D_FF + c1])
        acts.append((g * jax.nn.sigmoid(g) * u).astype(BF16))
    return x + 0.5 * _rms(_mm(jnp.concatenate(acts, axis=1), w_out_ref[...]), gpost)


PROJ_SPLIT = (("q", ATTN_W, BF16), ("k", KV_W, F32), ("v", KV_W, F32), ("u", SSM_W, F32),
              ("qm", MEM_W, BF16), ("k2", KV2_W, BF16), ("v2", KV2_W, BF16))


def _ffn1_proj_body(x_ref, gpre_ref, gpost_ref, wfi_ref, wfo_ref, gmix_ref, wp_ref, x1_ref, *out_refs):
    x1 = _ffn(x_ref[...], gpre_ref[...], gpost_ref[...], wfi_ref, wfo_ref)
    x1_ref[...] = x1
    p = _mm(_rms(x1, gmix_ref[...]).astype(BF16), wp_ref[...])
    col = 0
    for ref, (_, width, dtype) in zip(out_refs, PROJ_SPLIT):
        ref[...] = p[:, col:col + width].astype(dtype)
        col += width


def _ffn1_proj(x, gpre, gpost, wfi, wfo, gmix, wp):
    n = x.shape[0]
    row = lambda w: pl.BlockSpec((FFN_ROW_TILE, w), lambda i: (i, 0))
    return pl.pallas_call(
        _ffn1_proj_body,
        grid=(n // FFN_ROW_TILE,),
        in_specs=[row(D_MODEL), _resident((1, D_MODEL)), _resident((1, D_MODEL)),
                  _resident(wfi.shape), _resident(wfo.shape), _resident((1, D_MODEL)),
                  _resident(wp.shape)],
        out_specs=[row(D_MODEL)] + [row(w) for _, w, _ in PROJ_SPLIT],
        out_shape=[jax.ShapeDtypeStruct((n, D_MODEL), F32)]
        + [jax.ShapeDtypeStruct((n, w), d) for _, w, d in PROJ_SPLIT],
        compiler_params=_params("parallel"),
        name="ffn1_proj",
    )(x, gpre, gpost, wfi, wfo, gmix, wp)


def _mem_kv_body(m_ref, g_ref, w_ref, k_ref, v_ref):
    kv = _mm(_rms(m_ref[...], g_ref[...]).astype(BF16), w_ref[...])
    k_ref[...] = kv[:, :MEM_W]
    v_ref[...] = kv[:, MEM_W:]


def _mem_kv(mem, g, w):
    n = mem.shape[0]
    row = lambda wd: pl.BlockSpec((ROW_TILE, wd), lambda i: (i, 0))
    return pl.pallas_call(
        _mem_kv_body,
        grid=(n // ROW_TILE,),
        in_specs=[row(D_MODEL), _resident((1, D_MODEL)), _resident(w.shape)],
        out_specs=[row(MEM_W), row(MEM_W)],
        out_shape=[jax.ShapeDtypeStruct((n, MEM_W), F32)] * 2,
        compiler_params=_params("parallel"),
        name="mem_kv",
    )(mem, g, w)


def _qk(q, k):
    return lax.dot_general(q, k, (((1,), (1,)), ((), ())), preferred_element_type=F32)


def _softmax_pv(scores, values, sinks=None):
    s = jnp.concatenate(scores, axis=0)
    m = jnp.max(s, axis=-1, keepdims=True)
    if sinks is not None:
        sink = jnp.concatenate(sinks, axis=0)
        m = jnp.maximum(m, sink)
    e = jnp.exp(s - m)
    den = jnp.sum(e, axis=-1, keepdims=True)
    if sinks is not None:
        den = den + jnp.exp(sink - m)
    p = (e / den).astype(BF16)
    outs, r0 = [], 0
    for sc, v in zip(scores, values):
        outs.append(_mm(p[r0:r0 + sc.shape[0]], v))
        r0 += sc.shape[0]
    return outs


def _gqa_queries(q):
    nq = q.shape[0]
    low = lax.broadcasted_iota(jnp.int32, (nq, LANES), 1) < HEAD_DIM
    zero = jnp.zeros((nq, LANES), BF16)
    stacks = []
    for g in range(N_KV_HEADS):
        rows = []
        for r in range(KV_REP):
            h = g * KV_REP + r
            q2 = q[:, (h // 2) * LANES:(h // 2 + 1) * LANES]
            rows.append(jnp.where(low, q2, zero) if h % 2 == 0 else jnp.where(low, zero, q2))
        stacks.append(jnp.concatenate(rows, axis=0))
    return stacks


def _gqa_outputs(outs, nq):
    low = lax.broadcasted_iota(jnp.int32, (nq, LANES), 1) < HEAD_DIM
    pairs = []
    for o in outs:
        for r in range(0, KV_REP, 2):
            pairs.append(jnp.where(low, o[r * nq:(r + 1) * nq], o[(r + 1) * nq:(r + 2) * nq]))
    return jnp.concatenate(pairs, axis=1)


def _mem_scores(qm, mk_ref):
    return [_qk(qm[:, h * MEM_HEAD_DIM:(h + 1) * MEM_HEAD_DIM],
                mk_ref[:, h * MEM_HEAD_DIM:(h + 1) * MEM_HEAD_DIM].astype(BF16)) * (MEM_HEAD_DIM ** -0.5)
            for h in range(MEM_HEADS)]


def _mem_values(mv_ref):
    return [mv_ref[:, h * MEM_HEAD_DIM:(h + 1) * MEM_HEAD_DIM].astype(BF16) for h in range(MEM_HEADS)]


def _kv_heads(x):
    return [x[:, g * LANES:(g + 1) * LANES] for g in range(N_KV_HEADS)]


def _merge_pre(x1, y, gmix_ref, wg_ref, wglu_ref):
    h = _rms(x1, gmix_ref[...]).astype(BF16)
    logits = [_mm(h, wg_ref[:, j * D_MODEL:(j + 1) * D_MODEL]) for j in range(3)]
    y = y.astype(BF16)
    return logits, _mm(y, wglu_ref[:, :D_MODEL]), _mm(y, wglu_ref[:, D_MODEL:])


def _merge_post(x1, pre, attn, memo, wab_ref, wmb_ref, wo_ref, gpost_ref):
    logits, ya, yb = pre
    sig = jax.nn.sigmoid
    merged = sig(logits[0]) * _mm(attn.astype(BF16), wab_ref[...])
    merged = merged + sig(logits[1]) * (ya * sig(yb))
    merged = merged + sig(logits[2]) * _mm(memo.astype(BF16), wmb_ref[...])
    return x1 + _rms(_mm(merged.astype(BF16), wo_ref[...]), gpost_ref[...])


def _mixer_prompt_body(q_ref, k_ref, kh_ref, v_ref, vh_ref, bias_ref, sink_ref, qm_ref, mk_ref, mv_ref,
                       x1_ref, y_ref, gmix_ref, wg_ref, wglu_ref, wab_ref, wmb_ref, wo_ref, gpost_ref,
                       x2_ref):
    i = pl.program_id(1)
    tq = q_ref.shape[1]
    kf = jnp.concatenate([kh_ref[0], k_ref[0]], axis=0)
    vf = jnp.concatenate([vh_ref[0], v_ref[0]], axis=0)
    band = WINDOW + CHUNK
    scores, values, sinks = [], [], []
    for c in range(tq // CHUNK):
        lo = c * CHUNK
        valid = None
        if lo < WINDOW:
            key_pos = lax.broadcasted_iota(jnp.int32, (1, band), 1) + (i * tq + lo - WINDOW)
            valid = key_pos >= 0
        kb, vb = _kv_heads(kf[lo:lo + band]), _kv_heads(vf[lo:lo + band])
        for g, qs in enumerate(_gqa_queries(q_ref[0, lo:lo + CHUNK, :])):
            s = _qk(qs, kb[g]) + bias_ref[g]
            scores.append(s if valid is None else jnp.where(valid, s, NEG_INF))
            values.append(vb[g])
            sinks.append(sink_ref[g])
    mem_scores = _mem_scores(qm_ref[0], mk_ref.at[0])
    x1 = x1_ref[...]
    pre = _merge_pre(x1, y_ref[...], gmix_ref, wg_ref, wglu_ref)
    outs = _softmax_pv(scores, values, sinks)
    mem = _softmax_pv(mem_scores, _mem_values(mv_ref.at[0]))
    attn = jnp.concatenate([_gqa_outputs(outs[c * N_KV_HEADS:(c + 1) * N_KV_HEADS], CHUNK)
                            for c in range(tq // CHUNK)], axis=0)
    x2_ref[...] = _merge_post(x1, pre, attn, jnp.concatenate(mem, axis=1), wab_ref, wmb_ref, wo_ref, gpost_ref)


def _merge_weight_specs(weights):
    return [_resident(w.shape) for w in weights]


def _mixer_prompt(q, k2, v2, bias, sink, qm, mk, mv, x1, y, merge_w):
    b, l, _ = q.shape
    tq = ROW_TILE
    hb = tq // WINDOW
    nt = l // tq
    cur = lambda w: pl.BlockSpec((1, tq, w), lambda bi, i: (bi, i, 0))
    row = lambda w: pl.BlockSpec((tq, w), lambda bi, i: (bi * nt + i, 0))
    halo = pl.BlockSpec((1, WINDOW, KV2_W), lambda bi, i: (bi, jnp.maximum(i * hb - 1, 0), 0))
    mem = pl.BlockSpec((1, MEM_LEN, MEM_W), lambda bi, i: (bi, 0, 0))
    return pl.pallas_call(
        _mixer_prompt_body,
        grid=(b, nt),
        in_specs=[cur(ATTN_W), cur(KV2_W), halo, cur(KV2_W), halo, _resident(bias.shape),
                  _resident(sink.shape), cur(MEM_W), mem, mem, row(D_MODEL), row(SSM_W)]
        + _merge_weight_specs(merge_w),
        out_specs=row(D_MODEL),
        out_shape=jax.ShapeDtypeStruct((b * l, D_MODEL), F32),
        compiler_params=_params("parallel", "parallel"),
        name="mixer_prompt",
    )(q, k2, k2, v2, v2, bias, sink, qm, mk, mv, x1, y, *merge_w)


def _mixer_sample_body(q_ref, k_ref, v_ref, bias_ref, sink_ref, qm_ref, mk_ref, mv_ref,
                       x1_ref, y_ref, gmix_ref, wg_ref, wglu_ref, wab_ref, wmb_ref, wo_ref, gpost_ref,
                       x2_ref):
    nb, nq = q_ref.shape[0], q_ref.shape[1]
    scores, values, sinks, mem_scores, mem_values = [], [], [], [], []
    for b in range(nb):
        kb, vb = _kv_heads(k_ref[b]), _kv_heads(v_ref[b])
        for g, qs in enumerate(_gqa_queries(q_ref[b])):
            scores.append(_qk(qs, kb[g]) + bias_ref[g])
            values.append(vb[g])
            sinks.append(sink_ref[g])
        mem_scores += _mem_scores(qm_ref[b], mk_ref.at[b])
        mem_values += _mem_values(mv_ref.at[b])
    x1 = x1_ref[...]
    pre = _merge_pre(x1, y_ref[...], gmix_ref, wg_ref, wglu_ref)
    outs = _softmax_pv(scores, values, sinks)
    mem = _softmax_pv(mem_scores, mem_values)
    attn = jnp.concatenate([_gqa_outputs(outs[b * N_KV_HEADS:(b + 1) * N_KV_HEADS], nq)
                            for b in range(nb)], axis=0)
    memo = jnp.concatenate([jnp.concatenate(mem[b * MEM_HEADS:(b + 1) * MEM_HEADS], axis=1)
                            for b in range(nb)], axis=0)
    x2_ref[...] = _merge_post(x1, pre, attn, memo, wab_ref, wmb_ref, wo_ref, gpost_ref)


def _mixer_sample(q, kk2, vv2, bias, sink, qm, mk, mv, x1, y, merge_w):
    b, s, _ = q.shape
    nb = ROW_TILE // s
    nk = kk2.shape[1]
    blk = lambda *shape: pl.BlockSpec((nb,) + shape, lambda i: (i,) + (0,) * len(shape))
    row = lambda w: pl.BlockSpec((nb * s, w), lambda i: (i, 0))
    return pl.pallas_call(
        _mixer_sample_body,
        grid=(b // nb,),
        in_specs=[blk(s, ATTN_W), blk(nk, KV2_W), blk(nk, KV2_W),
                  _resident(bias.shape), _resident(sink.shape),
                  blk(s, MEM_W), blk(MEM_LEN, MEM_W), blk(MEM_LEN, MEM_W), row(D_MODEL), row(SSM_W)]
        + _merge_weight_specs(merge_w),
        out_specs=row(D_MODEL),
        out_shape=jax.ShapeDtypeStruct((b * s, D_MODEL), F32),
        compiler_params=_params("parallel"),
        name="mixer_sample",
    )(q, kk2, vv2, bias, sink, qm, mk, mv, x1, y, *merge_w)


def _ssm_weights_body(lam_re_row, lam_im_row, lam_re_col, lam_im_col, log_dt, b_re_t, b_im_t,
                      b_re_rows, b_im_rows, c_re_rows, c_im_rows, c_re_lanes, c_im_lanes, sel_ref,
                      wconv_ref, wst_ref, wot_ref, a_tab_ref):
    wst_ref[0] = jnp.zeros(wst_ref.shape[1:], BF16)
    wot_ref[0] = jnp.zeros(wot_ref.shape[1:], BF16)
    krows = []
    for gl in range(SSM_GB):
        krows.append(_ssm_group_weights(
            gl, *(r[gl] for r in (lam_re_row, lam_im_row, lam_re_col, lam_im_col, log_dt, b_re_t, b_im_t,
                                  b_re_rows, b_im_rows, c_re_rows, c_im_rows, c_re_lanes, c_im_lanes)),
            wst_ref, wot_ref, a_tab_ref))
    kstack = jnp.concatenate(krows, axis=0).astype(BF16)
    for d in range(SSM_NT):
        for ti in range(SSM_TPT):
            blk = _mm(kstack, sel_ref[d * SSM_TPT + ti])
            r0 = (SSM_NT - 1 - d) * MXU_TILE + ti * LANES
            for gl in range(SSM_GB):
                piece = blk[gl * SSM_GROUP:(gl + 1) * SSM_GROUP]
                piece = piece if gl == 0 else pltpu.roll(piece, gl * SSM_GROUP, 1)
                wconv_ref[0, r0 + gl * SSM_GROUP:r0 + (gl + 1) * SSM_GROUP, :] = piece.astype(BF16)


def _ssm_group_weights(gl, lam_re_row, lam_im_row, lam_re_col, lam_im_col, log_dt, b_re_t, b_im_t,
                       b_re_rows, b_im_rows, c_re_rows, c_im_rows, c_re_lanes, c_im_lanes,
                       wst_ref, wot_ref, a_tab_ref):
    dt = jnp.exp(log_dt)
    rows = lambda t: slice(t * LANES + gl * SSM_GROUP, t * LANES + (gl + 1) * SSM_GROUP)
    lanes = slice(gl * SSM_S2, (gl + 1) * SSM_S2)

    def zoh_coef(lr, li):
        mag = jnp.exp(lr * dt)
        a_re, a_im = mag * jnp.cos(li * dt), mag * jnp.sin(li * dt)
        den = lr * lr + li * li
        return ((a_re - 1.0) * lr + a_im * li) / den, (a_im * lr - (a_re - 1.0) * li) / den

    def a_power(lr, li, n):
        mag = jnp.exp(lr * dt * n)
        return mag * jnp.cos(li * dt * n), mag * jnp.sin(li * dt * n)

    lr, li = lam_re_row, lam_im_row
    cr, ci = zoh_coef(lr, li)
    t_row = (lax.broadcasted_iota(jnp.int32, (SSM_K, 1), 0) // SSM_GROUP).astype(F32)
    pr, pi = a_power(lr, li, (SSM_T - 1) - t_row)
    zr, zi = pr * cr - pi * ci, pr * ci + pi * cr
    br, bi = b_re_rows, b_im_rows
    wstate = jnp.concatenate([zr * br - zi * bi, zr * bi + zi * br], axis=1).astype(BF16)

    qr, qi = a_power(lr, li, t_row + 1.0)
    ccr, cci = c_re_rows, c_im_rows
    wout_t = jnp.concatenate([ccr * qr - cci * qi, -(ccr * qi + cci * qr)], axis=1).astype(BF16)
    for t in range(SSM_T):
        wst_ref[0, rows(t), lanes] = wstate[t * SSM_GROUP:(t + 1) * SSM_GROUP]
        wot_ref[0, rows(t), lanes] = wout_t[t * SSM_GROUP:(t + 1) * SSM_GROUP]

    lrc, lic = lam_re_col, lam_im_col
    t_lane = (lax.broadcasted_iota(jnp.int32, (1, SSM_K), 1) // SSM_GROUP).astype(F32)
    gr, gi = a_power(lrc, lic, t_lane)
    clr, cli = c_re_lanes, c_im_lanes
    g_re, g_im = clr * gr - cli * gi, clr * gi + cli * gr
    btr, bti = b_re_t, b_im_t
    bbr, bbi = cr * btr - ci * bti, cr * bti + ci * btr
    hi = lax.Precision.HIGHEST
    krow = (jnp.dot(bbr, g_re, precision=hi, preferred_element_type=F32)
            - jnp.dot(bbi, g_im, precision=hi, preferred_element_type=F32))

    idx = lax.broadcasted_iota(jnp.int32, (N_SCAN_TABLES * SCAN_ROWS, SSM_STATE), 0)
    tab, r = idx // SCAN_ROWS, idx % SCAN_ROWS
    stride = jnp.where(tab < 2, 1, jnp.where(tab < 4, 2, 4))
    n = jnp.where(tab < 6, stride, jnp.where(tab < 8, r, SCAN_ROWS))
    keep = jnp.logical_or(tab >= 6, r >= stride)
    er, ei = a_power(lr, li, (n * SSM_T).astype(F32))
    er, ei = jnp.where(keep, er, 0.0), jnp.where(keep, ei, 0.0)
    odd = tab % 2 == 1
    tabs = jnp.concatenate([jnp.where(odd, -ei, er), jnp.where(odd, ei, er)], axis=1)
    a_tab_ref[:, :, lanes] = tabs.reshape(N_SCAN_TABLES, SCAN_ROWS, SSM_S2)
    return krow


def _lag_selectors():
    sel = np.zeros((SSM_NT, SSM_TPT, SSM_K, MXU_TILE), np.float32)
    ch = np.arange(SSM_GROUP)
    for d in range(SSM_NT):
        for ti in range(SSM_TPT):
            for to in range(SSM_TPT):
                lag = SSM_TPT * d + to - ti
                if lag >= 0:
                    sel[d, ti, lag * SSM_GROUP + ch, to * LANES + ch] = 1.0
    return jnp.asarray(sel.reshape(SSM_NT * SSM_TPT, SSM_K, MXU_TILE), BF16)


def _ssm_weights(lam_re, lam_im, log_dt, b_re, b_im, c_re, c_im):
    g, p, c = b_re.shape
    row3 = lambda x: x.reshape(g, 1, p)
    col3 = lambda x: x.reshape(g, p, 1)
    t3 = lambda x: jnp.transpose(x, (0, 2, 1))
    b_rows = lambda x: jnp.tile(t3(x), (1, SSM_T, 1))
    c_rows = lambda x: jnp.tile(x, (1, SSM_T, 1))
    c_lanes = lambda x: jnp.tile(t3(x), (1, 1, SSM_T))
    args = (row3(lam_re), row3(lam_im), col3(lam_re), col3(lam_im), log_dt.reshape(g, 1, 1),
            t3(b_re), t3(b_im), b_rows(b_re), b_rows(b_im), c_rows(c_re), c_rows(c_im),
            c_lanes(c_re), c_lanes(c_im))
    sel = _lag_selectors()
    spec = lambda x: pl.BlockSpec((SSM_GB,) + x.shape[1:], lambda i: (i, 0, 0))
    per_gb = lambda *shape: pl.BlockSpec((1,) + shape, lambda i: (i, 0, 0))
    return pl.pallas_call(
        _ssm_weights_body,
        grid=(SSM_NGB,),
        in_specs=[spec(a) for a in args] + [_resident(sel.shape)],
        out_specs=[per_gb(SSM_NT * MXU_TILE, MXU_TILE), per_gb(SSM_XW, SSM_SW), per_gb(SSM_XW, SSM_SW),
                   pl.BlockSpec((N_SCAN_TABLES, SCAN_ROWS, SSM_SW), lambda i: (0, 0, i))],
        out_shape=[jax.ShapeDtypeStruct((SSM_NGB, SSM_NT * MXU_TILE, MXU_TILE), BF16),
                   jax.ShapeDtypeStruct((SSM_NGB, SSM_XW, SSM_SW), BF16),
                   jax.ShapeDtypeStruct((SSM_NGB, SSM_XW, SSM_SW), BF16),
                   jax.ShapeDtypeStruct((N_SCAN_TABLES, SCAN_ROWS, SSM_WIDTH), F32)],
        compiler_params=_params("parallel"),
        name="ssm_weights",
    )(*args, sel)


def _swap_halves(s):
    ax = s.ndim - 1
    return jnp.concatenate([pltpu.roll(s[..., l:l + SSM_S2], SSM_STATE, ax)
                            for l in range(0, s.shape[ax], SSM_S2)], axis=ax)


def _ssm_body(row_sets, slab_rows, u_ref, wconv_ref, wst_ref, wot_ref, d_ref, a_tab_ref, s0_ref,
              y_ref, s_ref, v_scr, s_scr):
    def piece(ref_set, first, n, stride, t):
        return ref_set, pl.ds(first + t, n, stride=stride)

    @pl.when(pl.program_id(1) == 0)
    def _():
        s_scr[...] = s0_ref[0]

    x = jnp.concatenate(
        [jnp.concatenate([u_ref[piece(*rs, t)] for t in range(SSM_T)], axis=1) for rs in row_sets],
        axis=0).astype(BF16)
    v_scr[...] = _mm(x, wst_ref[0])

    n_rows = v_scr.shape[0]
    if slab_rows:
        a_mul, a_swap = a_tab_ref[6, 1:2, :], a_tab_ref[7, 1:2, :]
        s = s_scr[...]
        for c in range(n_rows // slab_rows):
            rows = slice(c * slab_rows, (c + 1) * slab_rows)
            inc = v_scr[rows, :]
            v_scr[rows, :] = s
            s = a_mul * s + a_swap * _swap_halves(s) + inc
        s_scr[...] = s
    else:
        n_blocks = n_rows // SCAN_ROWS
        xs = v_scr[...].reshape(n_blocks, SCAN_ROWS, SSM_SW)
        for k in range(3):
            sh = pltpu.roll(xs, 1 << k, 1)
            xs = xs + a_tab_ref[2 * k] * sh + a_tab_ref[2 * k + 1] * _swap_halves(sh)
        carries, blk = [], 0
        for j, (_, _, n, _) in enumerate(row_sets):
            carry = s_scr[j * SCAN_ROWS:(j + 1) * SCAN_ROWS, :]
            for _ in range(n // SCAN_ROWS):
                carries.append(carry)
                last = jnp.broadcast_to(xs[blk, SCAN_ROWS - 1:, :], carry.shape)
                carry = last + a_tab_ref[8] * carry + a_tab_ref[9] * _swap_halves(carry)
                blk += 1
            s_scr[j * SCAN_ROWS:(j + 1) * SCAN_ROWS, :] = carry
        carries = jnp.stack(carries, axis=0)
        not_first = lax.broadcasted_iota(jnp.int32, (1, SCAN_ROWS, 1), 1) >= 1
        enter = (jnp.where(not_first, pltpu.roll(xs, 1, 1), 0.0)
                 + a_tab_ref[6] * carries + a_tab_ref[7] * _swap_halves(carries))
        v_scr[...] = enter.reshape(n_rows, SSM_SW)
    s_ref[0] = s_scr[...]

    enter = v_scr[...].astype(BF16)
    d = d_ref[...]
    for jo in range(SSM_NT):
        k_hi = (jo + 1) * MXU_TILE
        yt = _mm(x[:, :k_hi], wconv_ref[0, (SSM_NT - 1 - jo) * MXU_TILE:, :])
        yt = yt + _qk(enter, wot_ref[0, jo * MXU_TILE:(jo + 1) * MXU_TILE, :])
        r0 = 0
        for rs in row_sets:
            for tl in range(SSM_TPT):
                idx = piece(*rs, jo * SSM_TPT + tl)
                y_ref[idx] = yt[r0:r0 + rs[2], tl * LANES:(tl + 1) * LANES] + u_ref[idx] * d
            r0 += rs[2]


def _ssm(u, row_sets, slab_rows, block_rows, wconv, wst, wo, d_lanes, a_tab, s0):
    ns, r, _ = u.shape
    chunk_rows = sum(rs[2] for rs in row_sets)
    carry_rows = s0.shape[1]
    blk = pl.BlockSpec((ns, block_rows, LANES), lambda gb, i: (0, i, gb))
    per_gb = lambda x: pl.BlockSpec((1,) + x.shape[1:], lambda gb, i: (gb, 0, 0))
    return pl.pallas_call(
        functools.partial(_ssm_body, row_sets, slab_rows),
        grid=(SSM_NGB, r // block_rows),
        in_specs=[blk, per_gb(wconv), per_gb(wst), per_gb(wo),
                  pl.BlockSpec((1, LANES), lambda gb, i: (0, gb)),
                  pl.BlockSpec((N_SCAN_TABLES, SCAN_ROWS, SSM_SW), lambda gb, i: (0, 0, gb)),
                  per_gb(s0)],
        out_specs=[blk, per_gb(s0)],
        out_shape=[jax.ShapeDtypeStruct(u.shape, F32), jax.ShapeDtypeStruct(s0.shape, F32)],
        scratch_shapes=[pltpu.VMEM((chunk_rows, SSM_SW), F32), pltpu.VMEM((carry_rows, SSM_SW), F32)],
        compiler_params=_params("parallel", "arbitrary"),
        name="ssm_scan",
    )(u, wconv, wst, wo, d_lanes, a_tab, s0)


def _ssm_branch(u, s0, ssm_w):
    b, l, _ = u.shape
    nc = l // SSM_T
    by_gb = lambda s: s.reshape(s.shape[0], SSM_NGB, SSM_SW).transpose(1, 0, 2)
    if b % SCAN_ROWS == 0:
        row_sets = tuple((0, c * SSM_T, b, l) for c in range(nc))
        y, s_last = _ssm(u.reshape(1, b * l, SSM_W), row_sets, b, b * l, *ssm_w, by_gb(s0))
    else:
        cps = min(nc, SSM_CHUNKS_PER_STEP)
        row_sets = tuple((j, 0, cps, SSM_T) for j in range(b))
        y, s_last = _ssm(u, row_sets, 0, cps * SSM_T, *ssm_w, by_gb(jnp.repeat(s0, SCAN_ROWS, axis=0)))
        s_last = s_last[:, ::SCAN_ROWS]
    s_last = s_last.transpose(1, 0, 2).reshape(b, SSM_GROUPS, 2, SSM_STATE)
    return y.reshape(b, l, SSM_W), s_last[:, :, 0], s_last[:, :, 1]


def _ffn2_body(x_ref, gpre_ref, gpost_ref, wfi_ref, wfo_ref, o_ref):
    o_ref[...] = _ffn(x_ref[...], gpre_ref[...], gpost_ref[...], wfi_ref, wfo_ref)


def _ffn2(x, gpre, gpost, wfi, wfo):
    n = x.shape[0]
    row = pl.BlockSpec((FFN_ROW_TILE, D_MODEL), lambda i: (i, 0))
    vec = _resident((1, D_MODEL))
    return pl.pallas_call(
        _ffn2_body,
        grid=(n // FFN_ROW_TILE,),
        in_specs=[row, vec, vec, _resident(wfi.shape), _resident(wfo.shape)],
        out_specs=row,
        out_shape=jax.ShapeDtypeStruct((n, D_MODEL), F32),
        compiler_params=_params("parallel"),
        name="ffn2",
    )(x, gpre, gpost, wfi, wfo)


def _t5_bucket(rel):
    half = N_BUCKETS // 2
    max_exact = half // 2
    ret = (rel > 0).astype(np.int32) * half
    n = np.abs(rel)
    large = max_exact + (np.log(np.maximum(n, 1) / max_exact) / math.log(MAX_DISTANCE / max_exact)
                         * (half - max_exact)).astype(np.int32)
    large = np.minimum(large, half - 1)
    return ret + np.where(n < max_exact, n, large)


def _band_bias(rel_table, n_q, n_back, n_k):
    i = np.arange(n_q)[:, None]
    j = np.arange(n_k)[None, :]
    bucket = _t5_bucket((j - n_back) - i).reshape(-1)
    onehot = np.zeros((N_BUCKETS, bucket.size), np.float32)
    onehot[bucket, np.arange(bucket.size)] = 1.0
    b = jnp.dot(rel_table.astype(F32).T, jnp.asarray(onehot), precision=lax.Precision.HIGHEST)
    return b.reshape(N_KV_HEADS, KV_REP * n_q, n_k)


def _sink_rows(sink, n_q):
    return jnp.repeat(sink.astype(F32).reshape(N_KV_HEADS, KV_REP), n_q, axis=1)[:, :, None]


def _twice_per_head(x, axis):
    shape = x.shape
    x = x.reshape(shape[:axis] + (N_KV_HEADS, 1, HEAD_DIM) + shape[axis + 1:])
    x = jnp.concatenate([x, x], axis=axis + 1)
    return x.reshape(shape[:axis] + (KV2_W,) + shape[axis + 1:])


def kernel(x_prompt, x_sample, cache_swa_k, cache_swa_v, cache_mem_k, cache_mem_v, state_ssm_re, state_ssm_im, mem_prompt, rel_bias_table, ff1_pre_g, ff1_post_g, w_ff1_in, w_ff1_out, mix_pre_g, mix_post_g, w_in, mem_norm_g, w_mem_kv, attn_sink, ssm_lambda_re, ssm_lambda_im, ssm_log_dt, ssm_b_re, ssm_b_im, ssm_c_re, ssm_c_im, ssm_d, w_ssm_glu, w_attn_br, w_mem_br, w_out, ff2_pre_g, ff2_post_g, w_ff2_in, w_ff2_out):
    assert ff1_pre_g.shape[0] == 1, "single-layer step"
    bp, lp, _ = x_prompt.shape
    bs, ls, _ = x_sample.shape
    vec = lambda g: g[0].reshape(1, D_MODEL).astype(F32)
    w16 = lambda w: w[0].astype(BF16)

    wfi1, wfo1, wfi2, wfo2 = w16(w_ff1_in), w16(w_ff1_out), w16(w_ff2_in), w16(w_ff2_out)
    w_in16 = w16(w_in)
    c_k, c_v, c_u = ATTN_W, ATTN_W + KV_W, ATTN_W + 2 * KV_W
    c_g = c_u + SSM_W + MEM_W
    q_scale = HEAD_DIM ** -0.5
    assert math.frexp(q_scale)[0] == 0.5, "power-of-two scale: folding it into the q columns is exact"
    wp = jnp.concatenate([w_in16[:, :c_k] * jnp.asarray(q_scale, BF16), w_in16[:, c_k:c_g],
                          _twice_per_head(w_in16[:, c_k:c_v], 1), _twice_per_head(w_in16[:, c_v:c_u], 1)], axis=1)
    wg = w_in16[:, c_g:]
    wab, wglu, wmb, wo = w16(w_attn_br), w16(w_ssm_glu), w16(w_mem_br), w16(w_out)

    wconv, wst, wot, a_tab = _ssm_weights(
        ssm_lambda_re[0], ssm_lambda_im[0], ssm_log_dt[0], ssm_b_re[0], ssm_b_im[0], ssm_c_re[0], ssm_c_im[0])
    ssm_w = (wconv, wst, wot, ssm_d[0].astype(F32).reshape(1, SSM_W), a_tab)

    mk_p, mv_p = _mem_kv(mem_prompt.reshape(bp * MEM_LEN, D_MODEL), vec(mem_norm_g), w16(w_mem_kv))

    def group(x, mixer, s0, mem_k, mem_v):
        b, l, _ = x.shape
        r3 = lambda t: t.reshape(b, l, t.shape[-1])
        x1, q, k, v, u, qm, k2, v2 = _ffn1_proj(x.reshape(b * l, D_MODEL), vec(ff1_pre_g), vec(ff1_post_g),
                                                wfi1, wfo1, vec(mix_pre_g), wp)
        y_ssm, s_re, s_im = _ssm_branch(r3(u), s0, ssm_w)
        x2 = mixer(r3(q), r3(k2), r3(v2), r3(qm), mem_k, mem_v, x1, y_ssm.reshape(b * l, SSM_W))
        y = _ffn2(x2, vec(ff2_pre_g), vec(ff2_post_g), wfi2, wfo2)
        return r3(y), r3(k), r3(v), s_re, s_im

    merge_w = (vec(mix_pre_g), wg, wglu, wab, wmb, wo, vec(mix_post_g))

    def mixer_prompt(q, k2, v2, qm, mem_k, mem_v, x1, y_ssm):
        bias = _band_bias(rel_bias_table, CHUNK, WINDOW, WINDOW + CHUNK)
        return _mixer_prompt(q, k2, v2, bias, _sink_rows(attn_sink[0], CHUNK), qm, mem_k, mem_v, x1, y_ssm, merge_w)

    def mixer_sample(q, k2, v2, qm, mem_k, mem_v, x1, y_ssm):
        n_back = cache_swa_k.shape[2]
        cache2 = lambda c: _twice_per_head(c[0].reshape(bs, n_back, KV_W).astype(BF16), 2)
        kk = jnp.concatenate([cache2(cache_swa_k), k2], axis=1)
        vv = jnp.concatenate([cache2(cache_swa_v), v2], axis=1)
        bias = _band_bias(rel_bias_table, ls, n_back, n_back + ls)
        return _mixer_sample(q, kk, vv, bias, _sink_rows(attn_sink[0], ls), qm, mem_k, mem_v, x1, y_ssm, merge_w)

    yp, pk, pv, pre, pim = group(x_prompt, mixer_prompt, jnp.zeros((bp, SSM_WIDTH), F32),
                                 mk_p.reshape(bp, MEM_LEN, MEM_W), mv_p.reshape(bp, MEM_LEN, MEM_W))
    s0 = jnp.stack([state_ssm_re[0], state_ssm_im[0]], axis=2).reshape(bs, SSM_WIDTH).astype(F32)
    ys, sk, sv, sre, sim = group(x_sample, mixer_sample, s0,
                                 cache_mem_k[0].reshape(bs, MEM_LEN, MEM_W),
                                 cache_mem_v[0].reshape(bs, MEM_LEN, MEM_W))

    n_keep = min(WINDOW, lp)
    heads = lambda t: t.reshape(t.shape[0], t.shape[1], N_KV_HEADS, HEAD_DIM)[None]
    mem_heads = lambda t: t.reshape(bp, MEM_LEN, MEM_HEADS, MEM_HEAD_DIM)[None]
    return (yp, ys, heads(pk[:, -n_keep:]), heads(pv[:, -n_keep:]), mem_heads(mk_p), mem_heads(mv_p),
            pre[None], pim[None], heads(sk), heads(sv), sre[None], sim[None])
```

```python
import functools
import math

import numpy as np
import jax
import jax.numpy as jnp
from jax import lax
from jax.experimental import pallas as pl
from jax.experimental.pallas import tpu as pltpu

D_MODEL = 1024
CHUNK = 64
WINDOW = 128
HEAD_DIM = 64
MIX_W = D_MODEL // 2
N_HEADS = MIX_W // HEAD_DIM
N_KV_HEADS = N_HEADS // 4
KV_REP = N_HEADS // N_KV_HEADS
ATTN_W = N_HEADS * HEAD_DIM
KV_W = N_KV_HEADS * HEAD_DIM
SSM_GROUP = 16
SSM_W = MIX_W
SSM_GROUPS = SSM_W // SSM_GROUP
SSM_STATE = 64
MEM_LEN = 256
MEM_HEADS = 4
MEM_HEAD_DIM = MIX_W // MEM_HEADS
MEM_W = MEM_HEADS * MEM_HEAD_DIM
D_FF = 128 * ((8 * D_MODEL // 3 + 127) // 128)
N_BUCKETS = 32
MAX_DISTANCE = 128
RMS_EPS = 1e-6
NEG_INF = -1e30

LANES = 128
MXU_TILE = 256
ROW_TILE = 256
MIXER_TILE = 512
SAMPLE_MIXER_TILE = 256
MIXER_PARTS = 2
FFN_ROW_TILE = 512
FFN_PARTS = 2
FFN_CHUNK = 2 * MXU_TILE
VMEM_LIMIT = 60 * 1024 * 1024

KV2_W = N_KV_HEADS * LANES

SSM_T = 16
SSM_K = SSM_T * SSM_GROUP
SSM_S2 = 2 * SSM_STATE
SSM_WIDTH = SSM_GROUPS * SSM_S2
SSM_GB = LANES // SSM_GROUP
SSM_NGB = SSM_GROUPS // SSM_GB
SSM_XW = SSM_T * LANES
SSM_SW = SSM_GB * SSM_S2
SSM_TPT = MXU_TILE // LANES
SSM_NT = SSM_T // SSM_TPT
SCAN_ROWS = 8
N_SCAN_TABLES = 10
SSM_CHUNKS_PER_STEP = 128

F32 = jnp.float32
BF16 = jnp.bfloat16


def _params(*sem):
    return pltpu.CompilerParams(dimension_semantics=sem, vmem_limit_bytes=VMEM_LIMIT)


def _resident(shape):
    zeros = (0,) * len(shape)
    return pl.BlockSpec(shape, lambda *_: zeros, pipeline_mode=pl.Buffered(1))


def _rms(x, g):
    return x * lax.rsqrt(jnp.mean(x * x, axis=-1, keepdims=True) + RMS_EPS) * g


def _mm(a, b):
    return jnp.dot(a, b, preferred_element_type=F32)


def _row_parts(ref, n_parts):
    rows = ref.shape[0] // n_parts
    return [ref[i * rows:(i + 1) * rows, :] for i in range(n_parts)]


def _ffn(xs, gpre, gpost, w_in_ref, w_out_ref):
    hs = [_rms(x, gpre).astype(BF16) for x in xs]
    acts = [[] for _ in xs]
    for c0 in range(0, D_FF, FFN_CHUNK):
        c1 = min(c0 + FFN_CHUNK, D_FF)
        for h, a in zip(hs, acts):
            g = _mm(h, w_in_ref[:, c0:c1])
            u = _mm(h, w_in_ref[:, D_FF + c0:D_FF + c1])
            a.append((g * jax.nn.sigmoid(g) * u).astype(BF16))
    outs = [_mm(jnp.concatenate(a, axis=1), w_out_ref[...]) for a in acts]
    return [x + 0.5 * _rms(o, gpost) for x, o in zip(xs, outs)]


PROJ_SPLIT = (("q", ATTN_W, BF16), ("k", KV_W, F32), ("v", KV_W, F32), ("u", SSM_W, F32),
              ("qm", MEM_W, BF16), ("k2", KV2_W, BF16), ("v2", KV2_W, BF16))


def _ffn1_proj_body(x_ref, gpre_ref, gpost_ref, wfi_ref, wfo_ref, gmix_ref, wp_ref, x1_ref, *out_refs):
    x1s = _ffn(_row_parts(x_ref, FFN_PARTS), gpre_ref[...], gpost_ref[...], wfi_ref, wfo_ref)
    rows = x_ref.shape[0] // FFN_PARTS
    ps = [_mm(_rms(x1, gmix_ref[...]).astype(BF16), wp_ref[...]) for x1 in x1s]
    for i, (x1, p) in enumerate(zip(x1s, ps)):
        part = slice(i * rows, (i + 1) * rows)
        x1_ref[part, :] = x1
        col = 0
        for ref, (_, width, dtype) in zip(out_refs, PROJ_SPLIT):
            ref[part, :] = p[:, col:col + width].astype(dtype)
            col += width


def _ffn1_proj(x, gpre, gpost, wfi, wfo, gmix, wp):
    n = x.shape[0]
    row = lambda w: pl.BlockSpec((FFN_ROW_TILE, w), lambda i: (i, 0))
    return pl.pallas_call(
        _ffn1_proj_body,
        grid=(n // FFN_ROW_TILE,),
        in_specs=[row(D_MODEL), _resident((1, D_MODEL)), _resident((1, D_MODEL)),
                  _resident(wfi.shape), _resident(wfo.shape), _resident((1, D_MODEL)),
                  _resident(wp.shape)],
        out_specs=[row(D_MODEL)] + [row(w) for _, w, _ in PROJ_SPLIT],
        out_shape=[jax.ShapeDtypeStruct((n, D_MODEL), F32)]
        + [jax.ShapeDtypeStruct((n, w), d) for _, w, d in PROJ_SPLIT],
        compiler_params=_params("parallel"),
        name="ffn1_proj",
    )(x, gpre, gpost, wfi, wfo, gmix, wp)


def _mem_kv_body(m_ref, g_ref, w_ref, k_ref, v_ref):
    kv = _mm(_rms(m_ref[...], g_ref[...]).astype(BF16), w_ref[...])
    k_ref[...] = kv[:, :MEM_W]
    v_ref[...] = kv[:, MEM_W:]


def _mem_kv(mem, g, w):
    n = mem.shape[0]
    row = lambda wd: pl.BlockSpec((ROW_TILE, wd), lambda i: (i, 0))
    return pl.pallas_call(
        _mem_kv_body,
        grid=(n // ROW_TILE,),
        in_specs=[row(D_MODEL), _resident((1, D_MODEL)), _resident(w.shape)],
        out_specs=[row(MEM_W), row(MEM_W)],
        out_shape=[jax.ShapeDtypeStruct((n, MEM_W), F32)] * 2,
        compiler_params=_params("parallel"),
        name="mem_kv",
    )(mem, g, w)


def _qk(q, k):
    return lax.dot_general(q, k, (((1,), (1,)), ((), ())), preferred_element_type=F32)


def _softmax_pv(scores, values, sinks=None):
    s = jnp.concatenate(scores, axis=0)
    m = jnp.max(s, axis=-1, keepdims=True)
    if sinks is not None:
        sink = jnp.concatenate(sinks, axis=0)
        m = jnp.maximum(m, sink)
    e = jnp.exp(s - m)
    den = jnp.sum(e, axis=-1, keepdims=True)
    if sinks is not None:
        den = den + jnp.exp(sink - m)
    p = (e / den).astype(BF16)
    outs, r0 = [], 0
    for sc, v in zip(scores, values):
        outs.append(_mm(p[r0:r0 + sc.shape[0]], v))
        r0 += sc.shape[0]
    return outs


def _gqa_queries(q):
    nq = q.shape[0]
    low = lax.broadcasted_iota(jnp.int32, (nq, LANES), 1) < HEAD_DIM
    zero = jnp.zeros((nq, LANES), BF16)
    stacks = []
    for g in range(N_KV_HEADS):
        rows = []
        for r in range(KV_REP):
            h = g * KV_REP + r
            q2 = q[:, (h // 2) * LANES:(h // 2 + 1) * LANES]
            rows.append(jnp.where(low, q2, zero) if h % 2 == 0 else jnp.where(low, zero, q2))
        stacks.append(jnp.concatenate(rows, axis=0))
    return stacks


def _gqa_outputs(outs, nq):
    low = lax.broadcasted_iota(jnp.int32, (nq, LANES), 1) < HEAD_DIM
    pairs = []
    for o in outs:
        for r in range(0, KV_REP, 2):
            pairs.append(jnp.where(low, o[r * nq:(r + 1) * nq], o[(r + 1) * nq:(r + 2) * nq]))
    return jnp.concatenate(pairs, axis=1)


def _mem_scores(qm, mk_ref):
    return [_qk(qm[:, h * MEM_HEAD_DIM:(h + 1) * MEM_HEAD_DIM],
                mk_ref[:, h * MEM_HEAD_DIM:(h + 1) * MEM_HEAD_DIM].astype(BF16)) * (MEM_HEAD_DIM ** -0.5)
            for h in range(MEM_HEADS)]


def _mem_values(mv_ref):
    return [mv_ref[:, h * MEM_HEAD_DIM:(h + 1) * MEM_HEAD_DIM].astype(BF16) for h in range(MEM_HEADS)]


def _kv_heads(x):
    return [x[:, g * LANES:(g + 1) * LANES] for g in range(N_KV_HEADS)]


def _merge_pre(x1, y, gmix_ref, wg_ref, wglu_ref):
    h = _rms(x1, gmix_ref[...]).astype(BF16)
    logits = [_mm(h, wg_ref[:, j * D_MODEL:(j + 1) * D_MODEL]) for j in range(3)]
    y = y.astype(BF16)
    return logits, _mm(y, wglu_ref[:, :D_MODEL]), _mm(y, wglu_ref[:, D_MODEL:])


def _mix_parts(parts, x1_ref, y_ref, gmix_ref, wg_ref, wglu_ref, wab_ref, wmb_ref, wo_ref, gpost_ref, x2_ref):
    x1s, ys = _row_parts(x1_ref, len(parts)), _row_parts(y_ref, len(parts))
    pres = [_merge_pre(x1, y, gmix_ref, wg_ref, wglu_ref) for x1, y in zip(x1s, ys)]
    branches = [assemble(_softmax_pv(scores, values, sinks), _softmax_pv(mem_scores, mem_values))
                for scores, values, sinks, mem_scores, mem_values, assemble in parts]
    projected = [(_mm(attn.astype(BF16), wab_ref[...]), _mm(memo.astype(BF16), wmb_ref[...]))
                 for attn, memo in branches]
    sig = jax.nn.sigmoid
    merged = [sig(lg[0]) * pa + sig(lg[1]) * (ya * sig(yb)) + sig(lg[2]) * pm
              for (lg, ya, yb), (pa, pm) in zip(pres, projected)]
    outs = [_mm(m.astype(BF16), wo_ref[...]) for m in merged]
    rows = x1_ref.shape[0] // len(parts)
    for i, (x1, o) in enumerate(zip(x1s, outs)):
        x2_ref[i * rows:(i + 1) * rows, :] = x1 + _rms(o, gpost_ref[...])


def _mixer_prompt_body(q_ref, k_ref, kh_ref, v_ref, vh_ref, bias_ref, sink_ref, qm_ref, mk_ref, mv_ref,
                       x1_ref, y_ref, *merge_and_out_refs):
    i = pl.program_id(1)
    tq = q_ref.shape[1]
    kf = jnp.concatenate([kh_ref[0], k_ref[0]], axis=0)
    vf = jnp.concatenate([vh_ref[0], v_ref[0]], axis=0)
    band = WINDOW + CHUNK
    rows = tq // MIXER_PARTS
    parts = []
    for p in range(MIXER_PARTS):
        scores, values, sinks = [], [], []
        for lo in range(p * rows, (p + 1) * rows, CHUNK):
            valid = None
            if lo < WINDOW:
                key_pos = lax.broadcasted_iota(jnp.int32, (1, band), 1) + (i * tq + lo - WINDOW)
                valid = key_pos >= 0
            kb, vb = _kv_heads(kf[lo:lo + band]), _kv_heads(vf[lo:lo + band])
            for g, qs in enumerate(_gqa_queries(q_ref[0, lo:lo + CHUNK, :])):
                s = _qk(qs, kb[g]) + bias_ref[g]
                scores.append(s if valid is None else jnp.where(valid, s, NEG_INF))
                values.append(vb[g])
                sinks.append(sink_ref[g])
        mem_scores = _mem_scores(qm_ref[0, p * rows:(p + 1) * rows, :], mk_ref.at[0])

        def assemble(outs, mem):
            attn = jnp.concatenate([_gqa_outputs(outs[c:c + N_KV_HEADS], CHUNK)
                                    for c in range(0, len(outs), N_KV_HEADS)], axis=0)
            return attn, jnp.concatenate(mem, axis=1)

        parts.append((scores, values, sinks, mem_scores, _mem_values(mv_ref.at[0]), assemble))
    _mix_parts(parts, x1_ref, y_ref, *merge_and_out_refs)


def _merge_weight_specs(weights):
    return [_resident(w.shape) for w in weights]


def _mixer_prompt(q, k2, v2, bias, sink, qm, mk, mv, x1, y, merge_w):
    b, l, _ = q.shape
    tq = MIXER_TILE
    hb = tq // WINDOW
    nt = l // tq
    cur = lambda w: pl.BlockSpec((1, tq, w), lambda bi, i: (bi, i, 0))
    row = lambda w: pl.BlockSpec((tq, w), lambda bi, i: (bi * nt + i, 0))
    halo = pl.BlockSpec((1, WINDOW, KV2_W), lambda bi, i: (bi, jnp.maximum(i * hb - 1, 0), 0))
    mem = pl.BlockSpec((1, MEM_LEN, MEM_W), lambda bi, i: (bi, 0, 0))
    return pl.pallas_call(
        _mixer_prompt_body,
        grid=(b, nt),
        in_specs=[cur(ATTN_W), cur(KV2_W), halo, cur(KV2_W), halo, _resident(bias.shape),
                  _resident(sink.shape), cur(MEM_W), mem, mem, row(D_MODEL), row(SSM_W)]
        + _merge_weight_specs(merge_w),
        out_specs=row(D_MODEL),
        out_shape=jax.ShapeDtypeStruct((b * l, D_MODEL), F32),
        compiler_params=_params("parallel", "parallel"),
        name="mixer_prompt",
    )(q, k2, k2, v2, v2, bias, sink, qm, mk, mv, x1, y, *merge_w)


def _mixer_sample_body(q_ref, k_ref, v_ref, bias_ref, sink_ref, qm_ref, mk_ref, mv_ref,
                       x1_ref, y_ref, *merge_and_out_refs):
    nb, nq = q_ref.shape[0], q_ref.shape[1]
    per_part = nb // MIXER_PARTS
    parts = []
    for p in range(MIXER_PARTS):
        scores, values, sinks, mem_scores, mem_values = [], [], [], [], []
        for b in range(p * per_part, (p + 1) * per_part):
            kb, vb = _kv_heads(k_ref[b]), _kv_heads(v_ref[b])
            for g, qs in enumerate(_gqa_queries(q_ref[b])):
                scores.append(_qk(qs, kb[g]) + bias_ref[g])
                values.append(vb[g])
                sinks.append(sink_ref[g])
            mem_scores += _mem_scores(qm_ref[b], mk_ref.at[b])
            mem_values += _mem_values(mv_ref.at[b])

        def assemble(outs, mem):
            attn = jnp.concatenate([_gqa_outputs(outs[j:j + N_KV_HEADS], nq)
                                    for j in range(0, len(outs), N_KV_HEADS)], axis=0)
            memo = jnp.concatenate([jnp.concatenate(mem[j:j + MEM_HEADS], axis=1)
                                    for j in range(0, len(mem), MEM_HEADS)], axis=0)
            return attn, memo

        parts.append((scores, values, sinks, mem_scores, mem_values, assemble))
    _mix_parts(parts, x1_ref, y_ref, *merge_and_out_refs)


def _mixer_sample(q, kk2, vv2, bias, sink, qm, mk, mv, x1, y, merge_w):
    b, s, _ = q.shape
    nb = SAMPLE_MIXER_TILE // s
    nk = kk2.shape[1]
    blk = lambda *shape: pl.BlockSpec((nb,) + shape, lambda i: (i,) + (0,) * len(shape))
    row = lambda w: pl.BlockSpec((nb * s, w), lambda i: (i, 0))
    return pl.pallas_call(
        _mixer_sample_body,
        grid=(b // nb,),
        in_specs=[blk(s, ATTN_W), blk(nk, KV2_W), blk(nk, KV2_W),
                  _resident(bias.shape), _resident(sink.shape),
                  blk(s, MEM_W), blk(MEM_LEN, MEM_W), blk(MEM_LEN, MEM_W), row(D_MODEL), row(SSM_W)]
        + _merge_weight_specs(merge_w),
        out_specs=row(D_MODEL),
        out_shape=jax.ShapeDtypeStruct((b * s, D_MODEL), F32),
        compiler_params=_params("parallel"),
        name="mixer_sample",
    )(q, kk2, vv2, bias, sink, qm, mk, mv, x1, y, *merge_w)


def _ssm_weights_body(lam_re_row, lam_im_row, lam_re_col, lam_im_col, log_dt, b_re_t, b_im_t,
                      b_re_rows, b_im_rows, c_re_rows, c_im_rows, c_re_lanes, c_im_lanes, sel_ref,
                      wconv_ref, wst_ref, wot_ref, a_tab_ref):
    wst_ref[0] = jnp.zeros(wst_ref.shape[1:], BF16)
    wot_ref[0] = jnp.zeros(wot_ref.shape[1:], BF16)
    krows = []
    for gl in range(SSM_GB):
        krows.append(_ssm_group_weights(
            gl, *(r[gl] for r in (lam_re_row, lam_im_row, lam_re_col, lam_im_col, log_dt, b_re_t, b_im_t,
                                  b_re_rows, b_im_rows, c_re_rows, c_im_rows, c_re_lanes, c_im_lanes)),
            wst_ref, wot_ref, a_tab_ref))
    kstack = jnp.concatenate(krows, axis=0).astype(BF16)
    for d in range(SSM_NT):
        for ti in range(SSM_TPT):
            blk = _mm(kstack, sel_ref[d * SSM_TPT + ti])
            r0 = (SSM_NT - 1 - d) * MXU_TILE + ti * LANES
            for gl in range(SSM_GB):
                piece = blk[gl * SSM_GROUP:(gl + 1) * SSM_GROUP]
                piece = (piece if gl == 0 else pltpu.roll(piece, gl * SSM_GROUP, 1)).astype(BF16)
                r = r0 + gl * SSM_GROUP
                wconv_ref[0, r:r + SSM_GROUP, :MXU_TILE] = piece
                if r0 >= MXU_TILE:
                    wconv_ref[0, r - MXU_TILE:r - MXU_TILE + SSM_GROUP, MXU_TILE:] = piece
    wconv_ref[0, (SSM_NT - 1) * MXU_TILE:, MXU_TILE:] = jnp.zeros((MXU_TILE, MXU_TILE), BF16)


def _ssm_group_weights(gl, lam_re_row, lam_im_row, lam_re_col, lam_im_col, log_dt, b_re_t, b_im_t,
                       b_re_rows, b_im_rows, c_re_rows, c_im_rows, c_re_lanes, c_im_lanes,
                       wst_ref, wot_ref, a_tab_ref):
    dt = jnp.exp(log_dt)
    rows = lambda t: slice(t * LANES + gl * SSM_GROUP, t * LANES + (gl + 1) * SSM_GROUP)
    lanes = slice(gl * SSM_S2, (gl + 1) * SSM_S2)

    def zoh_coef(lr, li):
        mag = jnp.exp(lr * dt)
        a_re, a_im = mag * jnp.cos(li * dt), mag * jnp.sin(li * dt)
        den = lr * lr + li * li
        return ((a_re - 1.0) * lr + a_im * li) / den, (a_im * lr - (a_re - 1.0) * li) / den

    def a_power(lr, li, n):
        mag = jnp.exp(lr * dt * n)
        return mag * jnp.cos(li * dt * n), mag * jnp.sin(li * dt * n)

    lr, li = lam_re_row, lam_im_row
    cr, ci = zoh_coef(lr, li)
    t_row = (lax.broadcasted_iota(jnp.int32, (SSM_K, 1), 0) // SSM_GROUP).astype(F32)
    pr, pi = a_power(lr, li, (SSM_T - 1) - t_row)
    zr, zi = pr * cr - pi * ci, pr * ci + pi * cr
    br, bi = b_re_rows, b_im_rows
    wstate = jnp.concatenate([zr * br - zi * bi, zr * bi + zi * br], axis=1).astype(BF16)

    qr, qi = a_power(lr, li, t_row + 1.0)
    ccr, cci = c_re_rows, c_im_rows
    wout_t = jnp.concatenate([ccr * qr - cci * qi, -(ccr * qi + cci * qr)], axis=1).astype(BF16)
    for t in range(SSM_T):
        wst_ref[0, rows(t), lanes] = wstate[t * SSM_GROUP:(t + 1) * SSM_GROUP]
        wot_ref[0, rows(t), lanes] = wout_t[t * SSM_GROUP:(t + 1) * SSM_GROUP]

    lrc, lic = lam_re_col, lam_im_col
    t_lane = (lax.broadcasted_iota(jnp.int32, (1, SSM_K), 1) // SSM_GROUP).astype(F32)
    gr, gi = a_power(lrc, lic, t_lane)
    clr, cli = c_re_lanes, c_im_lanes
    g_re, g_im = clr * gr - cli * gi, clr * gi + cli * gr
    btr, bti = b_re_t, b_im_t
    bbr, bbi = cr * btr - ci * bti, cr * bti + ci * btr
    hi = lax.Precision.HIGHEST
    krow = (jnp.dot(bbr, g_re, precision=hi, preferred_element_type=F32)
            - jnp.dot(bbi, g_im, precision=hi, preferred_element_type=F32))

    idx = lax.broadcasted_iota(jnp.int32, (N_SCAN_TABLES * SCAN_ROWS, SSM_STATE), 0)
    tab, r = idx // SCAN_ROWS, idx % SCAN_ROWS
    stride = jnp.where(tab < 2, 1, jnp.where(tab < 4, 2, 4))
    n = jnp.where(tab < 6, stride, jnp.where(tab < 8, r, SCAN_ROWS))
    keep = jnp.logical_or(tab >= 6, r >= stride)
    er, ei = a_power(lr, li, (n * SSM_T).astype(F32))
    er, ei = jnp.where(keep, er, 0.0), jnp.where(keep, ei, 0.0)
    odd = tab % 2 == 1
    tabs = jnp.concatenate([jnp.where(odd, -ei, er), jnp.where(odd, ei, er)], axis=1)
    a_tab_ref[:, :, lanes] = tabs.reshape(N_SCAN_TABLES, SCAN_ROWS, SSM_S2)
    return krow


def _lag_selectors():
    sel = np.zeros((SSM_NT, SSM_TPT, SSM_K, MXU_TILE), np.float32)
    ch = np.arange(SSM_GROUP)
    for d in range(SSM_NT):
        for ti in range(SSM_TPT):
            for to in range(SSM_TPT):
                lag = SSM_TPT * d + to - ti
                if lag >= 0:
                    sel[d, ti, lag * SSM_GROUP + ch, to * LANES + ch] = 1.0
    return jnp.asarray(sel.reshape(SSM_NT * SSM_TPT, SSM_K, MXU_TILE), BF16)


def _ssm_weights(lam_re, lam_im, log_dt, b_re, b_im, c_re, c_im):
    g, p, c = b_re.shape
    row3 = lambda x: x.reshape(g, 1, p)
    col3 = lambda x: x.reshape(g, p, 1)
    t3 = lambda x: jnp.transpose(x, (0, 2, 1))
    b_rows = lambda x: jnp.tile(t3(x), (1, SSM_T, 1))
    c_rows = lambda x: jnp.tile(x, (1, SSM_T, 1))
    c_lanes = lambda x: jnp.tile(t3(x), (1, 1, SSM_T))
    args = (row3(lam_re), row3(lam_im), col3(lam_re), col3(lam_im), log_dt.reshape(g, 1, 1),
            t3(b_re), t3(b_im), b_rows(b_re), b_rows(b_im), c_rows(c_re), c_rows(c_im),
            c_lanes(c_re), c_lanes(c_im))
    sel = _lag_selectors()
    spec = lambda x: pl.BlockSpec((SSM_GB,) + x.shape[1:], lambda i: (i, 0, 0))
    per_gb = lambda *shape: pl.BlockSpec((1,) + shape, lambda i: (i, 0, 0))
    return pl.pallas_call(
        _ssm_weights_body,
        grid=(SSM_NGB,),
        in_specs=[spec(a) for a in args] + [_resident(sel.shape)],
        out_specs=[per_gb(SSM_NT * MXU_TILE, 2 * MXU_TILE), per_gb(SSM_XW, SSM_SW), per_gb(SSM_XW, SSM_SW),
                   pl.BlockSpec((N_SCAN_TABLES, SCAN_ROWS, SSM_SW), lambda i: (0, 0, i))],
        out_shape=[jax.ShapeDtypeStruct((SSM_NGB, SSM_NT * MXU_TILE, 2 * MXU_TILE), BF16),
                   jax.ShapeDtypeStruct((SSM_NGB, SSM_XW, SSM_SW), BF16),
                   jax.ShapeDtypeStruct((SSM_NGB, SSM_XW, SSM_SW), BF16),
                   jax.ShapeDtypeStruct((N_SCAN_TABLES, SCAN_ROWS, SSM_WIDTH), F32)],
        compiler_params=_params("parallel"),
        name="ssm_weights",
    )(*args, sel)


def _swap_halves(s):
    ax = s.ndim - 1
    return jnp.concatenate([pltpu.roll(s[..., l:l + SSM_S2], SSM_STATE, ax)
                            for l in range(0, s.shape[ax], SSM_S2)], axis=ax)


def _ssm_body(row_sets, slab_rows, u_ref, wconv_ref, wst_ref, wot_ref, d_ref, a_tab_ref, s0_ref,
              y_ref, s_ref, v_scr, s_scr):
    def piece(ref_set, first, n, stride, t):
        return ref_set, pl.ds(first + t, n, stride=stride)

    @pl.when(pl.program_id(1) == 0)
    def _():
        s_scr[...] = s0_ref[0]

    x = jnp.concatenate(
        [jnp.concatenate([u_ref[piece(*rs, t)] for t in range(SSM_T)], axis=1) for rs in row_sets],
        axis=0).astype(BF16)
    v_scr[...] = _mm(x, wst_ref[0])
    conv = []
    for jo in range(0, SSM_NT, 2):
        pair = _mm(x[:, :(jo + 2) * MXU_TILE], wconv_ref[0, (SSM_NT - 2 - jo) * MXU_TILE:, :])
        conv += [pair[:, MXU_TILE:], pair[:, :MXU_TILE]]

    n_rows = v_scr.shape[0]
    if slab_rows:
        a_mul, a_swap = a_tab_ref[6, 1:2, :], a_tab_ref[7, 1:2, :]
        s = s_scr[...]
        for c in range(n_rows // slab_rows):
            rows = slice(c * slab_rows, (c + 1) * slab_rows)
            inc = v_scr[rows, :]
            v_scr[rows, :] = s
            s = a_mul * s + a_swap * _swap_halves(s) + inc
        s_scr[...] = s
    else:
        n_blocks = n_rows // SCAN_ROWS
        xs = v_scr[...].reshape(n_blocks, SCAN_ROWS, SSM_SW)
        for k in range(3):
            sh = pltpu.roll(xs, 1 << k, 1)
            xs = xs + a_tab_ref[2 * k] * sh + a_tab_ref[2 * k + 1] * _swap_halves(sh)
        carries, blk = [], 0
        for j, (_, _, n, _) in enumerate(row_sets):
            carry = s_scr[j * SCAN_ROWS:(j + 1) * SCAN_ROWS, :]
            for _ in range(n // SCAN_ROWS):
                carries.append(carry)
                last = jnp.broadcast_to(xs[blk, SCAN_ROWS - 1:, :], carry.shape)
                carry = last + a_tab_ref[8] * carry + a_tab_ref[9] * _swap_halves(carry)
                blk += 1
            s_scr[j * SCAN_ROWS:(j + 1) * SCAN_ROWS, :] = carry
        carries = jnp.stack(carries, axis=0)
        not_first = lax.broadcasted_iota(jnp.int32, (1, SCAN_ROWS, 1), 1) >= 1
        enter = (jnp.where(not_first, pltpu.roll(xs, 1, 1), 0.0)
                 + a_tab_ref[6] * carries + a_tab_ref[7] * _swap_halves(carries))
        v_scr[...] = enter.reshape(n_rows, SSM_SW)
    s_ref[0] = s_scr[...]

    enter = v_scr[...].astype(BF16)
    d = d_ref[...]
    carried = _qk(enter, wot_ref[0])
    for jo in range(SSM_NT):
        yt = conv[jo] + carried[:, jo * MXU_TILE:(jo + 1) * MXU_TILE]
        r0 = 0
        for rs in row_sets:
            for tl in range(SSM_TPT):
                idx = piece(*rs, jo * SSM_TPT + tl)
                y_ref[idx] = yt[r0:r0 + rs[2], tl * LANES:(tl + 1) * LANES] + u_ref[idx] * d
            r0 += rs[2]


def _ssm(u, row_sets, slab_rows, block_rows, wconv, wst, wo, d_lanes, a_tab, s0):
    ns, r, _ = u.shape
    chunk_rows = sum(rs[2] for rs in row_sets)
    carry_rows = s0.shape[1]
    blk = pl.BlockSpec((ns, block_rows, LANES), lambda gb, i: (0, i, gb))
    per_gb = lambda x: pl.BlockSpec((1,) + x.shape[1:], lambda gb, i: (gb, 0, 0))
    return pl.pallas_call(
        functools.partial(_ssm_body, row_sets, slab_rows),
        grid=(SSM_NGB, r // block_rows),
        in_specs=[blk, per_gb(wconv), per_gb(wst), per_gb(wo),
                  pl.BlockSpec((1, LANES), lambda gb, i: (0, gb)),
                  pl.BlockSpec((N_SCAN_TABLES, SCAN_ROWS, SSM_SW), lambda gb, i: (0, 0, gb)),
                  per_gb(s0)],
        out_specs=[blk, per_gb(s0)],
        out_shape=[jax.ShapeDtypeStruct(u.shape, F32), jax.ShapeDtypeStruct(s0.shape, F32)],
        scratch_shapes=[pltpu.VMEM((chunk_rows, SSM_SW), F32), pltpu.VMEM((carry_rows, SSM_SW), F32)],
        compiler_params=_params("parallel", "arbitrary"),
        name="ssm_scan",
    )(u, wconv, wst, wo, d_lanes, a_tab, s0)


def _ssm_branch(u, s0, ssm_w):
    b, l, _ = u.shape
    nc = l // SSM_T
    by_gb = lambda s: s.reshape(s.shape[0], SSM_NGB, SSM_SW).transpose(1, 0, 2)
    if b % SCAN_ROWS == 0:
        row_sets = tuple((0, c * SSM_T, b, l) for c in range(nc))
        y, s_last = _ssm(u.reshape(1, b * l, SSM_W), row_sets, b, b * l, *ssm_w, by_gb(s0))
    else:
        cps = min(nc, SSM_CHUNKS_PER_STEP)
        row_sets = tuple((j, 0, cps, SSM_T) for j in range(b))
        y, s_last = _ssm(u, row_sets, 0, cps * SSM_T, *ssm_w, by_gb(jnp.repeat(s0, SCAN_ROWS, axis=0)))
        s_last = s_last[:, ::SCAN_ROWS]
    s_last = s_last.transpose(1, 0, 2).reshape(b, SSM_GROUPS, 2, SSM_STATE)
    return y.reshape(b, l, SSM_W), s_last[:, :, 0], s_last[:, :, 1]


def _ffn2_body(x_ref, gpre_ref, gpost_ref, wfi_ref, wfo_ref, o_ref):
    outs = _ffn(_row_parts(x_ref, FFN_PARTS), gpre_ref[...], gpost_ref[...], wfi_ref, wfo_ref)
    rows = x_ref.shape[0] // FFN_PARTS
    for i, o in enumerate(outs):
        o_ref[i * rows:(i + 1) * rows, :] = o


def _ffn2(x, gpre, gpost, wfi, wfo):
    n = x.shape[0]
    row = pl.BlockSpec((FFN_ROW_TILE, D_MODEL), lambda i: (i, 0))
    vec = _resident((1, D_MODEL))
    return pl.pallas_call(
        _ffn2_body,
        grid=(n // FFN_ROW_TILE,),
        in_specs=[row, vec, vec, _resident(wfi.shape), _resident(wfo.shape)],
        out_specs=row,
        out_shape=jax.ShapeDtypeStruct((n, D_MODEL), F32),
        compiler_params=_params("parallel"),
        name="ffn2",
    )(x, gpre, gpost, wfi, wfo)


def _t5_bucket(rel):
    half = N_BUCKETS // 2
    max_exact = half // 2
    ret = (rel > 0).astype(np.int32) * half
    n = np.abs(rel)
    large = max_exact + (np.log(np.maximum(n, 1) / max_exact) / math.log(MAX_DISTANCE / max_exact)
                         * (half - max_exact)).astype(np.int32)
    large = np.minimum(large, half - 1)
    return ret + np.where(n < max_exact, n, large)


def _band_bias(rel_table, n_q, n_back, n_k):
    i = np.arange(n_q)[:, None]
    j = np.arange(n_k)[None, :]
    bucket = _t5_bucket((j - n_back) - i).reshape(-1)
    onehot = np.zeros((N_BUCKETS, bucket.size), np.float32)
    onehot[bucket, np.arange(bucket.size)] = 1.0
    b = jnp.dot(rel_table.astype(F32).T, jnp.asarray(onehot), precision=lax.Precision.HIGHEST)
    return b.reshape(N_KV_HEADS, KV_REP * n_q, n_k)


def _sink_rows(sink, n_q):
    return jnp.repeat(sink.astype(F32).reshape(N_KV_HEADS, KV_REP), n_q, axis=1)[:, :, None]


def _twice_per_head(x, axis):
    shape = x.shape
    x = x.reshape(shape[:axis] + (N_KV_HEADS, 1, HEAD_DIM) + shape[axis + 1:])
    x = jnp.concatenate([x, x], axis=axis + 1)
    return x.reshape(shape[:axis] + (KV2_W,) + shape[axis + 1:])


def kernel(x_prompt, x_sample, cache_swa_k, cache_swa_v, cache_mem_k, cache_mem_v, state_ssm_re, state_ssm_im, mem_prompt, rel_bias_table, ff1_pre_g, ff1_post_g, w_ff1_in, w_ff1_out, mix_pre_g, mix_post_g, w_in, mem_norm_g, w_mem_kv, attn_sink, ssm_lambda_re, ssm_lambda_im, ssm_log_dt, ssm_b_re, ssm_b_im, ssm_c_re, ssm_c_im, ssm_d, w_ssm_glu, w_attn_br, w_mem_br, w_out, ff2_pre_g, ff2_post_g, w_ff2_in, w_ff2_out):
    assert ff1_pre_g.shape[0] == 1, "single-layer step"
    bp, lp, _ = x_prompt.shape
    bs, ls, _ = x_sample.shape
    vec = lambda g: g[0].reshape(1, D_MODEL).astype(F32)
    w16 = lambda w: w[0].astype(BF16)

    wfi1, wfo1, wfi2, wfo2 = w16(w_ff1_in), w16(w_ff1_out), w16(w_ff2_in), w16(w_ff2_out)
    w_in16 = w16(w_in)
    c_k, c_v, c_u = ATTN_W, ATTN_W + KV_W, ATTN_W + 2 * KV_W
    c_g = c_u + SSM_W + MEM_W
    q_scale = HEAD_DIM ** -0.5
    assert math.frexp(q_scale)[0] == 0.5, "power-of-two scale: folding it into the q columns is exact"
    wp = jnp.concatenate([w_in16[:, :c_k] * jnp.asarray(q_scale, BF16), w_in16[:, c_k:c_g],
                          _twice_per_head(w_in16[:, c_k:c_v], 1), _twice_per_head(w_in16[:, c_v:c_u], 1)], axis=1)
    wg = w_in16[:, c_g:]
    wab, wglu, wmb, wo = w16(w_attn_br), w16(w_ssm_glu), w16(w_mem_br), w16(w_out)

    wconv, wst, wot, a_tab = _ssm_weights(
        ssm_lambda_re[0], ssm_lambda_im[0], ssm_log_dt[0], ssm_b_re[0], ssm_b_im[0], ssm_c_re[0], ssm_c_im[0])
    ssm_w = (wconv, wst, wot, ssm_d[0].astype(F32).reshape(1, SSM_W), a_tab)

    mk_p, mv_p = _mem_kv(mem_prompt.reshape(bp * MEM_LEN, D_MODEL), vec(mem_norm_g), w16(w_mem_kv))

    def group(x, mixer, s0, mem_k, mem_v):
        b, l, _ = x.shape
        r3 = lambda t: t.reshape(b, l, t.shape[-1])
        x1, q, k, v, u, qm, k2, v2 = _ffn1_proj(x.reshape(b * l, D_MODEL), vec(ff1_pre_g), vec(ff1_post_g),
                                                wfi1, wfo1, vec(mix_pre_g), wp)
        y_ssm, s_re, s_im = _ssm_branch(r3(u), s0, ssm_w)
        x2 = mixer(r3(q), r3(k2), r3(v2), r3(qm), mem_k, mem_v, x1, y_ssm.reshape(b * l, SSM_W))
        y = _ffn2(x2, vec(ff2_pre_g), vec(ff2_post_g), wfi2, wfo2)
        return r3(y), r3(k), r3(v), s_re, s_im

    merge_w = (vec(mix_pre_g), wg, wglu, wab, wmb, wo, vec(mix_post_g))

    def mixer_prompt(q, k2, v2, qm, mem_k, mem_v, x1, y_ssm):
        bias = _band_bias(rel_bias_table, CHUNK, WINDOW, WINDOW + CHUNK)
        return _mixer_prompt(q, k2, v2, bias, _sink_rows(attn_sink[0], CHUNK), qm, mem_k, mem_v, x1, y_ssm, merge_w)

    def mixer_sample(q, k2, v2, qm, mem_k, mem_v, x1, y_ssm):
        n_back = cache_swa_k.shape[2]
        cache2 = lambda c: _twice_per_head(c[0].reshape(bs, n_back, KV_W).astype(BF16), 2)
        kk = jnp.concatenate([cache2(cache_swa_k), k2], axis=1)
        vv = jnp.concatenate([cache2(cache_swa_v), v2], axis=1)
        bias = _band_bias(rel_bias_table, ls, n_back, n_back + ls)
        return _mixer_sample(q, kk, vv, bias, _sink_rows(attn_sink[0], ls), qm, mem_k, mem_v, x1, y_ssm, merge_w)

    yp, pk, pv, pre, pim = group(x_prompt, mixer_prompt, jnp.zeros((bp, SSM_WIDTH), F32),
                                 mk_p.reshape(bp, MEM_LEN, MEM_W), mv_p.reshape(bp, MEM_LEN, MEM_W))
    s0 = jnp.stack([state_ssm_re[0], state_ssm_im[0]], axis=2).reshape(bs, SSM_WIDTH).astype(F32)
    ys, sk, sv, sre, sim = group(x_sample, mixer_sample, s0,
                                 cache_mem_k[0].reshape(bs, MEM_LEN, MEM_W),
                                 cache_mem_v[0].reshape(bs, MEM_LEN, MEM_W))

    n_keep = min(WINDOW, lp)
    heads = lambda t: t.reshape(t.shape[0], t.shape[1], N_KV_HEADS, HEAD_DIM)[None]
    mem_heads = lambda t: t.reshape(bp, MEM_LEN, MEM_HEADS, MEM_HEAD_DIM)[None]
    return (yp, ys, heads(pk[:, -n_keep:]), heads(pv[:, -n_keep:]), mem_heads(mk_p), mem_heads(mv_p),
            pre[None], pim[None], heads(sk), heads(sv), sre[None], sim[None])
```

```python
import functools
import math

import numpy as np
import jax
import jax.numpy as jnp
from jax import lax
from jax.experimental import pallas as pl
from jax.experimental.pallas import tpu as pltpu

D_MODEL = 1024
CHUNK = 64
WINDOW = 128
HEAD_DIM = 64
MIX_W = D_MODEL // 2
N_HEADS = MIX_W // HEAD_DIM
N_KV_HEADS = N_HEADS // 4
KV_REP = N_HEADS // N_KV_HEADS
ATTN_W = N_HEADS * HEAD_DIM
KV_W = N_KV_HEADS * HEAD_DIM
SSM_GROUP = 16
SSM_W = MIX_W
SSM_GROUPS = SSM_W // SSM_GROUP
SSM_STATE = 64
MEM_LEN = 256
MEM_HEADS = 4
MEM_HEAD_DIM = MIX_W // MEM_HEADS
MEM_W = MEM_HEADS * MEM_HEAD_DIM
D_FF = 128 * ((8 * D_MODEL // 3 + 127) // 128)
N_BUCKETS = 32
MAX_DISTANCE = 128
RMS_EPS = 1e-6
NEG_INF = -1e30

LANES = 128
MXU_TILE = 256
ROW_TILE = 256
MIXER_TILE = 512
SAMPLE_MIXER_TILE = 256
MIXER_PARTS = 2
FFN_ROW_TILE = 512
FFN_PARTS = 2
FFN_CHUNK = 2 * MXU_TILE
VMEM_LIMIT = 60 * 1024 * 1024

KV2_W = N_KV_HEADS * LANES

SSM_T = 16
SSM_K = SSM_T * SSM_GROUP
SSM_S2 = 2 * SSM_STATE
SSM_WIDTH = SSM_GROUPS * SSM_S2
SSM_GB = LANES // SSM_GROUP
SSM_NGB = SSM_GROUPS // SSM_GB
SSM_XW = SSM_T * LANES
SSM_SW = SSM_GB * SSM_S2
SSM_TPT = MXU_TILE // LANES
SSM_NT = SSM_T // SSM_TPT
SCAN_ROWS = 8
N_SCAN_TABLES = 10
SSM_CHUNKS_PER_STEP = 256

F32 = jnp.float32
BF16 = jnp.bfloat16


def _params(*sem):
    return pltpu.CompilerParams(dimension_semantics=sem, vmem_limit_bytes=VMEM_LIMIT)


def _resident(shape):
    zeros = (0,) * len(shape)
    return pl.BlockSpec(shape, lambda *_: zeros, pipeline_mode=pl.Buffered(1))


def _rms(x, g):
    return x * lax.rsqrt(jnp.mean(x * x, axis=-1, keepdims=True) + RMS_EPS) * g


def _mm(a, b):
    return jnp.dot(a, b, preferred_element_type=F32)


def _sigmoid(x):
    return 0.5 * jnp.tanh(0.5 * x) + 0.5


def _row_parts(ref, n_parts):
    rows = ref.shape[0] // n_parts
    return [ref[i * rows:(i + 1) * rows, :] for i in range(n_parts)]


def _ffn(xs, gpre, gpost, w_in_ref, w_out_ref):
    hs = [_rms(x, gpre).astype(BF16) for x in xs]
    acts = [[] for _ in xs]
    for c0 in range(0, D_FF, FFN_CHUNK):
        c1 = min(c0 + FFN_CHUNK, D_FF)
        for h, a in zip(hs, acts):
            g = _mm(h, w_in_ref[:, c0:c1])
            u = _mm(h, w_in_ref[:, D_FF + c0:D_FF + c1])
            a.append((g * jax.nn.sigmoid(g) * u).astype(BF16))
    outs = [_mm(jnp.concatenate(a, axis=1), w_out_ref[...]) for a in acts]
    return [x + 0.5 * _rms(o, gpost) for x, o in zip(xs, outs)]


PROJ_SPLIT = (("q", ATTN_W, BF16), ("k", KV_W, F32), ("v", KV_W, F32), ("u", SSM_W, F32),
              ("qm", MEM_W, BF16), ("k2", KV2_W, BF16), ("v2", KV2_W, BF16))


def _ffn1_proj_body(x_ref, gpre_ref, gpost_ref, wfi_ref, wfo_ref, gmix_ref, wp_ref, x1_ref, *out_refs):
    x1s = _ffn(_row_parts(x_ref, FFN_PARTS), gpre_ref[...], gpost_ref[...], wfi_ref, wfo_ref)
    rows = x_ref.shape[0] // FFN_PARTS
    ps = [_mm(_rms(x1, gmix_ref[...]).astype(BF16), wp_ref[...]) for x1 in x1s]
    for i, (x1, p) in enumerate(zip(x1s, ps)):
        part = slice(i * rows, (i + 1) * rows)
        x1_ref[part, :] = x1
        col = 0
        for ref, (_, width, dtype) in zip(out_refs, PROJ_SPLIT):
            ref[part, :] = p[:, col:col + width].astype(dtype)
            col += width


def _ffn1_proj(x, gpre, gpost, wfi, wfo, gmix, wp):
    n = x.shape[0]
    row = lambda w: pl.BlockSpec((FFN_ROW_TILE, w), lambda i: (i, 0))
    return pl.pallas_call(
        _ffn1_proj_body,
        grid=(n // FFN_ROW_TILE,),
        in_specs=[row(D_MODEL), _resident((1, D_MODEL)), _resident((1, D_MODEL)),
                  _resident(wfi.shape), _resident(wfo.shape), _resident((1, D_MODEL)),
                  _resident(wp.shape)],
        out_specs=[row(D_MODEL)] + [row(w) for _, w, _ in PROJ_SPLIT],
        out_shape=[jax.ShapeDtypeStruct((n, D_MODEL), F32)]
        + [jax.ShapeDtypeStruct((n, w), d) for _, w, d in PROJ_SPLIT],
        compiler_params=_params("parallel"),
        name="ffn1_proj",
    )(x, gpre, gpost, wfi, wfo, gmix, wp)


def _mem_kv_body(m_ref, g_ref, w_ref, k_ref, v_ref):
    kv = _mm(_rms(m_ref[...], g_ref[...]).astype(BF16), w_ref[...])
    k_ref[...] = kv[:, :MEM_W]
    v_ref[...] = kv[:, MEM_W:]


def _mem_kv(mem, g, w):
    n = mem.shape[0]
    row = lambda wd: pl.BlockSpec((ROW_TILE, wd), lambda i: (i, 0))
    return pl.pallas_call(
        _mem_kv_body,
        grid=(n // ROW_TILE,),
        in_specs=[row(D_MODEL), _resident((1, D_MODEL)), _resident(w.shape)],
        out_specs=[row(MEM_W), row(MEM_W)],
        out_shape=[jax.ShapeDtypeStruct((n, MEM_W), F32)] * 2,
        compiler_params=_params("parallel"),
        name="mem_kv",
    )(mem, g, w)


def _qk(q, k):
    return lax.dot_general(q, k, (((1,), (1,)), ((), ())), preferred_element_type=F32)


def _softmax_pv(scores, values, sinks=None):
    probs = []
    for i, s in enumerate(scores):
        m = jnp.max(s, axis=-1, keepdims=True)
        if sinks is not None:
            m = jnp.maximum(m, sinks[i])
        e = jnp.exp(s - m)
        den = jnp.sum(e, axis=-1, keepdims=True)
        if sinks is not None:
            den = den + jnp.exp(sinks[i] - m)
        probs.append((e * (1.0 / den)).astype(BF16))
    return [_mm(p, v) for p, v in zip(probs, values)]


def _gqa_queries(q):
    nq = q.shape[0]
    low = lax.broadcasted_iota(jnp.int32, (nq, LANES), 1) < HEAD_DIM
    zero = jnp.zeros((nq, LANES), BF16)
    stacks = []
    for g in range(N_KV_HEADS):
        rows = []
        for r in range(KV_REP):
            h = g * KV_REP + r
            q2 = q[:, (h // 2) * LANES:(h // 2 + 1) * LANES]
            rows.append(jnp.where(low, q2, zero) if h % 2 == 0 else jnp.where(low, zero, q2))
        stacks.append(jnp.concatenate(rows, axis=0))
    return stacks


def _gqa_outputs(outs, nq):
    low = lax.broadcasted_iota(jnp.int32, (nq, LANES), 1) < HEAD_DIM
    pairs = []
    for o in outs:
        for r in range(0, KV_REP, 2):
            pairs.append(jnp.where(low, o[r * nq:(r + 1) * nq], o[(r + 1) * nq:(r + 2) * nq]))
    return jnp.concatenate(pairs, axis=1)


def _mem_scores(qm, mk_ref):
    return [_qk(qm[:, h * MEM_HEAD_DIM:(h + 1) * MEM_HEAD_DIM],
                mk_ref[:, h * MEM_HEAD_DIM:(h + 1) * MEM_HEAD_DIM].astype(BF16)) * (MEM_HEAD_DIM ** -0.5)
            for h in range(MEM_HEADS)]


def _mem_values(mv_ref):
    return [mv_ref[:, h * MEM_HEAD_DIM:(h + 1) * MEM_HEAD_DIM].astype(BF16) for h in range(MEM_HEADS)]


def _kv_heads(x):
    return [x[:, g * LANES:(g + 1) * LANES] for g in range(N_KV_HEADS)]


def _merge_pre(x1, y, gmix_ref, wg_ref, wglu_ref):
    h = _rms(x1, gmix_ref[...]).astype(BF16)
    logits = [_mm(h, wg_ref[:, j * D_MODEL:(j + 1) * D_MODEL]) for j in range(3)]
    y = y.astype(BF16)
    return logits, _mm(y, wglu_ref[:, :D_MODEL]), _mm(y, wglu_ref[:, D_MODEL:])


def _mix_parts(parts, x1_ref, y_ref, gmix_ref, wg_ref, wglu_ref, wab_ref, wmb_ref, wo_ref, gpost_ref, x2_ref):
    x1s, ys = _row_parts(x1_ref, len(parts)), _row_parts(y_ref, len(parts))
    pres = [_merge_pre(x1, y, gmix_ref, wg_ref, wglu_ref) for x1, y in zip(x1s, ys)]
    branches = [assemble(_softmax_pv(scores, values, sinks), _softmax_pv(mem_scores, mem_values))
                for scores, values, sinks, mem_scores, mem_values, assemble in parts]
    projected = [(_mm(attn.astype(BF16), wab_ref[...]), _mm(memo.astype(BF16), wmb_ref[...]))
                 for attn, memo in branches]
    sig = _sigmoid
    merged = [sig(lg[0]) * pa + sig(lg[1]) * (ya * sig(yb)) + sig(lg[2]) * pm
              for (lg, ya, yb), (pa, pm) in zip(pres, projected)]
    outs = [_mm(m.astype(BF16), wo_ref[...]) for m in merged]
    rows = x1_ref.shape[0] // len(parts)
    for i, (x1, o) in enumerate(zip(x1s, outs)):
        x2_ref[i * rows:(i + 1) * rows, :] = x1 + _rms(o, gpost_ref[...])


def _mixer_prompt_body(q_ref, k_ref, kh_ref, v_ref, vh_ref, bias_ref, sink_ref, qm_ref, mk_ref, mv_ref,
                       x1_ref, y_ref, *merge_and_out_refs):
    i = pl.program_id(1)
    tq = q_ref.shape[1]
    kf = jnp.concatenate([kh_ref[0], k_ref[0]], axis=0)
    vf = jnp.concatenate([vh_ref[0], v_ref[0]], axis=0)
    band = WINDOW + CHUNK
    rows = tq // MIXER_PARTS
    parts = []
    for p in range(MIXER_PARTS):
        scores, values, sinks = [], [], []
        for lo in range(p * rows, (p + 1) * rows, CHUNK):
            valid = None
            if lo < WINDOW:
                key_pos = lax.broadcasted_iota(jnp.int32, (1, band), 1) + (i * tq + lo - WINDOW)
                valid = key_pos >= 0
            kb, vb = _kv_heads(kf[lo:lo + band]), _kv_heads(vf[lo:lo + band])
            for g, qs in enumerate(_gqa_queries(q_ref[0, lo:lo + CHUNK, :])):
                s = _qk(qs, kb[g]) + bias_ref[g]
                scores.append(s if valid is None else jnp.where(valid, s, NEG_INF))
                values.append(vb[g])
                sinks.append(sink_ref[g])
        mem_scores = _mem_scores(qm_ref[0, p * rows:(p + 1) * rows, :], mk_ref.at[0])

        def assemble(outs, mem):
            attn = jnp.concatenate([_gqa_outputs(outs[c:c + N_KV_HEADS], CHUNK)
                                    for c in range(0, len(outs), N_KV_HEADS)], axis=0)
            return attn, jnp.concatenate(mem, axis=1)

        parts.append((scores, values, sinks, mem_scores, _mem_values(mv_ref.at[0]), assemble))
    _mix_parts(parts, x1_ref, y_ref, *merge_and_out_refs)


def _merge_weight_specs(weights):
    return [_resident(w.shape) for w in weights]


def _mixer_prompt(q, k2, v2, bias, sink, qm, mk, mv, x1, y, merge_w):
    b, l, _ = q.shape
    tq = MIXER_TILE
    hb = tq // WINDOW
    nt = l // tq
    cur = lambda w: pl.BlockSpec((1, tq, w), lambda bi, i: (bi, i, 0))
    row = lambda w: pl.BlockSpec((tq, w), lambda bi, i: (bi * nt + i, 0))
    halo = pl.BlockSpec((1, WINDOW, KV2_W), lambda bi, i: (bi, jnp.maximum(i * hb - 1, 0), 0))
    mem = pl.BlockSpec((1, MEM_LEN, MEM_W), lambda bi, i: (bi, 0, 0))
    return pl.pallas_call(
        _mixer_prompt_body,
        grid=(b, nt),
        in_specs=[cur(ATTN_W), cur(KV2_W), halo, cur(KV2_W), halo, _resident(bias.shape),
                  _resident(sink.shape), cur(MEM_W), mem, mem, row(D_MODEL), row(SSM_W)]
        + _merge_weight_specs(merge_w),
        out_specs=row(D_MODEL),
        out_shape=jax.ShapeDtypeStruct((b * l, D_MODEL), F32),
        compiler_params=_params("parallel", "parallel"),
        name="mixer_prompt",
    )(q, k2, k2, v2, v2, bias, sink, qm, mk, mv, x1, y, *merge_w)


def _mixer_sample_body(q_ref, k_ref, v_ref, bias_ref, sink_ref, qm_ref, mk_ref, mv_ref,
                       x1_ref, y_ref, *merge_and_out_refs):
    nb, nq = q_ref.shape[0], q_ref.shape[1]
    per_part = nb // MIXER_PARTS
    parts = []
    for p in range(MIXER_PARTS):
        scores, values, sinks, mem_scores, mem_values = [], [], [], [], []
        for b in range(p * per_part, (p + 1) * per_part):
            kb, vb = _kv_heads(k_ref[b]), _kv_heads(v_ref[b])
            for g, qs in enumerate(_gqa_queries(q_ref[b])):
                scores.append(_qk(qs, kb[g]) + bias_ref[g])
                values.append(vb[g])
                sinks.append(sink_ref[g])
            mem_scores += _mem_scores(qm_ref[b], mk_ref.at[b])
            mem_values += _mem_values(mv_ref.at[b])

        def assemble(outs, mem):
            attn = jnp.concatenate([_gqa_outputs(outs[j:j + N_KV_HEADS], nq)
                                    for j in range(0, len(outs), N_KV_HEADS)], axis=0)
            memo = jnp.concatenate([jnp.concatenate(mem[j:j + MEM_HEADS], axis=1)
                                    for j in range(0, len(mem), MEM_HEADS)], axis=0)
            return attn, memo

        parts.append((scores, values, sinks, mem_scores, mem_values, assemble))
    _mix_parts(parts, x1_ref, y_ref, *merge_and_out_refs)


def _mixer_sample(q, kk2, vv2, bias, sink, qm, mk, mv, x1, y, merge_w):
    b, s, _ = q.shape
    nb = SAMPLE_MIXER_TILE // s
    nk = kk2.shape[1]
    blk = lambda *shape: pl.BlockSpec((nb,) + shape, lambda i: (i,) + (0,) * len(shape))
    row = lambda w: pl.BlockSpec((nb * s, w), lambda i: (i, 0))
    return pl.pallas_call(
        _mixer_sample_body,
        grid=(b // nb,),
        in_specs=[blk(s, ATTN_W), blk(nk, KV2_W), blk(nk, KV2_W),
                  _resident(bias.shape), _resident(sink.shape),
                  blk(s, MEM_W), blk(MEM_LEN, MEM_W), blk(MEM_LEN, MEM_W), row(D_MODEL), row(SSM_W)]
        + _merge_weight_specs(merge_w),
        out_specs=row(D_MODEL),
        out_shape=jax.ShapeDtypeStruct((b * s, D_MODEL), F32),
        compiler_params=_params("parallel"),
        name="mixer_sample",
    )(q, kk2, vv2, bias, sink, qm, mk, mv, x1, y, *merge_w)


def _ssm_weights_body(lam_re_row, lam_im_row, lam_re_col, lam_im_col, log_dt, b_re_t, b_im_t,
                      b_re_rows, b_im_rows, c_re_rows, c_im_rows, c_re_lanes, c_im_lanes, sel_ref,
                      wconv_ref, wst_ref, wot_ref, a_tab_ref):
    wst_ref[0] = jnp.zeros(wst_ref.shape[1:], BF16)
    wot_ref[0] = jnp.zeros(wot_ref.shape[1:], BF16)
    krows = []
    for gl in range(SSM_GB):
        krows.append(_ssm_group_weights(
            gl, *(r[gl] for r in (lam_re_row, lam_im_row, lam_re_col, lam_im_col, log_dt, b_re_t, b_im_t,
                                  b_re_rows, b_im_rows, c_re_rows, c_im_rows, c_re_lanes, c_im_lanes)),
            wst_ref, wot_ref, a_tab_ref))
    kstack = jnp.concatenate(krows, axis=0).astype(BF16)
    for d in range(SSM_NT):
        for ti in range(SSM_TPT):
            blk = _mm(kstack, sel_ref[d * SSM_TPT + ti])
            r0 = (SSM_NT - 1 - d) * MXU_TILE + ti * LANES
            for gl in range(SSM_GB):
                piece = blk[gl * SSM_GROUP:(gl + 1) * SSM_GROUP]
                piece = (piece if gl == 0 else pltpu.roll(piece, gl * SSM_GROUP, 1)).astype(BF16)
                r = r0 + gl * SSM_GROUP
                wconv_ref[0, r:r + SSM_GROUP, :MXU_TILE] = piece
                if r0 >= MXU_TILE:
                    wconv_ref[0, r - MXU_TILE:r - MXU_TILE + SSM_GROUP, MXU_TILE:] = piece
    wconv_ref[0, (SSM_NT - 1) * MXU_TILE:, MXU_TILE:] = jnp.zeros((MXU_TILE, MXU_TILE), BF16)


def _ssm_group_weights(gl, lam_re_row, lam_im_row, lam_re_col, lam_im_col, log_dt, b_re_t, b_im_t,
                       b_re_rows, b_im_rows, c_re_rows, c_im_rows, c_re_lanes, c_im_lanes,
                       wst_ref, wot_ref, a_tab_ref):
    dt = jnp.exp(log_dt)
    rows = lambda t: slice(t * LANES + gl * SSM_GROUP, t * LANES + (gl + 1) * SSM_GROUP)
    lanes = slice(gl * SSM_S2, (gl + 1) * SSM_S2)

    def zoh_coef(lr, li):
        mag = jnp.exp(lr * dt)
        a_re, a_im = mag * jnp.cos(li * dt), mag * jnp.sin(li * dt)
        den = lr * lr + li * li
        return ((a_re - 1.0) * lr + a_im * li) / den, (a_im * lr - (a_re - 1.0) * li) / den

    def a_power(lr, li, n):
        mag = jnp.exp(lr * dt * n)
        return mag * jnp.cos(li * dt * n), mag * jnp.sin(li * dt * n)

    lr, li = lam_re_row, lam_im_row
    cr, ci = zoh_coef(lr, li)
    n_rows = jnp.minimum(lax.broadcasted_iota(jnp.int32, (SSM_T + SCAN_ROWS, 1), 0), SSM_T).astype(F32)
    pw_r, pw_i = a_power(lr, li, n_rows)

    def over_channels(pw, exps):
        return jnp.concatenate([jnp.broadcast_to(pw[e:e + 1], (SSM_GROUP, SSM_STATE)) for e in exps], axis=0)

    back = [SSM_T - 1 - t for t in range(SSM_T)]
    pr, pi = over_channels(pw_r, back), over_channels(pw_i, back)
    zr, zi = pr * cr - pi * ci, pr * ci + pi * cr
    br, bi = b_re_rows, b_im_rows
    wstate = jnp.concatenate([zr * br - zi * bi, zr * bi + zi * br], axis=1).astype(BF16)

    fwd = [t + 1 for t in range(SSM_T)]
    qr, qi = over_channels(pw_r, fwd), over_channels(pw_i, fwd)
    ccr, cci = c_re_rows, c_im_rows
    wout_t = jnp.concatenate([ccr * qr - cci * qi, -(ccr * qi + cci * qr)], axis=1).astype(BF16)
    for t in range(SSM_T):
        wst_ref[0, rows(t), lanes] = wstate[t * SSM_GROUP:(t + 1) * SSM_GROUP]
        wot_ref[0, rows(t), lanes] = wout_t[t * SSM_GROUP:(t + 1) * SSM_GROUP]

    lrc, lic = lam_re_col, lam_im_col
    hi = lax.Precision.HIGHEST
    n_lanes = jnp.minimum(lax.broadcasted_iota(jnp.int32, (1, LANES), 1), SSM_T).astype(F32)
    pc_r, pc_i = a_power(lrc, lic, n_lanes)
    spread = (lax.broadcasted_iota(jnp.int32, (LANES, SSM_K), 0)
              == lax.broadcasted_iota(jnp.int32, (LANES, SSM_K), 1) // SSM_GROUP).astype(F32)
    gr = jnp.dot(pc_r, spread, precision=hi, preferred_element_type=F32)
    gi = jnp.dot(pc_i, spread, precision=hi, preferred_element_type=F32)
    clr, cli = c_re_lanes, c_im_lanes
    g_re, g_im = clr * gr - cli * gi, clr * gi + cli * gr
    btr, bti = b_re_t, b_im_t
    bbr, bbi = cr * btr - ci * bti, cr * bti + ci * btr
    krow = (jnp.dot(bbr, g_re, precision=hi, preferred_element_type=F32)
            - jnp.dot(bbi, g_im, precision=hi, preferred_element_type=F32))

    idx = lax.broadcasted_iota(jnp.int32, (N_SCAN_TABLES * SCAN_ROWS, SSM_STATE), 0)
    tab, r = idx // SCAN_ROWS, idx % SCAN_ROWS
    stride = jnp.where(tab < 2, 1, jnp.where(tab < 4, 2, 4))
    n = jnp.where(tab < 6, stride, jnp.where(tab < 8, r, SCAN_ROWS))
    keep = jnp.logical_or(tab >= 6, r >= stride)
    er, ei = a_power(lr, li, (n * SSM_T).astype(F32))
    er, ei = jnp.where(keep, er, 0.0), jnp.where(keep, ei, 0.0)
    odd = tab % 2 == 1
    tabs = jnp.concatenate([jnp.where(odd, -ei, er), jnp.where(odd, ei, er)], axis=1)
    a_tab_ref[:, :, lanes] = tabs.reshape(N_SCAN_TABLES, SCAN_ROWS, SSM_S2)
    return krow


def _lag_selectors():
    sel = np.zeros((SSM_NT, SSM_TPT, SSM_K, MXU_TILE), np.float32)
    ch = np.arange(SSM_GROUP)
    for d in range(SSM_NT):
        for ti in range(SSM_TPT):
            for to in range(SSM_TPT):
                lag = SSM_TPT * d + to - ti
                if lag >= 0:
                    sel[d, ti, lag * SSM_GROUP + ch, to * LANES + ch] = 1.0
    return jnp.asarray(sel.reshape(SSM_NT * SSM_TPT, SSM_K, MXU_TILE), BF16)


def _ssm_weights(lam_re, lam_im, log_dt, b_re, b_im, c_re, c_im):
    g, p, c = b_re.shape
    row3 = lambda x: x.reshape(g, 1, p)
    col3 = lambda x: x.reshape(g, p, 1)
    t3 = lambda x: jnp.transpose(x, (0, 2, 1))
    b_rows = lambda x: jnp.tile(t3(x), (1, SSM_T, 1))
    c_rows = lambda x: jnp.tile(x, (1, SSM_T, 1))
    c_lanes = lambda x: jnp.tile(t3(x), (1, 1, SSM_T))
    args = (row3(lam_re), row3(lam_im), col3(lam_re), col3(lam_im), log_dt.reshape(g, 1, 1),
            t3(b_re), t3(b_im), b_rows(b_re), b_rows(b_im), c_rows(c_re), c_rows(c_im),
            c_lanes(c_re), c_lanes(c_im))
    sel = _lag_selectors()
    spec = lambda x: pl.BlockSpec((SSM_GB,) + x.shape[1:], lambda i: (i, 0, 0))
    per_gb = lambda *shape: pl.BlockSpec((1,) + shape, lambda i: (i, 0, 0))
    return pl.pallas_call(
        _ssm_weights_body,
        grid=(SSM_NGB,),
        in_specs=[spec(a) for a in args] + [_resident(sel.shape)],
        out_specs=[per_gb(SSM_NT * MXU_TILE, 2 * MXU_TILE), per_gb(SSM_XW, SSM_SW), per_gb(SSM_XW, SSM_SW),
                   pl.BlockSpec((N_SCAN_TABLES, SCAN_ROWS, SSM_SW), lambda i: (0, 0, i))],
        out_shape=[jax.ShapeDtypeStruct((SSM_NGB, SSM_NT * MXU_TILE, 2 * MXU_TILE), BF16),
                   jax.ShapeDtypeStruct((SSM_NGB, SSM_XW, SSM_SW), BF16),
                   jax.ShapeDtypeStruct((SSM_NGB, SSM_XW, SSM_SW), BF16),
                   jax.ShapeDtypeStruct((N_SCAN_TABLES, SCAN_ROWS, SSM_WIDTH), F32)],
        compiler_params=_params("parallel"),
        name="ssm_weights",
    )(*args, sel)


def _swap_halves(s):
    ax = s.ndim - 1
    return jnp.concatenate([pltpu.roll(s[..., l:l + SSM_S2], SSM_STATE, ax)
                            for l in range(0, s.shape[ax], SSM_S2)], axis=ax)


def _ssm_body(row_sets, slab_rows, u_ref, wconv_ref, wst_ref, wot_ref, d_ref, a_tab_ref, s0_ref,
              y_ref, s_ref, v_scr, s_scr):
    def piece(ref_set, first, n, stride, t):
        return ref_set, pl.ds(first + t, n, stride=stride)

    @pl.when(pl.program_id(1) == 0)
    def _():
        s_scr[...] = s0_ref[0]

    x = jnp.concatenate(
        [jnp.concatenate([u_ref[piece(*rs, t)] for t in range(SSM_T)], axis=1) for rs in row_sets],
        axis=0).astype(BF16)
    v_scr[...] = _mm(x, wst_ref[0])
    conv = []
    for jo in range(0, SSM_NT, 2):
        pair = _mm(x[:, :(jo + 2) * MXU_TILE], wconv_ref[0, (SSM_NT - 2 - jo) * MXU_TILE:, :])
        conv += [pair[:, MXU_TILE:], pair[:, :MXU_TILE]]

    n_rows = v_scr.shape[0]
    if slab_rows:
        a_mul, a_swap = a_tab_ref[6, 1:2, :], a_tab_ref[7, 1:2, :]
        s = s_scr[...]
        for c in range(n_rows // slab_rows):
            rows = slice(c * slab_rows, (c + 1) * slab_rows)
            inc = v_scr[rows, :]
            v_scr[rows, :] = s
            s = a_mul * s + a_swap * _swap_halves(s) + inc
        s_scr[...] = s
    else:
        n_blocks = n_rows // SCAN_ROWS
        xs = v_scr[...].reshape(n_blocks, SCAN_ROWS, SSM_SW)
        for k in range(3):
            sh = pltpu.roll(xs, 1 << k, 1)
            xs = xs + a_tab_ref[2 * k] * sh + a_tab_ref[2 * k + 1] * _swap_halves(sh)
        carries, blk = [], 0
        for j, (_, _, n, _) in enumerate(row_sets):
            carry = s_scr[j * SCAN_ROWS:(j + 1) * SCAN_ROWS, :]
            for _ in range(n // SCAN_ROWS):
                carries.append(carry)
                last = jnp.broadcast_to(xs[blk, SCAN_ROWS - 1:, :], carry.shape)
                carry = last + a_tab_ref[8] * carry + a_tab_ref[9] * _swap_halves(carry)
                blk += 1
            s_scr[j * SCAN_ROWS:(j + 1) * SCAN_ROWS, :] = carry
        carries = jnp.stack(carries, axis=0)
        not_first = lax.broadcasted_iota(jnp.int32, (1, SCAN_ROWS, 1), 1) >= 1
        enter = (jnp.where(not_first, pltpu.roll(xs, 1, 1), 0.0)
                 + a_tab_ref[6] * carries + a_tab_ref[7] * _swap_halves(carries))
        v_scr[...] = enter.reshape(n_rows, SSM_SW)
    s_ref[0] = s_scr[...]

    enter = v_scr[...].astype(BF16)
    d = d_ref[...]
    carried = _qk(enter, wot_ref[0])
    for jo in range(SSM_NT):
        yt = conv[jo] + carried[:, jo * MXU_TILE:(jo + 1) * MXU_TILE]
        r0 = 0
        for rs in row_sets:
            for tl in range(SSM_TPT):
                idx = piece(*rs, jo * SSM_TPT + tl)
                y_ref[idx] = yt[r0:r0 + rs[2], tl * LANES:(tl + 1) * LANES] + u_ref[idx] * d
            r0 += rs[2]


def _ssm(u, row_sets, slab_rows, block_rows, wconv, wst, wo, d_lanes, a_tab, s0):
    ns, r, _ = u.shape
    chunk_rows = sum(rs[2] for rs in row_sets)
    carry_rows = s0.shape[1]
    blk = pl.BlockSpec((ns, block_rows, LANES), lambda gb, i: (0, i, gb))
    per_gb = lambda x: pl.BlockSpec((1,) + x.shape[1:], lambda gb, i: (gb, 0, 0))
    return pl.pallas_call(
        functools.partial(_ssm_body, row_sets, slab_rows),
        grid=(SSM_NGB, r // block_rows),
        in_specs=[blk, per_gb(wconv), per_gb(wst), per_gb(wo),
                  pl.BlockSpec((1, LANES), lambda gb, i: (0, gb)),
                  pl.BlockSpec((N_SCAN_TABLES, SCAN_ROWS, SSM_SW), lambda gb, i: (0, 0, gb)),
                  per_gb(s0)],
        out_specs=[blk, per_gb(s0)],
        out_shape=[jax.ShapeDtypeStruct(u.shape, F32), jax.ShapeDtypeStruct(s0.shape, F32)],
        scratch_shapes=[pltpu.VMEM((chunk_rows, SSM_SW), F32), pltpu.VMEM((carry_rows, SSM_SW), F32)],
        compiler_params=_params("parallel", "arbitrary"),
        name="ssm_scan",
    )(u, wconv, wst, wo, d_lanes, a_tab, s0)


def _ssm_branch(u, s0, ssm_w):
    b, l, _ = u.shape
    nc = l // SSM_T
    by_gb = lambda s: s.reshape(s.shape[0], SSM_NGB, SSM_SW).transpose(1, 0, 2)
    if b % SCAN_ROWS == 0:
        row_sets = tuple((0, c * SSM_T, b, l) for c in range(nc))
        y, s_last = _ssm(u.reshape(1, b * l, SSM_W), row_sets, b, b * l, *ssm_w, by_gb(s0))
    else:
        cps = min(nc, SSM_CHUNKS_PER_STEP)
        row_sets = tuple((j, 0, cps, SSM_T) for j in range(b))
        y, s_last = _ssm(u, row_sets, 0, cps * SSM_T, *ssm_w, by_gb(jnp.repeat(s0, SCAN_ROWS, axis=0)))
        s_last = s_last[:, ::SCAN_ROWS]
    s_last = s_last.transpose(1, 0, 2).reshape(b, SSM_GROUPS, 2, SSM_STATE)
    return y.reshape(b, l, SSM_W), s_last[:, :, 0], s_last[:, :, 1]


def _ffn2_body(x_ref, gpre_ref, gpost_ref, wfi_ref, wfo_ref, o_ref):
    outs = _ffn(_row_parts(x_ref, FFN_PARTS), gpre_ref[...], gpost_ref[...], wfi_ref, wfo_ref)
    rows = x_ref.shape[0] // FFN_PARTS
    for i, o in enumerate(outs):
        o_ref[i * rows:(i + 1) * rows, :] = o


def _ffn2(x, gpre, gpost, wfi, wfo):
    n = x.shape[0]
    row = pl.BlockSpec((FFN_ROW_TILE, D_MODEL), lambda i: (i, 0))
    vec = _resident((1, D_MODEL))
    return pl.pallas_call(
        _ffn2_body,
        grid=(n // FFN_ROW_TILE,),
        in_specs=[row, vec, vec, _resident(wfi.shape), _resident(wfo.shape)],
        out_specs=row,
        out_shape=jax.ShapeDtypeStruct((n, D_MODEL), F32),
        compiler_params=_params("parallel"),
        name="ffn2",
    )(x, gpre, gpost, wfi, wfo)


def _t5_bucket(rel):
    half = N_BUCKETS // 2
    max_exact = half // 2
    ret = (rel > 0).astype(np.int32) * half
    n = np.abs(rel)
    large = max_exact + (np.log(np.maximum(n, 1) / max_exact) / math.log(MAX_DISTANCE / max_exact)
                         * (half - max_exact)).astype(np.int32)
    large = np.minimum(large, half - 1)
    return ret + np.where(n < max_exact, n, large)


def _band_bias(rel_table, n_q, n_back, n_k):
    i = np.arange(n_q)[:, None]
    j = np.arange(n_k)[None, :]
    bucket = _t5_bucket((j - n_back) - i).reshape(-1)
    onehot = np.zeros((N_BUCKETS, bucket.size), np.float32)
    onehot[bucket, np.arange(bucket.size)] = 1.0
    b = jnp.dot(rel_table.astype(F32).T, jnp.asarray(onehot), precision=lax.Precision.HIGHEST)
    return b.reshape(N_KV_HEADS, KV_REP * n_q, n_k)


def _sink_rows(sink, n_q):
    return jnp.repeat(sink.astype(F32).reshape(N_KV_HEADS, KV_REP), n_q, axis=1)[:, :, None]


def _twice_per_head(x, axis):
    shape = x.shape
    x = x.reshape(shape[:axis] + (N_KV_HEADS, 1, HEAD_DIM) + shape[axis + 1:])
    x = jnp.concatenate([x, x], axis=axis + 1)
    return x.reshape(shape[:axis] + (KV2_W,) + shape[axis + 1:])


def kernel(x_prompt, x_sample, cache_swa_k, cache_swa_v, cache_mem_k, cache_mem_v, state_ssm_re, state_ssm_im, mem_prompt, rel_bias_table, ff1_pre_g, ff1_post_g, w_ff1_in, w_ff1_out, mix_pre_g, mix_post_g, w_in, mem_norm_g, w_mem_kv, attn_sink, ssm_lambda_re, ssm_lambda_im, ssm_log_dt, ssm_b_re, ssm_b_im, ssm_c_re, ssm_c_im, ssm_d, w_ssm_glu, w_attn_br, w_mem_br, w_out, ff2_pre_g, ff2_post_g, w_ff2_in, w_ff2_out):
    assert ff1_pre_g.shape[0] == 1, "single-layer step"
    bp, lp, _ = x_prompt.shape
    bs, ls, _ = x_sample.shape
    vec = lambda g: g[0].reshape(1, D_MODEL).astype(F32)
    w16 = lambda w: w[0].astype(BF16)

    wfi1, wfo1, wfi2, wfo2 = w16(w_ff1_in), w16(w_ff1_out), w16(w_ff2_in), w16(w_ff2_out)
    w_in16 = w16(w_in)
    c_k, c_v, c_u = ATTN_W, ATTN_W + KV_W, ATTN_W + 2 * KV_W
    c_g = c_u + SSM_W + MEM_W
    q_scale = HEAD_DIM ** -0.5
    assert math.frexp(q_scale)[0] == 0.5, "power-of-two scale: folding it into the q columns is exact"
    wp = jnp.concatenate([w_in16[:, :c_k] * jnp.asarray(q_scale, BF16), w_in16[:, c_k:c_g],
                          _twice_per_head(w_in16[:, c_k:c_v], 1), _twice_per_head(w_in16[:, c_v:c_u], 1)], axis=1)
    wg = w_in16[:, c_g:]
    wab, wglu, wmb, wo = w16(w_attn_br), w16(w_ssm_glu), w16(w_mem_br), w16(w_out)

    wconv, wst, wot, a_tab = _ssm_weights(
        ssm_lambda_re[0], ssm_lambda_im[0], ssm_log_dt[0], ssm_b_re[0], ssm_b_im[0], ssm_c_re[0], ssm_c_im[0])
    ssm_w = (wconv, wst, wot, ssm_d[0].astype(F32).reshape(1, SSM_W), a_tab)

    mk_p, mv_p = _mem_kv(mem_prompt.reshape(bp * MEM_LEN, D_MODEL), vec(mem_norm_g), w16(w_mem_kv))

    def group(x, mixer, s0, mem_k, mem_v):
        b, l, _ = x.shape
        r3 = lambda t: t.reshape(b, l, t.shape[-1])
        x1, q, k, v, u, qm, k2, v2 = _ffn1_proj(x.reshape(b * l, D_MODEL), vec(ff1_pre_g), vec(ff1_post_g),
                                                wfi1, wfo1, vec(mix_pre_g), wp)
        y_ssm, s_re, s_im = _ssm_branch(r3(u), s0, ssm_w)
        x2 = mixer(r3(q), r3(k2), r3(v2), r3(qm), mem_k, mem_v, x1, y_ssm.reshape(b * l, SSM_W))
        y = _ffn2(x2, vec(ff2_pre_g), vec(ff2_post_g), wfi2, wfo2)
        return r3(y), r3(k), r3(v), s_re, s_im

    merge_w = (vec(mix_pre_g), wg, wglu, wab, wmb, wo, vec(mix_post_g))

    def mixer_prompt(q, k2, v2, qm, mem_k, mem_v, x1, y_ssm):
        bias = _band_bias(rel_bias_table, CHUNK, WINDOW, WINDOW + CHUNK)
        return _mixer_prompt(q, k2, v2, bias, _sink_rows(attn_sink[0], CHUNK), qm, mem_k, mem_v, x1, y_ssm, merge_w)

    def mixer_sample(q, k2, v2, qm, mem_k, mem_v, x1, y_ssm):
        n_back = cache_swa_k.shape[2]
        cache2 = lambda c: _twice_per_head(c[0].reshape(bs, n_back, KV_W).astype(BF16), 2)
        kk = jnp.concatenate([cache2(cache_swa_k), k2], axis=1)
        vv = jnp.concatenate([cache2(cache_swa_v), v2], axis=1)
        bias = _band_bias(rel_bias_table, ls, n_back, n_back + ls)
        return _mixer_sample(q, kk, vv, bias, _sink_rows(attn_sink[0], ls), qm, mem_k, mem_v, x1, y_ssm, merge_w)

    yp, pk, pv, pre, pim = group(x_prompt, mixer_prompt, jnp.zeros((bp, SSM_WIDTH), F32),
                                 mk_p.reshape(bp, MEM_LEN, MEM_W), mv_p.reshape(bp, MEM_LEN, MEM_W))
    s0 = jnp.stack([state_ssm_re[0], state_ssm_im[0]], axis=2).reshape(bs, SSM_WIDTH).astype(F32)
    ys, sk, sv, sre, sim = group(x_sample, mixer_sample, s0,
                                 cache_mem_k[0].reshape(bs, MEM_LEN, MEM_W),
                                 cache_mem_v[0].reshape(bs, MEM_LEN, MEM_W))

    n_keep = min(WINDOW, lp)
    heads = lambda t: t.reshape(t.shape[0], t.shape[1], N_KV_HEADS, HEAD_DIM)[None]
    mem_heads = lambda t: t.reshape(bp, MEM_LEN, MEM_HEADS, MEM_HEAD_DIM)[None]
    return (yp, ys, heads(pk[:, -n_keep:]), heads(pv[:, -n_keep:]), mem_heads(mk_p), mem_heads(mv_p),
            pre[None], pim[None], heads(sk), heads(sv), sre[None], sim[None])
```

```python
import functools
import math

import numpy as np
import jax
import jax.numpy as jnp
from jax import lax
from jax.experimental import pallas as pl
from jax.experimental.pallas import tpu as pltpu

D_MODEL = 1024
CHUNK = 64
WINDOW = 128
HEAD_DIM = 64
MIX_W = D_MODEL // 2
N_HEADS = MIX_W // HEAD_DIM
N_KV_HEADS = N_HEADS // 4
KV_REP = N_HEADS // N_KV_HEADS
ATTN_W = N_HEADS * HEAD_DIM
KV_W = N_KV_HEADS * HEAD_DIM
SSM_GROUP = 16
SSM_W = MIX_W
SSM_GROUPS = SSM_W // SSM_GROUP
SSM_STATE = 64
MEM_LEN = 256
MEM_HEADS = 4
MEM_HEAD_DIM = MIX_W // MEM_HEADS
MEM_W = MEM_HEADS * MEM_HEAD_DIM
D_FF = 128 * ((8 * D_MODEL // 3 + 127) // 128)
N_BUCKETS = 32
MAX_DISTANCE = 128
RMS_EPS = 1e-6
NEG_INF = -1e30

LANES = 128
MXU_TILE = 256
ROW_TILE = 256
MIXER_TILE = 512
SAMPLE_MIXER_TILE = 256
MIXER_PARTS = 2
FFN_ROW_TILE = 512
FFN_PARTS = 2
FFN_CHUNK = 2 * MXU_TILE
VMEM_LIMIT = 60 * 1024 * 1024

KV2_W = N_KV_HEADS * LANES

SSM_T = 16
SSM_K = SSM_T * SSM_GROUP
SSM_S2 = 2 * SSM_STATE
SSM_WIDTH = SSM_GROUPS * SSM_S2
SSM_GB = LANES // SSM_GROUP
SSM_NGB = SSM_GROUPS // SSM_GB
SSM_XW = SSM_T * LANES
SSM_SW = SSM_GB * SSM_S2
SSM_TPT = MXU_TILE // LANES
SSM_NT = SSM_T // SSM_TPT
SCAN_ROWS = 8
N_SCAN_TABLES = 10
SSM_CHUNKS_PER_STEP = 256

F32 = jnp.float32
BF16 = jnp.bfloat16


def _params(*sem):
    return pltpu.CompilerParams(dimension_semantics=sem, vmem_limit_bytes=VMEM_LIMIT)


def _resident(shape):
    zeros = (0,) * len(shape)
    return pl.BlockSpec(shape, lambda *_: zeros, pipeline_mode=pl.Buffered(1))


def _rms(x, g):
    return x * lax.rsqrt(jnp.mean(x * x, axis=-1, keepdims=True) + RMS_EPS) * g


def _mm(a, b):
    return jnp.dot(a, b, preferred_element_type=F32)


def _sigmoid(x):
    return 0.5 * jnp.tanh(0.5 * x) + 0.5


def _row_parts(ref, n_parts):
    rows = ref.shape[0] // n_parts
    return [ref[i * rows:(i + 1) * rows, :] for i in range(n_parts)]


def _ffn(xs, gpre, gpost, w_in_ref, w_out_ref):
    hs = [_rms(x, gpre).astype(BF16) for x in xs]
    acts = [[] for _ in xs]
    for c0 in range(0, D_FF, FFN_CHUNK):
        c1 = min(c0 + FFN_CHUNK, D_FF)
        for h, a in zip(hs, acts):
            g = _mm(h, w_in_ref[:, c0:c1])
            u = _mm(h, w_in_ref[:, D_FF + c0:D_FF + c1])
            a.append((g * jax.nn.sigmoid(g) * u).astype(BF16))
    outs = [_mm(jnp.concatenate(a, axis=1), w_out_ref[...]) for a in acts]
    return [x + 0.5 * _rms(o, gpost) for x, o in zip(xs, outs)]


PROJ_SPLIT = (("q", ATTN_W, BF16), ("k", KV_W, F32), ("v", KV_W, F32), ("u", SSM_W, F32),
              ("qm", MEM_W, BF16), ("k2", KV2_W, BF16), ("v2", KV2_W, BF16))


def _two_group_specs(steps_a):
    first = lambda w: pl.BlockSpec((FFN_ROW_TILE, w), lambda i: (jnp.minimum(i, steps_a - 1), 0))
    second = lambda w: pl.BlockSpec((FFN_ROW_TILE, w), lambda i: (jnp.maximum(i - steps_a, 0), 0))
    return first, second


def _on_group(steps_a, run, refs_a, refs_b):
    i = pl.program_id(0)

    @pl.when(i < steps_a)
    def _():
        run(*refs_a)

    @pl.when(i >= steps_a)
    def _():
        run(*refs_b)


def _ffn1_proj_body(steps_a, xa_ref, xb_ref, gpre_ref, gpost_ref, wfi_ref, wfo_ref, gmix_ref, wp_ref, *out_refs):
    def run(x_ref, x1_ref, *proj_refs):
        x1s = _ffn(_row_parts(x_ref, FFN_PARTS), gpre_ref[...], gpost_ref[...], wfi_ref, wfo_ref)
        rows = x_ref.shape[0] // FFN_PARTS
        ps = [_mm(_rms(x1, gmix_ref[...]).astype(BF16), wp_ref[...]) for x1 in x1s]
        for i, (x1, p) in enumerate(zip(x1s, ps)):
            part = slice(i * rows, (i + 1) * rows)
            x1_ref[part, :] = x1
            col = 0
            for ref, (_, width, dtype) in zip(proj_refs, PROJ_SPLIT):
                ref[part, :] = p[:, col:col + width].astype(dtype)
                col += width

    n_out = len(out_refs) // 2
    _on_group(steps_a, run, (xa_ref,) + out_refs[:n_out], (xb_ref,) + out_refs[n_out:])


def _ffn1_proj(xa, xb, gpre, gpost, wfi, wfo, gmix, wp):
    steps_a, steps_b = xa.shape[0] // FFN_ROW_TILE, xb.shape[0] // FFN_ROW_TILE
    first, second = _two_group_specs(steps_a)
    widths = [D_MODEL] + [w for _, w, _ in PROJ_SPLIT]
    dtypes = [F32] + [d for _, _, d in PROJ_SPLIT]
    outs = pl.pallas_call(
        functools.partial(_ffn1_proj_body, steps_a),
        grid=(steps_a + steps_b,),
        in_specs=[first(D_MODEL), second(D_MODEL), _resident((1, D_MODEL)), _resident((1, D_MODEL)),
                  _resident(wfi.shape), _resident(wfo.shape), _resident((1, D_MODEL)),
                  _resident(wp.shape)],
        out_specs=[first(w) for w in widths] + [second(w) for w in widths],
        out_shape=[jax.ShapeDtypeStruct((x.shape[0], w), d) for x in (xa, xb) for w, d in zip(widths, dtypes)],
        compiler_params=_params("arbitrary"),
        name="ffn1_proj",
    )(xa, xb, gpre, gpost, wfi, wfo, gmix, wp)
    return outs[:len(widths)], outs[len(widths):]


def _mem_kv_body(m_ref, g_ref, w_ref, k_ref, v_ref):
    kv = _mm(_rms(m_ref[...], g_ref[...]).astype(BF16), w_ref[...])
    k_ref[...] = kv[:, :MEM_W]
    v_ref[...] = kv[:, MEM_W:]


def _mem_kv(mem, g, w):
    n = mem.shape[0]
    row = lambda wd: pl.BlockSpec((ROW_TILE, wd), lambda i: (i, 0))
    return pl.pallas_call(
        _mem_kv_body,
        grid=(n // ROW_TILE,),
        in_specs=[row(D_MODEL), _resident((1, D_MODEL)), _resident(w.shape)],
        out_specs=[row(MEM_W), row(MEM_W)],
        out_shape=[jax.ShapeDtypeStruct((n, MEM_W), F32)] * 2,
        compiler_params=_params("parallel"),
        name="mem_kv",
    )(mem, g, w)


def _qk(q, k):
    return lax.dot_general(q, k, (((1,), (1,)), ((), ())), preferred_element_type=F32)


def _softmax_pv(scores, values, sinks=None):
    probs = []
    for i, s in enumerate(scores):
        m = jnp.max(s, axis=-1, keepdims=True)
        if sinks is not None:
            m = jnp.maximum(m, sinks[i])
        e = jnp.exp(s - m)
        den = jnp.sum(e, axis=-1, keepdims=True)
        if sinks is not None:
            den = den + jnp.exp(sinks[i] - m)
        probs.append((e * (1.0 / den)).astype(BF16))
    return [_mm(p, v) for p, v in zip(probs, values)]


def _gqa_queries(q):
    nq = q.shape[0]
    low = lax.broadcasted_iota(jnp.int32, (nq, LANES), 1) < HEAD_DIM
    zero = jnp.zeros((nq, LANES), BF16)
    stacks = []
    for g in range(N_KV_HEADS):
        rows = []
        for r in range(KV_REP):
            h = g * KV_REP + r
            q2 = q[:, (h // 2) * LANES:(h // 2 + 1) * LANES]
            rows.append(jnp.where(low, q2, zero) if h % 2 == 0 else jnp.where(low, zero, q2))
        stacks.append(jnp.concatenate(rows, axis=0))
    return stacks


def _gqa_outputs(outs, nq):
    low = lax.broadcasted_iota(jnp.int32, (nq, LANES), 1) < HEAD_DIM
    pairs = []
    for o in outs:
        for r in range(0, KV_REP, 2):
            pairs.append(jnp.where(low, o[r * nq:(r + 1) * nq], o[(r + 1) * nq:(r + 2) * nq]))
    return jnp.concatenate(pairs, axis=1)


def _mem_scores(qm, mk_ref):
    return [_qk(qm[:, h * MEM_HEAD_DIM:(h + 1) * MEM_HEAD_DIM],
                mk_ref[:, h * MEM_HEAD_DIM:(h + 1) * MEM_HEAD_DIM].astype(BF16)) * (MEM_HEAD_DIM ** -0.5)
            for h in range(MEM_HEADS)]


def _mem_values(mv_ref):
    return [mv_ref[:, h * MEM_HEAD_DIM:(h + 1) * MEM_HEAD_DIM].astype(BF16) for h in range(MEM_HEADS)]


def _kv_heads(x):
    return [x[:, g * LANES:(g + 1) * LANES] for g in range(N_KV_HEADS)]


def _merge_pre(x1, y, gmix_ref, wg_ref, wglu_ref):
    h = _rms(x1, gmix_ref[...]).astype(BF16)
    logits = [_mm(h, wg_ref[:, j * D_MODEL:(j + 1) * D_MODEL]) for j in range(3)]
    y = y.astype(BF16)
    return logits, _mm(y, wglu_ref[:, :D_MODEL]), _mm(y, wglu_ref[:, D_MODEL:])


def _mix_parts(parts, x1_ref, y_ref, gmix_ref, wg_ref, wglu_ref, wab_ref, wmb_ref, wo_ref, gpost_ref, x2_ref):
    x1s, ys = _row_parts(x1_ref, len(parts)), _row_parts(y_ref, len(parts))
    pres = [_merge_pre(x1, y, gmix_ref, wg_ref, wglu_ref) for x1, y in zip(x1s, ys)]
    branches = [assemble(_softmax_pv(scores, values, sinks), _softmax_pv(mem_scores, mem_values))
                for scores, values, sinks, mem_scores, mem_values, assemble in parts]
    projected = [(_mm(attn.astype(BF16), wab_ref[...]), _mm(memo.astype(BF16), wmb_ref[...]))
                 for attn, memo in branches]
    sig = _sigmoid
    merged = [sig(lg[0]) * pa + sig(lg[1]) * (ya * sig(yb)) + sig(lg[2]) * pm
              for (lg, ya, yb), (pa, pm) in zip(pres, projected)]
    outs = [_mm(m.astype(BF16), wo_ref[...]) for m in merged]
    rows = x1_ref.shape[0] // len(parts)
    for i, (x1, o) in enumerate(zip(x1s, outs)):
        x2_ref[i * rows:(i + 1) * rows, :] = x1 + _rms(o, gpost_ref[...])


def _mixer_prompt_body(q_ref, k_ref, kh_ref, v_ref, vh_ref, bias_ref, sink_ref, qm_ref, mk_ref, mv_ref,
                       x1_ref, y_ref, *merge_and_out_refs):
    i = pl.program_id(1)
    tq = q_ref.shape[1]
    kf = jnp.concatenate([kh_ref[0], k_ref[0]], axis=0)
    vf = jnp.concatenate([vh_ref[0], v_ref[0]], axis=0)
    band = WINDOW + CHUNK
    rows = tq // MIXER_PARTS
    parts = []
    for p in range(MIXER_PARTS):
        scores, values, sinks = [], [], []
        for lo in range(p * rows, (p + 1) * rows, CHUNK):
            valid = None
            if lo < WINDOW:
                key_pos = lax.broadcasted_iota(jnp.int32, (1, band), 1) + (i * tq + lo - WINDOW)
                valid = key_pos >= 0
            kb, vb = _kv_heads(kf[lo:lo + band]), _kv_heads(vf[lo:lo + band])
            for g, qs in enumerate(_gqa_queries(q_ref[0, lo:lo + CHUNK, :])):
                s = _qk(qs, kb[g]) + bias_ref[g]
                scores.append(s if valid is None else jnp.where(valid, s, NEG_INF))
                values.append(vb[g])
                sinks.append(sink_ref[g])
        mem_scores = _mem_scores(qm_ref[0, p * rows:(p + 1) * rows, :], mk_ref.at[0])

        def assemble(outs, mem):
            attn = jnp.concatenate([_gqa_outputs(outs[c:c + N_KV_HEADS], CHUNK)
                                    for c in range(0, len(outs), N_KV_HEADS)], axis=0)
            return attn, jnp.concatenate(mem, axis=1)

        parts.append((scores, values, sinks, mem_scores, _mem_values(mv_ref.at[0]), assemble))
    _mix_parts(parts, x1_ref, y_ref, *merge_and_out_refs)


def _merge_weight_specs(weights):
    return [_resident(w.shape) for w in weights]


def _mixer_prompt(q, k2, v2, bias, sink, qm, mk, mv, x1, y, merge_w):
    b, l, _ = q.shape
    tq = MIXER_TILE
    hb = tq // WINDOW
    nt = l // tq
    cur = lambda w: pl.BlockSpec((1, tq, w), lambda bi, i: (bi, i, 0))
    row = lambda w: pl.BlockSpec((tq, w), lambda bi, i: (bi * nt + i, 0))
    halo = pl.BlockSpec((1, WINDOW, KV2_W), lambda bi, i: (bi, jnp.maximum(i * hb - 1, 0), 0))
    mem = pl.BlockSpec((1, MEM_LEN, MEM_W), lambda bi, i: (bi, 0, 0))
    return pl.pallas_call(
        _mixer_prompt_body,
        grid=(b, nt),
        in_specs=[cur(ATTN_W), cur(KV2_W), halo, cur(KV2_W), halo, _resident(bias.shape),
                  _resident(sink.shape), cur(MEM_W), mem, mem, row(D_MODEL), row(SSM_W)]
        + _merge_weight_specs(merge_w),
        out_specs=row(D_MODEL),
        out_shape=jax.ShapeDtypeStruct((b * l, D_MODEL), F32),
        compiler_params=_params("parallel", "parallel"),
        name="mixer_prompt",
    )(q, k2, k2, v2, v2, bias, sink, qm, mk, mv, x1, y, *merge_w)


def _mixer_sample_body(q_ref, k_ref, v_ref, bias_ref, sink_ref, qm_ref, mk_ref, mv_ref,
                       x1_ref, y_ref, *merge_and_out_refs):
    nb, nq = q_ref.shape[0], q_ref.shape[1]
    per_part = nb // MIXER_PARTS
    parts = []
    for p in range(MIXER_PARTS):
        scores, values, sinks, mem_scores, mem_values = [], [], [], [], []
        for b in range(p * per_part, (p + 1) * per_part):
            kb, vb = _kv_heads(k_ref[b]), _kv_heads(v_ref[b])
            for g, qs in enumerate(_gqa_queries(q_ref[b])):
                scores.append(_qk(qs, kb[g]) + bias_ref[g])
                values.append(vb[g])
                sinks.append(sink_ref[g])
            mem_scores += _mem_scores(qm_ref[b], mk_ref.at[b])
            mem_values += _mem_values(mv_ref.at[b])

        def assemble(outs, mem):
            attn = jnp.concatenate([_gqa_outputs(outs[j:j + N_KV_HEADS], nq)
                                    for j in range(0, len(outs), N_KV_HEADS)], axis=0)
            memo = jnp.concatenate([jnp.concatenate(mem[j:j + MEM_HEADS], axis=1)
                                    for j in range(0, len(mem), MEM_HEADS)], axis=0)
            return attn, memo

        parts.append((scores, values, sinks, mem_scores, mem_values, assemble))
    _mix_parts(parts, x1_ref, y_ref, *merge_and_out_refs)


def _mixer_sample(q, kk2, vv2, bias, sink, qm, mk, mv, x1, y, merge_w):
    b, s, _ = q.shape
    nb = SAMPLE_MIXER_TILE // s
    nk = kk2.shape[1]
    blk = lambda *shape: pl.BlockSpec((nb,) + shape, lambda i: (i,) + (0,) * len(shape))
    row = lambda w: pl.BlockSpec((nb * s, w), lambda i: (i, 0))
    return pl.pallas_call(
        _mixer_sample_body,
        grid=(b // nb,),
        in_specs=[blk(s, ATTN_W), blk(nk, KV2_W), blk(nk, KV2_W),
                  _resident(bias.shape), _resident(sink.shape),
                  blk(s, MEM_W), blk(MEM_LEN, MEM_W), blk(MEM_LEN, MEM_W), row(D_MODEL), row(SSM_W)]
        + _merge_weight_specs(merge_w),
        out_specs=row(D_MODEL),
        out_shape=jax.ShapeDtypeStruct((b * s, D_MODEL), F32),
        compiler_params=_params("parallel"),
        name="mixer_sample",
    )(q, kk2, vv2, bias, sink, qm, mk, mv, x1, y, *merge_w)


def _ssm_weights_body(lam_re_row, lam_im_row, lam_re_col, lam_im_col, log_dt, b_re_t, b_im_t,
                      b_re_rows, b_im_rows, c_re_rows, c_im_rows, c_re_lanes, c_im_lanes, sel_ref,
                      wconv_ref, wst_ref, wot_ref, a_tab_ref):
    wst_ref[0] = jnp.zeros(wst_ref.shape[1:], BF16)
    wot_ref[0] = jnp.zeros(wot_ref.shape[1:], BF16)
    krows = []
    for gl in range(SSM_GB):
        krows.append(_ssm_group_weights(
            gl, *(r[gl] for r in (lam_re_row, lam_im_row, lam_re_col, lam_im_col, log_dt, b_re_t, b_im_t,
                                  b_re_rows, b_im_rows, c_re_rows, c_im_rows, c_re_lanes, c_im_lanes)),
            wst_ref, wot_ref, a_tab_ref))
    kstack = jnp.concatenate(krows, axis=0).astype(BF16)
    for d in range(SSM_NT):
        for ti in range(SSM_TPT):
            blk = _mm(kstack, sel_ref[d * SSM_TPT + ti])
            r0 = (SSM_NT - 1 - d) * MXU_TILE + ti * LANES
            for gl in range(SSM_GB):
                piece = blk[gl * SSM_GROUP:(gl + 1) * SSM_GROUP]
                piece = (piece if gl == 0 else pltpu.roll(piece, gl * SSM_GROUP, 1)).astype(BF16)
                r = r0 + gl * SSM_GROUP
                wconv_ref[0, r:r + SSM_GROUP, :MXU_TILE] = piece
                if r0 >= MXU_TILE:
                    wconv_ref[0, r - MXU_TILE:r - MXU_TILE + SSM_GROUP, MXU_TILE:] = piece
    wconv_ref[0, (SSM_NT - 1) * MXU_TILE:, MXU_TILE:] = jnp.zeros((MXU_TILE, MXU_TILE), BF16)


def _ssm_group_weights(gl, lam_re_row, lam_im_row, lam_re_col, lam_im_col, log_dt, b_re_t, b_im_t,
                       b_re_rows, b_im_rows, c_re_rows, c_im_rows, c_re_lanes, c_im_lanes,
                       wst_ref, wot_ref, a_tab_ref):
    dt = jnp.exp(log_dt)
    rows = lambda t: slice(t * LANES + gl * SSM_GROUP, t * LANES + (gl + 1) * SSM_GROUP)
    lanes = slice(gl * SSM_S2, (gl + 1) * SSM_S2)

    def zoh_coef(lr, li):
        mag = jnp.exp(lr * dt)
        a_re, a_im = mag * jnp.cos(li * dt), mag * jnp.sin(li * dt)
        den = lr * lr + li * li
        return ((a_re - 1.0) * lr + a_im * li) / den, (a_im * lr - (a_re - 1.0) * li) / den

    def a_power(lr, li, n):
        mag = jnp.exp(lr * dt * n)
        return mag * jnp.cos(li * dt * n), mag * jnp.sin(li * dt * n)

    lr, li = lam_re_row, lam_im_row
    cr, ci = zoh_coef(lr, li)
    n_rows = jnp.minimum(lax.broadcasted_iota(jnp.int32, (SSM_T + SCAN_ROWS, 1), 0), SSM_T).astype(F32)
    pw_r, pw_i = a_power(lr, li, n_rows)

    def over_channels(pw, exps):
        return jnp.concatenate([jnp.broadcast_to(pw[e:e + 1], (SSM_GROUP, SSM_STATE)) for e in exps], axis=0)

    back = [SSM_T - 1 - t for t in range(SSM_T)]
    pr, pi = over_channels(pw_r, back), over_channels(pw_i, back)
    zr, zi = pr * cr - pi * ci, pr * ci + pi * cr
    br, bi = b_re_rows, b_im_rows
    wstate = jnp.concatenate([zr * br - zi * bi, zr * bi + zi * br], axis=1).astype(BF16)

    fwd = [t + 1 for t in range(SSM_T)]
    qr, qi = over_channels(pw_r, fwd), over_channels(pw_i, fwd)
    ccr, cci = c_re_rows, c_im_rows
    wout_t = jnp.concatenate([ccr * qr - cci * qi, -(ccr * qi + cci * qr)], axis=1).astype(BF16)
    for t in range(SSM_T):
        wst_ref[0, rows(t), lanes] = wstate[t * SSM_GROUP:(t + 1) * SSM_GROUP]
        wot_ref[0, rows(t), lanes] = wout_t[t * SSM_GROUP:(t + 1) * SSM_GROUP]

    lrc, lic = lam_re_col, lam_im_col
    hi = lax.Precision.HIGHEST
    n_lanes = jnp.minimum(lax.broadcasted_iota(jnp.int32, (1, LANES), 1), SSM_T).astype(F32)
    pc_r, pc_i = a_power(lrc, lic, n_lanes)
    spread = (lax.broadcasted_iota(jnp.int32, (LANES, SSM_K), 0)
              == lax.broadcasted_iota(jnp.int32, (LANES, SSM_K), 1) // SSM_GROUP).astype(F32)
    gr = jnp.dot(pc_r, spread, precision=hi, preferred_element_type=F32)
    gi = jnp.dot(pc_i, spread, precision=hi, preferred_element_type=F32)
    clr, cli = c_re_lanes, c_im_lanes
    g_re, g_im = clr * gr - cli * gi, clr * gi + cli * gr
    btr, bti = b_re_t, b_im_t
    bbr, bbi = cr * btr - ci * bti, cr * bti + ci * btr
    krow = (jnp.dot(bbr, g_re, precision=hi, preferred_element_type=F32)
            - jnp.dot(bbi, g_im, precision=hi, preferred_element_type=F32))

    idx = lax.broadcasted_iota(jnp.int32, (N_SCAN_TABLES * SCAN_ROWS, SSM_STATE), 0)
    tab, r = idx // SCAN_ROWS, idx % SCAN_ROWS
    stride = jnp.where(tab < 2, 1, jnp.where(tab < 4, 2, 4))
    n = jnp.where(tab < 6, stride, jnp.where(tab < 8, r, SCAN_ROWS))
    keep = jnp.logical_or(tab >= 6, r >= stride)
    er, ei = a_power(lr, li, (n * SSM_T).astype(F32))
    er, ei = jnp.where(keep, er, 0.0), jnp.where(keep, ei, 0.0)
    odd = tab % 2 == 1
    tabs = jnp.concatenate([jnp.where(odd, -ei, er), jnp.where(odd, ei, er)], axis=1)
    a_tab_ref[:, :, lanes] = tabs.reshape(N_SCAN_TABLES, SCAN_ROWS, SSM_S2)
    return krow


def _lag_selectors():
    sel = np.zeros((SSM_NT, SSM_TPT, SSM_K, MXU_TILE), np.float32)
    ch = np.arange(SSM_GROUP)
    for d in range(SSM_NT):
        for ti in range(SSM_TPT):
            for to in range(SSM_TPT):
                lag = SSM_TPT * d + to - ti
                if lag >= 0:
                    sel[d, ti, lag * SSM_GROUP + ch, to * LANES + ch] = 1.0
    return jnp.asarray(sel.reshape(SSM_NT * SSM_TPT, SSM_K, MXU_TILE), BF16)


def _ssm_weights(lam_re, lam_im, log_dt, b_re, b_im, c_re, c_im):
    g, p, c = b_re.shape
    row3 = lambda x: x.reshape(g, 1, p)
    col3 = lambda x: x.reshape(g, p, 1)
    t3 = lambda x: jnp.transpose(x, (0, 2, 1))
    b_rows = lambda x: jnp.tile(t3(x), (1, SSM_T, 1))
    c_rows = lambda x: jnp.tile(x, (1, SSM_T, 1))
    c_lanes = lambda x: jnp.tile(t3(x), (1, 1, SSM_T))
    args = (row3(lam_re), row3(lam_im), col3(lam_re), col3(lam_im), log_dt.reshape(g, 1, 1),
            t3(b_re), t3(b_im), b_rows(b_re), b_rows(b_im), c_rows(c_re), c_rows(c_im),
            c_lanes(c_re), c_lanes(c_im))
    sel = _lag_selectors()
    spec = lambda x: pl.BlockSpec((SSM_GB,) + x.shape[1:], lambda i: (i, 0, 0))
    per_gb = lambda *shape: pl.BlockSpec((1,) + shape, lambda i: (i, 0, 0))
    return pl.pallas_call(
        _ssm_weights_body,
        grid=(SSM_NGB,),
        in_specs=[spec(a) for a in args] + [_resident(sel.shape)],
        out_specs=[per_gb(SSM_NT * MXU_TILE, 2 * MXU_TILE), per_gb(SSM_XW, SSM_SW), per_gb(SSM_XW, SSM_SW),
                   pl.BlockSpec((N_SCAN_TABLES, SCAN_ROWS, SSM_SW), lambda i: (0, 0, i))],
        out_shape=[jax.ShapeDtypeStruct((SSM_NGB, SSM_NT * MXU_TILE, 2 * MXU_TILE), BF16),
                   jax.ShapeDtypeStruct((SSM_NGB, SSM_XW, SSM_SW), BF16),
                   jax.ShapeDtypeStruct((SSM_NGB, SSM_XW, SSM_SW), BF16),
                   jax.ShapeDtypeStruct((N_SCAN_TABLES, SCAN_ROWS, SSM_WIDTH), F32)],
        compiler_params=_params("parallel"),
        name="ssm_weights",
    )(*args, sel)


def _swap_halves(s):
    ax = s.ndim - 1
    return jnp.concatenate([pltpu.roll(s[..., l:l + SSM_S2], SSM_STATE, ax)
                            for l in range(0, s.shape[ax], SSM_S2)], axis=ax)


def _ssm_body(row_sets, slab_rows, u_ref, wconv_ref, wst_ref, wot_ref, d_ref, a_tab_ref, s0_ref,
              y_ref, s_ref, v_scr, s_scr):
    def piece(ref_set, first, n, stride, t):
        return ref_set, pl.ds(first + t, n, stride=stride)

    @pl.when(pl.program_id(1) == 0)
    def _():
        s_scr[...] = s0_ref[0]

    x = jnp.concatenate(
        [jnp.concatenate([u_ref[piece(*rs, t)] for t in range(SSM_T)], axis=1) for rs in row_sets],
        axis=0).astype(BF16)
    v_scr[...] = _mm(x, wst_ref[0])
    conv = []
    for jo in range(0, SSM_NT, 2):
        pair = _mm(x[:, :(jo + 2) * MXU_TILE], wconv_ref[0, (SSM_NT - 2 - jo) * MXU_TILE:, :])
        conv += [pair[:, MXU_TILE:], pair[:, :MXU_TILE]]

    n_rows = v_scr.shape[0]
    if slab_rows:
        a_mul, a_swap = a_tab_ref[6, 1:2, :], a_tab_ref[7, 1:2, :]
        s = s_scr[...]
        for c in range(n_rows // slab_rows):
            rows = slice(c * slab_rows, (c + 1) * slab_rows)
            inc = v_scr[rows, :]
            v_scr[rows, :] = s
            s = a_mul * s + a_swap * _swap_halves(s) + inc
        s_scr[...] = s
    else:
        n_blocks = n_rows // SCAN_ROWS
        xs = v_scr[...].reshape(n_blocks, SCAN_ROWS, SSM_SW)
        for k in range(3):
            sh = pltpu.roll(xs, 1 << k, 1)
            xs = xs + a_tab_ref[2 * k] * sh + a_tab_ref[2 * k + 1] * _swap_halves(sh)
        carries, blk = [], 0
        for j, (_, _, n, _) in enumerate(row_sets):
            carry = s_scr[j * SCAN_ROWS:(j + 1) * SCAN_ROWS, :]
            for _ in range(n // SCAN_ROWS):
                carries.append(carry)
                last = jnp.broadcast_to(xs[blk, SCAN_ROWS - 1:, :], carry.shape)
                carry = last + a_tab_ref[8] * carry + a_tab_ref[9] * _swap_halves(carry)
                blk += 1
            s_scr[j * SCAN_ROWS:(j + 1) * SCAN_ROWS, :] = carry
        carries = jnp.stack(carries, axis=0)
        not_first = lax.broadcasted_iota(jnp.int32, (1, SCAN_ROWS, 1), 1) >= 1
        enter = (jnp.where(not_first, pltpu.roll(xs, 1, 1), 0.0)
                 + a_tab_ref[6] * carries + a_tab_ref[7] * _swap_halves(carries))
        v_scr[...] = enter.reshape(n_rows, SSM_SW)
    s_ref[0] = s_scr[...]

    enter = v_scr[...].astype(BF16)
    d = d_ref[...]
    carried = _qk(enter, wot_ref[0])
    for jo in range(SSM_NT):
        yt = conv[jo] + carried[:, jo * MXU_TILE:(jo + 1) * MXU_TILE]
        r0 = 0
        for rs in row_sets:
            for tl in range(SSM_TPT):
                idx = piece(*rs, jo * SSM_TPT + tl)
                y_ref[idx] = yt[r0:r0 + rs[2], tl * LANES:(tl + 1) * LANES] + u_ref[idx] * d
            r0 += rs[2]


def _ssm(u, row_sets, slab_rows, block_rows, wconv, wst, wo, d_lanes, a_tab, s0):
    ns, r, _ = u.shape
    chunk_rows = sum(rs[2] for rs in row_sets)
    carry_rows = s0.shape[1]
    blk = pl.BlockSpec((ns, block_rows, LANES), lambda gb, i: (0, i, gb))
    per_gb = lambda x: pl.BlockSpec((1,) + x.shape[1:], lambda gb, i: (gb, 0, 0))
    return pl.pallas_call(
        functools.partial(_ssm_body, row_sets, slab_rows),
        grid=(SSM_NGB, r // block_rows),
        in_specs=[blk, per_gb(wconv), per_gb(wst), per_gb(wo),
                  pl.BlockSpec((1, LANES), lambda gb, i: (0, gb)),
                  pl.BlockSpec((N_SCAN_TABLES, SCAN_ROWS, SSM_SW), lambda gb, i: (0, 0, gb)),
                  per_gb(s0)],
        out_specs=[blk, per_gb(s0)],
        out_shape=[jax.ShapeDtypeStruct(u.shape, F32), jax.ShapeDtypeStruct(s0.shape, F32)],
        scratch_shapes=[pltpu.VMEM((chunk_rows, SSM_SW), F32), pltpu.VMEM((carry_rows, SSM_SW), F32)],
        compiler_params=_params("parallel", "arbitrary"),
        name="ssm_scan",
    )(u, wconv, wst, wo, d_lanes, a_tab, s0)


def _ssm_branch(u, s0, ssm_w):
    b, l, _ = u.shape
    nc = l // SSM_T
    by_gb = lambda s: s.reshape(s.shape[0], SSM_NGB, SSM_SW).transpose(1, 0, 2)
    if b % SCAN_ROWS == 0:
        row_sets = tuple((0, c * SSM_T, b, l) for c in range(nc))
        y, s_last = _ssm(u.reshape(1, b * l, SSM_W), row_sets, b, b * l, *ssm_w, by_gb(s0))
    else:
        cps = min(nc, SSM_CHUNKS_PER_STEP)
        row_sets = tuple((j, 0, cps, SSM_T) for j in range(b))
        y, s_last = _ssm(u, row_sets, 0, cps * SSM_T, *ssm_w, by_gb(jnp.repeat(s0, SCAN_ROWS, axis=0)))
        s_last = s_last[:, ::SCAN_ROWS]
    s_last = s_last.transpose(1, 0, 2).reshape(b, SSM_GROUPS, 2, SSM_STATE)
    return y.reshape(b, l, SSM_W), s_last[:, :, 0], s_last[:, :, 1]


def _ffn2_body(steps_a, xa_ref, xb_ref, gpre_ref, gpost_ref, wfi_ref, wfo_ref, oa_ref, ob_ref):
    def run(x_ref, o_ref):
        outs = _ffn(_row_parts(x_ref, FFN_PARTS), gpre_ref[...], gpost_ref[...], wfi_ref, wfo_ref)
        rows = x_ref.shape[0] // FFN_PARTS
        for i, o in enumerate(outs):
            o_ref[i * rows:(i + 1) * rows, :] = o

    _on_group(steps_a, run, (xa_ref, oa_ref), (xb_ref, ob_ref))


def _ffn2(xa, xb, gpre, gpost, wfi, wfo):
    steps_a, steps_b = xa.shape[0] // FFN_ROW_TILE, xb.shape[0] // FFN_ROW_TILE
    first, second = _two_group_specs(steps_a)
    vec = _resident((1, D_MODEL))
    return pl.pallas_call(
        functools.partial(_ffn2_body, steps_a),
        grid=(steps_a + steps_b,),
        in_specs=[first(D_MODEL), second(D_MODEL), vec, vec, _resident(wfi.shape), _resident(wfo.shape)],
        out_specs=[first(D_MODEL), second(D_MODEL)],
        out_shape=[jax.ShapeDtypeStruct(xa.shape, F32), jax.ShapeDtypeStruct(xb.shape, F32)],
        compiler_params=_params("arbitrary"),
        name="ffn2",
    )(xa, xb, gpre, gpost, wfi, wfo)


def _t5_bucket(rel):
    half = N_BUCKETS // 2
    max_exact = half // 2
    ret = (rel > 0).astype(np.int32) * half
    n = np.abs(rel)
    large = max_exact + (np.log(np.maximum(n, 1) / max_exact) / math.log(MAX_DISTANCE / max_exact)
                         * (half - max_exact)).astype(np.int32)
    large = np.minimum(large, half - 1)
    return ret + np.where(n < max_exact, n, large)


def _band_bias(rel_table, n_q, n_back, n_k):
    i = np.arange(n_q)[:, None]
    j = np.arange(n_k)[None, :]
    bucket = _t5_bucket((j - n_back) - i).reshape(-1)
    onehot = np.zeros((N_BUCKETS, bucket.size), np.float32)
    onehot[bucket, np.arange(bucket.size)] = 1.0
    b = jnp.dot(rel_table.astype(F32).T, jnp.asarray(onehot), precision=lax.Precision.HIGHEST)
    return b.reshape(N_KV_HEADS, KV_REP * n_q, n_k)


def _sink_rows(sink, n_q):
    return jnp.repeat(sink.astype(F32).reshape(N_KV_HEADS, KV_REP), n_q, axis=1)[:, :, None]


def _twice_per_head(x, axis):
    shape = x.shape
    x = x.reshape(shape[:axis] + (N_KV_HEADS, 1, HEAD_DIM) + shape[axis + 1:])
    x = jnp.concatenate([x, x], axis=axis + 1)
    return x.reshape(shape[:axis] + (KV2_W,) + shape[axis + 1:])


def kernel(x_prompt, x_sample, cache_swa_k, cache_swa_v, cache_mem_k, cache_mem_v, state_ssm_re, state_ssm_im, mem_prompt, rel_bias_table, ff1_pre_g, ff1_post_g, w_ff1_in, w_ff1_out, mix_pre_g, mix_post_g, w_in, mem_norm_g, w_mem_kv, attn_sink, ssm_lambda_re, ssm_lambda_im, ssm_log_dt, ssm_b_re, ssm_b_im, ssm_c_re, ssm_c_im, ssm_d, w_ssm_glu, w_attn_br, w_mem_br, w_out, ff2_pre_g, ff2_post_g, w_ff2_in, w_ff2_out):
    assert ff1_pre_g.shape[0] == 1, "single-layer step"
    bp, lp, _ = x_prompt.shape
    bs, ls, _ = x_sample.shape
    vec = lambda g: g[0].reshape(1, D_MODEL).astype(F32)
    w16 = lambda w: w[0].astype(BF16)

    wfi1, wfo1, wfi2, wfo2 = w16(w_ff1_in), w16(w_ff1_out), w16(w_ff2_in), w16(w_ff2_out)
    w_in16 = w16(w_in)
    c_k, c_v, c_u = ATTN_W, ATTN_W + KV_W, ATTN_W + 2 * KV_W
    c_g = c_u + SSM_W + MEM_W
    q_scale = HEAD_DIM ** -0.5
    assert math.frexp(q_scale)[0] == 0.5, "power-of-two scale: folding it into the q columns is exact"
    wp = jnp.concatenate([w_in16[:, :c_k] * jnp.asarray(q_scale, BF16), w_in16[:, c_k:c_g],
                          _twice_per_head(w_in16[:, c_k:c_v], 1), _twice_per_head(w_in16[:, c_v:c_u], 1)], axis=1)
    wg = w_in16[:, c_g:]
    wab, wglu, wmb, wo = w16(w_attn_br), w16(w_ssm_glu), w16(w_mem_br), w16(w_out)

    wconv, wst, wot, a_tab = _ssm_weights(
        ssm_lambda_re[0], ssm_lambda_im[0], ssm_log_dt[0], ssm_b_re[0], ssm_b_im[0], ssm_c_re[0], ssm_c_im[0])
    ssm_w = (wconv, wst, wot, ssm_d[0].astype(F32).reshape(1, SSM_W), a_tab)

    mk_p, mv_p = _mem_kv(mem_prompt.reshape(bp * MEM_LEN, D_MODEL), vec(mem_norm_g), w16(w_mem_kv))

    def mix(x_shape, proj, mixer, s0, mem_k, mem_v):
        b, l, _ = x_shape
        r3 = lambda t: t.reshape(b, l, t.shape[-1])
        x1, q, k, v, u, qm, k2, v2 = proj
        y_ssm, s_re, s_im = _ssm_branch(r3(u), s0, ssm_w)
        x2 = mixer(r3(q), r3(k2), r3(v2), r3(qm), mem_k, mem_v, x1, y_ssm.reshape(b * l, SSM_W))
        return x2, r3(k), r3(v), s_re, s_im

    merge_w = (vec(mix_pre_g), wg, wglu, wab, wmb, wo, vec(mix_post_g))

    def mixer_prompt(q, k2, v2, qm, mem_k, mem_v, x1, y_ssm):
        bias = _band_bias(rel_bias_table, CHUNK, WINDOW, WINDOW + CHUNK)
        return _mixer_prompt(q, k2, v2, bias, _sink_rows(attn_sink[0], CHUNK), qm, mem_k, mem_v, x1, y_ssm, merge_w)

    def mixer_sample(q, k2, v2, qm, mem_k, mem_v, x1, y_ssm):
        n_back = cache_swa_k.shape[2]
        cache2 = lambda c: _twice_per_head(c[0].reshape(bs, n_back, KV_W).astype(BF16), 2)
        kk = jnp.concatenate([cache2(cache_swa_k), k2], axis=1)
        vv = jnp.concatenate([cache2(cache_swa_v), v2], axis=1)
        bias = _band_bias(rel_bias_table, ls, n_back, n_back + ls)
        return _mixer_sample(q, kk, vv, bias, _sink_rows(attn_sink[0], ls), qm, mem_k, mem_v, x1, y_ssm, merge_w)

    proj_p, proj_s = _ffn1_proj(x_prompt.reshape(bp * lp, D_MODEL), x_sample.reshape(bs * ls, D_MODEL),
                                vec(ff1_pre_g), vec(ff1_post_g), wfi1, wfo1, vec(mix_pre_g), wp)
    x2p, pk, pv, pre, pim = mix(x_prompt.shape, proj_p, mixer_prompt, jnp.zeros((bp, SSM_WIDTH), F32),
                                mk_p.reshape(bp, MEM_LEN, MEM_W), mv_p.reshape(bp, MEM_LEN, MEM_W))
    s0 = jnp.stack([state_ssm_re[0], state_ssm_im[0]], axis=2).reshape(bs, SSM_WIDTH).astype(F32)
    x2s, sk, sv, sre, sim = mix(x_sample.shape, proj_s, mixer_sample, s0,
                                cache_mem_k[0].reshape(bs, MEM_LEN, MEM_W),
                                cache_mem_v[0].reshape(bs, MEM_LEN, MEM_W))
    yp, ys = _ffn2(x2p, x2s, vec(ff2_pre_g), vec(ff2_post_g), wfi2, wfo2)
    yp, ys = yp.reshape(x_prompt.shape), ys.reshape(x_sample.shape)

    n_keep = min(WINDOW, lp)
    heads = lambda t: t.reshape(t.shape[0], t.shape[1], N_KV_HEADS, HEAD_DIM)[None]
    mem_heads = lambda t: t.reshape(bp, MEM_LEN, MEM_HEADS, MEM_HEAD_DIM)[None]
    return (yp, ys, heads(pk[:, -n_keep:]), heads(pv[:, -n_keep:]), mem_heads(mk_p), mem_heads(mv_p),
            pre[None], pim[None], heads(sk), heads(sv), sre[None], sim[None])
```

```python
import functools
import math

import numpy as np
import jax
import jax.numpy as jnp
from jax import lax
from jax.experimental import pallas as pl
from jax.experimental.pallas import tpu as pltpu

D_MODEL = 1024
CHUNK = 64
WINDOW = 128
HEAD_DIM = 64
MIX_W = D_MODEL // 2
N_HEADS = MIX_W // HEAD_DIM
N_KV_HEADS = N_HEADS // 4
KV_REP = N_HEADS // N_KV_HEADS
ATTN_W = N_HEADS * HEAD_DIM
KV_W = N_KV_HEADS * HEAD_DIM
SSM_GROUP = 16
SSM_W = MIX_W
SSM_GROUPS = SSM_W // SSM_GROUP
SSM_STATE = 64
MEM_LEN = 256
MEM_HEADS = 4
MEM_HEAD_DIM = MIX_W // MEM_HEADS
MEM_W = MEM_HEADS * MEM_HEAD_DIM
D_FF = 128 * ((8 * D_MODEL // 3 + 127) // 128)
N_BUCKETS = 32
MAX_DISTANCE = 128
RMS_EPS = 1e-6
NEG_INF = -1e30

LANES = 128
MXU_TILE = 256
ROW_TILE = 256
MIXER_TILE = 512
SAMPLE_MIXER_TILE = 256
MIXER_PARTS = 2
FFN_ROW_TILE = 512
FFN_PARTS = 2
FFN_CHUNK = 2 * MXU_TILE
VMEM_LIMIT = 60 * 1024 * 1024

KV2_W = N_KV_HEADS * LANES

SSM_T = 16
SSM_K = SSM_T * SSM_GROUP
SSM_S2 = 2 * SSM_STATE
SSM_WIDTH = SSM_GROUPS * SSM_S2
SSM_GB = LANES // SSM_GROUP
SSM_NGB = SSM_GROUPS // SSM_GB
SSM_XW = SSM_T * LANES
SSM_SW = SSM_GB * SSM_S2
SSM_TPT = MXU_TILE // LANES
SSM_NT = SSM_T // SSM_TPT
SCAN_ROWS = 8
N_SCAN_TABLES = 10
SSM_CHUNKS_PER_STEP = 256

F32 = jnp.float32
BF16 = jnp.bfloat16


def _params(*sem):
    return pltpu.CompilerParams(dimension_semantics=sem, vmem_limit_bytes=VMEM_LIMIT)


def _resident(shape):
    zeros = (0,) * len(shape)
    return pl.BlockSpec(shape, lambda *_: zeros, pipeline_mode=pl.Buffered(1))


def _rms(x, g):
    return x * lax.rsqrt(jnp.mean(x * x, axis=-1, keepdims=True) + RMS_EPS) * g


def _mm(a, b):
    return jnp.dot(a, b, preferred_element_type=F32)


def _sigmoid(x):
    return 0.5 * jnp.tanh(0.5 * x) + 0.5


def _row_parts(ref, n_parts):
    rows = ref.shape[0] // n_parts
    return [ref[i * rows:(i + 1) * rows, :] for i in range(n_parts)]


def _ffn(xs, gpre, gpost, w_in_ref, w_out_ref):
    hs = [_rms(x, gpre).astype(BF16) for x in xs]
    acts = [[] for _ in xs]
    for c0 in range(0, D_FF, FFN_CHUNK):
        c1 = min(c0 + FFN_CHUNK, D_FF)
        for h, a in zip(hs, acts):
            g = _mm(h, w_in_ref[:, c0:c1])
            u = _mm(h, w_in_ref[:, D_FF + c0:D_FF + c1])
            a.append((g * jax.nn.sigmoid(g) * u).astype(BF16))
    outs = [_mm(jnp.concatenate(a, axis=1), w_out_ref[...]) for a in acts]
    return [x + 0.5 * _rms(o, gpost) for x, o in zip(xs, outs)]


PROJ_SPLIT = (("q", ATTN_W, BF16), ("k", KV_W, F32), ("v", KV_W, F32), ("u", SSM_W, F32),
              ("qm", MEM_W, BF16), ("k2", KV2_W, BF16), ("v2", KV2_W, BF16))
N_PROJ_MATMUL = 5


def _twice_per_head_lanes(x):
    assert N_KV_HEADS == 2 and KV_W == LANES
    swapped = pltpu.roll(x, HEAD_DIM, 1)
    low = lax.broadcasted_iota(jnp.int32, x.shape, 1) < HEAD_DIM
    return jnp.concatenate([jnp.where(low, x, swapped), jnp.where(low, swapped, x)], axis=1)


def _two_group_specs(steps_a):
    first = lambda w: pl.BlockSpec((FFN_ROW_TILE, w), lambda i: (jnp.minimum(i, steps_a - 1), 0))
    second = lambda w: pl.BlockSpec((FFN_ROW_TILE, w), lambda i: (jnp.maximum(i - steps_a, 0), 0))
    return first, second


def _on_group(steps_a, run, refs_a, refs_b):
    i = pl.program_id(0)

    @pl.when(i < steps_a)
    def _():
        run(*refs_a)

    @pl.when(i >= steps_a)
    def _():
        run(*refs_b)


def _ffn1_proj_body(steps_a, xa_ref, xb_ref, gpre_ref, gpost_ref, wfi_ref, wfo_ref, gmix_ref, wp_ref, *out_refs):
    def run(x_ref, x1_ref, *proj_refs):
        x1s = _ffn(_row_parts(x_ref, FFN_PARTS), gpre_ref[...], gpost_ref[...], wfi_ref, wfo_ref)
        rows = x_ref.shape[0] // FFN_PARTS
        ps = [_mm(_rms(x1, gmix_ref[...]).astype(BF16), wp_ref[...]) for x1 in x1s]
        for i, (x1, p) in enumerate(zip(x1s, ps)):
            part = slice(i * rows, (i + 1) * rows)
            x1_ref[part, :] = x1
            col, cols = 0, {}
            for name, width, _ in PROJ_SPLIT[:N_PROJ_MATMUL]:
                cols[name] = p[:, col:col + width]
                col += width
            cols["k2"], cols["v2"] = _twice_per_head_lanes(cols["k"]), _twice_per_head_lanes(cols["v"])
            for ref, (name, _, dtype) in zip(proj_refs, PROJ_SPLIT):
                ref[part, :] = cols[name].astype(dtype)

    n_out = len(out_refs) // 2
    _on_group(steps_a, run, (xa_ref,) + out_refs[:n_out], (xb_ref,) + out_refs[n_out:])


def _ffn1_proj(xa, xb, gpre, gpost, wfi, wfo, gmix, wp):
    steps_a, steps_b = xa.shape[0] // FFN_ROW_TILE, xb.shape[0] // FFN_ROW_TILE
    first, second = _two_group_specs(steps_a)
    widths = [D_MODEL] + [w for _, w, _ in PROJ_SPLIT]
    dtypes = [F32] + [d for _, _, d in PROJ_SPLIT]
    outs = pl.pallas_call(
        functools.partial(_ffn1_proj_body, steps_a),
        grid=(steps_a + steps_b,),
        in_specs=[first(D_MODEL), second(D_MODEL), _resident((1, D_MODEL)), _resident((1, D_MODEL)),
                  _resident(wfi.shape), _resident(wfo.shape), _resident((1, D_MODEL)),
                  _resident(wp.shape)],
        out_specs=[first(w) for w in widths] + [second(w) for w in widths],
        out_shape=[jax.ShapeDtypeStruct((x.shape[0], w), d) for x in (xa, xb) for w, d in zip(widths, dtypes)],
        compiler_params=_params("arbitrary"),
        name="ffn1_proj",
    )(xa, xb, gpre, gpost, wfi, wfo, gmix, wp)
    return outs[:len(widths)], outs[len(widths):]


def _mem_kv_body(m_ref, g_ref, w_ref, k_ref, v_ref):
    kv = _mm(_rms(m_ref[...], g_ref[...]).astype(BF16), w_ref[...])
    k_ref[...] = kv[:, :MEM_W]
    v_ref[...] = kv[:, MEM_W:]


def _mem_kv(mem, g, w):
    n = mem.shape[0]
    row = lambda wd: pl.BlockSpec((ROW_TILE, wd), lambda i: (i, 0))
    return pl.pallas_call(
        _mem_kv_body,
        grid=(n // ROW_TILE,),
        in_specs=[row(D_MODEL), _resident((1, D_MODEL)), _resident(w.shape)],
        out_specs=[row(MEM_W), row(MEM_W)],
        out_shape=[jax.ShapeDtypeStruct((n, MEM_W), F32)] * 2,
        compiler_params=_params("parallel"),
        name="mem_kv",
    )(mem, g, w)


def _qk(q, k):
    return lax.dot_general(q, k, (((1,), (1,)), ((), ())), preferred_element_type=F32)


def _softmax_pv(scores, values, sinks=None):
    probs = []
    for i, s in enumerate(scores):
        m = jnp.max(s, axis=-1, keepdims=True)
        if sinks is not None:
            m = jnp.maximum(m, sinks[i])
        e = jnp.exp(s - m)
        den = jnp.sum(e, axis=-1, keepdims=True)
        if sinks is not None:
            den = den + jnp.exp(sinks[i] - m)
        probs.append((e * (1.0 / den)).astype(BF16))
    return [_mm(p, v) for p, v in zip(probs, values)]


def _gqa_queries(q):
    nq = q.shape[0]
    low = lax.broadcasted_iota(jnp.int32, (nq, LANES), 1) < HEAD_DIM
    zero = jnp.zeros((nq, LANES), BF16)
    stacks = []
    for g in range(N_KV_HEADS):
        rows = []
        for r in range(KV_REP):
            h = g * KV_REP + r
            q2 = q[:, (h // 2) * LANES:(h // 2 + 1) * LANES]
            rows.append(jnp.where(low, q2, zero) if h % 2 == 0 else jnp.where(low, zero, q2))
        stacks.append(jnp.concatenate(rows, axis=0))
    return stacks


def _gqa_outputs(outs, nq):
    low = lax.broadcasted_iota(jnp.int32, (nq, LANES), 1) < HEAD_DIM
    pairs = []
    for o in outs:
        for r in range(0, KV_REP, 2):
            pairs.append(jnp.where(low, o[r * nq:(r + 1) * nq], o[(r + 1) * nq:(r + 2) * nq]))
    return jnp.concatenate(pairs, axis=1)


def _mem_scores(qm, k_head):
    return [_qk(qm[:, h * MEM_HEAD_DIM:(h + 1) * MEM_HEAD_DIM], k_head(h).astype(BF16)) * (MEM_HEAD_DIM ** -0.5)
            for h in range(MEM_HEADS)]


def _mem_values(v_head):
    return [v_head(h).astype(BF16) for h in range(MEM_HEADS)]


def _lane_heads(ref, i):
    return lambda h: ref[i, :, h * MEM_HEAD_DIM:(h + 1) * MEM_HEAD_DIM]


def _kv_heads(x):
    return [x[:, g * LANES:(g + 1) * LANES] for g in range(N_KV_HEADS)]


def _merge_pre(x1, y, gmix_ref, wg_ref, wglu_ref):
    h = _rms(x1, gmix_ref[...]).astype(BF16)
    logits = [_mm(h, wg_ref[:, j * D_MODEL:(j + 1) * D_MODEL]) for j in range(3)]
    y = y.astype(BF16)
    return logits, _mm(y, wglu_ref[:, :D_MODEL]), _mm(y, wglu_ref[:, D_MODEL:])


def _mix_parts(parts, x1_ref, y_ref, gmix_ref, wg_ref, wglu_ref, wab_ref, wmb_ref, wo_ref, gpost_ref, x2_ref):
    x1s, ys = _row_parts(x1_ref, len(parts)), _row_parts(y_ref, len(parts))
    pres = [_merge_pre(x1, y, gmix_ref, wg_ref, wglu_ref) for x1, y in zip(x1s, ys)]
    branches = [assemble(_softmax_pv(scores, values, sinks), _softmax_pv(mem_scores, mem_values))
                for scores, values, sinks, mem_scores, mem_values, assemble in parts]
    projected = [(_mm(attn.astype(BF16), wab_ref[...]), _mm(memo.astype(BF16), wmb_ref[...]))
                 for attn, memo in branches]
    sig = _sigmoid
    merged = [sig(lg[0]) * pa + sig(lg[1]) * (ya * sig(yb)) + sig(lg[2]) * pm
              for (lg, ya, yb), (pa, pm) in zip(pres, projected)]
    outs = [_mm(m.astype(BF16), wo_ref[...]) for m in merged]
    rows = x1_ref.shape[0] // len(parts)
    for i, (x1, o) in enumerate(zip(x1s, outs)):
        x2_ref[i * rows:(i + 1) * rows, :] = x1 + _rms(o, gpost_ref[...])


def _mixer_prompt_body(q_ref, k_ref, kh_ref, v_ref, vh_ref, bias_ref, sink_ref, qm_ref, mk_ref, mv_ref,
                       x1_ref, y_ref, *merge_and_out_refs):
    i = pl.program_id(1)
    tq = q_ref.shape[1]
    kf = jnp.concatenate([kh_ref[0], k_ref[0]], axis=0)
    vf = jnp.concatenate([vh_ref[0], v_ref[0]], axis=0)
    band = WINDOW + CHUNK
    rows = tq // MIXER_PARTS
    parts = []
    for p in range(MIXER_PARTS):
        scores, values, sinks = [], [], []
        for lo in range(p * rows, (p + 1) * rows, CHUNK):
            valid = None
            if lo < WINDOW:
                key_pos = lax.broadcasted_iota(jnp.int32, (1, band), 1) + (i * tq + lo - WINDOW)
                valid = key_pos >= 0
            kb, vb = _kv_heads(kf[lo:lo + band]), _kv_heads(vf[lo:lo + band])
            for g, qs in enumerate(_gqa_queries(q_ref[0, lo:lo + CHUNK, :])):
                s = _qk(qs, kb[g]) + bias_ref[g]
                scores.append(s if valid is None else jnp.where(valid, s, NEG_INF))
                values.append(vb[g])
                sinks.append(sink_ref[g])
        mem_scores = _mem_scores(qm_ref[0, p * rows:(p + 1) * rows, :], _lane_heads(mk_ref, 0))

        def assemble(outs, mem):
            attn = jnp.concatenate([_gqa_outputs(outs[c:c + N_KV_HEADS], CHUNK)
                                    for c in range(0, len(outs), N_KV_HEADS)], axis=0)
            return attn, jnp.concatenate(mem, axis=1)

        parts.append((scores, values, sinks, mem_scores, _mem_values(_lane_heads(mv_ref, 0)), assemble))
    _mix_parts(parts, x1_ref, y_ref, *merge_and_out_refs)


def _merge_weight_specs(weights):
    return [_resident(w.shape) for w in weights]


def _mixer_prompt(q, k2, v2, bias, sink, qm, mk, mv, x1, y, merge_w):
    b, l, _ = q.shape
    tq = MIXER_TILE
    hb = tq // WINDOW
    nt = l // tq
    cur = lambda w: pl.BlockSpec((1, tq, w), lambda bi, i: (bi, i, 0))
    row = lambda w: pl.BlockSpec((tq, w), lambda bi, i: (bi * nt + i, 0))
    halo = pl.BlockSpec((1, WINDOW, KV2_W), lambda bi, i: (bi, jnp.maximum(i * hb - 1, 0), 0))
    mem = pl.BlockSpec((1, MEM_LEN, MEM_W), lambda bi, i: (bi, 0, 0))
    return pl.pallas_call(
        _mixer_prompt_body,
        grid=(b, nt),
        in_specs=[cur(ATTN_W), cur(KV2_W), halo, cur(KV2_W), halo, _resident(bias.shape),
                  _resident(sink.shape), cur(MEM_W), mem, mem, row(D_MODEL), row(SSM_W)]
        + _merge_weight_specs(merge_w),
        out_specs=row(D_MODEL),
        out_shape=jax.ShapeDtypeStruct((b * l, D_MODEL), F32),
        compiler_params=_params("parallel", "parallel"),
        name="mixer_prompt",
    )(q, k2, k2, v2, v2, bias, sink, qm, mk, mv, x1, y, *merge_w)


def _mixer_sample_body(q_ref, k_ref, v_ref, bias_ref, sink_ref, qm_ref, mk_ref, mv_ref,
                       x1_ref, y_ref, *merge_and_out_refs):
    nb, nq = q_ref.shape[0], q_ref.shape[1]
    per_part = nb // MIXER_PARTS
    parts = []
    for p in range(MIXER_PARTS):
        scores, values, sinks, mem_scores, mem_values = [], [], [], [], []
        for b in range(p * per_part, (p + 1) * per_part):
            kb, vb = _kv_heads(k_ref[b]), _kv_heads(v_ref[b])
            for g, qs in enumerate(_gqa_queries(q_ref[b])):
                scores.append(_qk(qs, kb[g]) + bias_ref[g])
                values.append(vb[g])
                sinks.append(sink_ref[g])
            mem_scores += _mem_scores(qm_ref[b], _lane_heads(mk_ref, b))
            mem_values += _mem_values(_lane_heads(mv_ref, b))

        def assemble(outs, mem):
            attn = jnp.concatenate([_gqa_outputs(outs[j:j + N_KV_HEADS], nq)
                                    for j in range(0, len(outs), N_KV_HEADS)], axis=0)
            memo = jnp.concatenate([jnp.concatenate(mem[j:j + MEM_HEADS], axis=1)
                                    for j in range(0, len(mem), MEM_HEADS)], axis=0)
            return attn, memo

        parts.append((scores, values, sinks, mem_scores, mem_values, assemble))
    _mix_parts(parts, x1_ref, y_ref, *merge_and_out_refs)


def _mixer_sample(q, kk2, vv2, bias, sink, qm, mk, mv, x1, y, merge_w):
    b, s, _ = q.shape
    nb = SAMPLE_MIXER_TILE // s
    nk = kk2.shape[1]
    blk = lambda *shape: pl.BlockSpec((nb,) + shape, lambda i: (i,) + (0,) * len(shape))
    row = lambda w: pl.BlockSpec((nb * s, w), lambda i: (i, 0))
    return pl.pallas_call(
        _mixer_sample_body,
        grid=(b // nb,),
        in_specs=[blk(s, ATTN_W), blk(nk, KV2_W), blk(nk, KV2_W),
                  _resident(bias.shape), _resident(sink.shape),
                  blk(s, MEM_W), blk(*mk.shape[1:]), blk(*mv.shape[1:]), row(D_MODEL), row(SSM_W)]
        + _merge_weight_specs(merge_w),
        out_specs=row(D_MODEL),
        out_shape=jax.ShapeDtypeStruct((b * s, D_MODEL), F32),
        compiler_params=_params("parallel"),
        name="mixer_sample",
    )(q, kk2, vv2, bias, sink, qm, mk, mv, x1, y, *merge_w)


def _ssm_weights_body(lam_re_row, lam_im_row, lam_re_col, lam_im_col, log_dt, b_re_t, b_im_t,
                      b_re_rows, b_im_rows, c_re_rows, c_im_rows, c_re_lanes, c_im_lanes, sel_ref,
                      wconv_ref, wst_ref, wot_ref, a_tab_ref):
    wst_ref[0] = jnp.zeros(wst_ref.shape[1:], BF16)
    wot_ref[0] = jnp.zeros(wot_ref.shape[1:], BF16)
    krows = []
    for gl in range(SSM_GB):
        krows.append(_ssm_group_weights(
            gl, *(r[gl] for r in (lam_re_row, lam_im_row, lam_re_col, lam_im_col, log_dt, b_re_t, b_im_t,
                                  b_re_rows, b_im_rows, c_re_rows, c_im_rows, c_re_lanes, c_im_lanes)),
            wst_ref, wot_ref, a_tab_ref))
    kstack = jnp.concatenate(krows, axis=0).astype(BF16)
    for d in range(SSM_NT):
        for ti in range(SSM_TPT):
            blk = _mm(kstack, sel_ref[d * SSM_TPT + ti])
            r0 = (SSM_NT - 1 - d) * MXU_TILE + ti * LANES
            for gl in range(SSM_GB):
                piece = blk[gl * SSM_GROUP:(gl + 1) * SSM_GROUP]
                piece = (piece if gl == 0 else pltpu.roll(piece, gl * SSM_GROUP, 1)).astype(BF16)
                r = r0 + gl * SSM_GROUP
                wconv_ref[0, r:r + SSM_GROUP, :MXU_TILE] = piece
                if r0 >= MXU_TILE:
                    wconv_ref[0, r - MXU_TILE:r - MXU_TILE + SSM_GROUP, MXU_TILE:] = piece
    wconv_ref[0, (SSM_NT - 1) * MXU_TILE:, MXU_TILE:] = jnp.zeros((MXU_TILE, MXU_TILE), BF16)


def _ssm_group_weights(gl, lam_re_row, lam_im_row, lam_re_col, lam_im_col, log_dt, b_re_t, b_im_t,
                       b_re_rows, b_im_rows, c_re_rows, c_im_rows, c_re_lanes, c_im_lanes,
                       wst_ref, wot_ref, a_tab_ref):
    dt = jnp.exp(log_dt)
    rows = lambda t: slice(t * LANES + gl * SSM_GROUP, t * LANES + (gl + 1) * SSM_GROUP)
    lanes = slice(gl * SSM_S2, (gl + 1) * SSM_S2)

    def zoh_coef(lr, li):
        mag = jnp.exp(lr * dt)
        a_re, a_im = mag * jnp.cos(li * dt), mag * jnp.sin(li * dt)
        den = lr * lr + li * li
        return ((a_re - 1.0) * lr + a_im * li) / den, (a_im * lr - (a_re - 1.0) * li) / den

    def a_power(lr, li, n):
        mag = jnp.exp(lr * dt * n)
        return mag * jnp.cos(li * dt * n), mag * jnp.sin(li * dt * n)

    lr, li = lam_re_row, lam_im_row
    cr, ci = zoh_coef(lr, li)
    n_rows = jnp.minimum(lax.broadcasted_iota(jnp.int32, (SSM_T + SCAN_ROWS, 1), 0), SSM_T).astype(F32)
    pw_r, pw_i = a_power(lr, li, n_rows)

    def over_channels(pw, exps):
        return jnp.concatenate([jnp.broadcast_to(pw[e:e + 1], (SSM_GROUP, SSM_STATE)) for e in exps], axis=0)

    back = [SSM_T - 1 - t for t in range(SSM_T)]
    pr, pi = over_channels(pw_r, back), over_channels(pw_i, back)
    zr, zi = pr * cr - pi * ci, pr * ci + pi * cr
    br, bi = b_re_rows, b_im_rows
    wstate = jnp.concatenate([zr * br - zi * bi, zr * bi + zi * br], axis=1).astype(BF16)

    fwd = [t + 1 for t in range(SSM_T)]
    qr, qi = over_channels(pw_r, fwd), over_channels(pw_i, fwd)
    ccr, cci = c_re_rows, c_im_rows
    wout_t = jnp.concatenate([ccr * qr - cci * qi, -(ccr * qi + cci * qr)], axis=1).astype(BF16)
    for t in range(SSM_T):
        wst_ref[0, rows(t), lanes] = wstate[t * SSM_GROUP:(t + 1) * SSM_GROUP]
        wot_ref[0, rows(t), lanes] = wout_t[t * SSM_GROUP:(t + 1) * SSM_GROUP]

    lrc, lic = lam_re_col, lam_im_col
    hi = lax.Precision.HIGHEST
    n_lanes = jnp.minimum(lax.broadcasted_iota(jnp.int32, (1, LANES), 1), SSM_T).astype(F32)
    pc_r, pc_i = a_power(lrc, lic, n_lanes)
    spread = (lax.broadcasted_iota(jnp.int32, (LANES, SSM_K), 0)
              == lax.broadcasted_iota(jnp.int32, (LANES, SSM_K), 1) // SSM_GROUP).astype(F32)
    gr = jnp.dot(pc_r, spread, precision=hi, preferred_element_type=F32)
    gi = jnp.dot(pc_i, spread, precision=hi, preferred_element_type=F32)
    clr, cli = c_re_lanes, c_im_lanes
    g_re, g_im = clr * gr - cli * gi, clr * gi + cli * gr
    btr, bti = b_re_t, b_im_t
    bbr, bbi = cr * btr - ci * bti, cr * bti + ci * btr
    krow = (jnp.dot(bbr, g_re, precision=hi, preferred_element_type=F32)
            - jnp.dot(bbi, g_im, precision=hi, preferred_element_type=F32))

    idx = lax.broadcasted_iota(jnp.int32, (N_SCAN_TABLES * SCAN_ROWS, SSM_STATE), 0)
    tab, r = idx // SCAN_ROWS, idx % SCAN_ROWS
    stride = jnp.where(tab < 2, 1, jnp.where(tab < 4, 2, 4))
    n = jnp.where(tab < 6, stride, jnp.where(tab < 8, r, SCAN_ROWS))
    keep = jnp.logical_or(tab >= 6, r >= stride)
    er, ei = a_power(lr, li, (n * SSM_T).astype(F32))
    er, ei = jnp.where(keep, er, 0.0), jnp.where(keep, ei, 0.0)
    odd = tab % 2 == 1
    tabs = jnp.concatenate([jnp.where(odd, -ei, er), jnp.where(odd, ei, er)], axis=1)
    a_tab_ref[:, :, lanes] = tabs.reshape(N_SCAN_TABLES, SCAN_ROWS, SSM_S2)
    return krow


def _lag_selectors():
    sel = np.zeros((SSM_NT, SSM_TPT, SSM_K, MXU_TILE), np.float32)
    ch = np.arange(SSM_GROUP)
    for d in range(SSM_NT):
        for ti in range(SSM_TPT):
            for to in range(SSM_TPT):
                lag = SSM_TPT * d + to - ti
                if lag >= 0:
                    sel[d, ti, lag * SSM_GROUP + ch, to * LANES + ch] = 1.0
    return jnp.asarray(sel.reshape(SSM_NT * SSM_TPT, SSM_K, MXU_TILE), BF16)


def _ssm_weights(lam_re, lam_im, log_dt, b_re, b_im, c_re, c_im):
    g, p, c = b_re.shape
    row3 = lambda x: x.reshape(g, 1, p)
    col3 = lambda x: x.reshape(g, p, 1)
    t3 = lambda x: jnp.transpose(x, (0, 2, 1))
    b_rows = lambda x: jnp.tile(t3(x), (1, SSM_T, 1))
    c_rows = lambda x: jnp.tile(x, (1, SSM_T, 1))
    c_lanes = lambda x: jnp.tile(t3(x), (1, 1, SSM_T))
    args = (row3(lam_re), row3(lam_im), col3(lam_re), col3(lam_im), log_dt.reshape(g, 1, 1),
            t3(b_re), t3(b_im), b_rows(b_re), b_rows(b_im), c_rows(c_re), c_rows(c_im),
            c_lanes(c_re), c_lanes(c_im))
    sel = _lag_selectors()
    spec = lambda x: pl.BlockSpec((SSM_GB,) + x.shape[1:], lambda i: (i, 0, 0))
    per_gb = lambda *shape: pl.BlockSpec((1,) + shape, lambda i: (i, 0, 0))
    return pl.pallas_call(
        _ssm_weights_body,
        grid=(SSM_NGB,),
        in_specs=[spec(a) for a in args] + [_resident(sel.shape)],
        out_specs=[per_gb(SSM_NT * MXU_TILE, 2 * MXU_TILE), per_gb(SSM_XW, SSM_SW), per_gb(SSM_XW, SSM_SW),
                   pl.BlockSpec((N_SCAN_TABLES, SCAN_ROWS, SSM_SW), lambda i: (0, 0, i))],
        out_shape=[jax.ShapeDtypeStruct((SSM_NGB, SSM_NT * MXU_TILE, 2 * MXU_TILE), BF16),
                   jax.ShapeDtypeStruct((SSM_NGB, SSM_XW, SSM_SW), BF16),
                   jax.ShapeDtypeStruct((SSM_NGB, SSM_XW, SSM_SW), BF16),
                   jax.ShapeDtypeStruct((N_SCAN_TABLES, SCAN_ROWS, SSM_WIDTH), F32)],
        compiler_params=_params("parallel"),
        name="ssm_weights",
    )(*args, sel)


def _swap_halves(s):
    ax = s.ndim - 1
    return jnp.concatenate([pltpu.roll(s[..., l:l + SSM_S2], SSM_STATE, ax)
                            for l in range(0, s.shape[ax], SSM_S2)], axis=ax)


def _ssm_body(row_sets, slab_rows, u_ref, wconv_ref, wst_ref, wot_ref, d_ref, a_tab_ref, s0_ref,
              y_ref, s_ref, v_scr, s_scr):
    def piece(ref_set, first, n, stride, t):
        return ref_set, pl.ds(first + t, n, stride=stride)

    @pl.when(pl.program_id(1) == 0)
    def _():
        s_scr[...] = s0_ref[0]

    x = jnp.concatenate(
        [jnp.concatenate([u_ref[piece(*rs, t)] for t in range(SSM_T)], axis=1) for rs in row_sets],
        axis=0).astype(BF16)
    v_scr[...] = _mm(x, wst_ref[0])
    conv = []
    for jo in range(0, SSM_NT, 2):
        pair = _mm(x[:, :(jo + 2) * MXU_TILE], wconv_ref[0, (SSM_NT - 2 - jo) * MXU_TILE:, :])
        conv += [pair[:, MXU_TILE:], pair[:, :MXU_TILE]]

    n_rows = v_scr.shape[0]
    if slab_rows:
        a_mul, a_swap = a_tab_ref[6, 1:2, :], a_tab_ref[7, 1:2, :]
        s = s_scr[...]
        for c in range(n_rows // slab_rows):
            rows = slice(c * slab_rows, (c + 1) * slab_rows)
            inc = v_scr[rows, :]
            v_scr[rows, :] = s
            s = a_mul * s + a_swap * _swap_halves(s) + inc
        s_scr[...] = s
    else:
        n_blocks = n_rows // SCAN_ROWS
        xs = v_scr[...].reshape(n_blocks, SCAN_ROWS, SSM_SW)
        for k in range(3):
            sh = pltpu.roll(xs, 1 << k, 1)
            xs = xs + a_tab_ref[2 * k] * sh + a_tab_ref[2 * k + 1] * _swap_halves(sh)
        carries, blk = [], 0
        for j, (_, _, n, _) in enumerate(row_sets):
            carry = s_scr[j * SCAN_ROWS:(j + 1) * SCAN_ROWS, :]
            for _ in range(n // SCAN_ROWS):
                carries.append(carry)
                last = jnp.broadcast_to(xs[blk, SCAN_ROWS - 1:, :], carry.shape)
                carry = last + a_tab_ref[8] * carry + a_tab_ref[9] * _swap_halves(carry)
                blk += 1
            s_scr[j * SCAN_ROWS:(j + 1) * SCAN_ROWS, :] = carry
        carries = jnp.stack(carries, axis=0)
        not_first = lax.broadcasted_iota(jnp.int32, (1, SCAN_ROWS, 1), 1) >= 1
        enter = (jnp.where(not_first, pltpu.roll(xs, 1, 1), 0.0)
                 + a_tab_ref[6] * carries + a_tab_ref[7] * _swap_halves(carries))
        v_scr[...] = enter.reshape(n_rows, SSM_SW)
    s_ref[0] = s_scr[...]

    enter = v_scr[...].astype(BF16)
    d = d_ref[...]
    carried = _qk(enter, wot_ref[0])
    for jo in range(SSM_NT):
        yt = conv[jo] + carried[:, jo * MXU_TILE:(jo + 1) * MXU_TILE]
        r0 = 0
        for rs in row_sets:
            for tl in range(SSM_TPT):
                idx = piece(*rs, jo * SSM_TPT + tl)
                y_ref[idx] = yt[r0:r0 + rs[2], tl * LANES:(tl + 1) * LANES] + u_ref[idx] * d
            r0 += rs[2]


def _ssm(u, row_sets, slab_rows, block_rows, wconv, wst, wo, d_lanes, a_tab, s0):
    ns, r, _ = u.shape
    chunk_rows = sum(rs[2] for rs in row_sets)
    carry_rows = s0.shape[1]
    blk = pl.BlockSpec((ns, block_rows, LANES), lambda gb, i: (0, i, gb))
    per_gb = lambda x: pl.BlockSpec((1,) + x.shape[1:], lambda gb, i: (gb, 0, 0))
    return pl.pallas_call(
        functools.partial(_ssm_body, row_sets, slab_rows),
        grid=(SSM_NGB, r // block_rows),
        in_specs=[blk, per_gb(wconv), per_gb(wst), per_gb(wo),
                  pl.BlockSpec((1, LANES), lambda gb, i: (0, gb)),
                  pl.BlockSpec((N_SCAN_TABLES, SCAN_ROWS, SSM_SW), lambda gb, i: (0, 0, gb)),
                  per_gb(s0)],
        out_specs=[blk, per_gb(s0)],
        out_shape=[jax.ShapeDtypeStruct(u.shape, F32), jax.ShapeDtypeStruct(s0.shape, F32)],
        scratch_shapes=[pltpu.VMEM((chunk_rows, SSM_SW), F32), pltpu.VMEM((carry_rows, SSM_SW), F32)],
        compiler_params=_params("parallel", "arbitrary"),
        name="ssm_scan",
    )(u, wconv, wst, wo, d_lanes, a_tab, s0)


def _ssm_branch(u, s0, ssm_w):
    b, l, _ = u.shape
    nc = l // SSM_T
    by_gb = lambda s: s.reshape(s.shape[0], SSM_NGB, SSM_SW).transpose(1, 0, 2)
    if b % SCAN_ROWS == 0:
        row_sets = tuple((0, c * SSM_T, b, l) for c in range(nc))
        y, s_last = _ssm(u.reshape(1, b * l, SSM_W), row_sets, b, b * l, *ssm_w, by_gb(s0))
    else:
        cps = min(nc, SSM_CHUNKS_PER_STEP)
        row_sets = tuple((j, 0, cps, SSM_T) for j in range(b))
        y, s_last = _ssm(u, row_sets, 0, cps * SSM_T, *ssm_w, by_gb(jnp.repeat(s0, SCAN_ROWS, axis=0)))
        s_last = s_last[:, ::SCAN_ROWS]
    s_last = s_last.transpose(1, 0, 2).reshape(b, SSM_GROUPS, 2, SSM_STATE)
    return y.reshape(b, l, SSM_W), s_last[:, :, 0], s_last[:, :, 1]


def _ffn2_body(steps_a, xa_ref, xb_ref, gpre_ref, gpost_ref, wfi_ref, wfo_ref, oa_ref, ob_ref):
    def run(x_ref, o_ref):
        outs = _ffn(_row_parts(x_ref, FFN_PARTS), gpre_ref[...], gpost_ref[...], wfi_ref, wfo_ref)
        rows = x_ref.shape[0] // FFN_PARTS
        for i, o in enumerate(outs):
            o_ref[i * rows:(i + 1) * rows, :] = o

    _on_group(steps_a, run, (xa_ref, oa_ref), (xb_ref, ob_ref))


def _ffn2(xa, xb, gpre, gpost, wfi, wfo):
    steps_a, steps_b = xa.shape[0] // FFN_ROW_TILE, xb.shape[0] // FFN_ROW_TILE
    first, second = _two_group_specs(steps_a)
    vec = _resident((1, D_MODEL))
    return pl.pallas_call(
        functools.partial(_ffn2_body, steps_a),
        grid=(steps_a + steps_b,),
        in_specs=[first(D_MODEL), second(D_MODEL), vec, vec, _resident(wfi.shape), _resident(wfo.shape)],
        out_specs=[first(D_MODEL), second(D_MODEL)],
        out_shape=[jax.ShapeDtypeStruct(xa.shape, F32), jax.ShapeDtypeStruct(xb.shape, F32)],
        compiler_params=_params("arbitrary"),
        name="ffn2",
    )(xa, xb, gpre, gpost, wfi, wfo)


def _t5_bucket(rel):
    half = N_BUCKETS // 2
    max_exact = half // 2
    ret = (rel > 0).astype(np.int32) * half
    n = np.abs(rel)
    large = max_exact + (np.log(np.maximum(n, 1) / max_exact) / math.log(MAX_DISTANCE / max_exact)
                         * (half - max_exact)).astype(np.int32)
    large = np.minimum(large, half - 1)
    return ret + np.where(n < max_exact, n, large)


def _band_bias(rel_table, n_q, n_back, n_k):
    i = np.arange(n_q)[:, None]
    j = np.arange(n_k)[None, :]
    bucket = _t5_bucket((j - n_back) - i).reshape(-1)
    onehot = np.zeros((N_BUCKETS, bucket.size), np.float32)
    onehot[bucket, np.arange(bucket.size)] = 1.0
    b = jnp.dot(rel_table.astype(F32).T, jnp.asarray(onehot), precision=lax.Precision.HIGHEST)
    return b.reshape(N_KV_HEADS, KV_REP * n_q, n_k)


def _sink_rows(sink, n_q):
    return jnp.repeat(sink.astype(F32).reshape(N_KV_HEADS, KV_REP), n_q, axis=1)[:, :, None]


def _twice_per_head(x, axis):
    shape = x.shape
    x = x.reshape(shape[:axis] + (N_KV_HEADS, 1, HEAD_DIM) + shape[axis + 1:])
    x = jnp.concatenate([x, x], axis=axis + 1)
    return x.reshape(shape[:axis] + (KV2_W,) + shape[axis + 1:])


def kernel(x_prompt, x_sample, cache_swa_k, cache_swa_v, cache_mem_k, cache_mem_v, state_ssm_re, state_ssm_im, mem_prompt, rel_bias_table, ff1_pre_g, ff1_post_g, w_ff1_in, w_ff1_out, mix_pre_g, mix_post_g, w_in, mem_norm_g, w_mem_kv, attn_sink, ssm_lambda_re, ssm_lambda_im, ssm_log_dt, ssm_b_re, ssm_b_im, ssm_c_re, ssm_c_im, ssm_d, w_ssm_glu, w_attn_br, w_mem_br, w_out, ff2_pre_g, ff2_post_g, w_ff2_in, w_ff2_out):
    assert ff1_pre_g.shape[0] == 1, "single-layer step"
    bp, lp, _ = x_prompt.shape
    bs, ls, _ = x_sample.shape
    vec = lambda g: g[0].reshape(1, D_MODEL).astype(F32)
    w16 = lambda w: w[0].astype(BF16)

    wfi1, wfo1, wfi2, wfo2 = w16(w_ff1_in), w16(w_ff1_out), w16(w_ff2_in), w16(w_ff2_out)
    w_in16 = w16(w_in)
    c_k = ATTN_W
    c_g = ATTN_W + 2 * KV_W + SSM_W + MEM_W
    q_scale = HEAD_DIM ** -0.5
    assert math.frexp(q_scale)[0] == 0.5, "power-of-two scale: folding it into the q columns is exact"
    wp = jnp.concatenate([w_in16[:, :c_k] * jnp.asarray(q_scale, BF16), w_in16[:, c_k:c_g]], axis=1)
    wg = w_in16[:, c_g:]
    wab, wglu, wmb, wo = w16(w_attn_br), w16(w_ssm_glu), w16(w_mem_br), w16(w_out)

    wconv, wst, wot, a_tab = _ssm_weights(
        ssm_lambda_re[0], ssm_lambda_im[0], ssm_log_dt[0], ssm_b_re[0], ssm_b_im[0], ssm_c_re[0], ssm_c_im[0])
    ssm_w = (wconv, wst, wot, ssm_d[0].astype(F32).reshape(1, SSM_W), a_tab)

    mk_p, mv_p = _mem_kv(mem_prompt.reshape(bp * MEM_LEN, D_MODEL), vec(mem_norm_g), w16(w_mem_kv))

    def mix(x_shape, proj, mixer, s0, mem_k, mem_v):
        b, l, _ = x_shape
        r3 = lambda t: t.reshape(b, l, t.shape[-1])
        x1, q, k, v, u, qm, k2, v2 = proj
        y_ssm, s_re, s_im = _ssm_branch(r3(u), s0, ssm_w)
        x2 = mixer(r3(q), r3(k2), r3(v2), r3(qm), mem_k, mem_v, x1, y_ssm.reshape(b * l, SSM_W))
        return x2, r3(k), r3(v), s_re, s_im

    merge_w = (vec(mix_pre_g), wg, wglu, wab, wmb, wo, vec(mix_post_g))

    def mixer_prompt(q, k2, v2, qm, mem_k, mem_v, x1, y_ssm):
        bias = _band_bias(rel_bias_table, CHUNK, WINDOW, WINDOW + CHUNK)
        return _mixer_prompt(q, k2, v2, bias, _sink_rows(attn_sink[0], CHUNK), qm, mem_k, mem_v, x1, y_ssm, merge_w)

    def mixer_sample(q, k2, v2, qm, mem_k, mem_v, x1, y_ssm):
        n_back = cache_swa_k.shape[2]
        cache2 = lambda c: _twice_per_head(c[0].reshape(bs, n_back, KV_W).astype(BF16), 2)
        kk = jnp.concatenate([cache2(cache_swa_k), k2], axis=1)
        vv = jnp.concatenate([cache2(cache_swa_v), v2], axis=1)
        bias = _band_bias(rel_bias_table, ls, n_back, n_back + ls)
        return _mixer_sample(q, kk, vv, bias, _sink_rows(attn_sink[0], ls), qm, mem_k, mem_v, x1, y_ssm, merge_w)

    proj_p, proj_s = _ffn1_proj(x_prompt.reshape(bp * lp, D_MODEL), x_sample.reshape(bs * ls, D_MODEL),
                                vec(ff1_pre_g), vec(ff1_post_g), wfi1, wfo1, vec(mix_pre_g), wp)
    x2p, pk, pv, pre, pim = mix(x_prompt.shape, proj_p, mixer_prompt, jnp.zeros((bp, SSM_WIDTH), F32),
                                mk_p.reshape(bp, MEM_LEN, MEM_W), mv_p.reshape(bp, MEM_LEN, MEM_W))
    s0 = jnp.stack([state_ssm_re[0], state_ssm_im[0]], axis=2).reshape(bs, SSM_WIDTH).astype(F32)
    x2s, sk, sv, sre, sim = mix(x_sample.shape, proj_s, mixer_sample, s0,
                                cache_mem_k[0].reshape(bs, MEM_LEN, MEM_W),
                                cache_mem_v[0].reshape(bs, MEM_LEN, MEM_W))
    yp, ys = _ffn2(x2p, x2s, vec(ff2_pre_g), vec(ff2_post_g), wfi2, wfo2)
    yp, ys = yp.reshape(x_prompt.shape), ys.reshape(x_sample.shape)

    n_keep = min(WINDOW, lp)
    heads = lambda t: t.reshape(t.shape[0], t.shape[1], N_KV_HEADS, HEAD_DIM)[None]
    mem_heads = lambda t: t.reshape(bp, MEM_LEN, MEM_HEADS, MEM_HEAD_DIM)[None]
    return (yp, ys, heads(pk[:, -n_keep:]), heads(pv[:, -n_keep:]), mem_heads(mk_p), mem_heads(mv_p),
            pre[None], pim[None], heads(sk), heads(sv), sre[None], sim[None])
```

```python
import functools
import math

import numpy as np
import jax
import jax.numpy as jnp
from jax import lax
from jax.experimental import pallas as pl
from jax.experimental.pallas import tpu as pltpu

D_MODEL = 1024
CHUNK = 64
WINDOW = 128
HEAD_DIM = 64
MIX_W = D_MODEL // 2
N_HEADS = MIX_W // HEAD_DIM
N_KV_HEADS = N_HEADS // 4
KV_REP = N_HEADS // N_KV_HEADS
ATTN_W = N_HEADS * HEAD_DIM
KV_W = N_KV_HEADS * HEAD_DIM
SSM_GROUP = 16
SSM_W = MIX_W
SSM_GROUPS = SSM_W // SSM_GROUP
SSM_STATE = 64
MEM_LEN = 256
MEM_HEADS = 4
MEM_HEAD_DIM = MIX_W // MEM_HEADS
MEM_W = MEM_HEADS * MEM_HEAD_DIM
D_FF = 128 * ((8 * D_MODEL // 3 + 127) // 128)
N_BUCKETS = 32
MAX_DISTANCE = 128
RMS_EPS = 1e-6
NEG_INF = -1e30

LANES = 128
MXU_TILE = 256
ROW_TILE = 256
MIXER_TILE = 512
SAMPLE_MIXER_TILE = 256
MIXER_PARTS = 2
FFN_ROW_TILE = 512
FFN_PARTS = 2
FFN_CHUNK = 2 * MXU_TILE
VMEM_LIMIT = 60 * 1024 * 1024

KV2_W = N_KV_HEADS * LANES

SSM_T = 16
SSM_K = SSM_T * SSM_GROUP
SSM_S2 = 2 * SSM_STATE
SSM_WIDTH = SSM_GROUPS * SSM_S2
SSM_GB = LANES // SSM_GROUP
SSM_NGB = SSM_GROUPS // SSM_GB
SSM_XW = SSM_T * LANES
SSM_SW = SSM_GB * SSM_S2
SSM_TPT = MXU_TILE // LANES
SSM_NT = SSM_T // SSM_TPT
SCAN_ROWS = 8
N_SCAN_TABLES = 10
SSM_CHUNKS_PER_STEP = 256

F32 = jnp.float32
BF16 = jnp.bfloat16


def _params(*sem):
    return pltpu.CompilerParams(dimension_semantics=sem, vmem_limit_bytes=VMEM_LIMIT)


def _resident(shape):
    zeros = (0,) * len(shape)
    return pl.BlockSpec(shape, lambda *_: zeros, pipeline_mode=pl.Buffered(1))


def _rms(x, g):
    return x * lax.rsqrt(jnp.mean(x * x, axis=-1, keepdims=True) + RMS_EPS) * g


def _mm(a, b):
    return jnp.dot(a, b, preferred_element_type=F32)


def _sigmoid(x):
    return 0.5 * jnp.tanh(0.5 * x) + 0.5


def _row_parts(ref, n_parts):
    rows = ref.shape[0] // n_parts
    return [ref[i * rows:(i + 1) * rows, :] for i in range(n_parts)]


def _ffn(xs, gpre, gpost, w_in_ref, w_out_ref):
    hs = [_rms(x, gpre).astype(BF16) for x in xs]
    acts = [[] for _ in xs]
    for c0 in range(0, D_FF, FFN_CHUNK):
        c1 = min(c0 + FFN_CHUNK, D_FF)
        for h, a in zip(hs, acts):
            g = _mm(h, w_in_ref[:, c0:c1])
            u = _mm(h, w_in_ref[:, D_FF + c0:D_FF + c1])
            a.append((g * jax.nn.sigmoid(g) * u).astype(BF16))
    outs = [_mm(jnp.concatenate(a, axis=1), w_out_ref[...]) for a in acts]
    return [x + 0.5 * _rms(o, gpost) for x, o in zip(xs, outs)]


PROJ_SPLIT = (("q", ATTN_W, BF16), ("k", KV_W, F32), ("v", KV_W, F32), ("u", SSM_W, F32),
              ("qm", MEM_W, BF16), ("k2", KV2_W, BF16), ("v2", KV2_W, BF16))
N_PROJ_MATMUL = 5
PROJ_COLS = sum(w for _, w, _ in PROJ_SPLIT[:N_PROJ_MATMUL])
Q_SCALE = HEAD_DIM ** -0.5
assert math.frexp(Q_SCALE)[0] == 0.5, "power of two: scaling q before the bf16 rounding and the dot is exact"


def _twice_per_head_lanes(x):
    assert N_KV_HEADS == 2 and KV_W == LANES
    swapped = pltpu.roll(x, HEAD_DIM, 1)
    low = lax.broadcasted_iota(jnp.int32, x.shape, 1) < HEAD_DIM
    return jnp.concatenate([jnp.where(low, x, swapped), jnp.where(low, swapped, x)], axis=1)


def _two_group_specs(steps_a):
    first = lambda w: pl.BlockSpec((FFN_ROW_TILE, w), lambda i: (jnp.minimum(i, steps_a - 1), 0))
    second = lambda w: pl.BlockSpec((FFN_ROW_TILE, w), lambda i: (jnp.maximum(i - steps_a, 0), 0))
    return first, second


def _on_group(steps_a, run, refs_a, refs_b):
    i = pl.program_id(0)

    @pl.when(i < steps_a)
    def _():
        run(*refs_a)

    @pl.when(i >= steps_a)
    def _():
        run(*refs_b)


def _ffn1_proj_body(steps_a, xa_ref, xb_ref, gpre_ref, gpost_ref, wfi_ref, wfo_ref, gmix_ref, wp_ref, *out_refs):
    def run(x_ref, x1_ref, *proj_refs):
        x1s = _ffn(_row_parts(x_ref, FFN_PARTS), gpre_ref[...], gpost_ref[...], wfi_ref, wfo_ref)
        rows = x_ref.shape[0] // FFN_PARTS
        ps = [_mm(_rms(x1, gmix_ref[...]).astype(BF16), wp_ref[...]) for x1 in x1s]
        for i, (x1, p) in enumerate(zip(x1s, ps)):
            part = slice(i * rows, (i + 1) * rows)
            x1_ref[part, :] = x1
            col, cols = 0, {}
            for name, width, _ in PROJ_SPLIT[:N_PROJ_MATMUL]:
                cols[name] = p[:, col:col + width]
                col += width
            cols["q"] = cols["q"] * Q_SCALE
            cols["k2"], cols["v2"] = _twice_per_head_lanes(cols["k"]), _twice_per_head_lanes(cols["v"])
            for ref, (name, _, dtype) in zip(proj_refs, PROJ_SPLIT):
                ref[part, :] = cols[name].astype(dtype)

    n_out = len(out_refs) // 2
    _on_group(steps_a, run, (xa_ref,) + out_refs[:n_out], (xb_ref,) + out_refs[n_out:])


def _ffn1_proj(xa, xb, gpre, gpost, wfi, wfo, gmix, wp):
    steps_a, steps_b = xa.shape[0] // FFN_ROW_TILE, xb.shape[0] // FFN_ROW_TILE
    first, second = _two_group_specs(steps_a)
    widths = [D_MODEL] + [w for _, w, _ in PROJ_SPLIT]
    dtypes = [F32] + [d for _, _, d in PROJ_SPLIT]
    outs = pl.pallas_call(
        functools.partial(_ffn1_proj_body, steps_a),
        grid=(steps_a + steps_b,),
        in_specs=[first(D_MODEL), second(D_MODEL), _resident((1, D_MODEL)), _resident((1, D_MODEL)),
                  _resident(wfi.shape), _resident(wfo.shape), _resident((1, D_MODEL)),
                  _resident(wp.shape)],
        out_specs=[first(w) for w in widths] + [second(w) for w in widths],
        out_shape=[jax.ShapeDtypeStruct((x.shape[0], w), d) for x in (xa, xb) for w, d in zip(widths, dtypes)],
        compiler_params=_params("arbitrary"),
        name="ffn1_proj",
    )(xa, xb, gpre, gpost, wfi, wfo, gmix, wp)
    return outs[:len(widths)], outs[len(widths):]


def _mem_kv_body(m_ref, g_ref, w_ref, k_ref, v_ref):
    kv = _mm(_rms(m_ref[...], g_ref[...]).astype(BF16), w_ref[...])
    k_ref[...] = kv[:, :MEM_W]
    v_ref[...] = kv[:, MEM_W:]


def _mem_kv(mem, g, w):
    n = mem.shape[0]
    row = lambda wd: pl.BlockSpec((ROW_TILE, wd), lambda i: (i, 0))
    return pl.pallas_call(
        _mem_kv_body,
        grid=(n // ROW_TILE,),
        in_specs=[row(D_MODEL), _resident((1, D_MODEL)), _resident(w.shape)],
        out_specs=[row(MEM_W), row(MEM_W)],
        out_shape=[jax.ShapeDtypeStruct((n, MEM_W), F32)] * 2,
        compiler_params=_params("parallel"),
        name="mem_kv",
    )(mem, g, w)


def _qk(q, k):
    return lax.dot_general(q, k, (((1,), (1,)), ((), ())), preferred_element_type=F32)


def _softmax_pv(scores, values, sinks=None):
    probs = []
    for i, s in enumerate(scores):
        m = jnp.max(s, axis=-1, keepdims=True)
        if sinks is not None:
            m = jnp.maximum(m, sinks[i])
        e = jnp.exp(s - m)
        den = jnp.sum(e, axis=-1, keepdims=True)
        if sinks is not None:
            den = den + jnp.exp(sinks[i] - m)
        probs.append((e * (1.0 / den)).astype(BF16))
    return [_mm(p, v) for p, v in zip(probs, values)]


def _gqa_queries(q):
    nq = q.shape[0]
    low = lax.broadcasted_iota(jnp.int32, (nq, LANES), 1) < HEAD_DIM
    zero = jnp.zeros((nq, LANES), BF16)
    stacks = []
    for g in range(N_KV_HEADS):
        rows = []
        for r in range(KV_REP):
            h = g * KV_REP + r
            q2 = q[:, (h // 2) * LANES:(h // 2 + 1) * LANES]
            rows.append(jnp.where(low, q2, zero) if h % 2 == 0 else jnp.where(low, zero, q2))
        stacks.append(jnp.concatenate(rows, axis=0))
    return stacks


def _gqa_outputs(outs, nq):
    low = lax.broadcasted_iota(jnp.int32, (nq, LANES), 1) < HEAD_DIM
    pairs = []
    for o in outs:
        for r in range(0, KV_REP, 2):
            pairs.append(jnp.where(low, o[r * nq:(r + 1) * nq], o[(r + 1) * nq:(r + 2) * nq]))
    return jnp.concatenate(pairs, axis=1)


def _mem_scores(qm, k_head):
    return [_qk(qm[:, h * MEM_HEAD_DIM:(h + 1) * MEM_HEAD_DIM], k_head(h).astype(BF16)) * (MEM_HEAD_DIM ** -0.5)
            for h in range(MEM_HEADS)]


def _mem_values(v_head):
    return [v_head(h).astype(BF16) for h in range(MEM_HEADS)]


def _lane_heads(ref, i):
    return lambda h: ref[i, :, h * MEM_HEAD_DIM:(h + 1) * MEM_HEAD_DIM]


def _row_heads(ref, i):
    return lambda h: ref[i, pl.ds(h, MEM_LEN, stride=MEM_HEADS), :]


def _kv_heads(x):
    return [x[:, g * LANES:(g + 1) * LANES] for g in range(N_KV_HEADS)]


def _merge_pre(x1, y, gmix_ref, wg_ref, wglu_ref):
    h = _rms(x1, gmix_ref[...]).astype(BF16)
    logits = [_mm(h, wg_ref[:, PROJ_COLS + j * D_MODEL:PROJ_COLS + (j + 1) * D_MODEL]) for j in range(3)]
    y = y.astype(BF16)
    return logits, _mm(y, wglu_ref[:, :D_MODEL]), _mm(y, wglu_ref[:, D_MODEL:])


def _mix_parts(parts, x1_ref, y_ref, gmix_ref, wg_ref, wglu_ref, wab_ref, wmb_ref, wo_ref, gpost_ref, x2_ref):
    x1s, ys = _row_parts(x1_ref, len(parts)), _row_parts(y_ref, len(parts))
    pres = [_merge_pre(x1, y, gmix_ref, wg_ref, wglu_ref) for x1, y in zip(x1s, ys)]
    branches = [assemble(_softmax_pv(scores, values, sinks), _softmax_pv(mem_scores, mem_values))
                for scores, values, sinks, mem_scores, mem_values, assemble in parts]
    projected = [(_mm(attn.astype(BF16), wab_ref[...]), _mm(memo.astype(BF16), wmb_ref[...]))
                 for attn, memo in branches]
    sig = _sigmoid
    merged = [sig(lg[0]) * pa + sig(lg[1]) * (ya * sig(yb)) + sig(lg[2]) * pm
              for (lg, ya, yb), (pa, pm) in zip(pres, projected)]
    outs = [_mm(m.astype(BF16), wo_ref[...]) for m in merged]
    rows = x1_ref.shape[0] // len(parts)
    for i, (x1, o) in enumerate(zip(x1s, outs)):
        x2_ref[i * rows:(i + 1) * rows, :] = x1 + _rms(o, gpost_ref[...])


def _mixer_prompt_body(q_ref, k_ref, kh_ref, v_ref, vh_ref, bias_ref, sink_ref, qm_ref, mk_ref, mv_ref,
                       x1_ref, y_ref, *merge_and_out_refs):
    i = pl.program_id(1)
    tq = q_ref.shape[1]
    kf = jnp.concatenate([kh_ref[0], k_ref[0]], axis=0)
    vf = jnp.concatenate([vh_ref[0], v_ref[0]], axis=0)
    band = WINDOW + CHUNK
    rows = tq // MIXER_PARTS
    parts = []
    for p in range(MIXER_PARTS):
        scores, values, sinks = [], [], []
        for lo in range(p * rows, (p + 1) * rows, CHUNK):
            valid = None
            if lo < WINDOW:
                key_pos = lax.broadcasted_iota(jnp.int32, (1, band), 1) + (i * tq + lo - WINDOW)
                valid = key_pos >= 0
            kb, vb = _kv_heads(kf[lo:lo + band]), _kv_heads(vf[lo:lo + band])
            for g, qs in enumerate(_gqa_queries(q_ref[0, lo:lo + CHUNK, :])):
                s = _qk(qs, kb[g]) + bias_ref[g]
                scores.append(s if valid is None else jnp.where(valid, s, NEG_INF))
                values.append(vb[g])
                sinks.append(sink_ref[g])
        mem_scores = _mem_scores(qm_ref[0, p * rows:(p + 1) * rows, :], _lane_heads(mk_ref, 0))

        def assemble(outs, mem):
            attn = jnp.concatenate([_gqa_outputs(outs[c:c + N_KV_HEADS], CHUNK)
                                    for c in range(0, len(outs), N_KV_HEADS)], axis=0)
            return attn, jnp.concatenate(mem, axis=1)

        parts.append((scores, values, sinks, mem_scores, _mem_values(_lane_heads(mv_ref, 0)), assemble))
    _mix_parts(parts, x1_ref, y_ref, *merge_and_out_refs)


def _merge_weight_specs(weights):
    return [_resident(w.shape) for w in weights]


def _mixer_prompt(q, k2, v2, bias, sink, qm, mk, mv, x1, y, merge_w):
    b, l, _ = q.shape
    tq = MIXER_TILE
    hb = tq // WINDOW
    nt = l // tq
    cur = lambda w: pl.BlockSpec((1, tq, w), lambda bi, i: (bi, i, 0))
    row = lambda w: pl.BlockSpec((tq, w), lambda bi, i: (bi * nt + i, 0))
    halo = pl.BlockSpec((1, WINDOW, KV2_W), lambda bi, i: (bi, jnp.maximum(i * hb - 1, 0), 0))
    mem = pl.BlockSpec((1, MEM_LEN, MEM_W), lambda bi, i: (bi, 0, 0))
    return pl.pallas_call(
        _mixer_prompt_body,
        grid=(b, nt),
        in_specs=[cur(ATTN_W), cur(KV2_W), halo, cur(KV2_W), halo, _resident(bias.shape),
                  _resident(sink.shape), cur(MEM_W), mem, mem, row(D_MODEL), row(SSM_W)]
        + _merge_weight_specs(merge_w),
        out_specs=row(D_MODEL),
        out_shape=jax.ShapeDtypeStruct((b * l, D_MODEL), F32),
        compiler_params=_params("parallel", "parallel"),
        name="mixer_prompt",
    )(q, k2, k2, v2, v2, bias, sink, qm, mk, mv, x1, y, *merge_w)


def _mixer_sample_body(q_ref, k_ref, v_ref, bias_ref, sink_ref, qm_ref, mk_ref, mv_ref,
                       x1_ref, y_ref, *merge_and_out_refs):
    nb, nq = q_ref.shape[0], q_ref.shape[1]
    per_part = nb // MIXER_PARTS
    parts = []
    for p in range(MIXER_PARTS):
        scores, values, sinks, mem_scores, mem_values = [], [], [], [], []
        for b in range(p * per_part, (p + 1) * per_part):
            kb, vb = _kv_heads(k_ref[b]), _kv_heads(v_ref[b])
            for g, qs in enumerate(_gqa_queries(q_ref[b])):
                scores.append(_qk(qs, kb[g]) + bias_ref[g])
                values.append(vb[g])
                sinks.append(sink_ref[g])
            mem_scores += _mem_scores(qm_ref[b], _row_heads(mk_ref, b))
            mem_values += _mem_values(_row_heads(mv_ref, b))

        def assemble(outs, mem):
            attn = jnp.concatenate([_gqa_outputs(outs[j:j + N_KV_HEADS], nq)
                                    for j in range(0, len(outs), N_KV_HEADS)], axis=0)
            memo = jnp.concatenate([jnp.concatenate(mem[j:j + MEM_HEADS], axis=1)
                                    for j in range(0, len(mem), MEM_HEADS)], axis=0)
            return attn, memo

        parts.append((scores, values, sinks, mem_scores, mem_values, assemble))
    _mix_parts(parts, x1_ref, y_ref, *merge_and_out_refs)


def _mixer_sample(q, kk2, vv2, bias, sink, qm, mk, mv, x1, y, merge_w):
    b, s, _ = q.shape
    nb = SAMPLE_MIXER_TILE // s
    nk = kk2.shape[1]
    blk = lambda *shape: pl.BlockSpec((nb,) + shape, lambda i: (i,) + (0,) * len(shape))
    row = lambda w: pl.BlockSpec((nb * s, w), lambda i: (i, 0))
    return pl.pallas_call(
        _mixer_sample_body,
        grid=(b // nb,),
        in_specs=[blk(s, ATTN_W), blk(nk, KV2_W), blk(nk, KV2_W),
                  _resident(bias.shape), _resident(sink.shape),
                  blk(s, MEM_W), blk(*mk.shape[1:]), blk(*mv.shape[1:]), row(D_MODEL), row(SSM_W)]
        + _merge_weight_specs(merge_w),
        out_specs=row(D_MODEL),
        out_shape=jax.ShapeDtypeStruct((b * s, D_MODEL), F32),
        compiler_params=_params("parallel"),
        name="mixer_sample",
    )(q, kk2, vv2, bias, sink, qm, mk, mv, x1, y, *merge_w)


def _ssm_weights_body(lam_re_row, lam_im_row, lam_re_col, lam_im_col, log_dt, b_re_t, b_im_t,
                      b_re_rows, b_im_rows, c_re_rows, c_im_rows, c_re_lanes, c_im_lanes, sel_ref,
                      wconv_ref, wst_ref, wot_ref, a_tab_ref):
    wst_ref[0] = jnp.zeros(wst_ref.shape[1:], BF16)
    wot_ref[0] = jnp.zeros(wot_ref.shape[1:], BF16)
    krows = []
    for gl in range(SSM_GB):
        krows.append(_ssm_group_weights(
            gl, *(r[gl] for r in (lam_re_row, lam_im_row, lam_re_col, lam_im_col, log_dt, b_re_t, b_im_t,
                                  b_re_rows, b_im_rows, c_re_rows, c_im_rows, c_re_lanes, c_im_lanes)),
            wst_ref, wot_ref, a_tab_ref))
    kstack = jnp.concatenate(krows, axis=0).astype(BF16)
    for d in range(SSM_NT):
        for ti in range(SSM_TPT):
            blk = _mm(kstack, sel_ref[d * SSM_TPT + ti])
            r0 = (SSM_NT - 1 - d) * MXU_TILE + ti * LANES
            for gl in range(SSM_GB):
                piece = blk[gl * SSM_GROUP:(gl + 1) * SSM_GROUP]
                piece = (piece if gl == 0 else pltpu.roll(piece, gl * SSM_GROUP, 1)).astype(BF16)
                r = r0 + gl * SSM_GROUP
                wconv_ref[0, r:r + SSM_GROUP, :MXU_TILE] = piece
                if r0 >= MXU_TILE:
                    wconv_ref[0, r - MXU_TILE:r - MXU_TILE + SSM_GROUP, MXU_TILE:] = piece
    wconv_ref[0, (SSM_NT - 1) * MXU_TILE:, MXU_TILE:] = jnp.zeros((MXU_TILE, MXU_TILE), BF16)


def _ssm_group_weights(gl, lam_re_row, lam_im_row, lam_re_col, lam_im_col, log_dt, b_re_t, b_im_t,
                       b_re_rows, b_im_rows, c_re_rows, c_im_rows, c_re_lanes, c_im_lanes,
                       wst_ref, wot_ref, a_tab_ref):
    dt = jnp.exp(log_dt)
    rows = lambda t: slice(t * LANES + gl * SSM_GROUP, t * LANES + (gl + 1) * SSM_GROUP)
    lanes = slice(gl * SSM_S2, (gl + 1) * SSM_S2)

    def zoh_coef(lr, li):
        mag = jnp.exp(lr * dt)
        a_re, a_im = mag * jnp.cos(li * dt), mag * jnp.sin(li * dt)
        den = lr * lr + li * li
        return ((a_re - 1.0) * lr + a_im * li) / den, (a_im * lr - (a_re - 1.0) * li) / den

    def a_power(lr, li, n):
        mag = jnp.exp(lr * dt * n)
        return mag * jnp.cos(li * dt * n), mag * jnp.sin(li * dt * n)

    lr, li = lam_re_row, lam_im_row
    cr, ci = zoh_coef(lr, li)
    n_rows = jnp.minimum(lax.broadcasted_iota(jnp.int32, (SSM_T + SCAN_ROWS, 1), 0), SSM_T).astype(F32)
    pw_r, pw_i = a_power(lr, li, n_rows)

    def over_channels(pw, exps):
        return jnp.concatenate([jnp.broadcast_to(pw[e:e + 1], (SSM_GROUP, SSM_STATE)) for e in exps], axis=0)

    back = [SSM_T - 1 - t for t in range(SSM_T)]
    pr, pi = over_channels(pw_r, back), over_channels(pw_i, back)
    zr, zi = pr * cr - pi * ci, pr * ci + pi * cr
    br, bi = b_re_rows, b_im_rows
    wstate = jnp.concatenate([zr * br - zi * bi, zr * bi + zi * br], axis=1).astype(BF16)

    fwd = [t + 1 for t in range(SSM_T)]
    qr, qi = over_channels(pw_r, fwd), over_channels(pw_i, fwd)
    ccr, cci = c_re_rows, c_im_rows
    wout_t = jnp.concatenate([ccr * qr - cci * qi, -(ccr * qi + cci * qr)], axis=1).astype(BF16)
    for t in range(SSM_T):
        wst_ref[0, rows(t), lanes] = wstate[t * SSM_GROUP:(t + 1) * SSM_GROUP]
        wot_ref[0, rows(t), lanes] = wout_t[t * SSM_GROUP:(t + 1) * SSM_GROUP]

    lrc, lic = lam_re_col, lam_im_col
    hi = lax.Precision.HIGHEST
    n_lanes = jnp.minimum(lax.broadcasted_iota(jnp.int32, (1, LANES), 1), SSM_T).astype(F32)
    pc_r, pc_i = a_power(lrc, lic, n_lanes)
    spread = (lax.broadcasted_iota(jnp.int32, (LANES, SSM_K), 0)
              == lax.broadcasted_iota(jnp.int32, (LANES, SSM_K), 1) // SSM_GROUP).astype(F32)
    gr = jnp.dot(pc_r, spread, precision=hi, preferred_element_type=F32)
    gi = jnp.dot(pc_i, spread, precision=hi, preferred_element_type=F32)
    clr, cli = c_re_lanes, c_im_lanes
    g_re, g_im = clr * gr - cli * gi, clr * gi + cli * gr
    btr, bti = b_re_t, b_im_t
    bbr, bbi = cr * btr - ci * bti, cr * bti + ci * btr
    krow = (jnp.dot(bbr, g_re, precision=hi, preferred_element_type=F32)
            - jnp.dot(bbi, g_im, precision=hi, preferred_element_type=F32))

    idx = lax.broadcasted_iota(jnp.int32, (N_SCAN_TABLES * SCAN_ROWS, SSM_STATE), 0)
    tab, r = idx // SCAN_ROWS, idx % SCAN_ROWS
    stride = jnp.where(tab < 2, 1, jnp.where(tab < 4, 2, 4))
    n = jnp.where(tab < 6, stride, jnp.where(tab < 8, r, SCAN_ROWS))
    keep = jnp.logical_or(tab >= 6, r >= stride)
    er, ei = a_power(lr, li, (n * SSM_T).astype(F32))
    er, ei = jnp.where(keep, er, 0.0), jnp.where(keep, ei, 0.0)
    odd = tab % 2 == 1
    tabs = jnp.concatenate([jnp.where(odd, -ei, er), jnp.where(odd, ei, er)], axis=1)
    a_tab_ref[:, :, lanes] = tabs.reshape(N_SCAN_TABLES, SCAN_ROWS, SSM_S2)
    return krow


def _lag_selectors():
    sel = np.zeros((SSM_NT, SSM_TPT, SSM_K, MXU_TILE), np.float32)
    ch = np.arange(SSM_GROUP)
    for d in range(SSM_NT):
        for ti in range(SSM_TPT):
            for to in range(SSM_TPT):
                lag = SSM_TPT * d + to - ti
                if lag >= 0:
                    sel[d, ti, lag * SSM_GROUP + ch, to * LANES + ch] = 1.0
    return jnp.asarray(sel.reshape(SSM_NT * SSM_TPT, SSM_K, MXU_TILE), BF16)


def _ssm_weights(lam_re, lam_im, log_dt, b_re, b_im, c_re, c_im):
    g, p, c = b_re.shape
    row3 = lambda x: x.reshape(g, 1, p)
    col3 = lambda x: x.reshape(g, p, 1)
    t3 = lambda x: jnp.transpose(x, (0, 2, 1))
    b_rows = lambda x: jnp.tile(t3(x), (1, SSM_T, 1))
    c_rows = lambda x: jnp.tile(x, (1, SSM_T, 1))
    c_lanes = lambda x: jnp.tile(t3(x), (1, 1, SSM_T))
    args = (row3(lam_re), row3(lam_im), col3(lam_re), col3(lam_im), log_dt.reshape(g, 1, 1),
            t3(b_re), t3(b_im), b_rows(b_re), b_rows(b_im), c_rows(c_re), c_rows(c_im),
            c_lanes(c_re), c_lanes(c_im))
    sel = _lag_selectors()
    spec = lambda x: pl.BlockSpec((SSM_GB,) + x.shape[1:], lambda i: (i, 0, 0))
    per_gb = lambda *shape: pl.BlockSpec((1,) + shape, lambda i: (i, 0, 0))
    return pl.pallas_call(
        _ssm_weights_body,
        grid=(SSM_NGB,),
        in_specs=[spec(a) for a in args] + [_resident(sel.shape)],
        out_specs=[per_gb(SSM_NT * MXU_TILE, 2 * MXU_TILE), per_gb(SSM_XW, SSM_SW), per_gb(SSM_XW, SSM_SW),
                   pl.BlockSpec((N_SCAN_TABLES, SCAN_ROWS, SSM_SW), lambda i: (0, 0, i))],
        out_shape=[jax.ShapeDtypeStruct((SSM_NGB, SSM_NT * MXU_TILE, 2 * MXU_TILE), BF16),
                   jax.ShapeDtypeStruct((SSM_NGB, SSM_XW, SSM_SW), BF16),
                   jax.ShapeDtypeStruct((SSM_NGB, SSM_XW, SSM_SW), BF16),
                   jax.ShapeDtypeStruct((N_SCAN_TABLES, SCAN_ROWS, SSM_WIDTH), F32)],
        compiler_params=_params("parallel"),
        name="ssm_weights",
    )(*args, sel)


def _swap_halves(s):
    ax = s.ndim - 1
    return jnp.concatenate([pltpu.roll(s[..., l:l + SSM_S2], SSM_STATE, ax)
                            for l in range(0, s.shape[ax], SSM_S2)], axis=ax)


def _ssm_body(row_sets, slab_rows, u_ref, wconv_ref, wst_ref, wot_ref, d_ref, a_tab_ref, s0_ref,
              y_ref, s_ref, v_scr, s_scr):
    def piece(ref_set, first, n, stride, t):
        return ref_set, pl.ds(first + t, n, stride=stride)

    @pl.when(pl.program_id(1) == 0)
    def _():
        s_scr[...] = s0_ref[0]

    x = jnp.concatenate(
        [jnp.concatenate([u_ref[piece(*rs, t)] for t in range(SSM_T)], axis=1) for rs in row_sets],
        axis=0).astype(BF16)
    v_scr[...] = _mm(x, wst_ref[0])
    conv = []
    for jo in range(0, SSM_NT, 2):
        pair = _mm(x[:, :(jo + 2) * MXU_TILE], wconv_ref[0, (SSM_NT - 2 - jo) * MXU_TILE:, :])
        conv += [pair[:, MXU_TILE:], pair[:, :MXU_TILE]]

    n_rows = v_scr.shape[0]
    if slab_rows:
        a_mul, a_swap = a_tab_ref[6, 1:2, :], a_tab_ref[7, 1:2, :]
        s = s_scr[...]
        for c in range(n_rows // slab_rows):
            rows = slice(c * slab_rows, (c + 1) * slab_rows)
            inc = v_scr[rows, :]
            v_scr[rows, :] = s
            s = a_mul * s + a_swap * _swap_halves(s) + inc
        s_scr[...] = s
    else:
        n_blocks = n_rows // SCAN_ROWS
        xs = v_scr[...].reshape(n_blocks, SCAN_ROWS, SSM_SW)
        for k in range(3):
            sh = pltpu.roll(xs, 1 << k, 1)
            xs = xs + a_tab_ref[2 * k] * sh + a_tab_ref[2 * k + 1] * _swap_halves(sh)
        carries, blk = [], 0
        for j, (_, _, n, _) in enumerate(row_sets):
            carry = s_scr[j * SCAN_ROWS:(j + 1) * SCAN_ROWS, :]
            for _ in range(n // SCAN_ROWS):
                carries.append(carry)
                last = jnp.broadcast_to(xs[blk, SCAN_ROWS - 1:, :], carry.shape)
                carry = last + a_tab_ref[8] * carry + a_tab_ref[9] * _swap_halves(carry)
                blk += 1
            s_scr[j * SCAN_ROWS:(j + 1) * SCAN_ROWS, :] = carry
        carries = jnp.stack(carries, axis=0)
        not_first = lax.broadcasted_iota(jnp.int32, (1, SCAN_ROWS, 1), 1) >= 1
        enter = (jnp.where(not_first, pltpu.roll(xs, 1, 1), 0.0)
                 + a_tab_ref[6] * carries + a_tab_ref[7] * _swap_halves(carries))
        v_scr[...] = enter.reshape(n_rows, SSM_SW)
    s_ref[0] = s_scr[...]

    enter = v_scr[...].astype(BF16)
    d = d_ref[...]
    carried = _qk(enter, wot_ref[0])
    for jo in range(SSM_NT):
        yt = conv[jo] + carried[:, jo * MXU_TILE:(jo + 1) * MXU_TILE]
        r0 = 0
        for rs in row_sets:
            for tl in range(SSM_TPT):
                idx = piece(*rs, jo * SSM_TPT + tl)
                y_ref[idx] = yt[r0:r0 + rs[2], tl * LANES:(tl + 1) * LANES] + u_ref[idx] * d
            r0 += rs[2]


def _ssm(u, row_sets, slab_rows, block_rows, wconv, wst, wo, d_lanes, a_tab, s0):
    ns, r, _ = u.shape
    chunk_rows = sum(rs[2] for rs in row_sets)
    carry_rows = s0.shape[1]
    blk = pl.BlockSpec((ns, block_rows, LANES), lambda gb, i: (0, i, gb))
    per_gb = lambda x: pl.BlockSpec((1,) + x.shape[1:], lambda gb, i: (gb, 0, 0))
    return pl.pallas_call(
        functools.partial(_ssm_body, row_sets, slab_rows),
        grid=(SSM_NGB, r // block_rows),
        in_specs=[blk, per_gb(wconv), per_gb(wst), per_gb(wo),
                  pl.BlockSpec((1, LANES), lambda gb, i: (0, gb)),
                  pl.BlockSpec((N_SCAN_TABLES, SCAN_ROWS, SSM_SW), lambda gb, i: (0, 0, gb)),
                  per_gb(s0)],
        out_specs=[blk, per_gb(s0)],
        out_shape=[jax.ShapeDtypeStruct(u.shape, F32), jax.ShapeDtypeStruct(s0.shape, F32)],
        scratch_shapes=[pltpu.VMEM((chunk_rows, SSM_SW), F32), pltpu.VMEM((carry_rows, SSM_SW), F32)],
        compiler_params=_params("parallel", "arbitrary"),
        name="ssm_scan",
    )(u, wconv, wst, wo, d_lanes, a_tab, s0)


def _ssm_branch(u, s0, ssm_w):
    b, l, _ = u.shape
    nc = l // SSM_T
    by_gb = lambda s: s.reshape(s.shape[0], SSM_NGB, SSM_SW).transpose(1, 0, 2)
    if b % SCAN_ROWS == 0:
        row_sets = tuple((0, c * SSM_T, b, l) for c in range(nc))
        y, s_last = _ssm(u.reshape(1, b * l, SSM_W), row_sets, b, b * l, *ssm_w, by_gb(s0))
    else:
        cps = min(nc, SSM_CHUNKS_PER_STEP)
        row_sets = tuple((j, 0, cps, SSM_T) for j in range(b))
        y, s_last = _ssm(u, row_sets, 0, cps * SSM_T, *ssm_w, by_gb(jnp.repeat(s0, SCAN_ROWS, axis=0)))
        s_last = s_last[:, ::SCAN_ROWS]
    s_last = s_last.transpose(1, 0, 2).reshape(b, SSM_GROUPS, 2, SSM_STATE)
    return y.reshape(b, l, SSM_W), s_last[:, :, 0], s_last[:, :, 1]


def _ffn2_body(steps_a, xa_ref, xb_ref, gpre_ref, gpost_ref, wfi_ref, wfo_ref, oa_ref, ob_ref):
    def run(x_ref, o_ref):
        outs = _ffn(_row_parts(x_ref, FFN_PARTS), gpre_ref[...], gpost_ref[...], wfi_ref, wfo_ref)
        rows = x_ref.shape[0] // FFN_PARTS
        for i, o in enumerate(outs):
            o_ref[i * rows:(i + 1) * rows, :] = o

    _on_group(steps_a, run, (xa_ref, oa_ref), (xb_ref, ob_ref))


def _ffn2(xa, xb, gpre, gpost, wfi, wfo):
    steps_a, steps_b = xa.shape[0] // FFN_ROW_TILE, xb.shape[0] // FFN_ROW_TILE
    first, second = _two_group_specs(steps_a)
    vec = _resident((1, D_MODEL))
    return pl.pallas_call(
        functools.partial(_ffn2_body, steps_a),
        grid=(steps_a + steps_b,),
        in_specs=[first(D_MODEL), second(D_MODEL), vec, vec, _resident(wfi.shape), _resident(wfo.shape)],
        out_specs=[first(D_MODEL), second(D_MODEL)],
        out_shape=[jax.ShapeDtypeStruct(xa.shape, F32), jax.ShapeDtypeStruct(xb.shape, F32)],
        compiler_params=_params("arbitrary"),
        name="ffn2",
    )(xa, xb, gpre, gpost, wfi, wfo)


def _t5_bucket(rel):
    half = N_BUCKETS // 2
    max_exact = half // 2
    ret = (rel > 0).astype(np.int32) * half
    n = np.abs(rel)
    large = max_exact + (np.log(np.maximum(n, 1) / max_exact) / math.log(MAX_DISTANCE / max_exact)
                         * (half - max_exact)).astype(np.int32)
    large = np.minimum(large, half - 1)
    return ret + np.where(n < max_exact, n, large)


def _band_bias(rel_table, n_q, n_back, n_k):
    i = np.arange(n_q)[:, None]
    j = np.arange(n_k)[None, :]
    bucket = _t5_bucket((j - n_back) - i).reshape(-1)
    onehot = np.zeros((N_BUCKETS, bucket.size), np.float32)
    onehot[bucket, np.arange(bucket.size)] = 1.0
    b = jnp.dot(rel_table.astype(F32).T, jnp.asarray(onehot), precision=lax.Precision.HIGHEST)
    return b.reshape(N_KV_HEADS, KV_REP * n_q, n_k)


def _sink_rows(sink, n_q):
    return jnp.repeat(sink.astype(F32).reshape(N_KV_HEADS, KV_REP), n_q, axis=1)[:, :, None]


def _twice_per_head(x, axis):
    shape = x.shape
    x = x.reshape(shape[:axis] + (N_KV_HEADS, 1, HEAD_DIM) + shape[axis + 1:])
    x = jnp.concatenate([x, x], axis=axis + 1)
    return x.reshape(shape[:axis] + (KV2_W,) + shape[axis + 1:])


def kernel(x_prompt, x_sample, cache_swa_k, cache_swa_v, cache_mem_k, cache_mem_v, state_ssm_re, state_ssm_im, mem_prompt, rel_bias_table, ff1_pre_g, ff1_post_g, w_ff1_in, w_ff1_out, mix_pre_g, mix_post_g, w_in, mem_norm_g, w_mem_kv, attn_sink, ssm_lambda_re, ssm_lambda_im, ssm_log_dt, ssm_b_re, ssm_b_im, ssm_c_re, ssm_c_im, ssm_d, w_ssm_glu, w_attn_br, w_mem_br, w_out, ff2_pre_g, ff2_post_g, w_ff2_in, w_ff2_out):
    assert ff1_pre_g.shape[0] == 1, "single-layer step"
    bp, lp, _ = x_prompt.shape
    bs, ls, _ = x_sample.shape
    vec = lambda g: g[0].reshape(1, D_MODEL).astype(F32)
    w16 = lambda w: w[0].astype(BF16)

    wfi1, wfo1, wfi2, wfo2 = w16(w_ff1_in), w16(w_ff1_out), w16(w_ff2_in), w16(w_ff2_out)
    w_in16 = w16(w_in)
    wab, wglu, wmb, wo = w16(w_attn_br), w16(w_ssm_glu), w16(w_mem_br), w16(w_out)

    wconv, wst, wot, a_tab = _ssm_weights(
        ssm_lambda_re[0], ssm_lambda_im[0], ssm_log_dt[0], ssm_b_re[0], ssm_b_im[0], ssm_c_re[0], ssm_c_im[0])
    ssm_w = (wconv, wst, wot, ssm_d[0].astype(F32).reshape(1, SSM_W), a_tab)

    mk_p, mv_p = _mem_kv(mem_prompt.reshape(bp * MEM_LEN, D_MODEL), vec(mem_norm_g), w16(w_mem_kv))

    def mix(x_shape, proj, mixer, s0, mem_k, mem_v):
        b, l, _ = x_shape
        r3 = lambda t: t.reshape(b, l, t.shape[-1])
        x1, q, k, v, u, qm, k2, v2 = proj
        y_ssm, s_re, s_im = _ssm_branch(r3(u), s0, ssm_w)
        x2 = mixer(r3(q), r3(k2), r3(v2), r3(qm), mem_k, mem_v, x1, y_ssm.reshape(b * l, SSM_W))
        return x2, r3(k), r3(v), s_re, s_im

    merge_w = (vec(mix_pre_g), w_in16, wglu, wab, wmb, wo, vec(mix_post_g))

    def mixer_prompt(q, k2, v2, qm, mem_k, mem_v, x1, y_ssm):
        bias = _band_bias(rel_bias_table, CHUNK, WINDOW, WINDOW + CHUNK)
        return _mixer_prompt(q, k2, v2, bias, _sink_rows(attn_sink[0], CHUNK), qm, mem_k, mem_v, x1, y_ssm, merge_w)

    def mixer_sample(q, k2, v2, qm, mem_k, mem_v, x1, y_ssm):
        n_back = cache_swa_k.shape[2]
        cache2 = lambda c: _twice_per_head(c[0].reshape(bs, n_back, KV_W).astype(BF16), 2)
        kk = jnp.concatenate([cache2(cache_swa_k), k2], axis=1)
        vv = jnp.concatenate([cache2(cache_swa_v), v2], axis=1)
        bias = _band_bias(rel_bias_table, ls, n_back, n_back + ls)
        return _mixer_sample(q, kk, vv, bias, _sink_rows(attn_sink[0], ls), qm, mem_k, mem_v, x1, y_ssm, merge_w)

    proj_p, proj_s = _ffn1_proj(x_prompt.reshape(bp * lp, D_MODEL), x_sample.reshape(bs * ls, D_MODEL),
                                vec(ff1_pre_g), vec(ff1_post_g), wfi1, wfo1, vec(mix_pre_g), w_in16[:, :PROJ_COLS])
    x2p, pk, pv, pre, pim = mix(x_prompt.shape, proj_p, mixer_prompt, jnp.zeros((bp, SSM_WIDTH), F32),
                                mk_p.reshape(bp, MEM_LEN, MEM_W), mv_p.reshape(bp, MEM_LEN, MEM_W))
    s0 = jnp.stack([state_ssm_re[0], state_ssm_im[0]], axis=2).reshape(bs, SSM_WIDTH).astype(F32)
    x2s, sk, sv, sre, sim = mix(x_sample.shape, proj_s, mixer_sample, s0,
                                cache_mem_k[0].reshape(bs, MEM_LEN * MEM_HEADS, MEM_HEAD_DIM),
                                cache_mem_v[0].reshape(bs, MEM_LEN * MEM_HEADS, MEM_HEAD_DIM))
    yp, ys = _ffn2(x2p, x2s, vec(ff2_pre_g), vec(ff2_post_g), wfi2, wfo2)
    yp, ys = yp.reshape(x_prompt.shape), ys.reshape(x_sample.shape)

    n_keep = min(WINDOW, lp)
    heads = lambda t: t.reshape(t.shape[0], t.shape[1], N_KV_HEADS, HEAD_DIM)[None]
    mem_heads = lambda t: t.reshape(bp, MEM_LEN, MEM_HEADS, MEM_HEAD_DIM)[None]
    return (yp, ys, heads(pk[:, -n_keep:]), heads(pv[:, -n_keep:]), mem_heads(mk_p), mem_heads(mv_p),
            pre[None], pim[None], heads(sk), heads(sv), sre[None], sim[None])
```

```python
import functools
import math

import numpy as np
import jax
import jax.numpy as jnp
from jax import lax
from jax.experimental import pallas as pl
from jax.experimental.pallas import tpu as pltpu

D_MODEL = 1024
CHUNK = 64
WINDOW = 128
HEAD_DIM = 64
MIX_W = D_MODEL // 2
N_HEADS = MIX_W // HEAD_DIM
N_KV_HEADS = N_HEADS // 4
KV_REP = N_HEADS // N_KV_HEADS
ATTN_W = N_HEADS * HEAD_DIM
KV_W = N_KV_HEADS * HEAD_DIM
SSM_GROUP = 16
SSM_W = MIX_W
SSM_GROUPS = SSM_W // SSM_GROUP
SSM_STATE = 64
MEM_LEN = 256
MEM_HEADS = 4
MEM_HEAD_DIM = MIX_W // MEM_HEADS
MEM_W = MEM_HEADS * MEM_HEAD_DIM
D_FF = 128 * ((8 * D_MODEL // 3 + 127) // 128)
N_BUCKETS = 32
MAX_DISTANCE = 128
RMS_EPS = 1e-6
NEG_INF = -1e30

LANES = 128
BF16_ROWS = 16
MXU_TILE = 256
ROW_TILE = 256
MIXER_TILE = 512
SAMPLE_MIXER_TILE = 256
MIXER_PARTS = 2
FFN_ROW_TILE = 512
FFN_PARTS = 2
FFN_CHUNK = 2 * MXU_TILE
VMEM_LIMIT = 60 * 1024 * 1024

KV2_W = N_KV_HEADS * LANES

SSM_T = 16
SSM_K = SSM_T * SSM_GROUP
SSM_S2 = 2 * SSM_STATE
SSM_WIDTH = SSM_GROUPS * SSM_S2
SSM_GB = LANES // SSM_GROUP
SSM_NGB = SSM_GROUPS // SSM_GB
SSM_XW = SSM_T * LANES
SSM_SW = SSM_GB * SSM_S2
SSM_TPT = MXU_TILE // LANES
SSM_NT = SSM_T // SSM_TPT
SCAN_ROWS = 8
N_SCAN_TABLES = 10
SSM_CHUNKS_PER_STEP = 256

F32 = jnp.float32
BF16 = jnp.bfloat16


def _params(*sem):
    return pltpu.CompilerParams(dimension_semantics=sem, vmem_limit_bytes=VMEM_LIMIT)


def _resident(shape):
    zeros = (0,) * len(shape)
    return pl.BlockSpec(shape, lambda *_: zeros, pipeline_mode=pl.Buffered(1))


def _rms(x, g):
    return x * lax.rsqrt(jnp.mean(x * x, axis=-1, keepdims=True) + RMS_EPS) * g


def _mm(a, b):
    return jnp.dot(a, b, preferred_element_type=F32)


def _sigmoid(x):
    return 0.5 * jnp.tanh(0.5 * x) + 0.5


def _row_parts(ref, n_parts):
    rows = ref.shape[0] // n_parts
    return [ref[i * rows:(i + 1) * rows, :] for i in range(n_parts)]


def _ffn(xs, gpre, gpost, w_in_ref, w_out_ref):
    hs = [_rms(x, gpre).astype(BF16) for x in xs]
    acts = [[] for _ in xs]
    for c0 in range(0, D_FF, FFN_CHUNK):
        c1 = min(c0 + FFN_CHUNK, D_FF)
        for h, a in zip(hs, acts):
            g = _mm(h, w_in_ref[:, c0:c1])
            u = _mm(h, w_in_ref[:, D_FF + c0:D_FF + c1])
            a.append((g * jax.nn.sigmoid(g) * u).astype(BF16))
    outs = [_mm(jnp.concatenate(a, axis=1), w_out_ref[...]) for a in acts]
    return [x + 0.5 * _rms(o, gpost) for x, o in zip(xs, outs)]


PROJ_SPLIT = (("q", ATTN_W, BF16), ("k", KV_W, F32), ("v", KV_W, F32), ("u", SSM_W, F32),
              ("qm", MEM_W, BF16), ("k2", KV2_W, BF16), ("v2", KV2_W, BF16))
N_PROJ_MATMUL = 5
PROJ_COLS = sum(w for _, w, _ in PROJ_SPLIT[:N_PROJ_MATMUL])
Q_SCALE = HEAD_DIM ** -0.5
assert math.frexp(Q_SCALE)[0] == 0.5, "power of two: scaling q before the bf16 rounding and the dot is exact"


def _twice_per_head_lanes(x):
    assert N_KV_HEADS == 2 and KV_W == LANES
    swapped = pltpu.roll(x, HEAD_DIM, 1)
    low = lax.broadcasted_iota(jnp.int32, x.shape, 1) < HEAD_DIM
    return jnp.concatenate([jnp.where(low, x, swapped), jnp.where(low, swapped, x)], axis=1)


def _two_group_specs(steps_a):
    first = lambda w: pl.BlockSpec((FFN_ROW_TILE, w), lambda i: (jnp.minimum(i, steps_a - 1), 0))
    second = lambda w: pl.BlockSpec((FFN_ROW_TILE, w), lambda i: (jnp.maximum(i - steps_a, 0), 0))
    return first, second


def _on_group(steps_a, run, refs_a, refs_b):
    i = pl.program_id(0)

    @pl.when(i < steps_a)
    def _():
        run(*refs_a)

    @pl.when(i >= steps_a)
    def _():
        run(*refs_b)


def _ffn1_proj_body(steps_a, n_later, xa_ref, xb_ref, gpre_ref, gpost_ref, wfi_ref, wfo_ref, gmix_ref, wp_ref,
                    *refs):
    later_f32, out_refs, later_bf16 = refs[:n_later], refs[n_later:len(refs) - n_later], refs[len(refs) - n_later:]
    for src, dst in zip(later_f32, later_bf16):
        dst[...] = src[...].astype(BF16)

    def run(x_ref, x1_ref, *proj_refs):
        x1s = _ffn(_row_parts(x_ref, FFN_PARTS), gpre_ref[...], gpost_ref[...], wfi_ref, wfo_ref)
        rows = x_ref.shape[0] // FFN_PARTS
        ps = [_mm(_rms(x1, gmix_ref[...]).astype(BF16), wp_ref[...]) for x1 in x1s]
        for i, (x1, p) in enumerate(zip(x1s, ps)):
            part = slice(i * rows, (i + 1) * rows)
            x1_ref[part, :] = x1
            col, cols = 0, {}
            for name, width, _ in PROJ_SPLIT[:N_PROJ_MATMUL]:
                cols[name] = p[:, col:col + width]
                col += width
            cols["q"] = cols["q"] * Q_SCALE
            cols["k2"], cols["v2"] = _twice_per_head_lanes(cols["k"]), _twice_per_head_lanes(cols["v"])
            for ref, (name, _, dtype) in zip(proj_refs, PROJ_SPLIT):
                ref[part, :] = cols[name].astype(dtype)

    n_out = len(out_refs) // 2
    _on_group(steps_a, run, (xa_ref,) + out_refs[:n_out], (xb_ref,) + out_refs[n_out:])


def _row_chunk_spec(w, steps):
    rows = w.shape[0]
    n = next(n for n in range(min(steps, rows // BF16_ROWS), 0, -1)
             if rows % n == 0 and (rows // n) % BF16_ROWS == 0)
    return pl.BlockSpec((rows // n, w.shape[1]), lambda i: (jnp.minimum(i, n - 1), 0))


def _ffn1_proj(xa, xb, gpre, gpost, wfi, wfo, gmix, wp, later_weights):
    steps_a, steps_b = xa.shape[0] // FFN_ROW_TILE, xb.shape[0] // FFN_ROW_TILE
    first, second = _two_group_specs(steps_a)
    widths = [D_MODEL] + [w for _, w, _ in PROJ_SPLIT]
    dtypes = [F32] + [d for _, _, d in PROJ_SPLIT]
    chunk_specs = [_row_chunk_spec(w, steps_a + steps_b) for w in later_weights]
    outs = pl.pallas_call(
        functools.partial(_ffn1_proj_body, steps_a, len(later_weights)),
        grid=(steps_a + steps_b,),
        in_specs=[first(D_MODEL), second(D_MODEL), _resident((1, D_MODEL)), _resident((1, D_MODEL)),
                  _resident(wfi.shape), _resident(wfo.shape), _resident((1, D_MODEL)),
                  _resident(wp.shape)] + chunk_specs,
        out_specs=[first(w) for w in widths] + [second(w) for w in widths] + chunk_specs,
        out_shape=[jax.ShapeDtypeStruct((x.shape[0], w), d) for x in (xa, xb) for w, d in zip(widths, dtypes)]
        + [jax.ShapeDtypeStruct(w.shape, BF16) for w in later_weights],
        compiler_params=_params("arbitrary"),
        name="ffn1_proj",
    )(xa, xb, gpre, gpost, wfi, wfo, gmix, wp, *later_weights)
    n = len(widths)
    return outs[:n], outs[n:2 * n], outs[2 * n:]


def _mem_kv_body(m_ref, g_ref, w_ref, k_ref, v_ref):
    kv = _mm(_rms(m_ref[...], g_ref[...]).astype(BF16), w_ref[...])
    k_ref[...] = kv[:, :MEM_W]
    v_ref[...] = kv[:, MEM_W:]


def _mem_kv(mem, g, w):
    n = mem.shape[0]
    row = lambda wd: pl.BlockSpec((ROW_TILE, wd), lambda i: (i, 0))
    return pl.pallas_call(
        _mem_kv_body,
        grid=(n // ROW_TILE,),
        in_specs=[row(D_MODEL), _resident((1, D_MODEL)), _resident(w.shape)],
        out_specs=[row(MEM_W), row(MEM_W)],
        out_shape=[jax.ShapeDtypeStruct((n, MEM_W), F32)] * 2,
        compiler_params=_params("parallel"),
        name="mem_kv",
    )(mem, g, w)


def _qk(q, k):
    return lax.dot_general(q, k, (((1,), (1,)), ((), ())), preferred_element_type=F32)


def _softmax_pv(scores, values, sinks=None):
    probs = []
    for i, s in enumerate(scores):
        m = jnp.max(s, axis=-1, keepdims=True)
        if sinks is not None:
            m = jnp.maximum(m, sinks[i])
        e = jnp.exp(s - m)
        den = jnp.sum(e, axis=-1, keepdims=True)
        if sinks is not None:
            den = den + jnp.exp(sinks[i] - m)
        probs.append((e * (1.0 / den)).astype(BF16))
    return [_mm(p, v) for p, v in zip(probs, values)]


def _gqa_queries(q):
    nq = q.shape[0]
    low = lax.broadcasted_iota(jnp.int32, (nq, LANES), 1) < HEAD_DIM
    zero = jnp.zeros((nq, LANES), BF16)
    stacks = []
    for g in range(N_KV_HEADS):
        rows = []
        for r in range(KV_REP):
            h = g * KV_REP + r
            q2 = q[:, (h // 2) * LANES:(h // 2 + 1) * LANES]
            rows.append(jnp.where(low, q2, zero) if h % 2 == 0 else jnp.where(low, zero, q2))
        stacks.append(jnp.concatenate(rows, axis=0))
    return stacks


def _gqa_outputs(outs, nq):
    low = lax.broadcasted_iota(jnp.int32, (nq, LANES), 1) < HEAD_DIM
    pairs = []
    for o in outs:
        for r in range(0, KV_REP, 2):
            pairs.append(jnp.where(low, o[r * nq:(r + 1) * nq], o[(r + 1) * nq:(r + 2) * nq]))
    return jnp.concatenate(pairs, axis=1)


def _mem_scores(qm, k_head):
    return [_qk(qm[:, h * MEM_HEAD_DIM:(h + 1) * MEM_HEAD_DIM], k_head(h).astype(BF16)) * (MEM_HEAD_DIM ** -0.5)
            for h in range(MEM_HEADS)]


def _mem_values(v_head):
    return [v_head(h).astype(BF16) for h in range(MEM_HEADS)]


def _lane_heads(ref, i):
    return lambda h: ref[i, :, h * MEM_HEAD_DIM:(h + 1) * MEM_HEAD_DIM]


def _row_heads(ref, i):
    return lambda h: ref[i, pl.ds(h, MEM_LEN, stride=MEM_HEADS), :]


def _kv_heads(x):
    return [x[:, g * LANES:(g + 1) * LANES] for g in range(N_KV_HEADS)]


def _merge_pre(x1, y, gmix_ref, wg_ref, wglu_ref):
    h = _rms(x1, gmix_ref[...]).astype(BF16)
    logits = [_mm(h, wg_ref[:, PROJ_COLS + j * D_MODEL:PROJ_COLS + (j + 1) * D_MODEL]) for j in range(3)]
    y = y.astype(BF16)
    return logits, _mm(y, wglu_ref[:, :D_MODEL]), _mm(y, wglu_ref[:, D_MODEL:])


def _mix_parts(parts, x1_ref, y_ref, gmix_ref, wg_ref, wglu_ref, wab_ref, wmb_ref, wo_ref, gpost_ref, x2_ref):
    x1s, ys = _row_parts(x1_ref, len(parts)), _row_parts(y_ref, len(parts))
    pres = [_merge_pre(x1, y, gmix_ref, wg_ref, wglu_ref) for x1, y in zip(x1s, ys)]
    branches = [assemble(_softmax_pv(scores, values, sinks), _softmax_pv(mem_scores, mem_values))
                for scores, values, sinks, mem_scores, mem_values, assemble in parts]
    projected = [(_mm(attn.astype(BF16), wab_ref[...]), _mm(memo.astype(BF16), wmb_ref[...]))
                 for attn, memo in branches]
    sig = _sigmoid
    merged = [sig(lg[0]) * pa + sig(lg[1]) * (ya * sig(yb)) + sig(lg[2]) * pm
              for (lg, ya, yb), (pa, pm) in zip(pres, projected)]
    outs = [_mm(m.astype(BF16), wo_ref[...]) for m in merged]
    rows = x1_ref.shape[0] // len(parts)
    for i, (x1, o) in enumerate(zip(x1s, outs)):
        x2_ref[i * rows:(i + 1) * rows, :] = x1 + _rms(o, gpost_ref[...])


def _mixer_prompt_body(q_ref, k_ref, kh_ref, v_ref, vh_ref, bias_ref, sink_ref, qm_ref, mk_ref, mv_ref,
                       x1_ref, y_ref, *merge_and_out_refs):
    i = pl.program_id(1)
    tq = q_ref.shape[1]
    kf = jnp.concatenate([kh_ref[0], k_ref[0]], axis=0)
    vf = jnp.concatenate([vh_ref[0], v_ref[0]], axis=0)
    band = WINDOW + CHUNK
    rows = tq // MIXER_PARTS
    parts = []
    for p in range(MIXER_PARTS):
        scores, values, sinks = [], [], []
        for lo in range(p * rows, (p + 1) * rows, CHUNK):
            valid = None
            if lo < WINDOW:
                key_pos = lax.broadcasted_iota(jnp.int32, (1, band), 1) + (i * tq + lo - WINDOW)
                valid = key_pos >= 0
            kb, vb = _kv_heads(kf[lo:lo + band]), _kv_heads(vf[lo:lo + band])
            for g, qs in enumerate(_gqa_queries(q_ref[0, lo:lo + CHUNK, :])):
                s = _qk(qs, kb[g]) + bias_ref[g]
                scores.append(s if valid is None else jnp.where(valid, s, NEG_INF))
                values.append(vb[g])
                sinks.append(sink_ref[g])
        mem_scores = _mem_scores(qm_ref[0, p * rows:(p + 1) * rows, :], _lane_heads(mk_ref, 0))

        def assemble(outs, mem):
            attn = jnp.concatenate([_gqa_outputs(outs[c:c + N_KV_HEADS], CHUNK)
                                    for c in range(0, len(outs), N_KV_HEADS)], axis=0)
            return attn, jnp.concatenate(mem, axis=1)

        parts.append((scores, values, sinks, mem_scores, _mem_values(_lane_heads(mv_ref, 0)), assemble))
    _mix_parts(parts, x1_ref, y_ref, *merge_and_out_refs)


def _merge_weight_specs(weights):
    return [_resident(w.shape) for w in weights]


def _mixer_prompt(q, k2, v2, bias, sink, qm, mk, mv, x1, y, merge_w):
    b, l, _ = q.shape
    tq = MIXER_TILE
    hb = tq // WINDOW
    nt = l // tq
    cur = lambda w: pl.BlockSpec((1, tq, w), lambda bi, i: (bi, i, 0))
    row = lambda w: pl.BlockSpec((tq, w), lambda bi, i: (bi * nt + i, 0))
    halo = pl.BlockSpec((1, WINDOW, KV2_W), lambda bi, i: (bi, jnp.maximum(i * hb - 1, 0), 0))
    mem = pl.BlockSpec((1, MEM_LEN, MEM_W), lambda bi, i: (bi, 0, 0))
    return pl.pallas_call(
        _mixer_prompt_body,
        grid=(b, nt),
        in_specs=[cur(ATTN_W), cur(KV2_W), halo, cur(KV2_W), halo, _resident(bias.shape),
                  _resident(sink.shape), cur(MEM_W), mem, mem, row(D_MODEL), row(SSM_W)]
        + _merge_weight_specs(merge_w),
        out_specs=row(D_MODEL),
        out_shape=jax.ShapeDtypeStruct((b * l, D_MODEL), F32),
        compiler_params=_params("parallel", "parallel"),
        name="mixer_prompt",
    )(q, k2, k2, v2, v2, bias, sink, qm, mk, mv, x1, y, *merge_w)


def _mixer_sample_body(q_ref, k_ref, v_ref, bias_ref, sink_ref, qm_ref, mk_ref, mv_ref,
                       x1_ref, y_ref, *merge_and_out_refs):
    nb, nq = q_ref.shape[0], q_ref.shape[1]
    per_part = nb // MIXER_PARTS
    parts = []
    for p in range(MIXER_PARTS):
        scores, values, sinks, mem_scores, mem_values = [], [], [], [], []
        for b in range(p * per_part, (p + 1) * per_part):
            kb, vb = _kv_heads(k_ref[b]), _kv_heads(v_ref[b])
            for g, qs in enumerate(_gqa_queries(q_ref[b])):
                scores.append(_qk(qs, kb[g]) + bias_ref[g])
                values.append(vb[g])
                sinks.append(sink_ref[g])
            mem_scores += _mem_scores(qm_ref[b], _row_heads(mk_ref, b))
            mem_values += _mem_values(_row_heads(mv_ref, b))

        def assemble(outs, mem):
            attn = jnp.concatenate([_gqa_outputs(outs[j:j + N_KV_HEADS], nq)
                                    for j in range(0, len(outs), N_KV_HEADS)], axis=0)
            memo = jnp.concatenate([jnp.concatenate(mem[j:j + MEM_HEADS], axis=1)
                                    for j in range(0, len(mem), MEM_HEADS)], axis=0)
            return attn, memo

        parts.append((scores, values, sinks, mem_scores, mem_values, assemble))
    _mix_parts(parts, x1_ref, y_ref, *merge_and_out_refs)


def _mixer_sample(q, kk2, vv2, bias, sink, qm, mk, mv, x1, y, merge_w):
    b, s, _ = q.shape
    nb = SAMPLE_MIXER_TILE // s
    nk = kk2.shape[1]
    blk = lambda *shape: pl.BlockSpec((nb,) + shape, lambda i: (i,) + (0,) * len(shape))
    row = lambda w: pl.BlockSpec((nb * s, w), lambda i: (i, 0))
    return pl.pallas_call(
        _mixer_sample_body,
        grid=(b // nb,),
        in_specs=[blk(s, ATTN_W), blk(nk, KV2_W), blk(nk, KV2_W),
                  _resident(bias.shape), _resident(sink.shape),
                  blk(s, MEM_W), blk(*mk.shape[1:]), blk(*mv.shape[1:]), row(D_MODEL), row(SSM_W)]
        + _merge_weight_specs(merge_w),
        out_specs=row(D_MODEL),
        out_shape=jax.ShapeDtypeStruct((b * s, D_MODEL), F32),
        compiler_params=_params("parallel"),
        name="mixer_sample",
    )(q, kk2, vv2, bias, sink, qm, mk, mv, x1, y, *merge_w)


def _ssm_weights_body(lam_re_row, lam_im_row, lam_re_col, lam_im_col, log_dt, b_re_t, b_im_t,
                      b_re_rows, b_im_rows, c_re_rows, c_im_rows, c_re_lanes, c_im_lanes, sel_ref,
                      wconv_ref, wst_ref, wot_ref, a_tab_ref):
    wst_ref[0] = jnp.zeros(wst_ref.shape[1:], BF16)
    wot_ref[0] = jnp.zeros(wot_ref.shape[1:], BF16)
    krows = []
    for gl in range(SSM_GB):
        krows.append(_ssm_group_weights(
            gl, *(r[gl] for r in (lam_re_row, lam_im_row, lam_re_col, lam_im_col, log_dt, b_re_t, b_im_t,
                                  b_re_rows, b_im_rows, c_re_rows, c_im_rows, c_re_lanes, c_im_lanes)),
            wst_ref, wot_ref, a_tab_ref))
    kstack = jnp.concatenate(krows, axis=0).astype(BF16)
    for d in range(SSM_NT):
        for ti in range(SSM_TPT):
            blk = _mm(kstack, sel_ref[d * SSM_TPT + ti])
            r0 = (SSM_NT - 1 - d) * MXU_TILE + ti * LANES
            for gl in range(SSM_GB):
                piece = blk[gl * SSM_GROUP:(gl + 1) * SSM_GROUP]
                piece = (piece if gl == 0 else pltpu.roll(piece, gl * SSM_GROUP, 1)).astype(BF16)
                r = r0 + gl * SSM_GROUP
                wconv_ref[0, r:r + SSM_GROUP, :MXU_TILE] = piece
                if r0 >= MXU_TILE:
                    wconv_ref[0, r - MXU_TILE:r - MXU_TILE + SSM_GROUP, MXU_TILE:] = piece
    wconv_ref[0, (SSM_NT - 1) * MXU_TILE:, MXU_TILE:] = jnp.zeros((MXU_TILE, MXU_TILE), BF16)


def _ssm_group_weights(gl, lam_re_row, lam_im_row, lam_re_col, lam_im_col, log_dt, b_re_t, b_im_t,
                       b_re_rows, b_im_rows, c_re_rows, c_im_rows, c_re_lanes, c_im_lanes,
                       wst_ref, wot_ref, a_tab_ref):
    dt = jnp.exp(log_dt)
    rows = lambda t: slice(t * LANES + gl * SSM_GROUP, t * LANES + (gl + 1) * SSM_GROUP)
    lanes = slice(gl * SSM_S2, (gl + 1) * SSM_S2)

    def zoh_coef(lr, li):
        mag = jnp.exp(lr * dt)
        a_re, a_im = mag * jnp.cos(li * dt), mag * jnp.sin(li * dt)
        den = lr * lr + li * li
        return ((a_re - 1.0) * lr + a_im * li) / den, (a_im * lr - (a_re - 1.0) * li) / den

    def a_power(lr, li, n):
        mag = jnp.exp(lr * dt * n)
        return mag * jnp.cos(li * dt * n), mag * jnp.sin(li * dt * n)

    lr, li = lam_re_row, lam_im_row
    cr, ci = zoh_coef(lr, li)
    n_rows = jnp.minimum(lax.broadcasted_iota(jnp.int32, (SSM_T + SCAN_ROWS, 1), 0), SSM_T).astype(F32)
    pw_r, pw_i = a_power(lr, li, n_rows)

    def over_channels(pw, exps):
        return jnp.concatenate([jnp.broadcast_to(pw[e:e + 1], (SSM_GROUP, SSM_STATE)) for e in exps], axis=0)

    back = [SSM_T - 1 - t for t in range(SSM_T)]
    pr, pi = over_channels(pw_r, back), over_channels(pw_i, back)
    zr, zi = pr * cr - pi * ci, pr * ci + pi * cr
    br, bi = b_re_rows, b_im_rows
    wstate = jnp.concatenate([zr * br - zi * bi, zr * bi + zi * br], axis=1).astype(BF16)

    fwd = [t + 1 for t in range(SSM_T)]
    qr, qi = over_channels(pw_r, fwd), over_channels(pw_i, fwd)
    ccr, cci = c_re_rows, c_im_rows
    wout_t = jnp.concatenate([ccr * qr - cci * qi, -(ccr * qi + cci * qr)], axis=1).astype(BF16)
    for t in range(SSM_T):
        wst_ref[0, rows(t), lanes] = wstate[t * SSM_GROUP:(t + 1) * SSM_GROUP]
        wot_ref[0, rows(t), lanes] = wout_t[t * SSM_GROUP:(t + 1) * SSM_GROUP]

    lrc, lic = lam_re_col, lam_im_col
    hi = lax.Precision.HIGHEST
    n_lanes = jnp.minimum(lax.broadcasted_iota(jnp.int32, (1, LANES), 1), SSM_T).astype(F32)
    pc_r, pc_i = a_power(lrc, lic, n_lanes)
    spread = (lax.broadcasted_iota(jnp.int32, (LANES, SSM_K), 0)
              == lax.broadcasted_iota(jnp.int32, (LANES, SSM_K), 1) // SSM_GROUP).astype(F32)
    gr = jnp.dot(pc_r, spread, precision=hi, preferred_element_type=F32)
    gi = jnp.dot(pc_i, spread, precision=hi, preferred_element_type=F32)
    clr, cli = c_re_lanes, c_im_lanes
    g_re, g_im = clr * gr - cli * gi, clr * gi + cli * gr
    btr, bti = b_re_t, b_im_t
    bbr, bbi = cr * btr - ci * bti, cr * bti + ci * btr
    krow = (jnp.dot(bbr, g_re, precision=hi, preferred_element_type=F32)
            - jnp.dot(bbi, g_im, precision=hi, preferred_element_type=F32))

    idx = lax.broadcasted_iota(jnp.int32, (N_SCAN_TABLES * SCAN_ROWS, SSM_STATE), 0)
    tab, r = idx // SCAN_ROWS, idx % SCAN_ROWS
    stride = jnp.where(tab < 2, 1, jnp.where(tab < 4, 2, 4))
    n = jnp.where(tab < 6, stride, jnp.where(tab < 8, r, SCAN_ROWS))
    keep = jnp.logical_or(tab >= 6, r >= stride)
    er, ei = a_power(lr, li, (n * SSM_T).astype(F32))
    er, ei = jnp.where(keep, er, 0.0), jnp.where(keep, ei, 0.0)
    odd = tab % 2 == 1
    tabs = jnp.concatenate([jnp.where(odd, -ei, er), jnp.where(odd, ei, er)], axis=1)
    a_tab_ref[:, :, lanes] = tabs.reshape(N_SCAN_TABLES, SCAN_ROWS, SSM_S2)
    return krow


def _lag_selectors():
    sel = np.zeros((SSM_NT, SSM_TPT, SSM_K, MXU_TILE), np.float32)
    ch = np.arange(SSM_GROUP)
    for d in range(SSM_NT):
        for ti in range(SSM_TPT):
            for to in range(SSM_TPT):
                lag = SSM_TPT * d + to - ti
                if lag >= 0:
                    sel[d, ti, lag * SSM_GROUP + ch, to * LANES + ch] = 1.0
    return jnp.asarray(sel.reshape(SSM_NT * SSM_TPT, SSM_K, MXU_TILE), BF16)


def _ssm_weights(lam_re, lam_im, log_dt, b_re, b_im, c_re, c_im):
    g, p, c = b_re.shape
    row3 = lambda x: x.reshape(g, 1, p)
    col3 = lambda x: x.reshape(g, p, 1)
    t3 = lambda x: jnp.transpose(x, (0, 2, 1))
    b_rows = lambda x: jnp.tile(t3(x), (1, SSM_T, 1))
    c_rows = lambda x: jnp.tile(x, (1, SSM_T, 1))
    c_lanes = lambda x: jnp.tile(t3(x), (1, 1, SSM_T))
    args = (row3(lam_re), row3(lam_im), col3(lam_re), col3(lam_im), log_dt.reshape(g, 1, 1),
            t3(b_re), t3(b_im), b_rows(b_re), b_rows(b_im), c_rows(c_re), c_rows(c_im),
            c_lanes(c_re), c_lanes(c_im))
    sel = _lag_selectors()
    spec = lambda x: pl.BlockSpec((SSM_GB,) + x.shape[1:], lambda i: (i, 0, 0))
    per_gb = lambda *shape: pl.BlockSpec((1,) + shape, lambda i: (i, 0, 0))
    return pl.pallas_call(
        _ssm_weights_body,
        grid=(SSM_NGB,),
        in_specs=[spec(a) for a in args] + [_resident(sel.shape)],
        out_specs=[per_gb(SSM_NT * MXU_TILE, 2 * MXU_TILE), per_gb(SSM_XW, SSM_SW), per_gb(SSM_XW, SSM_SW),
                   pl.BlockSpec((N_SCAN_TABLES, SCAN_ROWS, SSM_SW), lambda i: (0, 0, i))],
        out_shape=[jax.ShapeDtypeStruct((SSM_NGB, SSM_NT * MXU_TILE, 2 * MXU_TILE), BF16),
                   jax.ShapeDtypeStruct((SSM_NGB, SSM_XW, SSM_SW), BF16),
                   jax.ShapeDtypeStruct((SSM_NGB, SSM_XW, SSM_SW), BF16),
                   jax.ShapeDtypeStruct((N_SCAN_TABLES, SCAN_ROWS, SSM_WIDTH), F32)],
        compiler_params=_params("parallel"),
        name="ssm_weights",
    )(*args, sel)


def _swap_halves(s):
    ax = s.ndim - 1
    return jnp.concatenate([pltpu.roll(s[..., l:l + SSM_S2], SSM_STATE, ax)
                            for l in range(0, s.shape[ax], SSM_S2)], axis=ax)


def _ssm_body(row_sets, slab_rows, u_ref, wconv_ref, wst_ref, wot_ref, d_ref, a_tab_ref, s0_ref,
              y_ref, s_ref, v_scr, s_scr):
    def piece(ref_set, first, n, stride, t):
        return ref_set, pl.ds(first + t, n, stride=stride)

    @pl.when(pl.program_id(1) == 0)
    def _():
        s_scr[...] = s0_ref[0]

    x = jnp.concatenate(
        [jnp.concatenate([u_ref[piece(*rs, t)] for t in range(SSM_T)], axis=1) for rs in row_sets],
        axis=0).astype(BF16)
    v_scr[...] = _mm(x, wst_ref[0])
    conv = []
    for jo in range(0, SSM_NT, 2):
        pair = _mm(x[:, :(jo + 2) * MXU_TILE], wconv_ref[0, (SSM_NT - 2 - jo) * MXU_TILE:, :])
        conv += [pair[:, MXU_TILE:], pair[:, :MXU_TILE]]

    n_rows = v_scr.shape[0]
    if slab_rows:
        a_mul, a_swap = a_tab_ref[6, 1:2, :], a_tab_ref[7, 1:2, :]
        s = s_scr[...]
        for c in range(n_rows // slab_rows):
            rows = slice(c * slab_rows, (c + 1) * slab_rows)
            inc = v_scr[rows, :]
            v_scr[rows, :] = s
            s = a_mul * s + a_swap * _swap_halves(s) + inc
        s_scr[...] = s
    else:
        n_blocks = n_rows // SCAN_ROWS
        xs = v_scr[...].reshape(n_blocks, SCAN_ROWS, SSM_SW)
        for k in range(3):
            sh = pltpu.roll(xs, 1 << k, 1)
            xs = xs + a_tab_ref[2 * k] * sh + a_tab_ref[2 * k + 1] * _swap_halves(sh)
        carries, blk = [], 0
        for j, (_, _, n, _) in enumerate(row_sets):
            carry = s_scr[j * SCAN_ROWS:(j + 1) * SCAN_ROWS, :]
            for _ in range(n // SCAN_ROWS):
                carries.append(carry)
                last = jnp.broadcast_to(xs[blk, SCAN_ROWS - 1:, :], carry.shape)
                carry = last + a_tab_ref[8] * carry + a_tab_ref[9] * _swap_halves(carry)
                blk += 1
            s_scr[j * SCAN_ROWS:(j + 1) * SCAN_ROWS, :] = carry
        carries = jnp.stack(carries, axis=0)
        not_first = lax.broadcasted_iota(jnp.int32, (1, SCAN_ROWS, 1), 1) >= 1
        enter = (jnp.where(not_first, pltpu.roll(xs, 1, 1), 0.0)
                 + a_tab_ref[6] * carries + a_tab_ref[7] * _swap_halves(carries))
        v_scr[...] = enter.reshape(n_rows, SSM_SW)
    s_ref[0] = s_scr[...]

    enter = v_scr[...].astype(BF16)
    d = d_ref[...]
    carried = _qk(enter, wot_ref[0])
    for jo in range(SSM_NT):
        yt = conv[jo] + carried[:, jo * MXU_TILE:(jo + 1) * MXU_TILE]
        r0 = 0
        for rs in row_sets:
            for tl in range(SSM_TPT):
                idx = piece(*rs, jo * SSM_TPT + tl)
                y_ref[idx] = yt[r0:r0 + rs[2], tl * LANES:(tl + 1) * LANES] + u_ref[idx] * d
            r0 += rs[2]


def _ssm(u, row_sets, slab_rows, block_rows, wconv, wst, wo, d_lanes, a_tab, s0):
    ns, r, _ = u.shape
    chunk_rows = sum(rs[2] for rs in row_sets)
    carry_rows = s0.shape[1]
    blk = pl.BlockSpec((ns, block_rows, LANES), lambda gb, i: (0, i, gb))
    per_gb = lambda x: pl.BlockSpec((1,) + x.shape[1:], lambda gb, i: (gb, 0, 0))
    return pl.pallas_call(
        functools.partial(_ssm_body, row_sets, slab_rows),
        grid=(SSM_NGB, r // block_rows),
        in_specs=[blk, per_gb(wconv), per_gb(wst), per_gb(wo),
                  pl.BlockSpec((1, LANES), lambda gb, i: (0, gb)),
                  pl.BlockSpec((N_SCAN_TABLES, SCAN_ROWS, SSM_SW), lambda gb, i: (0, 0, gb)),
                  per_gb(s0)],
        out_specs=[blk, per_gb(s0)],
        out_shape=[jax.ShapeDtypeStruct(u.shape, F32), jax.ShapeDtypeStruct(s0.shape, F32)],
        scratch_shapes=[pltpu.VMEM((chunk_rows, SSM_SW), F32), pltpu.VMEM((carry_rows, SSM_SW), F32)],
        compiler_params=_params("parallel", "arbitrary"),
        name="ssm_scan",
    )(u, wconv, wst, wo, d_lanes, a_tab, s0)


def _ssm_branch(u, s0, ssm_w):
    b, l, _ = u.shape
    nc = l // SSM_T
    by_gb = lambda s: s.reshape(s.shape[0], SSM_NGB, SSM_SW).transpose(1, 0, 2)
    if b % SCAN_ROWS == 0:
        row_sets = tuple((0, c * SSM_T, b, l) for c in range(nc))
        y, s_last = _ssm(u.reshape(1, b * l, SSM_W), row_sets, b, b * l, *ssm_w, by_gb(s0))
    else:
        cps = min(nc, SSM_CHUNKS_PER_STEP)
        row_sets = tuple((j, 0, cps, SSM_T) for j in range(b))
        y, s_last = _ssm(u, row_sets, 0, cps * SSM_T, *ssm_w, by_gb(jnp.repeat(s0, SCAN_ROWS, axis=0)))
        s_last = s_last[:, ::SCAN_ROWS]
    s_last = s_last.transpose(1, 0, 2).reshape(b, SSM_GROUPS, 2, SSM_STATE)
    return y.reshape(b, l, SSM_W), s_last[:, :, 0], s_last[:, :, 1]


def _ffn2_body(steps_a, xa_ref, xb_ref, gpre_ref, gpost_ref, wfi_ref, wfo_ref, oa_ref, ob_ref):
    def run(x_ref, o_ref):
        outs = _ffn(_row_parts(x_ref, FFN_PARTS), gpre_ref[...], gpost_ref[...], wfi_ref, wfo_ref)
        rows = x_ref.shape[0] // FFN_PARTS
        for i, o in enumerate(outs):
            o_ref[i * rows:(i + 1) * rows, :] = o

    _on_group(steps_a, run, (xa_ref, oa_ref), (xb_ref, ob_ref))


def _ffn2(xa, xb, gpre, gpost, wfi, wfo):
    steps_a, steps_b = xa.shape[0] // FFN_ROW_TILE, xb.shape[0] // FFN_ROW_TILE
    first, second = _two_group_specs(steps_a)
    vec = _resident((1, D_MODEL))
    return pl.pallas_call(
        functools.partial(_ffn2_body, steps_a),
        grid=(steps_a + steps_b,),
        in_specs=[first(D_MODEL), second(D_MODEL), vec, vec, _resident(wfi.shape), _resident(wfo.shape)],
        out_specs=[first(D_MODEL), second(D_MODEL)],
        out_shape=[jax.ShapeDtypeStruct(xa.shape, F32), jax.ShapeDtypeStruct(xb.shape, F32)],
        compiler_params=_params("arbitrary"),
        name="ffn2",
    )(xa, xb, gpre, gpost, wfi, wfo)


def _t5_bucket(rel):
    half = N_BUCKETS // 2
    max_exact = half // 2
    ret = (rel > 0).astype(np.int32) * half
    n = np.abs(rel)
    large = max_exact + (np.log(np.maximum(n, 1) / max_exact) / math.log(MAX_DISTANCE / max_exact)
                         * (half - max_exact)).astype(np.int32)
    large = np.minimum(large, half - 1)
    return ret + np.where(n < max_exact, n, large)


def _band_bias(rel_table, n_q, n_back, n_k):
    i = np.arange(n_q)[:, None]
    j = np.arange(n_k)[None, :]
    bucket = _t5_bucket((j - n_back) - i).reshape(-1)
    onehot = np.zeros((N_BUCKETS, bucket.size), np.float32)
    onehot[bucket, np.arange(bucket.size)] = 1.0
    b = jnp.dot(rel_table.astype(F32).T, jnp.asarray(onehot), precision=lax.Precision.HIGHEST)
    return b.reshape(N_KV_HEADS, KV_REP * n_q, n_k)


def _sink_rows(sink, n_q):
    return jnp.repeat(sink.astype(F32).reshape(N_KV_HEADS, KV_REP), n_q, axis=1)[:, :, None]


def _twice_per_head(x, axis):
    shape = x.shape
    x = x.reshape(shape[:axis] + (N_KV_HEADS, 1, HEAD_DIM) + shape[axis + 1:])
    x = jnp.concatenate([x, x], axis=axis + 1)
    return x.reshape(shape[:axis] + (KV2_W,) + shape[axis + 1:])


def kernel(x_prompt, x_sample, cache_swa_k, cache_swa_v, cache_mem_k, cache_mem_v, state_ssm_re, state_ssm_im, mem_prompt, rel_bias_table, ff1_pre_g, ff1_post_g, w_ff1_in, w_ff1_out, mix_pre_g, mix_post_g, w_in, mem_norm_g, w_mem_kv, attn_sink, ssm_lambda_re, ssm_lambda_im, ssm_log_dt, ssm_b_re, ssm_b_im, ssm_c_re, ssm_c_im, ssm_d, w_ssm_glu, w_attn_br, w_mem_br, w_out, ff2_pre_g, ff2_post_g, w_ff2_in, w_ff2_out):
    assert ff1_pre_g.shape[0] == 1, "single-layer step"
    bp, lp, _ = x_prompt.shape
    bs, ls, _ = x_sample.shape
    vec = lambda g: g[0].reshape(1, D_MODEL).astype(F32)
    w16 = lambda w: w[0].astype(BF16)

    later = (w_ff2_in, w_ff2_out, w_in, w_ssm_glu, w_attn_br, w_mem_br, w_out, w_mem_kv)
    proj_p, proj_s, (wfi2, wfo2, w_in16, wglu, wab, wmb, wo, wmkv) = _ffn1_proj(
        x_prompt.reshape(bp * lp, D_MODEL), x_sample.reshape(bs * ls, D_MODEL), vec(ff1_pre_g), vec(ff1_post_g),
        w16(w_ff1_in), w16(w_ff1_out), vec(mix_pre_g), w_in[0, :, :PROJ_COLS].astype(BF16), [w[0] for w in later])

    wconv, wst, wot, a_tab = _ssm_weights(
        ssm_lambda_re[0], ssm_lambda_im[0], ssm_log_dt[0], ssm_b_re[0], ssm_b_im[0], ssm_c_re[0], ssm_c_im[0])
    ssm_w = (wconv, wst, wot, ssm_d[0].astype(F32).reshape(1, SSM_W), a_tab)

    mk_p, mv_p = _mem_kv(mem_prompt.reshape(bp * MEM_LEN, D_MODEL), vec(mem_norm_g), wmkv)

    def mix(x_shape, proj, mixer, s0, mem_k, mem_v):
        b, l, _ = x_shape
        r3 = lambda t: t.reshape(b, l, t.shape[-1])
        x1, q, k, v, u, qm, k2, v2 = proj
        y_ssm, s_re, s_im = _ssm_branch(r3(u), s0, ssm_w)
        x2 = mixer(r3(q), r3(k2), r3(v2), r3(qm), mem_k, mem_v, x1, y_ssm.reshape(b * l, SSM_W))
        return x2, r3(k), r3(v), s_re, s_im

    merge_w = (vec(mix_pre_g), w_in16, wglu, wab, wmb, wo, vec(mix_post_g))

    def mixer_prompt(q, k2, v2, qm, mem_k, mem_v, x1, y_ssm):
        bias = _band_bias(rel_bias_table, CHUNK, WINDOW, WINDOW + CHUNK)
        return _mixer_prompt(q, k2, v2, bias, _sink_rows(attn_sink[0], CHUNK), qm, mem_k, mem_v, x1, y_ssm, merge_w)

    def mixer_sample(q, k2, v2, qm, mem_k, mem_v, x1, y_ssm):
        n_back = cache_swa_k.shape[2]
        cache2 = lambda c: _twice_per_head(c[0].reshape(bs, n_back, KV_W).astype(BF16), 2)
        kk = jnp.concatenate([cache2(cache_swa_k), k2], axis=1)
        vv = jnp.concatenate([cache2(cache_swa_v), v2], axis=1)
        bias = _band_bias(rel_bias_table, ls, n_back, n_back + ls)
        return _mixer_sample(q, kk, vv, bias, _sink_rows(attn_sink[0], ls), qm, mem_k, mem_v, x1, y_ssm, merge_w)

    x2p, pk, pv, pre, pim = mix(x_prompt.shape, proj_p, mixer_prompt, jnp.zeros((bp, SSM_WIDTH), F32),
                                mk_p.reshape(bp, MEM_LEN, MEM_W), mv_p.reshape(bp, MEM_LEN, MEM_W))
    s0 = jnp.stack([state_ssm_re[0], state_ssm_im[0]], axis=2).reshape(bs, SSM_WIDTH).astype(F32)
    x2s, sk, sv, sre, sim = mix(x_sample.shape, proj_s, mixer_sample, s0,
                                cache_mem_k[0].reshape(bs, MEM_LEN * MEM_HEADS, MEM_HEAD_DIM),
                                cache_mem_v[0].reshape(bs, MEM_LEN * MEM_HEADS, MEM_HEAD_DIM))
    yp, ys = _ffn2(x2p, x2s, vec(ff2_pre_g), vec(ff2_post_g), wfi2, wfo2)
    yp, ys = yp.reshape(x_prompt.shape), ys.reshape(x_sample.shape)

    n_keep = min(WINDOW, lp)
    heads = lambda t: t.reshape(t.shape[0], t.shape[1], N_KV_HEADS, HEAD_DIM)[None]
    mem_heads = lambda t: t.reshape(bp, MEM_LEN, MEM_HEADS, MEM_HEAD_DIM)[None]
    return (yp, ys, heads(pk[:, -n_keep:]), heads(pv[:, -n_keep:]), mem_heads(mk_p), mem_heads(mv_p),
            pre[None], pim[None], heads(sk), heads(sv), sre[None], sim[None])
```

```python
import functools
import math

import numpy as np
import jax
import jax.numpy as jnp
from jax import lax
from jax.experimental import pallas as pl
from jax.experimental.pallas import tpu as pltpu

D_MODEL = 1024
CHUNK = 64
WINDOW = 128
HEAD_DIM = 64
MIX_W = D_MODEL // 2
N_HEADS = MIX_W // HEAD_DIM
N_KV_HEADS = N_HEADS // 4
KV_REP = N_HEADS // N_KV_HEADS
ATTN_W = N_HEADS * HEAD_DIM
KV_W = N_KV_HEADS * HEAD_DIM
SSM_GROUP = 16
SSM_W = MIX_W
SSM_GROUPS = SSM_W // SSM_GROUP
SSM_STATE = 64
MEM_LEN = 256
MEM_HEADS = 4
MEM_HEAD_DIM = MIX_W // MEM_HEADS
MEM_W = MEM_HEADS * MEM_HEAD_DIM
D_FF = 128 * ((8 * D_MODEL // 3 + 127) // 128)
N_BUCKETS = 32
MAX_DISTANCE = 128
RMS_EPS = 1e-6
NEG_INF = -1e30

LANES = 128
BF16_ROWS = 16
MXU_TILE = 256
ROW_TILE = 256
MIXER_TILE = 512
SAMPLE_MIXER_TILE = 256
MIXER_PARTS = 2
FFN_ROW_TILE = 512
FFN_PARTS = 2
FFN_CHUNK = 2 * MXU_TILE
VMEM_LIMIT = 60 * 1024 * 1024

KV2_W = N_KV_HEADS * LANES

SSM_T = 16
SSM_K = SSM_T * SSM_GROUP
SSM_S2 = 2 * SSM_STATE
SSM_WIDTH = SSM_GROUPS * SSM_S2
SSM_GB = LANES // SSM_GROUP
SSM_NGB = SSM_GROUPS // SSM_GB
SSM_XW = SSM_T * LANES
SSM_SW = SSM_GB * SSM_S2
SSM_TPT = MXU_TILE // LANES
SSM_NT = SSM_T // SSM_TPT
SCAN_ROWS = 8
N_SCAN_TABLES = 10
SSM_CHUNKS_PER_STEP = 256

F32 = jnp.float32
BF16 = jnp.bfloat16


def _params(*sem):
    return pltpu.CompilerParams(dimension_semantics=sem, vmem_limit_bytes=VMEM_LIMIT)


def _resident(shape):
    zeros = (0,) * len(shape)
    return pl.BlockSpec(shape, lambda *_: zeros, pipeline_mode=pl.Buffered(1))


def _rms(x, g):
    return x * lax.rsqrt(jnp.mean(x * x, axis=-1, keepdims=True) + RMS_EPS) * g


def _mm(a, b):
    return jnp.dot(a, b, preferred_element_type=F32)


def _sigmoid(x):
    return 0.5 * jnp.tanh(0.5 * x) + 0.5


def _row_parts(ref, n_parts):
    rows = ref.shape[0] // n_parts
    return [ref[i * rows:(i + 1) * rows, :] for i in range(n_parts)]


def _ffn(xs, gpre, gpost, w_in_ref, w_out_ref):
    hs = [_rms(x, gpre).astype(BF16) for x in xs]
    acts = [[] for _ in xs]
    for c0 in range(0, D_FF, FFN_CHUNK):
        c1 = min(c0 + FFN_CHUNK, D_FF)
        for h, a in zip(hs, acts):
            g = _mm(h, w_in_ref[:, c0:c1])
            u = _mm(h, w_in_ref[:, D_FF + c0:D_FF + c1])
            a.append((g * jax.nn.sigmoid(g) * u).astype(BF16))
    outs = [_mm(jnp.concatenate(a, axis=1), w_out_ref[...]) for a in acts]
    return [x + 0.5 * _rms(o, gpost) for x, o in zip(xs, outs)]


PROJ_SPLIT = (("q", ATTN_W, BF16), ("k", KV_W, F32), ("v", KV_W, F32), ("u", SSM_W, F32),
              ("qm", MEM_W, BF16), ("k2", KV2_W, BF16), ("v2", KV2_W, BF16))
N_PROJ_MATMUL = 5
PROJ_COLS = sum(w for _, w, _ in PROJ_SPLIT[:N_PROJ_MATMUL])
Q_SCALE = HEAD_DIM ** -0.5
assert math.frexp(Q_SCALE)[0] == 0.5, "power of two: scaling q before the bf16 rounding and the dot is exact"


def _twice_per_head_lanes(x):
    assert N_KV_HEADS == 2 and KV_W == LANES
    swapped = pltpu.roll(x, HEAD_DIM, 1)
    low = lax.broadcasted_iota(jnp.int32, x.shape, 1) < HEAD_DIM
    return jnp.concatenate([jnp.where(low, x, swapped), jnp.where(low, swapped, x)], axis=1)


def _two_group_specs(steps_a):
    first = lambda w: pl.BlockSpec((FFN_ROW_TILE, w), lambda i: (jnp.minimum(i, steps_a - 1), 0))
    second = lambda w: pl.BlockSpec((FFN_ROW_TILE, w), lambda i: (jnp.maximum(i - steps_a, 0), 0))
    return first, second


def _on_group(steps_a, run, refs_a, refs_b):
    i = pl.program_id(0)

    @pl.when(i < steps_a)
    def _():
        run(*refs_a)

    @pl.when(i >= steps_a)
    def _():
        run(*refs_b)


def _ffn1_proj_body(steps_a, n_later, xa_ref, xb_ref, gpre_ref, gpost_ref, wfi_ref, wfo_ref, gmix_ref, wp_ref,
                    *refs):
    later_f32, out_refs, later_bf16 = refs[:n_later], refs[n_later:len(refs) - n_later], refs[len(refs) - n_later:]
    for src, dst in zip(later_f32, later_bf16):
        dst[...] = src[...].astype(BF16)

    def run(x_ref, x1_ref, *proj_refs):
        x1s = _ffn(_row_parts(x_ref, FFN_PARTS), gpre_ref[...], gpost_ref[...], wfi_ref, wfo_ref)
        rows = x_ref.shape[0] // FFN_PARTS
        ps = [_mm(_rms(x1, gmix_ref[...]).astype(BF16), wp_ref[...]) for x1 in x1s]
        for i, (x1, p) in enumerate(zip(x1s, ps)):
            part = slice(i * rows, (i + 1) * rows)
            x1_ref[part, :] = x1
            col, cols = 0, {}
            for name, width, _ in PROJ_SPLIT[:N_PROJ_MATMUL]:
                cols[name] = p[:, col:col + width]
                col += width
            cols["q"] = cols["q"] * Q_SCALE
            cols["k2"], cols["v2"] = _twice_per_head_lanes(cols["k"]), _twice_per_head_lanes(cols["v"])
            for ref, (name, _, dtype) in zip(proj_refs, PROJ_SPLIT):
                ref[part, :] = cols[name].astype(dtype)

    n_out = len(out_refs) // 2
    _on_group(steps_a, run, (xa_ref,) + out_refs[:n_out], (xb_ref,) + out_refs[n_out:])


def _row_chunk_spec(w, steps):
    rows = w.shape[0]
    n = next(n for n in range(min(steps, rows // BF16_ROWS), 0, -1)
             if rows % n == 0 and (rows // n) % BF16_ROWS == 0)
    return pl.BlockSpec((rows // n, w.shape[1]), lambda i: (jnp.minimum(i, n - 1), 0))


def _ffn1_proj(xa, xb, gpre, gpost, wfi, wfo, gmix, wp, later_weights):
    steps_a, steps_b = xa.shape[0] // FFN_ROW_TILE, xb.shape[0] // FFN_ROW_TILE
    first, second = _two_group_specs(steps_a)
    widths = [D_MODEL] + [w for _, w, _ in PROJ_SPLIT]
    dtypes = [F32] + [d for _, _, d in PROJ_SPLIT]
    chunk_specs = [_row_chunk_spec(w, steps_a + steps_b) for w in later_weights]
    outs = pl.pallas_call(
        functools.partial(_ffn1_proj_body, steps_a, len(later_weights)),
        grid=(steps_a + steps_b,),
        in_specs=[first(D_MODEL), second(D_MODEL), _resident((1, D_MODEL)), _resident((1, D_MODEL)),
                  _resident(wfi.shape), _resident(wfo.shape), _resident((1, D_MODEL)),
                  _resident(wp.shape)] + chunk_specs,
        out_specs=[first(w) for w in widths] + [second(w) for w in widths] + chunk_specs,
        out_shape=[jax.ShapeDtypeStruct((x.shape[0], w), d) for x in (xa, xb) for w, d in zip(widths, dtypes)]
        + [jax.ShapeDtypeStruct(w.shape, BF16) for w in later_weights],
        compiler_params=_params("arbitrary"),
        name="ffn1_proj",
    )(xa, xb, gpre, gpost, wfi, wfo, gmix, wp, *later_weights)
    n = len(widths)
    return outs[:n], outs[n:2 * n], outs[2 * n:]


def _mem_kv_body(m_ref, g_ref, w_ref, k_ref, v_ref):
    kv = _mm(_rms(m_ref[...], g_ref[...]).astype(BF16), w_ref[...])
    n = m_ref.shape[0]
    for h in range(MEM_HEADS):
        k_ref[pl.ds(h, n, stride=MEM_HEADS), :] = kv[:, h * MEM_HEAD_DIM:(h + 1) * MEM_HEAD_DIM]
        v_ref[pl.ds(h, n, stride=MEM_HEADS), :] = kv[:, MEM_W + h * MEM_HEAD_DIM:MEM_W + (h + 1) * MEM_HEAD_DIM]


def _mem_kv(mem, g, w):
    n = mem.shape[0]
    out = pl.BlockSpec((ROW_TILE * MEM_HEADS, MEM_HEAD_DIM), lambda i: (i, 0))
    return pl.pallas_call(
        _mem_kv_body,
        grid=(n // ROW_TILE,),
        in_specs=[pl.BlockSpec((ROW_TILE, D_MODEL), lambda i: (i, 0)), _resident((1, D_MODEL)), _resident(w.shape)],
        out_specs=[out, out],
        out_shape=[jax.ShapeDtypeStruct((n * MEM_HEADS, MEM_HEAD_DIM), F32)] * 2,
        compiler_params=_params("parallel"),
        name="mem_kv",
    )(mem, g, w)


def _qk(q, k):
    return lax.dot_general(q, k, (((1,), (1,)), ((), ())), preferred_element_type=F32)


def _softmax_pv(scores, values, sinks=None):
    probs = []
    for i, s in enumerate(scores):
        m = jnp.max(s, axis=-1, keepdims=True)
        if sinks is not None:
            m = jnp.maximum(m, sinks[i])
        e = jnp.exp(s - m)
        den = jnp.sum(e, axis=-1, keepdims=True)
        if sinks is not None:
            den = den + jnp.exp(sinks[i] - m)
        probs.append((e * (1.0 / den)).astype(BF16))
    return [_mm(p, v) for p, v in zip(probs, values)]


def _gqa_queries(q):
    nq = q.shape[0]
    low = lax.broadcasted_iota(jnp.int32, (nq, LANES), 1) < HEAD_DIM
    zero = jnp.zeros((nq, LANES), BF16)
    stacks = []
    for g in range(N_KV_HEADS):
        rows = []
        for r in range(KV_REP):
            h = g * KV_REP + r
            q2 = q[:, (h // 2) * LANES:(h // 2 + 1) * LANES]
            rows.append(jnp.where(low, q2, zero) if h % 2 == 0 else jnp.where(low, zero, q2))
        stacks.append(jnp.concatenate(rows, axis=0))
    return stacks


def _gqa_outputs(outs, nq):
    low = lax.broadcasted_iota(jnp.int32, (nq, LANES), 1) < HEAD_DIM
    pairs = []
    for o in outs:
        for r in range(0, KV_REP, 2):
            pairs.append(jnp.where(low, o[r * nq:(r + 1) * nq], o[(r + 1) * nq:(r + 2) * nq]))
    return jnp.concatenate(pairs, axis=1)


def _mem_scores(qm, k_head):
    return [_qk(qm[:, h * MEM_HEAD_DIM:(h + 1) * MEM_HEAD_DIM], k_head(h).astype(BF16)) * (MEM_HEAD_DIM ** -0.5)
            for h in range(MEM_HEADS)]


def _mem_values(v_head):
    return [v_head(h).astype(BF16) for h in range(MEM_HEADS)]


def _row_heads(ref, i):
    return lambda h: ref[i, pl.ds(h, MEM_LEN, stride=MEM_HEADS), :]


def _kv_heads(x):
    return [x[:, g * LANES:(g + 1) * LANES] for g in range(N_KV_HEADS)]


def _merge_pre(x1, y, gmix_ref, wg_ref, wglu_ref):
    h = _rms(x1, gmix_ref[...]).astype(BF16)
    logits = [_mm(h, wg_ref[:, PROJ_COLS + j * D_MODEL:PROJ_COLS + (j + 1) * D_MODEL]) for j in range(3)]
    y = y.astype(BF16)
    return logits, _mm(y, wglu_ref[:, :D_MODEL]), _mm(y, wglu_ref[:, D_MODEL:])


def _mix_parts(parts, x1_ref, y_ref, gmix_ref, wg_ref, wglu_ref, wab_ref, wmb_ref, wo_ref, gpost_ref, x2_ref):
    x1s, ys = _row_parts(x1_ref, len(parts)), _row_parts(y_ref, len(parts))
    pres = [_merge_pre(x1, y, gmix_ref, wg_ref, wglu_ref) for x1, y in zip(x1s, ys)]
    branches = [assemble(_softmax_pv(scores, values, sinks), _softmax_pv(mem_scores, mem_values))
                for scores, values, sinks, mem_scores, mem_values, assemble in parts]
    projected = [(_mm(attn.astype(BF16), wab_ref[...]), _mm(memo.astype(BF16), wmb_ref[...]))
                 for attn, memo in branches]
    sig = _sigmoid
    merged = [sig(lg[0]) * pa + sig(lg[1]) * (ya * sig(yb)) + sig(lg[2]) * pm
              for (lg, ya, yb), (pa, pm) in zip(pres, projected)]
    outs = [_mm(m.astype(BF16), wo_ref[...]) for m in merged]
    rows = x1_ref.shape[0] // len(parts)
    for i, (x1, o) in enumerate(zip(x1s, outs)):
        x2_ref[i * rows:(i + 1) * rows, :] = x1 + _rms(o, gpost_ref[...])


def _mixer_prompt_body(q_ref, k_ref, kh_ref, v_ref, vh_ref, bias_ref, sink_ref, qm_ref, mk_ref, mv_ref,
                       x1_ref, y_ref, *merge_and_out_refs):
    i = pl.program_id(1)
    tq = q_ref.shape[1]
    kf = jnp.concatenate([kh_ref[0], k_ref[0]], axis=0)
    vf = jnp.concatenate([vh_ref[0], v_ref[0]], axis=0)
    band = WINDOW + CHUNK
    rows = tq // MIXER_PARTS
    parts = []
    for p in range(MIXER_PARTS):
        scores, values, sinks = [], [], []
        for lo in range(p * rows, (p + 1) * rows, CHUNK):
            valid = None
            if lo < WINDOW:
                key_pos = lax.broadcasted_iota(jnp.int32, (1, band), 1) + (i * tq + lo - WINDOW)
                valid = key_pos >= 0
            kb, vb = _kv_heads(kf[lo:lo + band]), _kv_heads(vf[lo:lo + band])
            for g, qs in enumerate(_gqa_queries(q_ref[0, lo:lo + CHUNK, :])):
                s = _qk(qs, kb[g]) + bias_ref[g]
                scores.append(s if valid is None else jnp.where(valid, s, NEG_INF))
                values.append(vb[g])
                sinks.append(sink_ref[g])
        mem_scores = _mem_scores(qm_ref[0, p * rows:(p + 1) * rows, :], _row_heads(mk_ref, 0))

        def assemble(outs, mem):
            attn = jnp.concatenate([_gqa_outputs(outs[c:c + N_KV_HEADS], CHUNK)
                                    for c in range(0, len(outs), N_KV_HEADS)], axis=0)
            return attn, jnp.concatenate(mem, axis=1)

        parts.append((scores, values, sinks, mem_scores, _mem_values(_row_heads(mv_ref, 0)), assemble))
    _mix_parts(parts, x1_ref, y_ref, *merge_and_out_refs)


def _merge_weight_specs(weights):
    return [_resident(w.shape) for w in weights]


def _mixer_prompt(q, k2, v2, bias, sink, qm, mk, mv, x1, y, merge_w):
    b, l, _ = q.shape
    tq = MIXER_TILE
    hb = tq // WINDOW
    nt = l // tq
    cur = lambda w: pl.BlockSpec((1, tq, w), lambda bi, i: (bi, i, 0))
    row = lambda w: pl.BlockSpec((tq, w), lambda bi, i: (bi * nt + i, 0))
    halo = pl.BlockSpec((1, WINDOW, KV2_W), lambda bi, i: (bi, jnp.maximum(i * hb - 1, 0), 0))
    mem = pl.BlockSpec((1, MEM_LEN * MEM_HEADS, MEM_HEAD_DIM), lambda bi, i: (bi, 0, 0))
    return pl.pallas_call(
        _mixer_prompt_body,
        grid=(b, nt),
        in_specs=[cur(ATTN_W), cur(KV2_W), halo, cur(KV2_W), halo, _resident(bias.shape),
                  _resident(sink.shape), cur(MEM_W), mem, mem, row(D_MODEL), row(SSM_W)]
        + _merge_weight_specs(merge_w),
        out_specs=row(D_MODEL),
        out_shape=jax.ShapeDtypeStruct((b * l, D_MODEL), F32),
        compiler_params=_params("parallel", "parallel"),
        name="mixer_prompt",
    )(q, k2, k2, v2, v2, bias, sink, qm, mk, mv, x1, y, *merge_w)


def _mixer_sample_body(q_ref, k_ref, v_ref, bias_ref, sink_ref, qm_ref, mk_ref, mv_ref,
                       x1_ref, y_ref, *merge_and_out_refs):
    nb, nq = q_ref.shape[0], q_ref.shape[1]
    per_part = nb // MIXER_PARTS
    parts = []
    for p in range(MIXER_PARTS):
        scores, values, sinks, mem_scores, mem_values = [], [], [], [], []
        for b in range(p * per_part, (p + 1) * per_part):
            kb, vb = _kv_heads(k_ref[b]), _kv_heads(v_ref[b])
            for g, qs in enumerate(_gqa_queries(q_ref[b])):
                scores.append(_qk(qs, kb[g]) + bias_ref[g])
                values.append(vb[g])
                sinks.append(sink_ref[g])
            mem_scores += _mem_scores(qm_ref[b], _row_heads(mk_ref, b))
            mem_values += _mem_values(_row_heads(mv_ref, b))

        def assemble(outs, mem):
            attn = jnp.concatenate([_gqa_outputs(outs[j:j + N_KV_HEADS], nq)
                                    for j in range(0, len(outs), N_KV_HEADS)], axis=0)
            memo = jnp.concatenate([jnp.concatenate(mem[j:j + MEM_HEADS], axis=1)
                                    for j in range(0, len(mem), MEM_HEADS)], axis=0)
            return attn, memo

        parts.append((scores, values, sinks, mem_scores, mem_values, assemble))
    _mix_parts(parts, x1_ref, y_ref, *merge_and_out_refs)


def _mixer_sample(q, kk2, vv2, bias, sink, qm, mk, mv, x1, y, merge_w):
    b, s, _ = q.shape
    nb = SAMPLE_MIXER_TILE // s
    nk = kk2.shape[1]
    blk = lambda *shape: pl.BlockSpec((nb,) + shape, lambda i: (i,) + (0,) * len(shape))
    row = lambda w: pl.BlockSpec((nb * s, w), lambda i: (i, 0))
    return pl.pallas_call(
        _mixer_sample_body,
        grid=(b // nb,),
        in_specs=[blk(s, ATTN_W), blk(nk, KV2_W), blk(nk, KV2_W),
                  _resident(bias.shape), _resident(sink.shape),
                  blk(s, MEM_W), blk(*mk.shape[1:]), blk(*mv.shape[1:]), row(D_MODEL), row(SSM_W)]
        + _merge_weight_specs(merge_w),
        out_specs=row(D_MODEL),
        out_shape=jax.ShapeDtypeStruct((b * s, D_MODEL), F32),
        compiler_params=_params("parallel"),
        name="mixer_sample",
    )(q, kk2, vv2, bias, sink, qm, mk, mv, x1, y, *merge_w)


def _ssm_weights_body(lam_re_row, lam_im_row, lam_re_col, lam_im_col, log_dt, b_re_t, b_im_t,
                      c_re, c_im, c_re_lanes, c_im_lanes, sel_ref,
                      wconv_ref, wst_ref, wot_ref, a_tab_ref):
    wst_ref[0] = jnp.zeros(wst_ref.shape[1:], BF16)
    wot_ref[0] = jnp.zeros(wot_ref.shape[1:], BF16)
    krows = []
    for gl in range(SSM_GB):
        krows.append(_ssm_group_weights(
            gl, *(r[gl] for r in (lam_re_row, lam_im_row, lam_re_col, lam_im_col, log_dt, b_re_t, b_im_t,
                                  c_re, c_im, c_re_lanes, c_im_lanes)),
            wst_ref, wot_ref, a_tab_ref))
    kstack = jnp.concatenate(krows, axis=0).astype(BF16)
    for d in range(SSM_NT):
        for ti in range(SSM_TPT):
            blk = _mm(kstack, sel_ref[d * SSM_TPT + ti])
            r0 = (SSM_NT - 1 - d) * MXU_TILE + ti * LANES
            for gl in range(SSM_GB):
                piece = blk[gl * SSM_GROUP:(gl + 1) * SSM_GROUP]
                piece = (piece if gl == 0 else pltpu.roll(piece, gl * SSM_GROUP, 1)).astype(BF16)
                r = r0 + gl * SSM_GROUP
                wconv_ref[0, r:r + SSM_GROUP, :MXU_TILE] = piece
                if r0 >= MXU_TILE:
                    wconv_ref[0, r - MXU_TILE:r - MXU_TILE + SSM_GROUP, MXU_TILE:] = piece
    wconv_ref[0, (SSM_NT - 1) * MXU_TILE:, MXU_TILE:] = jnp.zeros((MXU_TILE, MXU_TILE), BF16)


def _ssm_group_weights(gl, lam_re_row, lam_im_row, lam_re_col, lam_im_col, log_dt, b_re_t, b_im_t,
                       c_re, c_im, c_re_lanes, c_im_lanes,
                       wst_ref, wot_ref, a_tab_ref):
    dt = jnp.exp(log_dt)
    rows = lambda t: slice(t * LANES + gl * SSM_GROUP, t * LANES + (gl + 1) * SSM_GROUP)
    lanes = slice(gl * SSM_S2, (gl + 1) * SSM_S2)

    def zoh_coef(lr, li):
        mag = jnp.exp(lr * dt)
        a_re, a_im = mag * jnp.cos(li * dt), mag * jnp.sin(li * dt)
        den = lr * lr + li * li
        return ((a_re - 1.0) * lr + a_im * li) / den, (a_im * lr - (a_re - 1.0) * li) / den

    def a_power(lr, li, n):
        mag = jnp.exp(lr * dt * n)
        return mag * jnp.cos(li * dt * n), mag * jnp.sin(li * dt * n)

    lr, li = lam_re_row, lam_im_row
    cr, ci = zoh_coef(lr, li)
    n_rows = jnp.minimum(lax.broadcasted_iota(jnp.int32, (SSM_T + SCAN_ROWS, 1), 0), SSM_T).astype(F32)
    pw_r, pw_i = a_power(lr, li, n_rows)

    def over_channels(pw, exps):
        return jnp.concatenate([jnp.broadcast_to(pw[e:e + 1], (SSM_GROUP, SSM_STATE)) for e in exps], axis=0)

    back = [SSM_T - 1 - t for t in range(SSM_T)]
    pr, pi = over_channels(pw_r, back), over_channels(pw_i, back)
    zr, zi = pr * cr - pi * ci, pr * ci + pi * cr
    over_positions = lambda x: jnp.concatenate([x] * SSM_T, axis=0)
    br, bi = over_positions(b_re_t), over_positions(b_im_t)
    wstate = jnp.concatenate([zr * br - zi * bi, zr * bi + zi * br], axis=1).astype(BF16)

    fwd = [t + 1 for t in range(SSM_T)]
    qr, qi = over_channels(pw_r, fwd), over_channels(pw_i, fwd)
    ccr, cci = over_positions(c_re), over_positions(c_im)
    wout_t = jnp.concatenate([ccr * qr - cci * qi, -(ccr * qi + cci * qr)], axis=1).astype(BF16)
    for t in range(SSM_T):
        wst_ref[0, rows(t), lanes] = wstate[t * SSM_GROUP:(t + 1) * SSM_GROUP]
        wot_ref[0, rows(t), lanes] = wout_t[t * SSM_GROUP:(t + 1) * SSM_GROUP]

    lrc, lic = lam_re_col, lam_im_col
    hi = lax.Precision.HIGHEST
    n_lanes = jnp.minimum(lax.broadcasted_iota(jnp.int32, (1, LANES), 1), SSM_T).astype(F32)
    pc_r, pc_i = a_power(lrc, lic, n_lanes)
    spread = (lax.broadcasted_iota(jnp.int32, (LANES, SSM_K), 0)
              == lax.broadcasted_iota(jnp.int32, (LANES, SSM_K), 1) // SSM_GROUP).astype(F32)
    gr = jnp.dot(pc_r, spread, precision=hi, preferred_element_type=F32)
    gi = jnp.dot(pc_i, spread, precision=hi, preferred_element_type=F32)
    clr, cli = c_re_lanes, c_im_lanes
    g_re, g_im = clr * gr - cli * gi, clr * gi + cli * gr
    btr, bti = b_re_t, b_im_t
    bbr, bbi = cr * btr - ci * bti, cr * bti + ci * btr
    krow = (jnp.dot(bbr, g_re, precision=hi, preferred_element_type=F32)
            - jnp.dot(bbi, g_im, precision=hi, preferred_element_type=F32))

    idx = lax.broadcasted_iota(jnp.int32, (N_SCAN_TABLES * SCAN_ROWS, SSM_STATE), 0)
    tab, r = idx // SCAN_ROWS, idx % SCAN_ROWS
    stride = jnp.where(tab < 2, 1, jnp.where(tab < 4, 2, 4))
    n = jnp.where(tab < 6, stride, jnp.where(tab < 8, r, SCAN_ROWS))
    keep = jnp.logical_or(tab >= 6, r >= stride)
    er, ei = a_power(lr, li, (n * SSM_T).astype(F32))
    er, ei = jnp.where(keep, er, 0.0), jnp.where(keep, ei, 0.0)
    odd = tab % 2 == 1
    tabs = jnp.concatenate([jnp.where(odd, -ei, er), jnp.where(odd, ei, er)], axis=1)
    a_tab_ref[:, :, lanes] = tabs.reshape(N_SCAN_TABLES, SCAN_ROWS, SSM_S2)
    return krow


def _lag_selectors():
    sel = np.zeros((SSM_NT, SSM_TPT, SSM_K, MXU_TILE), np.float32)
    ch = np.arange(SSM_GROUP)
    for d in range(SSM_NT):
        for ti in range(SSM_TPT):
            for to in range(SSM_TPT):
                lag = SSM_TPT * d + to - ti
                if lag >= 0:
                    sel[d, ti, lag * SSM_GROUP + ch, to * LANES + ch] = 1.0
    return jnp.asarray(sel.reshape(SSM_NT * SSM_TPT, SSM_K, MXU_TILE), BF16)


def _ssm_weights(lam_re, lam_im, log_dt, b_re, b_im, c_re, c_im):
    g, p, c = b_re.shape
    row3 = lambda x: x.reshape(g, 1, p)
    col3 = lambda x: x.reshape(g, p, 1)
    t3 = lambda x: jnp.transpose(x, (0, 2, 1))
    c_lanes = lambda x: jnp.tile(t3(x), (1, 1, SSM_T))
    args = (row3(lam_re), row3(lam_im), col3(lam_re), col3(lam_im), log_dt.reshape(g, 1, 1),
            t3(b_re), t3(b_im), c_re, c_im, c_lanes(c_re), c_lanes(c_im))
    sel = _lag_selectors()
    spec = lambda x: pl.BlockSpec((SSM_GB,) + x.shape[1:], lambda i: (i, 0, 0))
    per_gb = lambda *shape: pl.BlockSpec((1,) + shape, lambda i: (i, 0, 0))
    return pl.pallas_call(
        _ssm_weights_body,
        grid=(SSM_NGB,),
        in_specs=[spec(a) for a in args] + [_resident(sel.shape)],
        out_specs=[per_gb(SSM_NT * MXU_TILE, 2 * MXU_TILE), per_gb(SSM_XW, SSM_SW), per_gb(SSM_XW, SSM_SW),
                   pl.BlockSpec((N_SCAN_TABLES, SCAN_ROWS, SSM_SW), lambda i: (0, 0, i))],
        out_shape=[jax.ShapeDtypeStruct((SSM_NGB, SSM_NT * MXU_TILE, 2 * MXU_TILE), BF16),
                   jax.ShapeDtypeStruct((SSM_NGB, SSM_XW, SSM_SW), BF16),
                   jax.ShapeDtypeStruct((SSM_NGB, SSM_XW, SSM_SW), BF16),
                   jax.ShapeDtypeStruct((N_SCAN_TABLES, SCAN_ROWS, SSM_WIDTH), F32)],
        compiler_params=_params("parallel"),
        name="ssm_weights",
    )(*args, sel)


def _swap_halves(s):
    ax = s.ndim - 1
    return jnp.concatenate([pltpu.roll(s[..., l:l + SSM_S2], SSM_STATE, ax)
                            for l in range(0, s.shape[ax], SSM_S2)], axis=ax)


def _ssm_body(row_sets, slab_rows, u_ref, wconv_ref, wst_ref, wot_ref, d_ref, a_tab_ref, s0_ref,
              y_ref, s_ref, v_scr, s_scr):
    def piece(ref_set, first, n, stride, t):
        return ref_set, pl.ds(first + t, n, stride=stride)

    @pl.when(pl.program_id(1) == 0)
    def _():
        s_scr[...] = s0_ref[0]

    x = jnp.concatenate(
        [jnp.concatenate([u_ref[piece(*rs, t)] for t in range(SSM_T)], axis=1) for rs in row_sets],
        axis=0).astype(BF16)
    v_scr[...] = _mm(x, wst_ref[0])
    conv = []
    for jo in range(0, SSM_NT, 2):
        pair = _mm(x[:, :(jo + 2) * MXU_TILE], wconv_ref[0, (SSM_NT - 2 - jo) * MXU_TILE:, :])
        conv += [pair[:, MXU_TILE:], pair[:, :MXU_TILE]]

    n_rows = v_scr.shape[0]
    if slab_rows:
        a_mul, a_swap = a_tab_ref[6, 1:2, :], a_tab_ref[7, 1:2, :]
        s = s_scr[...]
        for c in range(n_rows // slab_rows):
            rows = slice(c * slab_rows, (c + 1) * slab_rows)
            inc = v_scr[rows, :]
            v_scr[rows, :] = s
            s = a_mul * s + a_swap * _swap_halves(s) + inc
        s_scr[...] = s
    else:
        n_blocks = n_rows // SCAN_ROWS
        xs = v_scr[...].reshape(n_blocks, SCAN_ROWS, SSM_SW)
        for k in range(3):
            sh = pltpu.roll(xs, 1 << k, 1)
            xs = xs + a_tab_ref[2 * k] * sh + a_tab_ref[2 * k + 1] * _swap_halves(sh)
        carries, blk = [], 0
        for j, (_, _, n, _) in enumerate(row_sets):
            carry = s_scr[j * SCAN_ROWS:(j + 1) * SCAN_ROWS, :]
            for _ in range(n // SCAN_ROWS):
                carries.append(carry)
                last = jnp.broadcast_to(xs[blk, SCAN_ROWS - 1:, :], carry.shape)
                carry = last + a_tab_ref[8] * carry + a_tab_ref[9] * _swap_halves(carry)
                blk += 1
            s_scr[j * SCAN_ROWS:(j + 1) * SCAN_ROWS, :] = carry
        carries = jnp.stack(carries, axis=0)
        not_first = lax.broadcasted_iota(jnp.int32, (1, SCAN_ROWS, 1), 1) >= 1
        enter = (jnp.where(not_first, pltpu.roll(xs, 1, 1), 0.0)
                 + a_tab_ref[6] * carries + a_tab_ref[7] * _swap_halves(carries))
        v_scr[...] = enter.reshape(n_rows, SSM_SW)
    s_ref[0] = s_scr[...]

    enter = v_scr[...].astype(BF16)
    d = d_ref[...]
    carried = _qk(enter, wot_ref[0])
    for jo in range(SSM_NT):
        yt = conv[jo] + carried[:, jo * MXU_TILE:(jo + 1) * MXU_TILE]
        r0 = 0
        for rs in row_sets:
            for tl in range(SSM_TPT):
                idx = piece(*rs, jo * SSM_TPT + tl)
                y_ref[idx] = yt[r0:r0 + rs[2], tl * LANES:(tl + 1) * LANES] + u_ref[idx] * d
            r0 += rs[2]


def _ssm(u, row_sets, slab_rows, block_rows, wconv, wst, wo, d_lanes, a_tab, s0):
    ns, r, _ = u.shape
    chunk_rows = sum(rs[2] for rs in row_sets)
    carry_rows = s0.shape[1]
    blk = pl.BlockSpec((ns, block_rows, LANES), lambda gb, i: (0, i, gb))
    per_gb = lambda x: pl.BlockSpec((1,) + x.shape[1:], lambda gb, i: (gb, 0, 0))
    return pl.pallas_call(
        functools.partial(_ssm_body, row_sets, slab_rows),
        grid=(SSM_NGB, r // block_rows),
        in_specs=[blk, per_gb(wconv), per_gb(wst), per_gb(wo),
                  pl.BlockSpec((1, LANES), lambda gb, i: (0, gb)),
                  pl.BlockSpec((N_SCAN_TABLES, SCAN_ROWS, SSM_SW), lambda gb, i: (0, 0, gb)),
                  per_gb(s0)],
        out_specs=[blk, per_gb(s0)],
        out_shape=[jax.ShapeDtypeStruct(u.shape, F32), jax.ShapeDtypeStruct(s0.shape, F32)],
        scratch_shapes=[pltpu.VMEM((chunk_rows, SSM_SW), F32), pltpu.VMEM((carry_rows, SSM_SW), F32)],
        compiler_params=_params("parallel", "arbitrary"),
        name="ssm_scan",
    )(u, wconv, wst, wo, d_lanes, a_tab, s0)


def _ssm_branch(u, s0, ssm_w):
    b, l, _ = u.shape
    nc = l // SSM_T
    by_gb = lambda s: s.reshape(s.shape[0], SSM_NGB, SSM_SW).transpose(1, 0, 2)
    if b % SCAN_ROWS == 0:
        row_sets = tuple((0, c * SSM_T, b, l) for c in range(nc))
        y, s_last = _ssm(u.reshape(1, b * l, SSM_W), row_sets, b, b * l, *ssm_w, by_gb(s0))
    else:
        cps = min(nc, SSM_CHUNKS_PER_STEP)
        row_sets = tuple((j, 0, cps, SSM_T) for j in range(b))
        y, s_last = _ssm(u, row_sets, 0, cps * SSM_T, *ssm_w, by_gb(jnp.repeat(s0, SCAN_ROWS, axis=0)))
        s_last = s_last[:, ::SCAN_ROWS]
    s_last = s_last.transpose(1, 0, 2).reshape(b, SSM_GROUPS, 2, SSM_STATE)
    return y.reshape(b, l, SSM_W), s_last[:, :, 0], s_last[:, :, 1]


def _ffn2_body(steps_a, xa_ref, xb_ref, gpre_ref, gpost_ref, wfi_ref, wfo_ref, oa_ref, ob_ref):
    def run(x_ref, o_ref):
        outs = _ffn(_row_parts(x_ref, FFN_PARTS), gpre_ref[...], gpost_ref[...], wfi_ref, wfo_ref)
        rows = x_ref.shape[0] // FFN_PARTS
        for i, o in enumerate(outs):
            o_ref[i * rows:(i + 1) * rows, :] = o

    _on_group(steps_a, run, (xa_ref, oa_ref), (xb_ref, ob_ref))


def _ffn2(xa, xb, gpre, gpost, wfi, wfo):
    steps_a, steps_b = xa.shape[0] // FFN_ROW_TILE, xb.shape[0] // FFN_ROW_TILE
    first, second = _two_group_specs(steps_a)
    vec = _resident((1, D_MODEL))
    return pl.pallas_call(
        functools.partial(_ffn2_body, steps_a),
        grid=(steps_a + steps_b,),
        in_specs=[first(D_MODEL), second(D_MODEL), vec, vec, _resident(wfi.shape), _resident(wfo.shape)],
        out_specs=[first(D_MODEL), second(D_MODEL)],
        out_shape=[jax.ShapeDtypeStruct(xa.shape, F32), jax.ShapeDtypeStruct(xb.shape, F32)],
        compiler_params=_params("arbitrary"),
        name="ffn2",
    )(xa, xb, gpre, gpost, wfi, wfo)


def _t5_bucket(rel):
    half = N_BUCKETS // 2
    max_exact = half // 2
    ret = (rel > 0).astype(np.int32) * half
    n = np.abs(rel)
    large = max_exact + (np.log(np.maximum(n, 1) / max_exact) / math.log(MAX_DISTANCE / max_exact)
                         * (half - max_exact)).astype(np.int32)
    large = np.minimum(large, half - 1)
    return ret + np.where(n < max_exact, n, large)


def _band_bias(rel_table, n_q, n_back, n_k):
    i = np.arange(n_q)[:, None]
    j = np.arange(n_k)[None, :]
    bucket = _t5_bucket((j - n_back) - i).reshape(-1)
    onehot = np.zeros((N_BUCKETS, bucket.size), np.float32)
    onehot[bucket, np.arange(bucket.size)] = 1.0
    b = jnp.dot(rel_table.astype(F32).T, jnp.asarray(onehot), precision=lax.Precision.HIGHEST)
    return b.reshape(N_KV_HEADS, KV_REP * n_q, n_k)


def _sink_rows(sink, n_q):
    return jnp.repeat(sink.astype(F32).reshape(N_KV_HEADS, KV_REP), n_q, axis=1)[:, :, None]


def _twice_per_head(x, axis):
    shape = x.shape
    x = x.reshape(shape[:axis] + (N_KV_HEADS, 1, HEAD_DIM) + shape[axis + 1:])
    x = jnp.concatenate([x, x], axis=axis + 1)
    return x.reshape(shape[:axis] + (KV2_W,) + shape[axis + 1:])


def kernel(x_prompt, x_sample, cache_swa_k, cache_swa_v, cache_mem_k, cache_mem_v, state_ssm_re, state_ssm_im, mem_prompt, rel_bias_table, ff1_pre_g, ff1_post_g, w_ff1_in, w_ff1_out, mix_pre_g, mix_post_g, w_in, mem_norm_g, w_mem_kv, attn_sink, ssm_lambda_re, ssm_lambda_im, ssm_log_dt, ssm_b_re, ssm_b_im, ssm_c_re, ssm_c_im, ssm_d, w_ssm_glu, w_attn_br, w_mem_br, w_out, ff2_pre_g, ff2_post_g, w_ff2_in, w_ff2_out):
    assert ff1_pre_g.shape[0] == 1, "single-layer step"
    bp, lp, _ = x_prompt.shape
    bs, ls, _ = x_sample.shape
    vec = lambda g: g[0].reshape(1, D_MODEL).astype(F32)
    w16 = lambda w: w[0].astype(BF16)

    later = (w_ff2_in, w_ff2_out, w_in, w_ssm_glu, w_attn_br, w_mem_br, w_out, w_mem_kv)
    proj_p, proj_s, (wfi2, wfo2, w_in16, wglu, wab, wmb, wo, wmkv) = _ffn1_proj(
        x_prompt.reshape(bp * lp, D_MODEL), x_sample.reshape(bs * ls, D_MODEL), vec(ff1_pre_g), vec(ff1_post_g),
        w16(w_ff1_in), w16(w_ff1_out), vec(mix_pre_g), w_in[0, :, :PROJ_COLS].astype(BF16), [w[0] for w in later])

    wconv, wst, wot, a_tab = _ssm_weights(
        ssm_lambda_re[0], ssm_lambda_im[0], ssm_log_dt[0], ssm_b_re[0], ssm_b_im[0], ssm_c_re[0], ssm_c_im[0])
    ssm_w = (wconv, wst, wot, ssm_d[0].astype(F32).reshape(1, SSM_W), a_tab)

    mk_p, mv_p = _mem_kv(mem_prompt.reshape(bp * MEM_LEN, D_MODEL), vec(mem_norm_g), wmkv)

    def mix(x_shape, proj, mixer, s0, mem_k, mem_v):
        b, l, _ = x_shape
        r3 = lambda t: t.reshape(b, l, t.shape[-1])
        x1, q, k, v, u, qm, k2, v2 = proj
        y_ssm, s_re, s_im = _ssm_branch(r3(u), s0, ssm_w)
        x2 = mixer(r3(q), r3(k2), r3(v2), r3(qm), mem_k, mem_v, x1, y_ssm.reshape(b * l, SSM_W))
        return x2, r3(k), r3(v), s_re, s_im

    merge_w = (vec(mix_pre_g), w_in16, wglu, wab, wmb, wo, vec(mix_post_g))

    def mixer_prompt(q, k2, v2, qm, mem_k, mem_v, x1, y_ssm):
        bias = _band_bias(rel_bias_table, CHUNK, WINDOW, WINDOW + CHUNK)
        return _mixer_prompt(q, k2, v2, bias, _sink_rows(attn_sink[0], CHUNK), qm, mem_k, mem_v, x1, y_ssm, merge_w)

    def mixer_sample(q, k2, v2, qm, mem_k, mem_v, x1, y_ssm):
        n_back = cache_swa_k.shape[2]
        cache2 = lambda c: _twice_per_head(c[0].reshape(bs, n_back, KV_W).astype(BF16), 2)
        kk = jnp.concatenate([cache2(cache_swa_k), k2], axis=1)
        vv = jnp.concatenate([cache2(cache_swa_v), v2], axis=1)
        bias = _band_bias(rel_bias_table, ls, n_back, n_back + ls)
        return _mixer_sample(q, kk, vv, bias, _sink_rows(attn_sink[0], ls), qm, mem_k, mem_v, x1, y_ssm, merge_w)

    x2p, pk, pv, pre, pim = mix(x_prompt.shape, proj_p, mixer_prompt, jnp.zeros((bp, SSM_WIDTH), F32),
                                mk_p.reshape(bp, MEM_LEN * MEM_HEADS, MEM_HEAD_DIM),
                                mv_p.reshape(bp, MEM_LEN * MEM_HEADS, MEM_HEAD_DIM))
    s0 = jnp.stack([state_ssm_re[0], state_ssm_im[0]], axis=2).reshape(bs, SSM_WIDTH).astype(F32)
    x2s, sk, sv, sre, sim = mix(x_sample.shape, proj_s, mixer_sample, s0,
                                cache_mem_k[0].reshape(bs, MEM_LEN * MEM_HEADS, MEM_HEAD_DIM),
                                cache_mem_v[0].reshape(bs, MEM_LEN * MEM_HEADS, MEM_HEAD_DIM))
    yp, ys = _ffn2(x2p, x2s, vec(ff2_pre_g), vec(ff2_post_g), wfi2, wfo2)
    yp, ys = yp.reshape(x_prompt.shape), ys.reshape(x_sample.shape)

    n_keep = min(WINDOW, lp)
    heads = lambda t: t.reshape(t.shape[0], t.shape[1], N_KV_HEADS, HEAD_DIM)[None]
    mem_heads = lambda t: t.reshape(bp, MEM_LEN, MEM_HEADS, MEM_HEAD_DIM)[None]
    return (yp, ys, heads(pk[:, -n_keep:]), heads(pv[:, -n_keep:]), mem_heads(mk_p), mem_heads(mv_p),
            pre[None], pim[None], heads(sk), heads(sv), sre[None], sim[None])
```

```python
import functools
import math

import numpy as np
import jax
import jax.numpy as jnp
from jax import lax
from jax.experimental import pallas as pl
from jax.experimental.pallas import tpu as pltpu

D_MODEL = 1024
CHUNK = 64
WINDOW = 128
HEAD_DIM = 64
MIX_W = D_MODEL // 2
N_HEADS = MIX_W // HEAD_DIM
N_KV_HEADS = N_HEADS // 4
KV_REP = N_HEADS // N_KV_HEADS
ATTN_W = N_HEADS * HEAD_DIM
KV_W = N_KV_HEADS * HEAD_DIM
SSM_GROUP = 16
SSM_W = MIX_W
SSM_GROUPS = SSM_W // SSM_GROUP
SSM_STATE = 64
MEM_LEN = 256
MEM_HEADS = 4
MEM_HEAD_DIM = MIX_W // MEM_HEADS
MEM_W = MEM_HEADS * MEM_HEAD_DIM
D_FF = 128 * ((8 * D_MODEL // 3 + 127) // 128)
N_BUCKETS = 32
MAX_DISTANCE = 128
RMS_EPS = 1e-6
NEG_INF = -1e30

LANES = 128
BF16_ROWS = 16
MXU_TILE = 256
ROW_TILE = 256
MIXER_TILE = 512
SAMPLE_MIXER_TILE = 256
MIXER_PARTS = 2
FFN_ROW_TILE = 512
FFN_PARTS = 2
FFN_CHUNK = 2 * MXU_TILE
VMEM_LIMIT = 60 * 1024 * 1024

KV2_W = N_KV_HEADS * LANES

SSM_T = 16
SSM_K = SSM_T * SSM_GROUP
SSM_S2 = 2 * SSM_STATE
SSM_WIDTH = SSM_GROUPS * SSM_S2
SSM_GB = LANES // SSM_GROUP
SSM_NGB = SSM_GROUPS // SSM_GB
SSM_XW = SSM_T * LANES
SSM_SW = SSM_GB * SSM_S2
SSM_TPT = MXU_TILE // LANES
SSM_NT = SSM_T // SSM_TPT
SCAN_ROWS = 8
N_SCAN_TABLES = 10
SSM_CHUNKS_PER_STEP = 256

F32 = jnp.float32
BF16 = jnp.bfloat16


def _params(*sem):
    return pltpu.CompilerParams(dimension_semantics=sem, vmem_limit_bytes=VMEM_LIMIT)


def _resident(shape):
    zeros = (0,) * len(shape)
    return pl.BlockSpec(shape, lambda *_: zeros, pipeline_mode=pl.Buffered(1))


def _rms(x, g):
    return x * lax.rsqrt(jnp.mean(x * x, axis=-1, keepdims=True) + RMS_EPS) * g


def _mm(a, b):
    return jnp.dot(a, b, preferred_element_type=F32)


def _sigmoid(x):
    return 0.5 * jnp.tanh(0.5 * x) + 0.5


def _row_parts(ref, n_parts):
    rows = ref.shape[0] // n_parts
    return [ref[i * rows:(i + 1) * rows, :] for i in range(n_parts)]


def _ffn(xs, gpre, gpost, w_in_ref, w_out_ref):
    hs = [_rms(x, gpre).astype(BF16) for x in xs]
    acts = [[] for _ in xs]
    for c0 in range(0, D_FF, FFN_CHUNK):
        c1 = min(c0 + FFN_CHUNK, D_FF)
        for h, a in zip(hs, acts):
            g = _mm(h, w_in_ref[:, c0:c1])
            u = _mm(h, w_in_ref[:, D_FF + c0:D_FF + c1])
            a.append((g * jax.nn.sigmoid(g) * u).astype(BF16))
    outs = [_mm(jnp.concatenate(a, axis=1), w_out_ref[...]) for a in acts]
    return [x + 0.5 * _rms(o, gpost) for x, o in zip(xs, outs)]


PROJ_SPLIT = (("q", ATTN_W, BF16), ("k", KV_W, F32), ("v", KV_W, F32), ("u", SSM_W, F32),
              ("qm", MEM_W, BF16), ("k2", KV2_W, BF16), ("v2", KV2_W, BF16))
N_PROJ_MATMUL = 5
PROJ_COLS = sum(w for _, w, _ in PROJ_SPLIT[:N_PROJ_MATMUL])
Q_SCALE = HEAD_DIM ** -0.5
assert math.frexp(Q_SCALE)[0] == 0.5, "power of two: scaling q before the bf16 rounding and the dot is exact"


def _twice_per_head_lanes(x):
    assert N_KV_HEADS == 2 and KV_W == LANES
    swapped = pltpu.roll(x, HEAD_DIM, 1)
    low = lax.broadcasted_iota(jnp.int32, x.shape, 1) < HEAD_DIM
    return jnp.concatenate([jnp.where(low, x, swapped), jnp.where(low, swapped, x)], axis=1)


def _two_group_specs(steps_a):
    first = lambda w: pl.BlockSpec((FFN_ROW_TILE, w), lambda i: (jnp.minimum(i, steps_a - 1), 0))
    second = lambda w: pl.BlockSpec((FFN_ROW_TILE, w), lambda i: (jnp.maximum(i - steps_a, 0), 0))
    return first, second


def _on_group(steps_a, run, refs_a, refs_b):
    i = pl.program_id(0)

    @pl.when(i < steps_a)
    def _():
        run(*refs_a)

    @pl.when(i >= steps_a)
    def _():
        run(*refs_b)


def _ffn1_proj_body(steps_a, n_later, xa_ref, xb_ref, gpre_ref, gpost_ref, wfi_ref, wfo_ref, gmix_ref, wp_ref,
                    *refs):
    later_f32, out_refs, later_bf16 = refs[:n_later], refs[n_later:len(refs) - n_later], refs[len(refs) - n_later:]
    _cast_chunks(later_f32, later_bf16)

    def run(x_ref, x1_ref, *proj_refs):
        x1s = _ffn(_row_parts(x_ref, FFN_PARTS), gpre_ref[...], gpost_ref[...], wfi_ref, wfo_ref)
        rows = x_ref.shape[0] // FFN_PARTS
        ps = [_mm(_rms(x1, gmix_ref[...]).astype(BF16), wp_ref[...]) for x1 in x1s]
        for i, (x1, p) in enumerate(zip(x1s, ps)):
            part = slice(i * rows, (i + 1) * rows)
            x1_ref[part, :] = x1
            col, cols = 0, {}
            for name, width, _ in PROJ_SPLIT[:N_PROJ_MATMUL]:
                cols[name] = p[:, col:col + width]
                col += width
            cols["q"] = cols["q"] * Q_SCALE
            cols["k2"], cols["v2"] = _twice_per_head_lanes(cols["k"]), _twice_per_head_lanes(cols["v"])
            for ref, (name, _, dtype) in zip(proj_refs, PROJ_SPLIT):
                ref[part, :] = cols[name].astype(dtype)

    n_out = len(out_refs) // 2
    _on_group(steps_a, run, (xa_ref,) + out_refs[:n_out], (xb_ref,) + out_refs[n_out:])


def _row_chunk_spec(w, steps, cols=None):
    rows = w.shape[0]
    n = next(n for n in range(min(steps, rows // BF16_ROWS), 0, -1)
             if rows % n == 0 and (rows // n) % BF16_ROWS == 0)
    return pl.BlockSpec((rows // n, cols or w.shape[1]), lambda i: (jnp.minimum(i, n - 1), 0))


def _cast_chunks(f32_refs, bf16_refs):
    for src, dst in zip(f32_refs, bf16_refs):
        dst[...] = src[...].astype(BF16)


def _ffn1_proj(xa, xb, gpre, gpost, wfi, wfo, gmix, wp, later_weights):
    steps_a, steps_b = xa.shape[0] // FFN_ROW_TILE, xb.shape[0] // FFN_ROW_TILE
    first, second = _two_group_specs(steps_a)
    widths = [D_MODEL] + [w for _, w, _ in PROJ_SPLIT]
    dtypes = [F32] + [d for _, _, d in PROJ_SPLIT]
    chunk_specs = [_row_chunk_spec(w, steps_a + steps_b) for w in later_weights]
    outs = pl.pallas_call(
        functools.partial(_ffn1_proj_body, steps_a, len(later_weights)),
        grid=(steps_a + steps_b,),
        in_specs=[first(D_MODEL), second(D_MODEL), _resident((1, D_MODEL)), _resident((1, D_MODEL)),
                  _resident(wfi.shape), _resident(wfo.shape), _resident((1, D_MODEL)),
                  _resident(wp.shape)] + chunk_specs,
        out_specs=[first(w) for w in widths] + [second(w) for w in widths] + chunk_specs,
        out_shape=[jax.ShapeDtypeStruct((x.shape[0], w), d) for x in (xa, xb) for w, d in zip(widths, dtypes)]
        + [jax.ShapeDtypeStruct(w.shape, BF16) for w in later_weights],
        compiler_params=_params("arbitrary"),
        name="ffn1_proj",
    )(xa, xb, gpre, gpost, wfi, wfo, gmix, wp, *later_weights)
    n = len(widths)
    return outs[:n], outs[n:2 * n], outs[2 * n:]


def _mem_kv_body(m_ref, g_ref, w_ref, k_ref, v_ref):
    kv = _mm(_rms(m_ref[...], g_ref[...]).astype(BF16), w_ref[...])
    n = m_ref.shape[0]
    for h in range(MEM_HEADS):
        k_ref[pl.ds(h, n, stride=MEM_HEADS), :] = kv[:, h * MEM_HEAD_DIM:(h + 1) * MEM_HEAD_DIM]
        v_ref[pl.ds(h, n, stride=MEM_HEADS), :] = kv[:, MEM_W + h * MEM_HEAD_DIM:MEM_W + (h + 1) * MEM_HEAD_DIM]


def _mem_kv(mem, g, w):
    n = mem.shape[0]
    out = pl.BlockSpec((ROW_TILE * MEM_HEADS, MEM_HEAD_DIM), lambda i: (i, 0))
    return pl.pallas_call(
        _mem_kv_body,
        grid=(n // ROW_TILE,),
        in_specs=[pl.BlockSpec((ROW_TILE, D_MODEL), lambda i: (i, 0)), _resident((1, D_MODEL)), _resident(w.shape)],
        out_specs=[out, out],
        out_shape=[jax.ShapeDtypeStruct((n * MEM_HEADS, MEM_HEAD_DIM), F32)] * 2,
        compiler_params=_params("parallel"),
        name="mem_kv",
    )(mem, g, w)


def _qk(q, k):
    return lax.dot_general(q, k, (((1,), (1,)), ((), ())), preferred_element_type=F32)


def _softmax_pv(scores, values, sinks=None):
    probs = []
    for i, s in enumerate(scores):
        m = jnp.max(s, axis=-1, keepdims=True)
        if sinks is not None:
            m = jnp.maximum(m, sinks[i])
        e = jnp.exp(s - m)
        den = jnp.sum(e, axis=-1, keepdims=True)
        if sinks is not None:
            den = den + jnp.exp(sinks[i] - m)
        probs.append((e * (1.0 / den)).astype(BF16))
    return [_mm(p, v) for p, v in zip(probs, values)]


def _gqa_queries(q):
    nq = q.shape[0]
    low = lax.broadcasted_iota(jnp.int32, (nq, LANES), 1) < HEAD_DIM
    zero = jnp.zeros((nq, LANES), BF16)
    stacks = []
    for g in range(N_KV_HEADS):
        rows = []
        for r in range(KV_REP):
            h = g * KV_REP + r
            q2 = q[:, (h // 2) * LANES:(h // 2 + 1) * LANES]
            rows.append(jnp.where(low, q2, zero) if h % 2 == 0 else jnp.where(low, zero, q2))
        stacks.append(jnp.concatenate(rows, axis=0))
    return stacks


def _gqa_outputs(outs, nq):
    low = lax.broadcasted_iota(jnp.int32, (nq, LANES), 1) < HEAD_DIM
    pairs = []
    for o in outs:
        for r in range(0, KV_REP, 2):
            pairs.append(jnp.where(low, o[r * nq:(r + 1) * nq], o[(r + 1) * nq:(r + 2) * nq]))
    return jnp.concatenate(pairs, axis=1)


def _mem_scores(qm, k_head):
    return [_qk(qm[:, h * MEM_HEAD_DIM:(h + 1) * MEM_HEAD_DIM], k_head(h).astype(BF16)) * (MEM_HEAD_DIM ** -0.5)
            for h in range(MEM_HEADS)]


def _mem_values(v_head):
    return [v_head(h).astype(BF16) for h in range(MEM_HEADS)]


def _row_heads(ref, i):
    return lambda h: ref[i, pl.ds(h, MEM_LEN, stride=MEM_HEADS), :]


def _kv_heads(x):
    return [x[:, g * LANES:(g + 1) * LANES] for g in range(N_KV_HEADS)]


def _merge_pre(x1, y, gmix_ref, wg_ref, wglu_ref):
    h = _rms(x1, gmix_ref[...]).astype(BF16)
    logits = [_mm(h, wg_ref[:, PROJ_COLS + j * D_MODEL:PROJ_COLS + (j + 1) * D_MODEL]) for j in range(3)]
    y = y.astype(BF16)
    return logits, _mm(y, wglu_ref[:, :D_MODEL]), _mm(y, wglu_ref[:, D_MODEL:])


def _mix_parts(parts, x1_ref, y_ref, gmix_ref, wg_ref, wglu_ref, wab_ref, wmb_ref, wo_ref, gpost_ref, x2_ref):
    x1s, ys = _row_parts(x1_ref, len(parts)), _row_parts(y_ref, len(parts))
    pres = [_merge_pre(x1, y, gmix_ref, wg_ref, wglu_ref) for x1, y in zip(x1s, ys)]
    branches = [assemble(_softmax_pv(scores, values, sinks), _softmax_pv(mem_scores, mem_values))
                for scores, values, sinks, mem_scores, mem_values, assemble in parts]
    projected = [(_mm(attn.astype(BF16), wab_ref[...]), _mm(memo.astype(BF16), wmb_ref[...]))
                 for attn, memo in branches]
    sig = _sigmoid
    merged = [sig(lg[0]) * pa + sig(lg[1]) * (ya * sig(yb)) + sig(lg[2]) * pm
              for (lg, ya, yb), (pa, pm) in zip(pres, projected)]
    outs = [_mm(m.astype(BF16), wo_ref[...]) for m in merged]
    rows = x1_ref.shape[0] // len(parts)
    for i, (x1, o) in enumerate(zip(x1s, outs)):
        x2_ref[i * rows:(i + 1) * rows, :] = x1 + _rms(o, gpost_ref[...])


def _mixer_prompt_body(q_ref, k_ref, kh_ref, v_ref, vh_ref, bias_ref, sink_ref, qm_ref, mk_ref, mv_ref,
                       x1_ref, y_ref, *merge_and_out_refs):
    i = pl.program_id(1)
    tq = q_ref.shape[1]
    kf = jnp.concatenate([kh_ref[0], k_ref[0]], axis=0)
    vf = jnp.concatenate([vh_ref[0], v_ref[0]], axis=0)
    band = WINDOW + CHUNK
    rows = tq // MIXER_PARTS
    parts = []
    for p in range(MIXER_PARTS):
        scores, values, sinks = [], [], []
        for lo in range(p * rows, (p + 1) * rows, CHUNK):
            valid = None
            if lo < WINDOW:
                key_pos = lax.broadcasted_iota(jnp.int32, (1, band), 1) + (i * tq + lo - WINDOW)
                valid = key_pos >= 0
            kb, vb = _kv_heads(kf[lo:lo + band]), _kv_heads(vf[lo:lo + band])
            for g, qs in enumerate(_gqa_queries(q_ref[0, lo:lo + CHUNK, :])):
                s = _qk(qs, kb[g]) + bias_ref[g]
                scores.append(s if valid is None else jnp.where(valid, s, NEG_INF))
                values.append(vb[g])
                sinks.append(sink_ref[g])
        mem_scores = _mem_scores(qm_ref[0, p * rows:(p + 1) * rows, :], _row_heads(mk_ref, 0))

        def assemble(outs, mem):
            attn = jnp.concatenate([_gqa_outputs(outs[c:c + N_KV_HEADS], CHUNK)
                                    for c in range(0, len(outs), N_KV_HEADS)], axis=0)
            return attn, jnp.concatenate(mem, axis=1)

        parts.append((scores, values, sinks, mem_scores, _mem_values(_row_heads(mv_ref, 0)), assemble))
    _mix_parts(parts, x1_ref, y_ref, *merge_and_out_refs)


def _merge_weight_specs(weights):
    return [_resident(w.shape) for w in weights]


def _mixer_prompt(q, k2, v2, bias, sink, qm, mk, mv, x1, y, merge_w):
    b, l, _ = q.shape
    tq = MIXER_TILE
    hb = tq // WINDOW
    nt = l // tq
    cur = lambda w: pl.BlockSpec((1, tq, w), lambda bi, i: (bi, i, 0))
    row = lambda w: pl.BlockSpec((tq, w), lambda bi, i: (bi * nt + i, 0))
    halo = pl.BlockSpec((1, WINDOW, KV2_W), lambda bi, i: (bi, jnp.maximum(i * hb - 1, 0), 0))
    mem = pl.BlockSpec((1, MEM_LEN * MEM_HEADS, MEM_HEAD_DIM), lambda bi, i: (bi, 0, 0))
    return pl.pallas_call(
        _mixer_prompt_body,
        grid=(b, nt),
        in_specs=[cur(ATTN_W), cur(KV2_W), halo, cur(KV2_W), halo, _resident(bias.shape),
                  _resident(sink.shape), cur(MEM_W), mem, mem, row(D_MODEL), row(SSM_W)]
        + _merge_weight_specs(merge_w),
        out_specs=row(D_MODEL),
        out_shape=jax.ShapeDtypeStruct((b * l, D_MODEL), F32),
        compiler_params=_params("parallel", "parallel"),
        name="mixer_prompt",
    )(q, k2, k2, v2, v2, bias, sink, qm, mk, mv, x1, y, *merge_w)


def _mixer_sample_body(q_ref, k_ref, v_ref, bias_ref, sink_ref, qm_ref, mk_ref, mv_ref,
                       x1_ref, y_ref, *merge_and_out_refs):
    nb, nq = q_ref.shape[0], q_ref.shape[1]
    per_part = nb // MIXER_PARTS
    parts = []
    for p in range(MIXER_PARTS):
        scores, values, sinks, mem_scores, mem_values = [], [], [], [], []
        for b in range(p * per_part, (p + 1) * per_part):
            kb, vb = _kv_heads(k_ref[b]), _kv_heads(v_ref[b])
            for g, qs in enumerate(_gqa_queries(q_ref[b])):
                scores.append(_qk(qs, kb[g]) + bias_ref[g])
                values.append(vb[g])
                sinks.append(sink_ref[g])
            mem_scores += _mem_scores(qm_ref[b], _row_heads(mk_ref, b))
            mem_values += _mem_values(_row_heads(mv_ref, b))

        def assemble(outs, mem):
            attn = jnp.concatenate([_gqa_outputs(outs[j:j + N_KV_HEADS], nq)
                                    for j in range(0, len(outs), N_KV_HEADS)], axis=0)
            memo = jnp.concatenate([jnp.concatenate(mem[j:j + MEM_HEADS], axis=1)
                                    for j in range(0, len(mem), MEM_HEADS)], axis=0)
            return attn, memo

        parts.append((scores, values, sinks, mem_scores, mem_values, assemble))
    _mix_parts(parts, x1_ref, y_ref, *merge_and_out_refs)


def _mixer_sample(q, kk2, vv2, bias, sink, qm, mk, mv, x1, y, merge_w):
    b, s, _ = q.shape
    nb = SAMPLE_MIXER_TILE // s
    nk = kk2.shape[1]
    blk = lambda *shape: pl.BlockSpec((nb,) + shape, lambda i: (i,) + (0,) * len(shape))
    row = lambda w: pl.BlockSpec((nb * s, w), lambda i: (i, 0))
    return pl.pallas_call(
        _mixer_sample_body,
        grid=(b // nb,),
        in_specs=[blk(s, ATTN_W), blk(nk, KV2_W), blk(nk, KV2_W),
                  _resident(bias.shape), _resident(sink.shape),
                  blk(s, MEM_W), blk(*mk.shape[1:]), blk(*mv.shape[1:]), row(D_MODEL), row(SSM_W)]
        + _merge_weight_specs(merge_w),
        out_specs=row(D_MODEL),
        out_shape=jax.ShapeDtypeStruct((b * s, D_MODEL), F32),
        compiler_params=_params("parallel"),
        name="mixer_sample",
    )(q, kk2, vv2, bias, sink, qm, mk, mv, x1, y, *merge_w)


def _ssm_weights_body(n_first, lam_re_row, lam_im_row, lam_re_col, lam_im_col, log_dt, b_re_t, b_im_t,
                      c_re, c_im, c_re_lanes, c_im_lanes, sel_ref, *refs):
    first_f32, (wconv_ref, wst_ref, wot_ref, a_tab_ref), first_bf16 = (
        refs[:n_first], refs[n_first:n_first + 4], refs[n_first + 4:])
    _cast_chunks(first_f32, first_bf16)
    _ssm_block_weights(lam_re_row, lam_im_row, lam_re_col, lam_im_col, log_dt, b_re_t, b_im_t,
                       c_re, c_im, c_re_lanes, c_im_lanes, sel_ref, wconv_ref, wst_ref, wot_ref, a_tab_ref)


def _ssm_block_weights(lam_re_row, lam_im_row, lam_re_col, lam_im_col, log_dt, b_re_t, b_im_t,
                       c_re, c_im, c_re_lanes, c_im_lanes, sel_ref,
                       wconv_ref, wst_ref, wot_ref, a_tab_ref):
    wst_ref[0] = jnp.zeros(wst_ref.shape[1:], BF16)
    wot_ref[0] = jnp.zeros(wot_ref.shape[1:], BF16)
    krows = []
    for gl in range(SSM_GB):
        krows.append(_ssm_group_weights(
            gl, *(r[gl] for r in (lam_re_row, lam_im_row, lam_re_col, lam_im_col, log_dt, b_re_t, b_im_t,
                                  c_re, c_im, c_re_lanes, c_im_lanes)),
            wst_ref, wot_ref, a_tab_ref))
    kstack = jnp.concatenate(krows, axis=0).astype(BF16)
    for d in range(SSM_NT):
        for ti in range(SSM_TPT):
            blk = _mm(kstack, sel_ref[d * SSM_TPT + ti])
            r0 = (SSM_NT - 1 - d) * MXU_TILE + ti * LANES
            for gl in range(SSM_GB):
                piece = blk[gl * SSM_GROUP:(gl + 1) * SSM_GROUP]
                piece = (piece if gl == 0 else pltpu.roll(piece, gl * SSM_GROUP, 1)).astype(BF16)
                r = r0 + gl * SSM_GROUP
                wconv_ref[0, r:r + SSM_GROUP, :MXU_TILE] = piece
                if r0 >= MXU_TILE:
                    wconv_ref[0, r - MXU_TILE:r - MXU_TILE + SSM_GROUP, MXU_TILE:] = piece
    wconv_ref[0, (SSM_NT - 1) * MXU_TILE:, MXU_TILE:] = jnp.zeros((MXU_TILE, MXU_TILE), BF16)


def _ssm_group_weights(gl, lam_re_row, lam_im_row, lam_re_col, lam_im_col, log_dt, b_re_t, b_im_t,
                       c_re, c_im, c_re_lanes, c_im_lanes,
                       wst_ref, wot_ref, a_tab_ref):
    dt = jnp.exp(log_dt)
    rows = lambda t: slice(t * LANES + gl * SSM_GROUP, t * LANES + (gl + 1) * SSM_GROUP)
    lanes = slice(gl * SSM_S2, (gl + 1) * SSM_S2)

    def zoh_coef(lr, li):
        mag = jnp.exp(lr * dt)
        a_re, a_im = mag * jnp.cos(li * dt), mag * jnp.sin(li * dt)
        den = lr * lr + li * li
        return ((a_re - 1.0) * lr + a_im * li) / den, (a_im * lr - (a_re - 1.0) * li) / den

    def a_power(lr, li, n):
        mag = jnp.exp(lr * dt * n)
        return mag * jnp.cos(li * dt * n), mag * jnp.sin(li * dt * n)

    lr, li = lam_re_row, lam_im_row
    cr, ci = zoh_coef(lr, li)
    n_rows = jnp.minimum(lax.broadcasted_iota(jnp.int32, (SSM_T + SCAN_ROWS, 1), 0), SSM_T).astype(F32)
    pw_r, pw_i = a_power(lr, li, n_rows)

    def over_channels(pw, exps):
        return jnp.concatenate([jnp.broadcast_to(pw[e:e + 1], (SSM_GROUP, SSM_STATE)) for e in exps], axis=0)

    back = [SSM_T - 1 - t for t in range(SSM_T)]
    pr, pi = over_channels(pw_r, back), over_channels(pw_i, back)
    zr, zi = pr * cr - pi * ci, pr * ci + pi * cr
    over_positions = lambda x: jnp.concatenate([x] * SSM_T, axis=0)
    br, bi = over_positions(b_re_t), over_positions(b_im_t)
    wstate = jnp.concatenate([zr * br - zi * bi, zr * bi + zi * br], axis=1).astype(BF16)

    fwd = [t + 1 for t in range(SSM_T)]
    qr, qi = over_channels(pw_r, fwd), over_channels(pw_i, fwd)
    ccr, cci = over_positions(c_re), over_positions(c_im)
    wout_t = jnp.concatenate([ccr * qr - cci * qi, -(ccr * qi + cci * qr)], axis=1).astype(BF16)
    for t in range(SSM_T):
        wst_ref[0, rows(t), lanes] = wstate[t * SSM_GROUP:(t + 1) * SSM_GROUP]
        wot_ref[0, rows(t), lanes] = wout_t[t * SSM_GROUP:(t + 1) * SSM_GROUP]

    lrc, lic = lam_re_col, lam_im_col
    hi = lax.Precision.HIGHEST
    n_lanes = jnp.minimum(lax.broadcasted_iota(jnp.int32, (1, LANES), 1), SSM_T).astype(F32)
    pc_r, pc_i = a_power(lrc, lic, n_lanes)
    spread = (lax.broadcasted_iota(jnp.int32, (LANES, SSM_K), 0)
              == lax.broadcasted_iota(jnp.int32, (LANES, SSM_K), 1) // SSM_GROUP).astype(F32)
    gr = jnp.dot(pc_r, spread, precision=hi, preferred_element_type=F32)
    gi = jnp.dot(pc_i, spread, precision=hi, preferred_element_type=F32)
    clr, cli = c_re_lanes, c_im_lanes
    g_re, g_im = clr * gr - cli * gi, clr * gi + cli * gr
    btr, bti = b_re_t, b_im_t
    bbr, bbi = cr * btr - ci * bti, cr * bti + ci * btr
    krow = (jnp.dot(bbr, g_re, precision=hi, preferred_element_type=F32)
            - jnp.dot(bbi, g_im, precision=hi, preferred_element_type=F32))

    idx = lax.broadcasted_iota(jnp.int32, (N_SCAN_TABLES * SCAN_ROWS, SSM_STATE), 0)
    tab, r = idx // SCAN_ROWS, idx % SCAN_ROWS
    stride = jnp.where(tab < 2, 1, jnp.where(tab < 4, 2, 4))
    n = jnp.where(tab < 6, stride, jnp.where(tab < 8, r, SCAN_ROWS))
    keep = jnp.logical_or(tab >= 6, r >= stride)
    er, ei = a_power(lr, li, (n * SSM_T).astype(F32))
    er, ei = jnp.where(keep, er, 0.0), jnp.where(keep, ei, 0.0)
    odd = tab % 2 == 1
    tabs = jnp.concatenate([jnp.where(odd, -ei, er), jnp.where(odd, ei, er)], axis=1)
    a_tab_ref[:, :, lanes] = tabs.reshape(N_SCAN_TABLES, SCAN_ROWS, SSM_S2)
    return krow


def _lag_selectors():
    sel = np.zeros((SSM_NT, SSM_TPT, SSM_K, MXU_TILE), np.float32)
    ch = np.arange(SSM_GROUP)
    for d in range(SSM_NT):
        for ti in range(SSM_TPT):
            for to in range(SSM_TPT):
                lag = SSM_TPT * d + to - ti
                if lag >= 0:
                    sel[d, ti, lag * SSM_GROUP + ch, to * LANES + ch] = 1.0
    return jnp.asarray(sel.reshape(SSM_NT * SSM_TPT, SSM_K, MXU_TILE), BF16)


def _ssm_weights(lam_re, lam_im, log_dt, b_re, b_im, c_re, c_im, first_weights):
    g, p, c = b_re.shape
    row3 = lambda x: x.reshape(g, 1, p)
    col3 = lambda x: x.reshape(g, p, 1)
    t3 = lambda x: jnp.transpose(x, (0, 2, 1))
    c_lanes = lambda x: jnp.tile(t3(x), (1, 1, SSM_T))
    args = (row3(lam_re), row3(lam_im), col3(lam_re), col3(lam_im), log_dt.reshape(g, 1, 1),
            t3(b_re), t3(b_im), c_re, c_im, c_lanes(c_re), c_lanes(c_im))
    sel = _lag_selectors()
    spec = lambda x: pl.BlockSpec((SSM_GB,) + x.shape[1:], lambda i: (i, 0, 0))
    per_gb = lambda *shape: pl.BlockSpec((1,) + shape, lambda i: (i, 0, 0))
    chunk_specs = [_row_chunk_spec(w, SSM_NGB, cols) for w, cols in first_weights]
    outs = pl.pallas_call(
        functools.partial(_ssm_weights_body, len(first_weights)),
        grid=(SSM_NGB,),
        in_specs=[spec(a) for a in args] + [_resident(sel.shape)] + chunk_specs,
        out_specs=[per_gb(SSM_NT * MXU_TILE, 2 * MXU_TILE), per_gb(SSM_XW, SSM_SW), per_gb(SSM_XW, SSM_SW),
                   pl.BlockSpec((N_SCAN_TABLES, SCAN_ROWS, SSM_SW), lambda i: (0, 0, i))] + chunk_specs,
        out_shape=[jax.ShapeDtypeStruct((SSM_NGB, SSM_NT * MXU_TILE, 2 * MXU_TILE), BF16),
                   jax.ShapeDtypeStruct((SSM_NGB, SSM_XW, SSM_SW), BF16),
                   jax.ShapeDtypeStruct((SSM_NGB, SSM_XW, SSM_SW), BF16),
                   jax.ShapeDtypeStruct((N_SCAN_TABLES, SCAN_ROWS, SSM_WIDTH), F32)]
        + [jax.ShapeDtypeStruct((w.shape[0], cols or w.shape[1]), BF16) for w, cols in first_weights],
        compiler_params=_params("arbitrary"),
        name="ssm_weights",
    )(*args, sel, *[w for w, _ in first_weights])
    return outs[:4], outs[4:]


def _swap_halves(s):
    ax = s.ndim - 1
    return jnp.concatenate([pltpu.roll(s[..., l:l + SSM_S2], SSM_STATE, ax)
                            for l in range(0, s.shape[ax], SSM_S2)], axis=ax)


def _ssm_body(row_sets, slab_rows, u_ref, wconv_ref, wst_ref, wot_ref, d_ref, a_tab_ref, s0_ref,
              y_ref, s_ref, v_scr, s_scr):
    def piece(ref_set, first, n, stride, t):
        return ref_set, pl.ds(first + t, n, stride=stride)

    @pl.when(pl.program_id(1) == 0)
    def _():
        s_scr[...] = s0_ref[0]

    x = jnp.concatenate(
        [jnp.concatenate([u_ref[piece(*rs, t)] for t in range(SSM_T)], axis=1) for rs in row_sets],
        axis=0).astype(BF16)
    v_scr[...] = _mm(x, wst_ref[0])
    conv = []
    for jo in range(0, SSM_NT, 2):
        pair = _mm(x[:, :(jo + 2) * MXU_TILE], wconv_ref[0, (SSM_NT - 2 - jo) * MXU_TILE:, :])
        conv += [pair[:, MXU_TILE:], pair[:, :MXU_TILE]]

    n_rows = v_scr.shape[0]
    if slab_rows:
        a_mul, a_swap = a_tab_ref[6, 1:2, :], a_tab_ref[7, 1:2, :]
        s = s_scr[...]
        for c in range(n_rows // slab_rows):
            rows = slice(c * slab_rows, (c + 1) * slab_rows)
            inc = v_scr[rows, :]
            v_scr[rows, :] = s
            s = a_mul * s + a_swap * _swap_halves(s) + inc
        s_scr[...] = s
    else:
        n_blocks = n_rows // SCAN_ROWS
        xs = v_scr[...].reshape(n_blocks, SCAN_ROWS, SSM_SW)
        for k in range(3):
            sh = pltpu.roll(xs, 1 << k, 1)
            xs = xs + a_tab_ref[2 * k] * sh + a_tab_ref[2 * k + 1] * _swap_halves(sh)
        carries, blk = [], 0
        for j, (_, _, n, _) in enumerate(row_sets):
            carry = s_scr[j * SCAN_ROWS:(j + 1) * SCAN_ROWS, :]
            for _ in range(n // SCAN_ROWS):
                carries.append(carry)
                last = jnp.broadcast_to(xs[blk, SCAN_ROWS - 1:, :], carry.shape)
                carry = last + a_tab_ref[8] * carry + a_tab_ref[9] * _swap_halves(carry)
                blk += 1
            s_scr[j * SCAN_ROWS:(j + 1) * SCAN_ROWS, :] = carry
        carries = jnp.stack(carries, axis=0)
        not_first = lax.broadcasted_iota(jnp.int32, (1, SCAN_ROWS, 1), 1) >= 1
        enter = (jnp.where(not_first, pltpu.roll(xs, 1, 1), 0.0)
                 + a_tab_ref[6] * carries + a_tab_ref[7] * _swap_halves(carries))
        v_scr[...] = enter.reshape(n_rows, SSM_SW)
    s_ref[0] = s_scr[...]

    enter = v_scr[...].astype(BF16)
    d = d_ref[...]
    carried = _qk(enter, wot_ref[0])
    for jo in range(SSM_NT):
        yt = conv[jo] + carried[:, jo * MXU_TILE:(jo + 1) * MXU_TILE]
        r0 = 0
        for rs in row_sets:
            for tl in range(SSM_TPT):
                idx = piece(*rs, jo * SSM_TPT + tl)
                y_ref[idx] = yt[r0:r0 + rs[2], tl * LANES:(tl + 1) * LANES] + u_ref[idx] * d
            r0 += rs[2]


def _ssm(u, row_sets, slab_rows, block_rows, wconv, wst, wo, d_lanes, a_tab, s0):
    ns, r, _ = u.shape
    chunk_rows = sum(rs[2] for rs in row_sets)
    carry_rows = s0.shape[1]
    blk = pl.BlockSpec((ns, block_rows, LANES), lambda gb, i: (0, i, gb))
    per_gb = lambda x: pl.BlockSpec((1,) + x.shape[1:], lambda gb, i: (gb, 0, 0))
    return pl.pallas_call(
        functools.partial(_ssm_body, row_sets, slab_rows),
        grid=(SSM_NGB, r // block_rows),
        in_specs=[blk, per_gb(wconv), per_gb(wst), per_gb(wo),
                  pl.BlockSpec((1, LANES), lambda gb, i: (0, gb)),
                  pl.BlockSpec((N_SCAN_TABLES, SCAN_ROWS, SSM_SW), lambda gb, i: (0, 0, gb)),
                  per_gb(s0)],
        out_specs=[blk, per_gb(s0)],
        out_shape=[jax.ShapeDtypeStruct(u.shape, F32), jax.ShapeDtypeStruct(s0.shape, F32)],
        scratch_shapes=[pltpu.VMEM((chunk_rows, SSM_SW), F32), pltpu.VMEM((carry_rows, SSM_SW), F32)],
        compiler_params=_params("parallel", "arbitrary"),
        name="ssm_scan",
    )(u, wconv, wst, wo, d_lanes, a_tab, s0)


def _ssm_branch(u, s0, ssm_w):
    b, l, _ = u.shape
    nc = l // SSM_T
    by_gb = lambda s: s.reshape(s.shape[0], SSM_NGB, SSM_SW).transpose(1, 0, 2)
    if b % SCAN_ROWS == 0:
        row_sets = tuple((0, c * SSM_T, b, l) for c in range(nc))
        y, s_last = _ssm(u.reshape(1, b * l, SSM_W), row_sets, b, b * l, *ssm_w, by_gb(s0))
    else:
        cps = min(nc, SSM_CHUNKS_PER_STEP)
        row_sets = tuple((j, 0, cps, SSM_T) for j in range(b))
        y, s_last = _ssm(u, row_sets, 0, cps * SSM_T, *ssm_w, by_gb(jnp.repeat(s0, SCAN_ROWS, axis=0)))
        s_last = s_last[:, ::SCAN_ROWS]
    s_last = s_last.transpose(1, 0, 2).reshape(b, SSM_GROUPS, 2, SSM_STATE)
    return y.reshape(b, l, SSM_W), s_last[:, :, 0], s_last[:, :, 1]


def _ffn2_body(steps_a, xa_ref, xb_ref, gpre_ref, gpost_ref, wfi_ref, wfo_ref, oa_ref, ob_ref):
    def run(x_ref, o_ref):
        outs = _ffn(_row_parts(x_ref, FFN_PARTS), gpre_ref[...], gpost_ref[...], wfi_ref, wfo_ref)
        rows = x_ref.shape[0] // FFN_PARTS
        for i, o in enumerate(outs):
            o_ref[i * rows:(i + 1) * rows, :] = o

    _on_group(steps_a, run, (xa_ref, oa_ref), (xb_ref, ob_ref))


def _ffn2(xa, xb, gpre, gpost, wfi, wfo):
    steps_a, steps_b = xa.shape[0] // FFN_ROW_TILE, xb.shape[0] // FFN_ROW_TILE
    first, second = _two_group_specs(steps_a)
    vec = _resident((1, D_MODEL))
    return pl.pallas_call(
        functools.partial(_ffn2_body, steps_a),
        grid=(steps_a + steps_b,),
        in_specs=[first(D_MODEL), second(D_MODEL), vec, vec, _resident(wfi.shape), _resident(wfo.shape)],
        out_specs=[first(D_MODEL), second(D_MODEL)],
        out_shape=[jax.ShapeDtypeStruct(xa.shape, F32), jax.ShapeDtypeStruct(xb.shape, F32)],
        compiler_params=_params("arbitrary"),
        name="ffn2",
    )(xa, xb, gpre, gpost, wfi, wfo)


def _t5_bucket(rel):
    half = N_BUCKETS // 2
    max_exact = half // 2
    ret = (rel > 0).astype(np.int32) * half
    n = np.abs(rel)
    large = max_exact + (np.log(np.maximum(n, 1) / max_exact) / math.log(MAX_DISTANCE / max_exact)
                         * (half - max_exact)).astype(np.int32)
    large = np.minimum(large, half - 1)
    return ret + np.where(n < max_exact, n, large)


def _band_bias(rel_table, n_q, n_back, n_k):
    i = np.arange(n_q)[:, None]
    j = np.arange(n_k)[None, :]
    bucket = _t5_bucket((j - n_back) - i).reshape(-1)
    onehot = np.zeros((N_BUCKETS, bucket.size), np.float32)
    onehot[bucket, np.arange(bucket.size)] = 1.0
    b = jnp.dot(rel_table.astype(F32).T, jnp.asarray(onehot), precision=lax.Precision.HIGHEST)
    return b.reshape(N_KV_HEADS, KV_REP * n_q, n_k)


def _sink_rows(sink, n_q):
    return jnp.repeat(sink.astype(F32).reshape(N_KV_HEADS, KV_REP), n_q, axis=1)[:, :, None]


def _twice_per_head(x, axis):
    shape = x.shape
    x = x.reshape(shape[:axis] + (N_KV_HEADS, 1, HEAD_DIM) + shape[axis + 1:])
    x = jnp.concatenate([x, x], axis=axis + 1)
    return x.reshape(shape[:axis] + (KV2_W,) + shape[axis + 1:])


def kernel(x_prompt, x_sample, cache_swa_k, cache_swa_v, cache_mem_k, cache_mem_v, state_ssm_re, state_ssm_im, mem_prompt, rel_bias_table, ff1_pre_g, ff1_post_g, w_ff1_in, w_ff1_out, mix_pre_g, mix_post_g, w_in, mem_norm_g, w_mem_kv, attn_sink, ssm_lambda_re, ssm_lambda_im, ssm_log_dt, ssm_b_re, ssm_b_im, ssm_c_re, ssm_c_im, ssm_d, w_ssm_glu, w_attn_br, w_mem_br, w_out, ff2_pre_g, ff2_post_g, w_ff2_in, w_ff2_out):
    assert ff1_pre_g.shape[0] == 1, "single-layer step"
    bp, lp, _ = x_prompt.shape
    bs, ls, _ = x_sample.shape
    vec = lambda g: g[0].reshape(1, D_MODEL).astype(F32)

    (wconv, wst, wot, a_tab), (wfi1, wfo1, wp) = _ssm_weights(
        ssm_lambda_re[0], ssm_lambda_im[0], ssm_log_dt[0], ssm_b_re[0], ssm_b_im[0], ssm_c_re[0], ssm_c_im[0],
        [(w_ff1_in[0], None), (w_ff1_out[0], None), (w_in[0], PROJ_COLS)])
    ssm_w = (wconv, wst, wot, ssm_d[0].astype(F32).reshape(1, SSM_W), a_tab)

    later = (w_ff2_in, w_ff2_out, w_in, w_ssm_glu, w_attn_br, w_mem_br, w_out, w_mem_kv)
    proj_p, proj_s, (wfi2, wfo2, w_in16, wglu, wab, wmb, wo, wmkv) = _ffn1_proj(
        x_prompt.reshape(bp * lp, D_MODEL), x_sample.reshape(bs * ls, D_MODEL), vec(ff1_pre_g), vec(ff1_post_g),
        wfi1, wfo1, vec(mix_pre_g), wp, [w[0] for w in later])

    mk_p, mv_p = _mem_kv(mem_prompt.reshape(bp * MEM_LEN, D_MODEL), vec(mem_norm_g), wmkv)

    def mix(x_shape, proj, mixer, s0, mem_k, mem_v):
        b, l, _ = x_shape
        r3 = lambda t: t.reshape(b, l, t.shape[-1])
        x1, q, k, v, u, qm, k2, v2 = proj
        y_ssm, s_re, s_im = _ssm_branch(r3(u), s0, ssm_w)
        x2 = mixer(r3(q), r3(k2), r3(v2), r3(qm), mem_k, mem_v, x1, y_ssm.reshape(b * l, SSM_W))
        return x2, r3(k), r3(v), s_re, s_im

    merge_w = (vec(mix_pre_g), w_in16, wglu, wab, wmb, wo, vec(mix_post_g))

    def mixer_prompt(q, k2, v2, qm, mem_k, mem_v, x1, y_ssm):
        bias = _band_bias(rel_bias_table, CHUNK, WINDOW, WINDOW + CHUNK)
        return _mixer_prompt(q, k2, v2, bias, _sink_rows(attn_sink[0], CHUNK), qm, mem_k, mem_v, x1, y_ssm, merge_w)

    def mixer_sample(q, k2, v2, qm, mem_k, mem_v, x1, y_ssm):
        n_back = cache_swa_k.shape[2]
        cache2 = lambda c: _twice_per_head(c[0].reshape(bs, n_back, KV_W).astype(BF16), 2)
        kk = jnp.concatenate([cache2(cache_swa_k), k2], axis=1)
        vv = jnp.concatenate([cache2(cache_swa_v), v2], axis=1)
        bias = _band_bias(rel_bias_table, ls, n_back, n_back + ls)
        return _mixer_sample(q, kk, vv, bias, _sink_rows(attn_sink[0], ls), qm, mem_k, mem_v, x1, y_ssm, merge_w)

    x2p, pk, pv, pre, pim = mix(x_prompt.shape, proj_p, mixer_prompt, jnp.zeros((bp, SSM_WIDTH), F32),
                                mk_p.reshape(bp, MEM_LEN * MEM_HEADS, MEM_HEAD_DIM),
                                mv_p.reshape(bp, MEM_LEN * MEM_HEADS, MEM_HEAD_DIM))
    s0 = jnp.stack([state_ssm_re[0], state_ssm_im[0]], axis=2).reshape(bs, SSM_WIDTH).astype(F32)
    x2s, sk, sv, sre, sim = mix(x_sample.shape, proj_s, mixer_sample, s0,
                                cache_mem_k[0].reshape(bs, MEM_LEN * MEM_HEADS, MEM_HEAD_DIM),
                                cache_mem_v[0].reshape(bs, MEM_LEN * MEM_HEADS, MEM_HEAD_DIM))
    yp, ys = _ffn2(x2p, x2s, vec(ff2_pre_g), vec(ff2_post_g), wfi2, wfo2)
    yp, ys = yp.reshape(x_prompt.shape), ys.reshape(x_sample.shape)

    n_keep = min(WINDOW, lp)
    heads = lambda t: t.reshape(t.shape[0], t.shape[1], N_KV_HEADS, HEAD_DIM)[None]
    mem_heads = lambda t: t.reshape(bp, MEM_LEN, MEM_HEADS, MEM_HEAD_DIM)[None]
    return (yp, ys, heads(pk[:, -n_keep:]), heads(pv[:, -n_keep:]), mem_heads(mk_p), mem_heads(mv_p),
            pre[None], pim[None], heads(sk), heads(sv), sre[None], sim[None])
```

```python
import functools
import math

import numpy as np
import jax
import jax.numpy as jnp
from jax import lax
from jax.experimental import pallas as pl
from jax.experimental.pallas import tpu as pltpu

D_MODEL = 1024
CHUNK = 64
WINDOW = 128
HEAD_DIM = 64
MIX_W = D_MODEL // 2
N_HEADS = MIX_W // HEAD_DIM
N_KV_HEADS = N_HEADS // 4
KV_REP = N_HEADS // N_KV_HEADS
ATTN_W = N_HEADS * HEAD_DIM
KV_W = N_KV_HEADS * HEAD_DIM
SSM_GROUP = 16
SSM_W = MIX_W
SSM_GROUPS = SSM_W // SSM_GROUP
SSM_STATE = 64
MEM_LEN = 256
MEM_HEADS = 4
MEM_HEAD_DIM = MIX_W // MEM_HEADS
MEM_W = MEM_HEADS * MEM_HEAD_DIM
D_FF = 128 * ((8 * D_MODEL // 3 + 127) // 128)
N_BUCKETS = 32
MAX_DISTANCE = 128
RMS_EPS = 1e-6
NEG_INF = -1e30

LANES = 128
BF16_ROWS = 16
MXU_TILE = 256
ROW_TILE = 256
MIXER_TILE = 512
SAMPLE_MIXER_TILE = 256
MIXER_PARTS = 2
FFN_ROW_TILE = 512
FFN2_ROW_TILE = 1024
FFN_PARTS = 2
FFN_CHUNK = 2 * MXU_TILE
VMEM_LIMIT = 60 * 1024 * 1024

KV2_W = N_KV_HEADS * LANES

SSM_T = 16
SSM_K = SSM_T * SSM_GROUP
SSM_S2 = 2 * SSM_STATE
SSM_WIDTH = SSM_GROUPS * SSM_S2
SSM_GB = LANES // SSM_GROUP
SSM_NGB = SSM_GROUPS // SSM_GB
SSM_XW = SSM_T * LANES
SSM_SW = SSM_GB * SSM_S2
SSM_TPT = MXU_TILE // LANES
SSM_NT = SSM_T // SSM_TPT
SCAN_ROWS = 8
N_SCAN_TABLES = 10
SSM_CHUNKS_PER_STEP = 256

F32 = jnp.float32
BF16 = jnp.bfloat16


def _params(*sem):
    return pltpu.CompilerParams(dimension_semantics=sem, vmem_limit_bytes=VMEM_LIMIT)


def _resident(shape):
    zeros = (0,) * len(shape)
    return pl.BlockSpec(shape, lambda *_: zeros, pipeline_mode=pl.Buffered(1))


def _rms(x, g):
    return x * lax.rsqrt(jnp.mean(x * x, axis=-1, keepdims=True) + RMS_EPS) * g


def _mm(a, b):
    return jnp.dot(a, b, preferred_element_type=F32)


def _sigmoid(x):
    return 0.5 * jnp.tanh(0.5 * x) + 0.5


def _row_parts(ref, n_parts):
    rows = ref.shape[0] // n_parts
    return [ref[i * rows:(i + 1) * rows, :] for i in range(n_parts)]


def _ffn(xs, gpre, gpost, w_in_ref, w_out_ref):
    hs = [_rms(x, gpre).astype(BF16) for x in xs]
    acts = [[] for _ in xs]
    for c0 in range(0, D_FF, FFN_CHUNK):
        c1 = min(c0 + FFN_CHUNK, D_FF)
        for h, a in zip(hs, acts):
            g = _mm(h, w_in_ref[:, c0:c1])
            u = _mm(h, w_in_ref[:, D_FF + c0:D_FF + c1])
            a.append((g * jax.nn.sigmoid(g) * u).astype(BF16))
    outs = [_mm(jnp.concatenate(a, axis=1), w_out_ref[...]) for a in acts]
    return [x + 0.5 * _rms(o, gpost) for x, o in zip(xs, outs)]


PROJ_SPLIT = (("q", ATTN_W, BF16), ("k", KV_W, F32), ("v", KV_W, F32), ("u", SSM_W, F32),
              ("qm", MEM_W, BF16), ("k2", KV2_W, BF16), ("v2", KV2_W, BF16))
N_PROJ_MATMUL = 5
PROJ_COLS = sum(w for _, w, _ in PROJ_SPLIT[:N_PROJ_MATMUL])
Q_SCALE = HEAD_DIM ** -0.5
assert math.frexp(Q_SCALE)[0] == 0.5, "power of two: scaling q before the bf16 rounding and the dot is exact"


def _twice_per_head_lanes(x):
    assert N_KV_HEADS == 2 and KV_W == LANES
    swapped = pltpu.roll(x, HEAD_DIM, 1)
    low = lax.broadcasted_iota(jnp.int32, x.shape, 1) < HEAD_DIM
    return jnp.concatenate([jnp.where(low, x, swapped), jnp.where(low, swapped, x)], axis=1)


def _two_group_specs(steps_a, tile):
    first = lambda w: pl.BlockSpec((tile, w), lambda i: (jnp.minimum(i, steps_a - 1), 0))
    second = lambda w: pl.BlockSpec((tile, w), lambda i: (jnp.maximum(i - steps_a, 0), 0))
    return first, second


def _on_group(steps_a, run, refs_a, refs_b):
    i = pl.program_id(0)

    @pl.when(i < steps_a)
    def _():
        run(*refs_a)

    @pl.when(i >= steps_a)
    def _():
        run(*refs_b)


def _ffn1_proj_body(steps_a, n_later, xa_ref, xb_ref, gpre_ref, gpost_ref, wfi_ref, wfo_ref, gmix_ref, wp_ref,
                    *refs):
    later_f32, out_refs, later_bf16 = refs[:n_later], refs[n_later:len(refs) - n_later], refs[len(refs) - n_later:]
    _cast_chunks(later_f32, later_bf16)

    def run(x_ref, x1_ref, *proj_refs):
        x1s = _ffn(_row_parts(x_ref, FFN_PARTS), gpre_ref[...], gpost_ref[...], wfi_ref, wfo_ref)
        rows = x_ref.shape[0] // FFN_PARTS
        ps = [_mm(_rms(x1, gmix_ref[...]).astype(BF16), wp_ref[...]) for x1 in x1s]
        for i, (x1, p) in enumerate(zip(x1s, ps)):
            part = slice(i * rows, (i + 1) * rows)
            x1_ref[part, :] = x1
            col, cols = 0, {}
            for name, width, _ in PROJ_SPLIT[:N_PROJ_MATMUL]:
                cols[name] = p[:, col:col + width]
                col += width
            cols["q"] = cols["q"] * Q_SCALE
            cols["k2"], cols["v2"] = _twice_per_head_lanes(cols["k"]), _twice_per_head_lanes(cols["v"])
            for ref, (name, _, dtype) in zip(proj_refs, PROJ_SPLIT):
                ref[part, :] = cols[name].astype(dtype)

    n_out = len(out_refs) // 2
    _on_group(steps_a, run, (xa_ref,) + out_refs[:n_out], (xb_ref,) + out_refs[n_out:])


def _row_chunk_spec(w, steps, cols=None):
    rows = w.shape[0]
    n = next(n for n in range(min(steps, rows // BF16_ROWS), 0, -1)
             if rows % n == 0 and (rows // n) % BF16_ROWS == 0)
    return pl.BlockSpec((rows // n, cols or w.shape[1]), lambda i: (jnp.minimum(i, n - 1), 0))


def _cast_chunks(f32_refs, bf16_refs):
    for src, dst in zip(f32_refs, bf16_refs):
        dst[...] = src[...].astype(BF16)


def _ffn1_proj(xa, xb, gpre, gpost, wfi, wfo, gmix, wp, later_weights):
    steps_a, steps_b = xa.shape[0] // FFN_ROW_TILE, xb.shape[0] // FFN_ROW_TILE
    first, second = _two_group_specs(steps_a, FFN_ROW_TILE)
    widths = [D_MODEL] + [w for _, w, _ in PROJ_SPLIT]
    dtypes = [F32] + [d for _, _, d in PROJ_SPLIT]
    chunk_specs = [_row_chunk_spec(w, steps_a + steps_b) for w in later_weights]
    outs = pl.pallas_call(
        functools.partial(_ffn1_proj_body, steps_a, len(later_weights)),
        grid=(steps_a + steps_b,),
        in_specs=[first(D_MODEL), second(D_MODEL), _resident((1, D_MODEL)), _resident((1, D_MODEL)),
                  _resident(wfi.shape), _resident(wfo.shape), _resident((1, D_MODEL)),
                  _resident(wp.shape)] + chunk_specs,
        out_specs=[first(w) for w in widths] + [second(w) for w in widths] + chunk_specs,
        out_shape=[jax.ShapeDtypeStruct((x.shape[0], w), d) for x in (xa, xb) for w, d in zip(widths, dtypes)]
        + [jax.ShapeDtypeStruct(w.shape, BF16) for w in later_weights],
        compiler_params=_params("arbitrary"),
        name="ffn1_proj",
    )(xa, xb, gpre, gpost, wfi, wfo, gmix, wp, *later_weights)
    n = len(widths)
    return outs[:n], outs[n:2 * n], outs[2 * n:]


def _mem_kv_body(m_ref, g_ref, w_ref, k_ref, v_ref):
    kv = _mm(_rms(m_ref[...], g_ref[...]).astype(BF16), w_ref[...])
    n = m_ref.shape[0]
    for h in range(MEM_HEADS):
        k_ref[pl.ds(h, n, stride=MEM_HEADS), :] = kv[:, h * MEM_HEAD_DIM:(h + 1) * MEM_HEAD_DIM]
        v_ref[pl.ds(h, n, stride=MEM_HEADS), :] = kv[:, MEM_W + h * MEM_HEAD_DIM:MEM_W + (h + 1) * MEM_HEAD_DIM]


def _mem_kv(mem, g, w):
    n = mem.shape[0]
    out = pl.BlockSpec((ROW_TILE * MEM_HEADS, MEM_HEAD_DIM), lambda i: (i, 0))
    return pl.pallas_call(
        _mem_kv_body,
        grid=(n // ROW_TILE,),
        in_specs=[pl.BlockSpec((ROW_TILE, D_MODEL), lambda i: (i, 0)), _resident((1, D_MODEL)), _resident(w.shape)],
        out_specs=[out, out],
        out_shape=[jax.ShapeDtypeStruct((n * MEM_HEADS, MEM_HEAD_DIM), F32)] * 2,
        compiler_params=_params("parallel"),
        name="mem_kv",
    )(mem, g, w)


def _qk(q, k):
    return lax.dot_general(q, k, (((1,), (1,)), ((), ())), preferred_element_type=F32)


def _softmax_pv(scores, values, sinks=None):
    probs = []
    for i, s in enumerate(scores):
        m = jnp.max(s, axis=-1, keepdims=True)
        if sinks is not None:
            m = jnp.maximum(m, sinks[i])
        e = jnp.exp(s - m)
        den = jnp.sum(e, axis=-1, keepdims=True)
        if sinks is not None:
            den = den + jnp.exp(sinks[i] - m)
        probs.append((e * (1.0 / den)).astype(BF16))
    return [_mm(p, v) for p, v in zip(probs, values)]


def _gqa_queries(q):
    nq = q.shape[0]
    low = lax.broadcasted_iota(jnp.int32, (nq, LANES), 1) < HEAD_DIM
    zero = jnp.zeros((nq, LANES), BF16)
    stacks = []
    for g in range(N_KV_HEADS):
        rows = []
        for r in range(KV_REP):
            h = g * KV_REP + r
            q2 = q[:, (h // 2) * LANES:(h // 2 + 1) * LANES]
            rows.append(jnp.where(low, q2, zero) if h % 2 == 0 else jnp.where(low, zero, q2))
        stacks.append(jnp.concatenate(rows, axis=0))
    return stacks


def _gqa_outputs(outs, nq):
    low = lax.broadcasted_iota(jnp.int32, (nq, LANES), 1) < HEAD_DIM
    pairs = []
    for o in outs:
        for r in range(0, KV_REP, 2):
            pairs.append(jnp.where(low, o[r * nq:(r + 1) * nq], o[(r + 1) * nq:(r + 2) * nq]))
    return jnp.concatenate(pairs, axis=1)


def _mem_scores(qm, k_head):
    return [_qk(qm[:, h * MEM_HEAD_DIM:(h + 1) * MEM_HEAD_DIM], k_head(h).astype(BF16)) * (MEM_HEAD_DIM ** -0.5)
            for h in range(MEM_HEADS)]


def _mem_values(v_head):
    return [v_head(h).astype(BF16) for h in range(MEM_HEADS)]


def _row_heads(ref, i):
    return lambda h: ref[i, pl.ds(h, MEM_LEN, stride=MEM_HEADS), :]


def _kv_heads(x):
    return [x[:, g * LANES:(g + 1) * LANES] for g in range(N_KV_HEADS)]


def _merge_pre(x1, y, gmix_ref, wg_ref, wglu_ref):
    h = _rms(x1, gmix_ref[...]).astype(BF16)
    logits = [_mm(h, wg_ref[:, PROJ_COLS + j * D_MODEL:PROJ_COLS + (j + 1) * D_MODEL]) for j in range(3)]
    y = y.astype(BF16)
    return logits, _mm(y, wglu_ref[:, :D_MODEL]), _mm(y, wglu_ref[:, D_MODEL:])


def _mix_parts(parts, x1_ref, y_ref, gmix_ref, wg_ref, wglu_ref, wab_ref, wmb_ref, wo_ref, gpost_ref, x2_ref):
    x1s, ys = _row_parts(x1_ref, len(parts)), _row_parts(y_ref, len(parts))
    pres = [_merge_pre(x1, y, gmix_ref, wg_ref, wglu_ref) for x1, y in zip(x1s, ys)]
    branches = [assemble(_softmax_pv(scores, values, sinks), _softmax_pv(mem_scores, mem_values))
                for scores, values, sinks, mem_scores, mem_values, assemble in parts]
    projected = [(_mm(attn.astype(BF16), wab_ref[...]), _mm(memo.astype(BF16), wmb_ref[...]))
                 for attn, memo in branches]
    sig = _sigmoid
    merged = [sig(lg[0]) * pa + sig(lg[1]) * (ya * sig(yb)) + sig(lg[2]) * pm
              for (lg, ya, yb), (pa, pm) in zip(pres, projected)]
    outs = [_mm(m.astype(BF16), wo_ref[...]) for m in merged]
    rows = x1_ref.shape[0] // len(parts)
    for i, (x1, o) in enumerate(zip(x1s, outs)):
        x2_ref[i * rows:(i + 1) * rows, :] = x1 + _rms(o, gpost_ref[...])


def _mixer_prompt_body(q_ref, k_ref, kh_ref, v_ref, vh_ref, bias_ref, sink_ref, qm_ref, mk_ref, mv_ref,
                       x1_ref, y_ref, *merge_and_out_refs):
    i = pl.program_id(1)
    tq = q_ref.shape[1]
    kf = jnp.concatenate([kh_ref[0], k_ref[0]], axis=0)
    vf = jnp.concatenate([vh_ref[0], v_ref[0]], axis=0)
    band = WINDOW + CHUNK
    rows = tq // MIXER_PARTS
    parts = []
    for p in range(MIXER_PARTS):
        scores, values, sinks = [], [], []
        for lo in range(p * rows, (p + 1) * rows, CHUNK):
            valid = None
            if lo < WINDOW:
                key_pos = lax.broadcasted_iota(jnp.int32, (1, band), 1) + (i * tq + lo - WINDOW)
                valid = key_pos >= 0
            kb, vb = _kv_heads(kf[lo:lo + band]), _kv_heads(vf[lo:lo + band])
            for g, qs in enumerate(_gqa_queries(q_ref[0, lo:lo + CHUNK, :])):
                s = _qk(qs, kb[g]) + bias_ref[g]
                scores.append(s if valid is None else jnp.where(valid, s, NEG_INF))
                values.append(vb[g])
                sinks.append(sink_ref[g])
        mem_scores = _mem_scores(qm_ref[0, p * rows:(p + 1) * rows, :], _row_heads(mk_ref, 0))

        def assemble(outs, mem):
            attn = jnp.concatenate([_gqa_outputs(outs[c:c + N_KV_HEADS], CHUNK)
                                    for c in range(0, len(outs), N_KV_HEADS)], axis=0)
            return attn, jnp.concatenate(mem, axis=1)

        parts.append((scores, values, sinks, mem_scores, _mem_values(_row_heads(mv_ref, 0)), assemble))
    _mix_parts(parts, x1_ref, y_ref, *merge_and_out_refs)


def _merge_weight_specs(weights):
    return [_resident(w.shape) for w in weights]


def _mixer_prompt(q, k2, v2, bias, sink, qm, mk, mv, x1, y, merge_w):
    b, l, _ = q.shape
    tq = MIXER_TILE
    hb = tq // WINDOW
    nt = l // tq
    cur = lambda w: pl.BlockSpec((1, tq, w), lambda bi, i: (bi, i, 0))
    row = lambda w: pl.BlockSpec((tq, w), lambda bi, i: (bi * nt + i, 0))
    halo = pl.BlockSpec((1, WINDOW, KV2_W), lambda bi, i: (bi, jnp.maximum(i * hb - 1, 0), 0))
    mem = pl.BlockSpec((1, MEM_LEN * MEM_HEADS, MEM_HEAD_DIM), lambda bi, i: (bi, 0, 0))
    return pl.pallas_call(
        _mixer_prompt_body,
        grid=(b, nt),
        in_specs=[cur(ATTN_W), cur(KV2_W), halo, cur(KV2_W), halo, _resident(bias.shape),
                  _resident(sink.shape), cur(MEM_W), mem, mem, row(D_MODEL), row(SSM_W)]
        + _merge_weight_specs(merge_w),
        out_specs=row(D_MODEL),
        out_shape=jax.ShapeDtypeStruct((b * l, D_MODEL), F32),
        compiler_params=_params("parallel", "parallel"),
        name="mixer_prompt",
    )(q, k2, k2, v2, v2, bias, sink, qm, mk, mv, x1, y, *merge_w)


def _mixer_sample_body(q_ref, k_ref, v_ref, bias_ref, sink_ref, qm_ref, mk_ref, mv_ref,
                       x1_ref, y_ref, *merge_and_out_refs):
    nb, nq = q_ref.shape[0], q_ref.shape[1]
    per_part = nb // MIXER_PARTS
    parts = []
    for p in range(MIXER_PARTS):
        scores, values, sinks, mem_scores, mem_values = [], [], [], [], []
        for b in range(p * per_part, (p + 1) * per_part):
            kb, vb = _kv_heads(k_ref[b]), _kv_heads(v_ref[b])
            for g, qs in enumerate(_gqa_queries(q_ref[b])):
                scores.append(_qk(qs, kb[g]) + bias_ref[g])
                values.append(vb[g])
                sinks.append(sink_ref[g])
            mem_scores += _mem_scores(qm_ref[b], _row_heads(mk_ref, b))
            mem_values += _mem_values(_row_heads(mv_ref, b))

        def assemble(outs, mem):
            attn = jnp.concatenate([_gqa_outputs(outs[j:j + N_KV_HEADS], nq)
                                    for j in range(0, len(outs), N_KV_HEADS)], axis=0)
            memo = jnp.concatenate([jnp.concatenate(mem[j:j + MEM_HEADS], axis=1)
                                    for j in range(0, len(mem), MEM_HEADS)], axis=0)
            return attn, memo

        parts.append((scores, values, sinks, mem_scores, mem_values, assemble))
    _mix_parts(parts, x1_ref, y_ref, *merge_and_out_refs)


def _mixer_sample(q, kk2, vv2, bias, sink, qm, mk, mv, x1, y, merge_w):
    b, s, _ = q.shape
    nb = SAMPLE_MIXER_TILE // s
    nk = kk2.shape[1]
    blk = lambda *shape: pl.BlockSpec((nb,) + shape, lambda i: (i,) + (0,) * len(shape))
    row = lambda w: pl.BlockSpec((nb * s, w), lambda i: (i, 0))
    return pl.pallas_call(
        _mixer_sample_body,
        grid=(b // nb,),
        in_specs=[blk(s, ATTN_W), blk(nk, KV2_W), blk(nk, KV2_W),
                  _resident(bias.shape), _resident(sink.shape),
                  blk(s, MEM_W), blk(*mk.shape[1:]), blk(*mv.shape[1:]), row(D_MODEL), row(SSM_W)]
        + _merge_weight_specs(merge_w),
        out_specs=row(D_MODEL),
        out_shape=jax.ShapeDtypeStruct((b * s, D_MODEL), F32),
        compiler_params=_params("parallel"),
        name="mixer_sample",
    )(q, kk2, vv2, bias, sink, qm, mk, mv, x1, y, *merge_w)


def _ssm_weights_body(n_first, lam_re_row, lam_im_row, lam_re_col, lam_im_col, log_dt, b_re_t, b_im_t,
                      c_re, c_im, c_re_lanes, c_im_lanes, sel_ref, *refs):
    first_f32, (wconv_ref, wst_ref, wot_ref, a_tab_ref), first_bf16 = (
        refs[:n_first], refs[n_first:n_first + 4], refs[n_first + 4:])
    _cast_chunks(first_f32, first_bf16)
    _ssm_block_weights(lam_re_row, lam_im_row, lam_re_col, lam_im_col, log_dt, b_re_t, b_im_t,
                       c_re, c_im, c_re_lanes, c_im_lanes, sel_ref, wconv_ref, wst_ref, wot_ref, a_tab_ref)


def _ssm_block_weights(lam_re_row, lam_im_row, lam_re_col, lam_im_col, log_dt, b_re_t, b_im_t,
                       c_re, c_im, c_re_lanes, c_im_lanes, sel_ref,
                       wconv_ref, wst_ref, wot_ref, a_tab_ref):
    wst_ref[0] = jnp.zeros(wst_ref.shape[1:], BF16)
    wot_ref[0] = jnp.zeros(wot_ref.shape[1:], BF16)
    krows = []
    for gl in range(SSM_GB):
        krows.append(_ssm_group_weights(
            gl, *(r[gl] for r in (lam_re_row, lam_im_row, lam_re_col, lam_im_col, log_dt, b_re_t, b_im_t,
                                  c_re, c_im, c_re_lanes, c_im_lanes)),
            wst_ref, wot_ref, a_tab_ref))
    kstack = jnp.concatenate(krows, axis=0).astype(BF16)
    for d in range(SSM_NT):
        for ti in range(SSM_TPT):
            blk = _mm(kstack, sel_ref[d * SSM_TPT + ti])
            r0 = (SSM_NT - 1 - d) * MXU_TILE + ti * LANES
            for gl in range(SSM_GB):
                piece = blk[gl * SSM_GROUP:(gl + 1) * SSM_GROUP]
                piece = (piece if gl == 0 else pltpu.roll(piece, gl * SSM_GROUP, 1)).astype(BF16)
                r = r0 + gl * SSM_GROUP
                wconv_ref[0, r:r + SSM_GROUP, :MXU_TILE] = piece
                if r0 >= MXU_TILE:
                    wconv_ref[0, r - MXU_TILE:r - MXU_TILE + SSM_GROUP, MXU_TILE:] = piece
    wconv_ref[0, (SSM_NT - 1) * MXU_TILE:, MXU_TILE:] = jnp.zeros((MXU_TILE, MXU_TILE), BF16)


def _ssm_group_weights(gl, lam_re_row, lam_im_row, lam_re_col, lam_im_col, log_dt, b_re_t, b_im_t,
                       c_re, c_im, c_re_lanes, c_im_lanes,
                       wst_ref, wot_ref, a_tab_ref):
    dt = jnp.exp(log_dt)
    rows = lambda t: slice(t * LANES + gl * SSM_GROUP, t * LANES + (gl + 1) * SSM_GROUP)
    lanes = slice(gl * SSM_S2, (gl + 1) * SSM_S2)

    def zoh_coef(lr, li):
        mag = jnp.exp(lr * dt)
        a_re, a_im = mag * jnp.cos(li * dt), mag * jnp.sin(li * dt)
        den = lr * lr + li * li
        return ((a_re - 1.0) * lr + a_im * li) / den, (a_im * lr - (a_re - 1.0) * li) / den

    def a_power(lr, li, n):
        mag = jnp.exp(lr * dt * n)
        return mag * jnp.cos(li * dt * n), mag * jnp.sin(li * dt * n)

    lr, li = lam_re_row, lam_im_row
    cr, ci = zoh_coef(lr, li)
    n_rows = jnp.minimum(lax.broadcasted_iota(jnp.int32, (SSM_T + SCAN_ROWS, 1), 0), SSM_T).astype(F32)
    pw_r, pw_i = a_power(lr, li, n_rows)

    def over_channels(pw, exps):
        return jnp.concatenate([jnp.broadcast_to(pw[e:e + 1], (SSM_GROUP, SSM_STATE)) for e in exps], axis=0)

    back = [SSM_T - 1 - t for t in range(SSM_T)]
    pr, pi = over_channels(pw_r, back), over_channels(pw_i, back)
    zr, zi = pr * cr - pi * ci, pr * ci + pi * cr
    over_positions = lambda x: jnp.concatenate([x] * SSM_T, axis=0)
    br, bi = over_positions(b_re_t), over_positions(b_im_t)
    wstate = jnp.concatenate([zr * br - zi * bi, zr * bi + zi * br], axis=1).astype(BF16)

    fwd = [t + 1 for t in range(SSM_T)]
    qr, qi = over_channels(pw_r, fwd), over_channels(pw_i, fwd)
    ccr, cci = over_positions(c_re), over_positions(c_im)
    wout_t = jnp.concatenate([ccr * qr - cci * qi, -(ccr * qi + cci * qr)], axis=1).astype(BF16)
    for t in range(SSM_T):
        wst_ref[0, rows(t), lanes] = wstate[t * SSM_GROUP:(t + 1) * SSM_GROUP]
        wot_ref[0, rows(t), lanes] = wout_t[t * SSM_GROUP:(t + 1) * SSM_GROUP]

    lrc, lic = lam_re_col, lam_im_col
    hi = lax.Precision.HIGHEST
    n_lanes = jnp.minimum(lax.broadcasted_iota(jnp.int32, (1, LANES), 1), SSM_T).astype(F32)
    pc_r, pc_i = a_power(lrc, lic, n_lanes)
    spread = (lax.broadcasted_iota(jnp.int32, (LANES, SSM_K), 0)
              == lax.broadcasted_iota(jnp.int32, (LANES, SSM_K), 1) // SSM_GROUP).astype(F32)
    gr = jnp.dot(pc_r, spread, precision=hi, preferred_element_type=F32)
    gi = jnp.dot(pc_i, spread, precision=hi, preferred_element_type=F32)
    clr, cli = c_re_lanes, c_im_lanes
    g_re, g_im = clr * gr - cli * gi, clr * gi + cli * gr
    btr, bti = b_re_t, b_im_t
    bbr, bbi = cr * btr - ci * bti, cr * bti + ci * btr
    krow = (jnp.dot(bbr, g_re, precision=hi, preferred_element_type=F32)
            - jnp.dot(bbi, g_im, precision=hi, preferred_element_type=F32))

    idx = lax.broadcasted_iota(jnp.int32, (N_SCAN_TABLES * SCAN_ROWS, SSM_STATE), 0)
    tab, r = idx // SCAN_ROWS, idx % SCAN_ROWS
    stride = jnp.where(tab < 2, 1, jnp.where(tab < 4, 2, 4))
    n = jnp.where(tab < 6, stride, jnp.where(tab < 8, r, SCAN_ROWS))
    keep = jnp.logical_or(tab >= 6, r >= stride)
    er, ei = a_power(lr, li, (n * SSM_T).astype(F32))
    er, ei = jnp.where(keep, er, 0.0), jnp.where(keep, ei, 0.0)
    odd = tab % 2 == 1
    tabs = jnp.concatenate([jnp.where(odd, -ei, er), jnp.where(odd, ei, er)], axis=1)
    a_tab_ref[:, :, lanes] = tabs.reshape(N_SCAN_TABLES, SCAN_ROWS, SSM_S2)
    return krow


def _lag_selectors():
    sel = np.zeros((SSM_NT, SSM_TPT, SSM_K, MXU_TILE), np.float32)
    ch = np.arange(SSM_GROUP)
    for d in range(SSM_NT):
        for ti in range(SSM_TPT):
            for to in range(SSM_TPT):
                lag = SSM_TPT * d + to - ti
                if lag >= 0:
                    sel[d, ti, lag * SSM_GROUP + ch, to * LANES + ch] = 1.0
    return jnp.asarray(sel.reshape(SSM_NT * SSM_TPT, SSM_K, MXU_TILE), BF16)


def _ssm_weights(lam_re, lam_im, log_dt, b_re, b_im, c_re, c_im, first_weights):
    g, p, c = b_re.shape
    row3 = lambda x: x.reshape(g, 1, p)
    col3 = lambda x: x.reshape(g, p, 1)
    t3 = lambda x: jnp.transpose(x, (0, 2, 1))
    c_lanes = lambda x: jnp.tile(t3(x), (1, 1, SSM_T))
    args = (row3(lam_re), row3(lam_im), col3(lam_re), col3(lam_im), log_dt.reshape(g, 1, 1),
            t3(b_re), t3(b_im), c_re, c_im, c_lanes(c_re), c_lanes(c_im))
    sel = _lag_selectors()
    spec = lambda x: pl.BlockSpec((SSM_GB,) + x.shape[1:], lambda i: (i, 0, 0))
    per_gb = lambda *shape: pl.BlockSpec((1,) + shape, lambda i: (i, 0, 0))
    chunk_specs = [_row_chunk_spec(w, SSM_NGB, cols) for w, cols in first_weights]
    outs = pl.pallas_call(
        functools.partial(_ssm_weights_body, len(first_weights)),
        grid=(SSM_NGB,),
        in_specs=[spec(a) for a in args] + [_resident(sel.shape)] + chunk_specs,
        out_specs=[per_gb(SSM_NT * MXU_TILE, 2 * MXU_TILE), per_gb(SSM_XW, SSM_SW), per_gb(SSM_XW, SSM_SW),
                   pl.BlockSpec((N_SCAN_TABLES, SCAN_ROWS, SSM_SW), lambda i: (0, 0, i))] + chunk_specs,
        out_shape=[jax.ShapeDtypeStruct((SSM_NGB, SSM_NT * MXU_TILE, 2 * MXU_TILE), BF16),
                   jax.ShapeDtypeStruct((SSM_NGB, SSM_XW, SSM_SW), BF16),
                   jax.ShapeDtypeStruct((SSM_NGB, SSM_XW, SSM_SW), BF16),
                   jax.ShapeDtypeStruct((N_SCAN_TABLES, SCAN_ROWS, SSM_WIDTH), F32)]
        + [jax.ShapeDtypeStruct((w.shape[0], cols or w.shape[1]), BF16) for w, cols in first_weights],
        compiler_params=_params("arbitrary"),
        name="ssm_weights",
    )(*args, sel, *[w for w, _ in first_weights])
    return outs[:4], outs[4:]


def _swap_halves(s):
    ax = s.ndim - 1
    return jnp.concatenate([pltpu.roll(s[..., l:l + SSM_S2], SSM_STATE, ax)
                            for l in range(0, s.shape[ax], SSM_S2)], axis=ax)


def _ssm_body(row_sets, slab_rows, u_ref, wconv_ref, wst_ref, wot_ref, d_ref, a_tab_ref, s0_ref,
              y_ref, s_ref, v_scr, s_scr):
    def piece(ref_set, first, n, stride, t):
        return ref_set, pl.ds(first + t, n, stride=stride)

    @pl.when(pl.program_id(1) == 0)
    def _():
        s_scr[...] = s0_ref[0]

    x = jnp.concatenate(
        [jnp.concatenate([u_ref[piece(*rs, t)] for t in range(SSM_T)], axis=1) for rs in row_sets],
        axis=0).astype(BF16)
    v_scr[...] = _mm(x, wst_ref[0])
    conv = []
    for jo in range(0, SSM_NT, 2):
        pair = _mm(x[:, :(jo + 2) * MXU_TILE], wconv_ref[0, (SSM_NT - 2 - jo) * MXU_TILE:, :])
        conv += [pair[:, MXU_TILE:], pair[:, :MXU_TILE]]

    n_rows = v_scr.shape[0]
    if slab_rows:
        a_mul, a_swap = a_tab_ref[6, 1:2, :], a_tab_ref[7, 1:2, :]
        s = s_scr[...]
        for c in range(n_rows // slab_rows):
            rows = slice(c * slab_rows, (c + 1) * slab_rows)
            inc = v_scr[rows, :]
            v_scr[rows, :] = s
            s = a_mul * s + a_swap * _swap_halves(s) + inc
        s_scr[...] = s
    else:
        n_blocks = n_rows // SCAN_ROWS
        xs = v_scr[...].reshape(n_blocks, SCAN_ROWS, SSM_SW)
        for k in range(3):
            sh = pltpu.roll(xs, 1 << k, 1)
            xs = xs + a_tab_ref[2 * k] * sh + a_tab_ref[2 * k + 1] * _swap_halves(sh)
        carries, blk = [], 0
        for j, (_, _, n, _) in enumerate(row_sets):
            carry = s_scr[j * SCAN_ROWS:(j + 1) * SCAN_ROWS, :]
            for _ in range(n // SCAN_ROWS):
                carries.append(carry)
                last = jnp.broadcast_to(xs[blk, SCAN_ROWS - 1:, :], carry.shape)
                carry = last + a_tab_ref[8] * carry + a_tab_ref[9] * _swap_halves(carry)
                blk += 1
            s_scr[j * SCAN_ROWS:(j + 1) * SCAN_ROWS, :] = carry
        carries = jnp.stack(carries, axis=0)
        not_first = lax.broadcasted_iota(jnp.int32, (1, SCAN_ROWS, 1), 1) >= 1
        enter = (jnp.where(not_first, pltpu.roll(xs, 1, 1), 0.0)
                 + a_tab_ref[6] * carries + a_tab_ref[7] * _swap_halves(carries))
        v_scr[...] = enter.reshape(n_rows, SSM_SW)
    s_ref[0] = s_scr[...]

    enter = v_scr[...].astype(BF16)
    d = d_ref[...]
    carried = _qk(enter, wot_ref[0])
    for jo in range(SSM_NT):
        yt = conv[jo] + carried[:, jo * MXU_TILE:(jo + 1) * MXU_TILE]
        r0 = 0
        for rs in row_sets:
            for tl in range(SSM_TPT):
                idx = piece(*rs, jo * SSM_TPT + tl)
                y_ref[idx] = yt[r0:r0 + rs[2], tl * LANES:(tl + 1) * LANES] + u_ref[idx] * d
            r0 += rs[2]


def _ssm(u, row_sets, slab_rows, block_rows, wconv, wst, wo, d_lanes, a_tab, s0):
    ns, r, _ = u.shape
    chunk_rows = sum(rs[2] for rs in row_sets)
    carry_rows = s0.shape[1]
    blk = pl.BlockSpec((ns, block_rows, LANES), lambda gb, i: (0, i, gb))
    per_gb = lambda x: pl.BlockSpec((1,) + x.shape[1:], lambda gb, i: (gb, 0, 0))
    return pl.pallas_call(
        functools.partial(_ssm_body, row_sets, slab_rows),
        grid=(SSM_NGB, r // block_rows),
        in_specs=[blk, per_gb(wconv), per_gb(wst), per_gb(wo),
                  pl.BlockSpec((1, LANES), lambda gb, i: (0, gb)),
                  pl.BlockSpec((N_SCAN_TABLES, SCAN_ROWS, SSM_SW), lambda gb, i: (0, 0, gb)),
                  per_gb(s0)],
        out_specs=[blk, per_gb(s0)],
        out_shape=[jax.ShapeDtypeStruct(u.shape, F32), jax.ShapeDtypeStruct(s0.shape, F32)],
        scratch_shapes=[pltpu.VMEM((chunk_rows, SSM_SW), F32), pltpu.VMEM((carry_rows, SSM_SW), F32)],
        compiler_params=_params("parallel", "arbitrary"),
        name="ssm_scan",
    )(u, wconv, wst, wo, d_lanes, a_tab, s0)


def _ssm_branch(u, s0, ssm_w):
    b, l, _ = u.shape
    nc = l // SSM_T
    by_gb = lambda s: s.reshape(s.shape[0], SSM_NGB, SSM_SW).transpose(1, 0, 2)
    if b % SCAN_ROWS == 0:
        row_sets = tuple((0, c * SSM_T, b, l) for c in range(nc))
        y, s_last = _ssm(u.reshape(1, b * l, SSM_W), row_sets, b, b * l, *ssm_w, by_gb(s0))
    else:
        cps = min(nc, SSM_CHUNKS_PER_STEP)
        row_sets = tuple((j, 0, cps, SSM_T) for j in range(b))
        y, s_last = _ssm(u, row_sets, 0, cps * SSM_T, *ssm_w, by_gb(jnp.repeat(s0, SCAN_ROWS, axis=0)))
        s_last = s_last[:, ::SCAN_ROWS]
    s_last = s_last.transpose(1, 0, 2).reshape(b, SSM_GROUPS, 2, SSM_STATE)
    return y.reshape(b, l, SSM_W), s_last[:, :, 0], s_last[:, :, 1]


def _ffn2_body(steps_a, xa_ref, xb_ref, gpre_ref, gpost_ref, wfi_ref, wfo_ref, oa_ref, ob_ref):
    def run(x_ref, o_ref):
        outs = _ffn(_row_parts(x_ref, FFN_PARTS), gpre_ref[...], gpost_ref[...], wfi_ref, wfo_ref)
        rows = x_ref.shape[0] // FFN_PARTS
        for i, o in enumerate(outs):
            o_ref[i * rows:(i + 1) * rows, :] = o

    _on_group(steps_a, run, (xa_ref, oa_ref), (xb_ref, ob_ref))


def _ffn2(xa, xb, gpre, gpost, wfi, wfo):
    steps_a, steps_b = xa.shape[0] // FFN2_ROW_TILE, xb.shape[0] // FFN2_ROW_TILE
    first, second = _two_group_specs(steps_a, FFN2_ROW_TILE)
    vec = _resident((1, D_MODEL))
    return pl.pallas_call(
        functools.partial(_ffn2_body, steps_a),
        grid=(steps_a + steps_b,),
        in_specs=[first(D_MODEL), second(D_MODEL), vec, vec, _resident(wfi.shape), _resident(wfo.shape)],
        out_specs=[first(D_MODEL), second(D_MODEL)],
        out_shape=[jax.ShapeDtypeStruct(xa.shape, F32), jax.ShapeDtypeStruct(xb.shape, F32)],
        compiler_params=_params("arbitrary"),
        name="ffn2",
    )(xa, xb, gpre, gpost, wfi, wfo)


def _t5_bucket(rel):
    half = N_BUCKETS // 2
    max_exact = half // 2
    ret = (rel > 0).astype(np.int32) * half
    n = np.abs(rel)
    large = max_exact + (np.log(np.maximum(n, 1) / max_exact) / math.log(MAX_DISTANCE / max_exact)
                         * (half - max_exact)).astype(np.int32)
    large = np.minimum(large, half - 1)
    return ret + np.where(n < max_exact, n, large)


def _band_bias(rel_table, n_q, n_back, n_k):
    i = np.arange(n_q)[:, None]
    j = np.arange(n_k)[None, :]
    bucket = _t5_bucket((j - n_back) - i).reshape(-1)
    onehot = np.zeros((N_BUCKETS, bucket.size), np.float32)
    onehot[bucket, np.arange(bucket.size)] = 1.0
    b = jnp.dot(rel_table.astype(F32).T, jnp.asarray(onehot), precision=lax.Precision.HIGHEST)
    return b.reshape(N_KV_HEADS, KV_REP * n_q, n_k)


def _sink_rows(sink, n_q):
    return jnp.repeat(sink.astype(F32).reshape(N_KV_HEADS, KV_REP), n_q, axis=1)[:, :, None]


def _twice_per_head(x, axis):
    shape = x.shape
    x = x.reshape(shape[:axis] + (N_KV_HEADS, 1, HEAD_DIM) + shape[axis + 1:])
    x = jnp.concatenate([x, x], axis=axis + 1)
    return x.reshape(shape[:axis] + (KV2_W,) + shape[axis + 1:])


def kernel(x_prompt, x_sample, cache_swa_k, cache_swa_v, cache_mem_k, cache_mem_v, state_ssm_re, state_ssm_im, mem_prompt, rel_bias_table, ff1_pre_g, ff1_post_g, w_ff1_in, w_ff1_out, mix_pre_g, mix_post_g, w_in, mem_norm_g, w_mem_kv, attn_sink, ssm_lambda_re, ssm_lambda_im, ssm_log_dt, ssm_b_re, ssm_b_im, ssm_c_re, ssm_c_im, ssm_d, w_ssm_glu, w_attn_br, w_mem_br, w_out, ff2_pre_g, ff2_post_g, w_ff2_in, w_ff2_out):
    assert ff1_pre_g.shape[0] == 1, "single-layer step"
    bp, lp, _ = x_prompt.shape
    bs, ls, _ = x_sample.shape
    vec = lambda g: g[0].reshape(1, D_MODEL).astype(F32)

    (wconv, wst, wot, a_tab), (wfi1, wfo1, wp) = _ssm_weights(
        ssm_lambda_re[0], ssm_lambda_im[0], ssm_log_dt[0], ssm_b_re[0], ssm_b_im[0], ssm_c_re[0], ssm_c_im[0],
        [(w_ff1_in[0], None), (w_ff1_out[0], None), (w_in[0], PROJ_COLS)])
    ssm_w = (wconv, wst, wot, ssm_d[0].astype(F32).reshape(1, SSM_W), a_tab)

    later = (w_ff2_in, w_ff2_out, w_in, w_ssm_glu, w_attn_br, w_mem_br, w_out, w_mem_kv)
    proj_p, proj_s, (wfi2, wfo2, w_in16, wglu, wab, wmb, wo, wmkv) = _ffn1_proj(
        x_prompt.reshape(bp * lp, D_MODEL), x_sample.reshape(bs * ls, D_MODEL), vec(ff1_pre_g), vec(ff1_post_g),
        wfi1, wfo1, vec(mix_pre_g), wp, [w[0] for w in later])

    mk_p, mv_p = _mem_kv(mem_prompt.reshape(bp * MEM_LEN, D_MODEL), vec(mem_norm_g), wmkv)

    def mix(x_shape, proj, mixer, s0, mem_k, mem_v):
        b, l, _ = x_shape
        r3 = lambda t: t.reshape(b, l, t.shape[-1])
        x1, q, k, v, u, qm, k2, v2 = proj
        y_ssm, s_re, s_im = _ssm_branch(r3(u), s0, ssm_w)
        x2 = mixer(r3(q), r3(k2), r3(v2), r3(qm), mem_k, mem_v, x1, y_ssm.reshape(b * l, SSM_W))
        return x2, r3(k), r3(v), s_re, s_im

    merge_w = (vec(mix_pre_g), w_in16, wglu, wab, wmb, wo, vec(mix_post_g))

    def mixer_prompt(q, k2, v2, qm, mem_k, mem_v, x1, y_ssm):
        bias = _band_bias(rel_bias_table, CHUNK, WINDOW, WINDOW + CHUNK)
        return _mixer_prompt(q, k2, v2, bias, _sink_rows(attn_sink[0], CHUNK), qm, mem_k, mem_v, x1, y_ssm, merge_w)

    def mixer_sample(q, k2, v2, qm, mem_k, mem_v, x1, y_ssm):
        n_back = cache_swa_k.shape[2]
        cache2 = lambda c: _twice_per_head(c[0].reshape(bs, n_back, KV_W).astype(BF16), 2)
        kk = jnp.concatenate([cache2(cache_swa_k), k2], axis=1)
        vv = jnp.concatenate([cache2(cache_swa_v), v2], axis=1)
        bias = _band_bias(rel_bias_table, ls, n_back, n_back + ls)
        return _mixer_sample(q, kk, vv, bias, _sink_rows(attn_sink[0], ls), qm, mem_k, mem_v, x1, y_ssm, merge_w)

    x2p, pk, pv, pre, pim = mix(x_prompt.shape, proj_p, mixer_prompt, jnp.zeros((bp, SSM_WIDTH), F32),
                                mk_p.reshape(bp, MEM_LEN * MEM_HEADS, MEM_HEAD_DIM),
                                mv_p.reshape(bp, MEM_LEN * MEM_HEADS, MEM_HEAD_DIM))
    s0 = jnp.stack([state_ssm_re[0], state_ssm_im[0]], axis=2).reshape(bs, SSM_WIDTH).astype(F32)
    x2s, sk, sv, sre, sim = mix(x_sample.shape, proj_s, mixer_sample, s0,
                                cache_mem_k[0].reshape(bs, MEM_LEN * MEM_HEADS, MEM_HEAD_DIM),
                                cache_mem_v[0].reshape(bs, MEM_LEN * MEM_HEADS, MEM_HEAD_DIM))
    yp, ys = _ffn2(x2p, x2s, vec(ff2_pre_g), vec(ff2_post_g), wfi2, wfo2)
    yp, ys = yp.reshape(x_prompt.shape), ys.reshape(x_sample.shape)

    n_keep = min(WINDOW, lp)
    heads = lambda t: t.reshape(t.shape[0], t.shape[1], N_KV_HEADS, HEAD_DIM)[None]
    mem_heads = lambda t: t.reshape(bp, MEM_LEN, MEM_HEADS, MEM_HEAD_DIM)[None]
    return (yp, ys, heads(pk[:, -n_keep:]), heads(pv[:, -n_keep:]), mem_heads(mk_p), mem_heads(mv_p),
            pre[None], pim[None], heads(sk), heads(sv), sre[None], sim[None])
```

```python
import functools
import math

import numpy as np
import jax
import jax.numpy as jnp
from jax import lax
from jax.experimental import pallas as pl
from jax.experimental.pallas import tpu as pltpu

D_MODEL = 1024
CHUNK = 64
WINDOW = 128
HEAD_DIM = 64
MIX_W = D_MODEL // 2
N_HEADS = MIX_W // HEAD_DIM
N_KV_HEADS = N_HEADS // 4
KV_REP = N_HEADS // N_KV_HEADS
ATTN_W = N_HEADS * HEAD_DIM
KV_W = N_KV_HEADS * HEAD_DIM
SSM_GROUP = 16
SSM_W = MIX_W
SSM_GROUPS = SSM_W // SSM_GROUP
SSM_STATE = 64
MEM_LEN = 256
MEM_HEADS = 4
MEM_HEAD_DIM = MIX_W // MEM_HEADS
MEM_W = MEM_HEADS * MEM_HEAD_DIM
D_FF = 128 * ((8 * D_MODEL // 3 + 127) // 128)
N_BUCKETS = 32
MAX_DISTANCE = 128
RMS_EPS = 1e-6
NEG_INF = -1e30

LANES = 128
BF16_ROWS = 16
MXU_TILE = 256
ROW_TILE = 256
MIXER_TILE = 512
SAMPLE_MIXER_TILE = 256
MIXER_PARTS = 2
FFN_ROW_TILE = 512
FFN_PARTS = 2
FFN_CHUNK = 2 * MXU_TILE
VMEM_LIMIT = 60 * 1024 * 1024

KV2_W = N_KV_HEADS * LANES

SSM_T = 16
SSM_K = SSM_T * SSM_GROUP
SSM_S2 = 2 * SSM_STATE
SSM_WIDTH = SSM_GROUPS * SSM_S2
SSM_GB = LANES // SSM_GROUP
SSM_NGB = SSM_GROUPS // SSM_GB
SSM_XW = SSM_T * LANES
SSM_SW = SSM_GB * SSM_S2
SSM_TPT = MXU_TILE // LANES
SSM_NT = SSM_T // SSM_TPT
SCAN_ROWS = 8
N_SCAN_TABLES = 10
SSM_CHUNKS_PER_STEP = 256

F32 = jnp.float32
BF16 = jnp.bfloat16


def _params(*sem):
    return pltpu.CompilerParams(dimension_semantics=sem, vmem_limit_bytes=VMEM_LIMIT)


def _resident(shape):
    zeros = (0,) * len(shape)
    return pl.BlockSpec(shape, lambda *_: zeros, pipeline_mode=pl.Buffered(1))


def _rms(x, g):
    return x * lax.rsqrt(jnp.mean(x * x, axis=-1, keepdims=True) + RMS_EPS) * g


def _mm(a, b):
    return jnp.dot(a, b, preferred_element_type=F32)


def _sigmoid(x):
    return 0.5 * jnp.tanh(0.5 * x) + 0.5


def _row_parts(ref, n_parts):
    rows = ref.shape[0] // n_parts
    return [ref[i * rows:(i + 1) * rows, :] for i in range(n_parts)]


def _ffn(xs, gpre, gpost, w_in_ref, w_out_ref):
    hs = [_rms(x, gpre).astype(BF16) for x in xs]
    acts = [[] for _ in xs]
    for c0 in range(0, D_FF, FFN_CHUNK):
        c1 = min(c0 + FFN_CHUNK, D_FF)
        for h, a in zip(hs, acts):
            g = _mm(h, w_in_ref[:, c0:c1])
            u = _mm(h, w_in_ref[:, D_FF + c0:D_FF + c1])
            a.append((g * jax.nn.sigmoid(g) * u).astype(BF16))
    outs = [_mm(jnp.concatenate(a, axis=1), w_out_ref[...]) for a in acts]
    return [x + 0.5 * _rms(o, gpost) for x, o in zip(xs, outs)]


PROJ_SPLIT = (("q", ATTN_W, BF16), ("k", KV_W, F32), ("v", KV_W, F32), ("u", SSM_W, F32),
              ("qm", MEM_W, BF16), ("k2", KV2_W, BF16), ("v2", KV2_W, BF16))
N_PROJ_MATMUL = 5
PROJ_COLS = sum(w for _, w, _ in PROJ_SPLIT[:N_PROJ_MATMUL])
Q_SCALE = HEAD_DIM ** -0.5
assert math.frexp(Q_SCALE)[0] == 0.5, "power of two: scaling q before the bf16 rounding and the dot is exact"


def _twice_per_head_lanes(x):
    assert N_KV_HEADS == 2 and KV_W == LANES
    swapped = pltpu.roll(x, HEAD_DIM, 1)
    low = lax.broadcasted_iota(jnp.int32, x.shape, 1) < HEAD_DIM
    return jnp.concatenate([jnp.where(low, x, swapped), jnp.where(low, swapped, x)], axis=1)


def _two_group_specs(steps_a):
    first = lambda w: pl.BlockSpec((FFN_ROW_TILE, w), lambda i: (jnp.minimum(i, steps_a - 1), 0))
    second = lambda w: pl.BlockSpec((FFN_ROW_TILE, w), lambda i: (jnp.maximum(i - steps_a, 0), 0))
    return first, second


def _on_group(steps_a, run, refs_a, refs_b):
    i = pl.program_id(0)

    @pl.when(i < steps_a)
    def _():
        run(*refs_a)

    @pl.when(i >= steps_a)
    def _():
        run(*refs_b)


def _ffn1_proj_body(steps_a, n_later, xa_ref, xb_ref, gpre_ref, gpost_ref, wfi_ref, wfo_ref, gmix_ref, wp_ref,
                    *refs):
    later_f32, out_refs, later_bf16 = refs[:n_later], refs[n_later:len(refs) - n_later], refs[len(refs) - n_later:]
    _cast_chunks(later_f32, later_bf16)

    def run(x_ref, x1_ref, *proj_refs):
        x1s = _ffn(_row_parts(x_ref, FFN_PARTS), gpre_ref[...], gpost_ref[...], wfi_ref, wfo_ref)
        rows = x_ref.shape[0] // FFN_PARTS
        ps = [_mm(_rms(x1, gmix_ref[...]).astype(BF16), wp_ref[...]) for x1 in x1s]
        for i, (x1, p) in enumerate(zip(x1s, ps)):
            part = slice(i * rows, (i + 1) * rows)
            x1_ref[part, :] = x1
            col, cols = 0, {}
            for name, width, _ in PROJ_SPLIT[:N_PROJ_MATMUL]:
                cols[name] = p[:, col:col + width]
                col += width
            cols["q"] = cols["q"] * Q_SCALE
            cols["k2"], cols["v2"] = _twice_per_head_lanes(cols["k"]), _twice_per_head_lanes(cols["v"])
            for ref, (name, _, dtype) in zip(proj_refs, PROJ_SPLIT):
                ref[part, :] = cols[name].astype(dtype)

    n_out = len(out_refs) // 2
    _on_group(steps_a, run, (xa_ref,) + out_refs[:n_out], (xb_ref,) + out_refs[n_out:])


def _row_chunk_spec(w, steps, cols=None):
    rows = w.shape[0]
    n = next(n for n in range(min(steps, rows // BF16_ROWS), 0, -1)
             if rows % n == 0 and (rows // n) % BF16_ROWS == 0)
    return pl.BlockSpec((rows // n, cols or w.shape[1]), lambda i: (jnp.minimum(i, n - 1), 0))


def _cast_chunks(f32_refs, bf16_refs):
    for src, dst in zip(f32_refs, bf16_refs):
        dst[...] = src[...].astype(BF16)


def _ffn1_proj(xa, xb, gpre, gpost, wfi, wfo, gmix, wp, later_weights):
    steps_a, steps_b = xa.shape[0] // FFN_ROW_TILE, xb.shape[0] // FFN_ROW_TILE
    first, second = _two_group_specs(steps_a)
    widths = [D_MODEL] + [w for _, w, _ in PROJ_SPLIT]
    dtypes = [F32] + [d for _, _, d in PROJ_SPLIT]
    chunk_specs = [_row_chunk_spec(w, steps_a + steps_b) for w in later_weights]
    outs = pl.pallas_call(
        functools.partial(_ffn1_proj_body, steps_a, len(later_weights)),
        grid=(steps_a + steps_b,),
        in_specs=[first(D_MODEL), second(D_MODEL), _resident((1, D_MODEL)), _resident((1, D_MODEL)),
                  _resident(wfi.shape), _resident(wfo.shape), _resident((1, D_MODEL)),
                  _resident(wp.shape)] + chunk_specs,
        out_specs=[first(w) for w in widths] + [second(w) for w in widths] + chunk_specs,
        out_shape=[jax.ShapeDtypeStruct((x.shape[0], w), d) for x in (xa, xb) for w, d in zip(widths, dtypes)]
        + [jax.ShapeDtypeStruct(w.shape, BF16) for w in later_weights],
        compiler_params=_params("arbitrary"),
        name="ffn1_proj",
    )(xa, xb, gpre, gpost, wfi, wfo, gmix, wp, *later_weights)
    n = len(widths)
    return outs[:n], outs[n:2 * n], outs[2 * n:]


def _mem_kv_body(m_ref, g_ref, w_ref, k_ref, v_ref):
    kv = _mm(_rms(m_ref[...], g_ref[...]).astype(BF16), w_ref[...])
    n = m_ref.shape[0]
    for h in range(MEM_HEADS):
        k_ref[pl.ds(h, n, stride=MEM_HEADS), :] = kv[:, h * MEM_HEAD_DIM:(h + 1) * MEM_HEAD_DIM]
        v_ref[pl.ds(h, n, stride=MEM_HEADS), :] = kv[:, MEM_W + h * MEM_HEAD_DIM:MEM_W + (h + 1) * MEM_HEAD_DIM]


def _mem_kv(mem, g, w):
    n = mem.shape[0]
    out = pl.BlockSpec((ROW_TILE * MEM_HEADS, MEM_HEAD_DIM), lambda i: (i, 0))
    return pl.pallas_call(
        _mem_kv_body,
        grid=(n // ROW_TILE,),
        in_specs=[pl.BlockSpec((ROW_TILE, D_MODEL), lambda i: (i, 0)), _resident((1, D_MODEL)), _resident(w.shape)],
        out_specs=[out, out],
        out_shape=[jax.ShapeDtypeStruct((n * MEM_HEADS, MEM_HEAD_DIM), F32)] * 2,
        compiler_params=_params("parallel"),
        name="mem_kv",
    )(mem, g, w)


def _qk(q, k):
    return lax.dot_general(q, k, (((1,), (1,)), ((), ())), preferred_element_type=F32)


def _softmax_pv(scores, values, sinks=None):
    probs = []
    for i, s in enumerate(scores):
        m = jnp.max(s, axis=-1, keepdims=True)
        if sinks is not None:
            m = jnp.maximum(m, sinks[i])
        e = jnp.exp(s - m)
        den = jnp.sum(e, axis=-1, keepdims=True)
        if sinks is not None:
            den = den + jnp.exp(sinks[i] - m)
        probs.append((e * (1.0 / den)).astype(BF16))
    return [_mm(p, v) for p, v in zip(probs, values)]


def _gqa_queries(q):
    nq = q.shape[0]
    low = lax.broadcasted_iota(jnp.int32, (nq, LANES), 1) < HEAD_DIM
    zero = jnp.zeros((nq, LANES), BF16)
    stacks = []
    for g in range(N_KV_HEADS):
        rows = []
        for r in range(KV_REP):
            h = g * KV_REP + r
            q2 = q[:, (h // 2) * LANES:(h // 2 + 1) * LANES]
            rows.append(jnp.where(low, q2, zero) if h % 2 == 0 else jnp.where(low, zero, q2))
        stacks.append(jnp.concatenate(rows, axis=0))
    return stacks


def _gqa_outputs(outs, nq):
    low = lax.broadcasted_iota(jnp.int32, (nq, LANES), 1) < HEAD_DIM
    pairs = []
    for o in outs:
        for r in range(0, KV_REP, 2):
            pairs.append(jnp.where(low, o[r * nq:(r + 1) * nq], o[(r + 1) * nq:(r + 2) * nq]))
    return jnp.concatenate(pairs, axis=1)


def _mem_scores(qm, k_head):
    return [_qk(qm[:, h * MEM_HEAD_DIM:(h + 1) * MEM_HEAD_DIM], k_head(h).astype(BF16)) * (MEM_HEAD_DIM ** -0.5)
            for h in range(MEM_HEADS)]


def _mem_values(v_head):
    return [v_head(h).astype(BF16) for h in range(MEM_HEADS)]


def _row_heads(ref, i):
    return lambda h: ref[i, pl.ds(h, MEM_LEN, stride=MEM_HEADS), :]


def _kv_heads(x):
    return [x[:, g * LANES:(g + 1) * LANES] for g in range(N_KV_HEADS)]


def _merge_pre(x1, y, gmix_ref, wg_ref, wglu_ref):
    h = _rms(x1, gmix_ref[...]).astype(BF16)
    logits = [_mm(h, wg_ref[:, PROJ_COLS + j * D_MODEL:PROJ_COLS + (j + 1) * D_MODEL]) for j in range(3)]
    y = y.astype(BF16)
    return logits, _mm(y, wglu_ref[:, :D_MODEL]), _mm(y, wglu_ref[:, D_MODEL:])


def _mix_parts(parts, x1_ref, y_ref, gmix_ref, wg_ref, wglu_ref, wab_ref, wmb_ref, wo_ref, gpost_ref, x2_ref):
    x1s, ys = _row_parts(x1_ref, len(parts)), _row_parts(y_ref, len(parts))
    pres = [_merge_pre(x1, y, gmix_ref, wg_ref, wglu_ref) for x1, y in zip(x1s, ys)]
    branches = [assemble(_softmax_pv(scores, values, sinks), _softmax_pv(mem_scores, mem_values))
                for scores, values, sinks, mem_scores, mem_values, assemble in parts]
    projected = [(_mm(attn.astype(BF16), wab_ref[...]), _mm(memo.astype(BF16), wmb_ref[...]))
                 for attn, memo in branches]
    sig = _sigmoid
    merged = [sig(lg[0]) * pa + sig(lg[1]) * (ya * sig(yb)) + sig(lg[2]) * pm
              for (lg, ya, yb), (pa, pm) in zip(pres, projected)]
    outs = [_mm(m.astype(BF16), wo_ref[...]) for m in merged]
    rows = x1_ref.shape[0] // len(parts)
    for i, (x1, o) in enumerate(zip(x1s, outs)):
        x2_ref[i * rows:(i + 1) * rows, :] = x1 + _rms(o, gpost_ref[...])


def _mixer_prompt_body(q_ref, k_ref, kh_ref, v_ref, vh_ref, bias_ref, sink_ref, qm_ref, mk_ref, mv_ref,
                       x1_ref, y_ref, *merge_and_out_refs):
    i = pl.program_id(1)
    tq = q_ref.shape[1]
    kf = jnp.concatenate([kh_ref[0], k_ref[0]], axis=0)
    vf = jnp.concatenate([vh_ref[0], v_ref[0]], axis=0)
    band = WINDOW + CHUNK
    rows = tq // MIXER_PARTS
    parts = []
    for p in range(MIXER_PARTS):
        scores, values, sinks = [], [], []
        for lo in range(p * rows, (p + 1) * rows, CHUNK):
            valid = None
            if lo < WINDOW:
                key_pos = lax.broadcasted_iota(jnp.int32, (1, band), 1) + (i * tq + lo - WINDOW)
                valid = key_pos >= 0
            kb, vb = _kv_heads(kf[lo:lo + band]), _kv_heads(vf[lo:lo + band])
            for g, qs in enumerate(_gqa_queries(q_ref[0, lo:lo + CHUNK, :])):
                s = _qk(qs, kb[g]) + bias_ref[g]
                scores.append(s if valid is None else jnp.where(valid, s, NEG_INF))
                values.append(vb[g])
                sinks.append(sink_ref[g])
        mem_scores = _mem_scores(qm_ref[0, p * rows:(p + 1) * rows, :], _row_heads(mk_ref, 0))

        def assemble(outs, mem):
            attn = jnp.concatenate([_gqa_outputs(outs[c:c + N_KV_HEADS], CHUNK)
                                    for c in range(0, len(outs), N_KV_HEADS)], axis=0)
            return attn, jnp.concatenate(mem, axis=1)

        parts.append((scores, values, sinks, mem_scores, _mem_values(_row_heads(mv_ref, 0)), assemble))
    _mix_parts(parts, x1_ref, y_ref, *merge_and_out_refs)


def _merge_weight_specs(weights):
    return [_resident(w.shape) for w in weights]


def _mixer_prompt(q, k2, v2, bias, sink, qm, mk, mv, x1, y, merge_w):
    b, l, _ = q.shape
    tq = MIXER_TILE
    hb = tq // WINDOW
    nt = l // tq
    cur = lambda w: pl.BlockSpec((1, tq, w), lambda bi, i: (bi, i, 0))
    row = lambda w: pl.BlockSpec((tq, w), lambda bi, i: (bi * nt + i, 0))
    halo = pl.BlockSpec((1, WINDOW, KV2_W), lambda bi, i: (bi, jnp.maximum(i * hb - 1, 0), 0))
    mem = pl.BlockSpec((1, MEM_LEN * MEM_HEADS, MEM_HEAD_DIM), lambda bi, i: (bi, 0, 0))
    return pl.pallas_call(
        _mixer_prompt_body,
        grid=(b, nt),
        in_specs=[cur(ATTN_W), cur(KV2_W), halo, cur(KV2_W), halo, _resident(bias.shape),
                  _resident(sink.shape), cur(MEM_W), mem, mem, row(D_MODEL), row(SSM_W)]
        + _merge_weight_specs(merge_w),
        out_specs=row(D_MODEL),
        out_shape=jax.ShapeDtypeStruct((b * l, D_MODEL), F32),
        compiler_params=_params("parallel", "parallel"),
        name="mixer_prompt",
    )(q, k2, k2, v2, v2, bias, sink, qm, mk, mv, x1, y, *merge_w)


def _mixer_sample_body(q_ref, k_ref, v_ref, bias_ref, sink_ref, qm_ref, mk_ref, mv_ref,
                       x1_ref, y_ref, *merge_and_out_refs):
    nb, nq = q_ref.shape[0], q_ref.shape[1]
    per_part = nb // MIXER_PARTS
    parts = []
    for p in range(MIXER_PARTS):
        scores, values, sinks, mem_scores, mem_values = [], [], [], [], []
        for b in range(p * per_part, (p + 1) * per_part):
            kb, vb = _kv_heads(k_ref[b]), _kv_heads(v_ref[b])
            for g, qs in enumerate(_gqa_queries(q_ref[b])):
                scores.append(_qk(qs, kb[g]) + bias_ref[g])
                values.append(vb[g])
                sinks.append(sink_ref[g])
            mem_scores += _mem_scores(qm_ref[b], _row_heads(mk_ref, b))
            mem_values += _mem_values(_row_heads(mv_ref, b))

        def assemble(outs, mem):
            attn = jnp.concatenate([_gqa_outputs(outs[j:j + N_KV_HEADS], nq)
                                    for j in range(0, len(outs), N_KV_HEADS)], axis=0)
            memo = jnp.concatenate([jnp.concatenate(mem[j:j + MEM_HEADS], axis=1)
                                    for j in range(0, len(mem), MEM_HEADS)], axis=0)
            return attn, memo

        parts.append((scores, values, sinks, mem_scores, mem_values, assemble))
    _mix_parts(parts, x1_ref, y_ref, *merge_and_out_refs)


def _mixer_sample(q, kk2, vv2, bias, sink, qm, mk, mv, x1, y, merge_w):
    b, s, _ = q.shape
    nb = SAMPLE_MIXER_TILE // s
    nk = kk2.shape[1]
    blk = lambda *shape: pl.BlockSpec((nb,) + shape, lambda i: (i,) + (0,) * len(shape))
    row = lambda w: pl.BlockSpec((nb * s, w), lambda i: (i, 0))
    return pl.pallas_call(
        _mixer_sample_body,
        grid=(b // nb,),
        in_specs=[blk(s, ATTN_W), blk(nk, KV2_W), blk(nk, KV2_W),
                  _resident(bias.shape), _resident(sink.shape),
                  blk(s, MEM_W), blk(*mk.shape[1:]), blk(*mv.shape[1:]), row(D_MODEL), row(SSM_W)]
        + _merge_weight_specs(merge_w),
        out_specs=row(D_MODEL),
        out_shape=jax.ShapeDtypeStruct((b * s, D_MODEL), F32),
        compiler_params=_params("parallel"),
        name="mixer_sample",
    )(q, kk2, vv2, bias, sink, qm, mk, mv, x1, y, *merge_w)


def _ssm_weights_body(n_first, lam_re_row, lam_im_row, lam_re_col, lam_im_col, log_dt, b_re_t, b_im_t,
                      c_re, c_im, c_re_lanes, c_im_lanes, sel_ref, *refs):
    first_f32, (wconv_ref, wst_ref, wot_ref, a_tab_ref), first_bf16 = (
        refs[:n_first], refs[n_first:n_first + 4], refs[n_first + 4:])
    _cast_chunks(first_f32, first_bf16)
    _ssm_block_weights(lam_re_row, lam_im_row, lam_re_col, lam_im_col, log_dt, b_re_t, b_im_t,
                       c_re, c_im, c_re_lanes, c_im_lanes, sel_ref, wconv_ref, wst_ref, wot_ref, a_tab_ref)


def _ssm_block_weights(lam_re_row, lam_im_row, lam_re_col, lam_im_col, log_dt, b_re_t, b_im_t,
                       c_re, c_im, c_re_lanes, c_im_lanes, sel_ref,
                       wconv_ref, wst_ref, wot_ref, a_tab_ref):
    wst_ref[0] = jnp.zeros(wst_ref.shape[1:], BF16)
    wot_ref[0] = jnp.zeros(wot_ref.shape[1:], BF16)
    krows = []
    for gl in range(SSM_GB):
        krows.append(_ssm_group_weights(
            gl, *(r[gl] for r in (lam_re_row, lam_im_row, lam_re_col, lam_im_col, log_dt, b_re_t, b_im_t,
                                  c_re, c_im, c_re_lanes, c_im_lanes)),
            wst_ref, wot_ref, a_tab_ref))
    kstack = jnp.concatenate(krows, axis=0).astype(BF16)
    for d in range(SSM_NT):
        for ti in range(SSM_TPT):
            blk = _mm(kstack, sel_ref[d * SSM_TPT + ti])
            r0 = (SSM_NT - 1 - d) * MXU_TILE + ti * LANES
            for gl in range(SSM_GB):
                piece = blk[gl * SSM_GROUP:(gl + 1) * SSM_GROUP]
                piece = (piece if gl == 0 else pltpu.roll(piece, gl * SSM_GROUP, 1)).astype(BF16)
                r = r0 + gl * SSM_GROUP
                wconv_ref[0, r:r + SSM_GROUP, :MXU_TILE] = piece
                if r0 >= MXU_TILE:
                    wconv_ref[0, r - MXU_TILE:r - MXU_TILE + SSM_GROUP, MXU_TILE:] = piece
    wconv_ref[0, (SSM_NT - 1) * MXU_TILE:, MXU_TILE:] = jnp.zeros((MXU_TILE, MXU_TILE), BF16)


def _ssm_group_weights(gl, lam_re_row, lam_im_row, lam_re_col, lam_im_col, log_dt, b_re_t, b_im_t,
                       c_re, c_im, c_re_lanes, c_im_lanes,
                       wst_ref, wot_ref, a_tab_ref):
    dt = jnp.exp(log_dt)
    rows = lambda t: slice(t * LANES + gl * SSM_GROUP, t * LANES + (gl + 1) * SSM_GROUP)
    lanes = slice(gl * SSM_S2, (gl + 1) * SSM_S2)

    def zoh_coef(lr, li):
        mag = jnp.exp(lr * dt)
        a_re, a_im = mag * jnp.cos(li * dt), mag * jnp.sin(li * dt)
        den = lr * lr + li * li
        return ((a_re - 1.0) * lr + a_im * li) / den, (a_im * lr - (a_re - 1.0) * li) / den

    def a_power(lr, li, n):
        mag = jnp.exp(lr * dt * n)
        return mag * jnp.cos(li * dt * n), mag * jnp.sin(li * dt * n)

    lr, li = lam_re_row, lam_im_row
    cr, ci = zoh_coef(lr, li)
    n_rows = jnp.minimum(lax.broadcasted_iota(jnp.int32, (SSM_T + SCAN_ROWS, 1), 0), SSM_T).astype(F32)
    pw_r, pw_i = a_power(lr, li, n_rows)

    def over_channels(pw, exps):
        return jnp.concatenate([jnp.broadcast_to(pw[e:e + 1], (SSM_GROUP, SSM_STATE)) for e in exps], axis=0)

    back = [SSM_T - 1 - t for t in range(SSM_T)]
    pr, pi = over_channels(pw_r, back), over_channels(pw_i, back)
    zr, zi = pr * cr - pi * ci, pr * ci + pi * cr
    over_positions = lambda x: jnp.concatenate([x] * SSM_T, axis=0)
    br, bi = over_positions(b_re_t), over_positions(b_im_t)
    wstate = jnp.concatenate([zr * br - zi * bi, zr * bi + zi * br], axis=1).astype(BF16)

    fwd = [t + 1 for t in range(SSM_T)]
    qr, qi = over_channels(pw_r, fwd), over_channels(pw_i, fwd)
    ccr, cci = over_positions(c_re), over_positions(c_im)
    wout_t = jnp.concatenate([ccr * qr - cci * qi, -(ccr * qi + cci * qr)], axis=1).astype(BF16)
    for t in range(SSM_T):
        wst_ref[0, rows(t), lanes] = wstate[t * SSM_GROUP:(t + 1) * SSM_GROUP]
        wot_ref[0, rows(t), lanes] = wout_t[t * SSM_GROUP:(t + 1) * SSM_GROUP]

    lrc, lic = lam_re_col, lam_im_col
    hi = lax.Precision.HIGHEST
    n_lanes = jnp.minimum(lax.broadcasted_iota(jnp.int32, (1, LANES), 1), SSM_T).astype(F32)
    pc_r, pc_i = a_power(lrc, lic, n_lanes)
    spread = (lax.broadcasted_iota(jnp.int32, (LANES, SSM_K), 0)
              == lax.broadcasted_iota(jnp.int32, (LANES, SSM_K), 1) // SSM_GROUP).astype(F32)
    gr = jnp.dot(pc_r, spread, precision=hi, preferred_element_type=F32)
    gi = jnp.dot(pc_i, spread, precision=hi, preferred_element_type=F32)
    clr, cli = c_re_lanes, c_im_lanes
    g_re, g_im = clr * gr - cli * gi, clr * gi + cli * gr
    btr, bti = b_re_t, b_im_t
    bbr, bbi = cr * btr - ci * bti, cr * bti + ci * btr
    krow = (jnp.dot(bbr, g_re, precision=hi, preferred_element_type=F32)
            - jnp.dot(bbi, g_im, precision=hi, preferred_element_type=F32))

    idx = lax.broadcasted_iota(jnp.int32, (N_SCAN_TABLES * SCAN_ROWS, SSM_STATE), 0)
    tab, r = idx // SCAN_ROWS, idx % SCAN_ROWS
    stride = jnp.where(tab < 2, 1, jnp.where(tab < 4, 2, 4))
    n = jnp.where(tab < 6, stride, jnp.where(tab < 8, r, SCAN_ROWS))
    keep = jnp.logical_or(tab >= 6, r >= stride)
    er, ei = a_power(lr, li, (n * SSM_T).astype(F32))
    er, ei = jnp.where(keep, er, 0.0), jnp.where(keep, ei, 0.0)
    odd = tab % 2 == 1
    tabs = jnp.concatenate([jnp.where(odd, -ei, er), jnp.where(odd, ei, er)], axis=1)
    a_tab_ref[:, :, lanes] = tabs.reshape(N_SCAN_TABLES, SCAN_ROWS, SSM_S2)
    return krow


def _lag_selectors():
    sel = np.zeros((SSM_NT, SSM_TPT, SSM_K, MXU_TILE), np.float32)
    ch = np.arange(SSM_GROUP)
    for d in range(SSM_NT):
        for ti in range(SSM_TPT):
            for to in range(SSM_TPT):
                lag = SSM_TPT * d + to - ti
                if lag >= 0:
                    sel[d, ti, lag * SSM_GROUP + ch, to * LANES + ch] = 1.0
    return jnp.asarray(sel.reshape(SSM_NT * SSM_TPT, SSM_K, MXU_TILE), BF16)


def _ssm_weights(lam_re, lam_im, log_dt, b_re, b_im, c_re, c_im, first_weights):
    g, p, c = b_re.shape
    row3 = lambda x: x.reshape(g, 1, p)
    col3 = lambda x: x.reshape(g, p, 1)
    t3 = lambda x: jnp.transpose(x, (0, 2, 1))
    c_lanes = lambda x: jnp.tile(t3(x), (1, 1, SSM_T))
    args = (row3(lam_re), row3(lam_im), col3(lam_re), col3(lam_im), log_dt.reshape(g, 1, 1),
            t3(b_re), t3(b_im), c_re, c_im, c_lanes(c_re), c_lanes(c_im))
    sel = _lag_selectors()
    spec = lambda x: pl.BlockSpec((SSM_GB,) + x.shape[1:], lambda i: (i, 0, 0))
    per_gb = lambda *shape: pl.BlockSpec((1,) + shape, lambda i: (i, 0, 0))
    chunk_specs = [_row_chunk_spec(w, SSM_NGB, cols) for w, cols in first_weights]
    outs = pl.pallas_call(
        functools.partial(_ssm_weights_body, len(first_weights)),
        grid=(SSM_NGB,),
        in_specs=[spec(a) for a in args] + [_resident(sel.shape)] + chunk_specs,
        out_specs=[per_gb(SSM_NT * MXU_TILE, 2 * MXU_TILE), per_gb(SSM_XW, SSM_SW), per_gb(SSM_XW, SSM_SW),
                   pl.BlockSpec((N_SCAN_TABLES, SCAN_ROWS, SSM_SW), lambda i: (0, 0, i))] + chunk_specs,
        out_shape=[jax.ShapeDtypeStruct((SSM_NGB, SSM_NT * MXU_TILE, 2 * MXU_TILE), BF16),
                   jax.ShapeDtypeStruct((SSM_NGB, SSM_XW, SSM_SW), BF16),
                   jax.ShapeDtypeStruct((SSM_NGB, SSM_XW, SSM_SW), BF16),
                   jax.ShapeDtypeStruct((N_SCAN_TABLES, SCAN_ROWS, SSM_WIDTH), F32)]
        + [jax.ShapeDtypeStruct((w.shape[0], cols or w.shape[1]), BF16) for w, cols in first_weights],
        compiler_params=_params("arbitrary"),
        name="ssm_weights",
    )(*args, sel, *[w for w, _ in first_weights])
    return outs[:4], outs[4:]


def _swap_halves(s):
    ax = s.ndim - 1
    return jnp.concatenate([pltpu.roll(s[..., l:l + SSM_S2], SSM_STATE, ax)
                            for l in range(0, s.shape[ax], SSM_S2)], axis=ax)


def _ssm_body(row_sets, slab_rows, u_ref, wconv_ref, wst_ref, wot_ref, d_ref, a_tab_ref, s0_ref,
              y_ref, s_ref, v_scr, s_scr):
    def piece(ref_set, first, n, stride, t):
        return ref_set, pl.ds(first + t, n, stride=stride)

    @pl.when(pl.program_id(1) == 0)
    def _():
        s_scr[...] = s0_ref[0]

    x = jnp.concatenate(
        [jnp.concatenate([u_ref[piece(*rs, t)] for t in range(SSM_T)], axis=1) for rs in row_sets],
        axis=0).astype(BF16)
    v_scr[...] = _mm(x, wst_ref[0])
    conv = []
    for jo in range(0, SSM_NT, 2):
        pair = _mm(x[:, :(jo + 2) * MXU_TILE], wconv_ref[0, (SSM_NT - 2 - jo) * MXU_TILE:, :])
        conv += [pair[:, MXU_TILE:], pair[:, :MXU_TILE]]

    n_rows = v_scr.shape[0]
    if slab_rows:
        a_mul, a_swap = a_tab_ref[6, 1:2, :], a_tab_ref[7, 1:2, :]
        s = s_scr[...]
        for c in range(n_rows // slab_rows):
            rows = slice(c * slab_rows, (c + 1) * slab_rows)
            inc = v_scr[rows, :]
            v_scr[rows, :] = s
            s = a_mul * s + a_swap * _swap_halves(s) + inc
        s_scr[...] = s
    else:
        n_blocks = n_rows // SCAN_ROWS
        xs = v_scr[...].reshape(n_blocks, SCAN_ROWS, SSM_SW)
        for k in range(3):
            sh = pltpu.roll(xs, 1 << k, 1)
            xs = xs + a_tab_ref[2 * k] * sh + a_tab_ref[2 * k + 1] * _swap_halves(sh)
        carries, blk = [], 0
        for j, (_, _, n, _) in enumerate(row_sets):
            carry = s_scr[j * SCAN_ROWS:(j + 1) * SCAN_ROWS, :]
            for _ in range(n // SCAN_ROWS):
                carries.append(carry)
                last = jnp.broadcast_to(xs[blk, SCAN_ROWS - 1:, :], carry.shape)
                carry = last + a_tab_ref[8] * carry + a_tab_ref[9] * _swap_halves(carry)
                blk += 1
            s_scr[j * SCAN_ROWS:(j + 1) * SCAN_ROWS, :] = carry
        carries = jnp.stack(carries, axis=0)
        not_first = lax.broadcasted_iota(jnp.int32, (1, SCAN_ROWS, 1), 1) >= 1
        enter = (jnp.where(not_first, pltpu.roll(xs, 1, 1), 0.0)
                 + a_tab_ref[6] * carries + a_tab_ref[7] * _swap_halves(carries))
        v_scr[...] = enter.reshape(n_rows, SSM_SW)
    s_ref[0] = s_scr[...]

    enter = v_scr[...].astype(BF16)
    d = d_ref[...]
    carried = _qk(enter, wot_ref[0])
    for jo in range(SSM_NT):
        yt = conv[jo] + carried[:, jo * MXU_TILE:(jo + 1) * MXU_TILE]
        r0 = 0
        for rs in row_sets:
            for tl in range(SSM_TPT):
                idx = piece(*rs, jo * SSM_TPT + tl)
                y_ref[idx] = yt[r0:r0 + rs[2], tl * LANES:(tl + 1) * LANES] + u_ref[idx] * d
            r0 += rs[2]


def _ssm(u, row_sets, slab_rows, block_rows, wconv, wst, wo, d_lanes, a_tab, s0):
    ns, r, _ = u.shape
    chunk_rows = sum(rs[2] for rs in row_sets)
    carry_rows = s0.shape[1]
    blk = pl.BlockSpec((ns, block_rows, LANES), lambda gb, i: (0, i, gb))
    per_gb = lambda x: pl.BlockSpec((1,) + x.shape[1:], lambda gb, i: (gb, 0, 0))
    return pl.pallas_call(
        functools.partial(_ssm_body, row_sets, slab_rows),
        grid=(SSM_NGB, r // block_rows),
        in_specs=[blk, per_gb(wconv), per_gb(wst), per_gb(wo),
                  pl.BlockSpec((1, LANES), lambda gb, i: (0, gb)),
                  pl.BlockSpec((N_SCAN_TABLES, SCAN_ROWS, SSM_SW), lambda gb, i: (0, 0, gb)),
                  per_gb(s0)],
        out_specs=[blk, per_gb(s0)],
        out_shape=[jax.ShapeDtypeStruct(u.shape, F32), jax.ShapeDtypeStruct(s0.shape, F32)],
        scratch_shapes=[pltpu.VMEM((chunk_rows, SSM_SW), F32), pltpu.VMEM((carry_rows, SSM_SW), F32)],
        compiler_params=_params("parallel", "arbitrary"),
        name="ssm_scan",
    )(u, wconv, wst, wo, d_lanes, a_tab, s0)


def _ssm_branch(u, s0, ssm_w):
    b, l, _ = u.shape
    nc = l // SSM_T
    by_gb = lambda s: s.reshape(s.shape[0], SSM_NGB, SSM_SW).transpose(1, 0, 2)
    if b % SCAN_ROWS == 0:
        row_sets = tuple((0, c * SSM_T, b, l) for c in range(nc))
        y, s_last = _ssm(u.reshape(1, b * l, SSM_W), row_sets, b, b * l, *ssm_w, by_gb(s0))
    else:
        cps = min(nc, SSM_CHUNKS_PER_STEP)
        row_sets = tuple((j, 0, cps, SSM_T) for j in range(b))
        y, s_last = _ssm(u, row_sets, 0, cps * SSM_T, *ssm_w, by_gb(jnp.repeat(s0, SCAN_ROWS, axis=0)))
        s_last = s_last[:, ::SCAN_ROWS]
    s_last = s_last.transpose(1, 0, 2).reshape(b, SSM_GROUPS, 2, SSM_STATE)
    return y.reshape(b, l, SSM_W), s_last[:, :, 0], s_last[:, :, 1]


def _ffn2_body(steps_a, xa_ref, xb_ref, gpre_ref, gpost_ref, wfi_ref, wfo_ref, oa_ref, ob_ref):
    def run(x_ref, o_ref):
        outs = _ffn(_row_parts(x_ref, FFN_PARTS), gpre_ref[...], gpost_ref[...], wfi_ref, wfo_ref)
        rows = x_ref.shape[0] // FFN_PARTS
        for i, o in enumerate(outs):
            o_ref[i * rows:(i + 1) * rows, :] = o

    _on_group(steps_a, run, (xa_ref, oa_ref), (xb_ref, ob_ref))


def _ffn2(xa, xb, gpre, gpost, wfi, wfo):
    steps_a, steps_b = xa.shape[0] // FFN_ROW_TILE, xb.shape[0] // FFN_ROW_TILE
    first, second = _two_group_specs(steps_a)
    vec = _resident((1, D_MODEL))
    return pl.pallas_call(
        functools.partial(_ffn2_body, steps_a),
        grid=(steps_a + steps_b,),
        in_specs=[first(D_MODEL), second(D_MODEL), vec, vec, _resident(wfi.shape), _resident(wfo.shape)],
        out_specs=[first(D_MODEL), second(D_MODEL)],
        out_shape=[jax.ShapeDtypeStruct(xa.shape, F32), jax.ShapeDtypeStruct(xb.shape, F32)],
        compiler_params=_params("arbitrary"),
        name="ffn2",
    )(xa, xb, gpre, gpost, wfi, wfo)


def _t5_bucket(rel):
    half = N_BUCKETS // 2
    max_exact = half // 2
    ret = (rel > 0).astype(np.int32) * half
    n = np.abs(rel)
    large = max_exact + (np.log(np.maximum(n, 1) / max_exact) / math.log(MAX_DISTANCE / max_exact)
                         * (half - max_exact)).astype(np.int32)
    large = np.minimum(large, half - 1)
    return ret + np.where(n < max_exact, n, large)


def _band_bias(rel_table, n_q, n_back, n_k):
    i = np.arange(n_q)[:, None]
    j = np.arange(n_k)[None, :]
    bucket = _t5_bucket((j - n_back) - i).reshape(-1)
    onehot = np.zeros((N_BUCKETS, bucket.size), np.float32)
    onehot[bucket, np.arange(bucket.size)] = 1.0
    b = jnp.dot(rel_table.astype(F32).T, jnp.asarray(onehot), precision=lax.Precision.HIGHEST)
    return b.reshape(N_KV_HEADS, KV_REP * n_q, n_k)


def _sink_rows(sink, n_q):
    return jnp.repeat(sink.astype(F32).reshape(N_KV_HEADS, KV_REP), n_q, axis=1)[:, :, None]


def _twice_per_head(x, axis):
    shape = x.shape
    x = x.reshape(shape[:axis] + (N_KV_HEADS, 1, HEAD_DIM) + shape[axis + 1:])
    x = jnp.concatenate([x, x], axis=axis + 1)
    return x.reshape(shape[:axis] + (KV2_W,) + shape[axis + 1:])


def kernel(x_prompt, x_sample, cache_swa_k, cache_swa_v, cache_mem_k, cache_mem_v, state_ssm_re, state_ssm_im, mem_prompt, rel_bias_table, ff1_pre_g, ff1_post_g, w_ff1_in, w_ff1_out, mix_pre_g, mix_post_g, w_in, mem_norm_g, w_mem_kv, attn_sink, ssm_lambda_re, ssm_lambda_im, ssm_log_dt, ssm_b_re, ssm_b_im, ssm_c_re, ssm_c_im, ssm_d, w_ssm_glu, w_attn_br, w_mem_br, w_out, ff2_pre_g, ff2_post_g, w_ff2_in, w_ff2_out):
    assert ff1_pre_g.shape[0] == 1, "single-layer step"
    bp, lp, _ = x_prompt.shape
    bs, ls, _ = x_sample.shape
    vec = lambda g: g[0].reshape(1, D_MODEL).astype(F32)

    (wconv, wst, wot, a_tab), (wfi1, wfo1, wp) = _ssm_weights(
        ssm_lambda_re[0], ssm_lambda_im[0], ssm_log_dt[0], ssm_b_re[0], ssm_b_im[0], ssm_c_re[0], ssm_c_im[0],
        [(w_ff1_in[0], None), (w_ff1_out[0], None), (w_in[0], PROJ_COLS)])
    ssm_w = (wconv, wst, wot, ssm_d[0].astype(F32).reshape(1, SSM_W), a_tab)

    later = (w_ff2_in, w_ff2_out, w_in, w_ssm_glu, w_attn_br, w_mem_br, w_out, w_mem_kv)
    proj_p, proj_s, (wfi2, wfo2, w_in16, wglu, wab, wmb, wo, wmkv) = _ffn1_proj(
        x_prompt.reshape(bp * lp, D_MODEL), x_sample.reshape(bs * ls, D_MODEL), vec(ff1_pre_g), vec(ff1_post_g),
        wfi1, wfo1, vec(mix_pre_g), wp, [w[0] for w in later])

    mk_p, mv_p = _mem_kv(mem_prompt.reshape(bp * MEM_LEN, D_MODEL), vec(mem_norm_g), wmkv)

    def mix(x_shape, proj, mixer, s0, mem_k, mem_v):
        b, l, _ = x_shape
        r3 = lambda t: t.reshape(b, l, t.shape[-1])
        x1, q, k, v, u, qm, k2, v2 = proj
        y_ssm, s_re, s_im = _ssm_branch(r3(u), s0, ssm_w)
        x2 = mixer(r3(q), r3(k2), r3(v2), r3(qm), mem_k, mem_v, x1, y_ssm.reshape(b * l, SSM_W))
        return x2, r3(k), r3(v), s_re, s_im

    merge_w = (vec(mix_pre_g), w_in16, wglu, wab, wmb, wo, vec(mix_post_g))

    def mixer_prompt(q, k2, v2, qm, mem_k, mem_v, x1, y_ssm):
        bias = _band_bias(rel_bias_table, CHUNK, WINDOW, WINDOW + CHUNK)
        return _mixer_prompt(q, k2, v2, bias, _sink_rows(attn_sink[0], CHUNK), qm, mem_k, mem_v, x1, y_ssm, merge_w)

    def mixer_sample(q, k2, v2, qm, mem_k, mem_v, x1, y_ssm):
        n_back = cache_swa_k.shape[2]
        cache2 = lambda c: _twice_per_head(c[0].reshape(bs, n_back, KV_W).astype(BF16), 2)
        kk = jnp.concatenate([cache2(cache_swa_k), k2], axis=1)
        vv = jnp.concatenate([cache2(cache_swa_v), v2], axis=1)
        bias = _band_bias(rel_bias_table, ls, n_back, n_back + ls)
        return _mixer_sample(q, kk, vv, bias, _sink_rows(attn_sink[0], ls), qm, mem_k, mem_v, x1, y_ssm, merge_w)

    x2p, pk, pv, pre, pim = mix(x_prompt.shape, proj_p, mixer_prompt, jnp.zeros((bp, SSM_WIDTH), F32),
                                mk_p.reshape(bp, MEM_LEN * MEM_HEADS, MEM_HEAD_DIM),
                                mv_p.reshape(bp, MEM_LEN * MEM_HEADS, MEM_HEAD_DIM))
    s0 = jnp.stack([state_ssm_re[0], state_ssm_im[0]], axis=2).reshape(bs, SSM_WIDTH).astype(F32)
    x2s, sk, sv, sre, sim = mix(x_sample.shape, proj_s, mixer_sample, s0,
                                cache_mem_k[0].reshape(bs, MEM_LEN * MEM_HEADS, MEM_HEAD_DIM),
                                cache_mem_v[0].reshape(bs, MEM_LEN * MEM_HEADS, MEM_HEAD_DIM))
    yp, ys = _ffn2(x2p, x2s, vec(ff2_pre_g), vec(ff2_post_g), wfi2, wfo2)
    yp, ys = yp.reshape(x_prompt.shape), ys.reshape(x_sample.shape)

    n_keep = min(WINDOW, lp)
    heads = lambda t: t.reshape(t.shape[0], t.shape[1], N_KV_HEADS, HEAD_DIM)[None]
    mem_heads = lambda t: t.reshape(bp, MEM_LEN, MEM_HEADS, MEM_HEAD_DIM)[None]
    return (yp, ys, heads(pk[:, -n_keep:]), heads(pv[:, -n_keep:]), mem_heads(mk_p), mem_heads(mv_p),
            pre[None], pim[None], heads(sk), heads(sv), sre[None], sim[None])
```

```python
import functools
import math

import numpy as np
import jax
import jax.numpy as jnp
from jax import lax
from jax.experimental import pallas as pl
from jax.experimental.pallas import tpu as pltpu

D_MODEL = 1024
CHUNK = 64
WINDOW = 128
HEAD_DIM = 64
MIX_W = D_MODEL // 2
N_HEADS = MIX_W // HEAD_DIM
N_KV_HEADS = N_HEADS // 4
KV_REP = N_HEADS // N_KV_HEADS
ATTN_W = N_HEADS * HEAD_DIM
KV_W = N_KV_HEADS * HEAD_DIM
SSM_GROUP = 16
SSM_W = MIX_W
SSM_GROUPS = SSM_W // SSM_GROUP
SSM_STATE = 64
MEM_LEN = 256
MEM_HEADS = 4
MEM_HEAD_DIM = MIX_W // MEM_HEADS
MEM_W = MEM_HEADS * MEM_HEAD_DIM
D_FF = 128 * ((8 * D_MODEL // 3 + 127) // 128)
N_BUCKETS = 32
MAX_DISTANCE = 128
RMS_EPS = 1e-6
NEG_INF = -1e30

LANES = 128
BF16_ROWS = 16
MXU_TILE = 256
ROW_TILE = 256
MIXER_TILE = 512
SAMPLE_MIXER_TILE = 256
MIXER_PARTS = 2
FFN_ROW_TILE = 512
FFN_PARTS = 2
FFN_CHUNK = 2 * MXU_TILE
VMEM_LIMIT = 60 * 1024 * 1024

KV2_W = N_KV_HEADS * LANES

SSM_T = 8
SCAN_PAIR = 2
SCAN_T = SSM_T * SCAN_PAIR
SSM_K = SSM_T * SSM_GROUP
SSM_S2 = 2 * SSM_STATE
SSM_WIDTH = SSM_GROUPS * SSM_S2
SSM_GB = LANES // SSM_GROUP
SSM_NGB = SSM_GROUPS // SSM_GB
SSM_XW = SSM_T * LANES
SSM_SW = SSM_GB * SSM_S2
SSM_TPT = MXU_TILE // LANES
SSM_NT = SSM_T // SSM_TPT
SCAN_ROWS = 8
N_SCAN_TABLES = 12
SSM_CHUNKS_PER_STEP = 512

F32 = jnp.float32
BF16 = jnp.bfloat16


def _params(*sem):
    return pltpu.CompilerParams(dimension_semantics=sem, vmem_limit_bytes=VMEM_LIMIT)


def _resident(shape):
    zeros = (0,) * len(shape)
    return pl.BlockSpec(shape, lambda *_: zeros, pipeline_mode=pl.Buffered(1))


def _rms(x, g):
    return x * lax.rsqrt(jnp.mean(x * x, axis=-1, keepdims=True) + RMS_EPS) * g


def _mm(a, b):
    return jnp.dot(a, b, preferred_element_type=F32)


def _sigmoid(x):
    return 0.5 * jnp.tanh(0.5 * x) + 0.5


def _row_parts(ref, n_parts):
    rows = ref.shape[0] // n_parts
    return [ref[i * rows:(i + 1) * rows, :] for i in range(n_parts)]


def _ffn(xs, gpre, gpost, w_in_ref, w_out_ref):
    hs = [_rms(x, gpre).astype(BF16) for x in xs]
    acts = [[] for _ in xs]
    for c0 in range(0, D_FF, FFN_CHUNK):
        c1 = min(c0 + FFN_CHUNK, D_FF)
        for h, a in zip(hs, acts):
            g = _mm(h, w_in_ref[:, c0:c1])
            u = _mm(h, w_in_ref[:, D_FF + c0:D_FF + c1])
            a.append((g * jax.nn.sigmoid(g) * u).astype(BF16))
    outs = [_mm(jnp.concatenate(a, axis=1), w_out_ref[...]) for a in acts]
    return [x + 0.5 * _rms(o, gpost) for x, o in zip(xs, outs)]


PROJ_SPLIT = (("q", ATTN_W, BF16), ("k", KV_W, F32), ("v", KV_W, F32), ("u", SSM_W, F32),
              ("qm", MEM_W, BF16), ("k2", KV2_W, BF16), ("v2", KV2_W, BF16))
N_PROJ_MATMUL = 5
PROJ_COLS = sum(w for _, w, _ in PROJ_SPLIT[:N_PROJ_MATMUL])
Q_SCALE = HEAD_DIM ** -0.5
assert math.frexp(Q_SCALE)[0] == 0.5, "power of two: scaling q before the bf16 rounding and the dot is exact"


def _twice_per_head_lanes(x):
    assert N_KV_HEADS == 2 and KV_W == LANES
    swapped = pltpu.roll(x, HEAD_DIM, 1)
    low = lax.broadcasted_iota(jnp.int32, x.shape, 1) < HEAD_DIM
    return jnp.concatenate([jnp.where(low, x, swapped), jnp.where(low, swapped, x)], axis=1)


def _two_group_specs(steps_a):
    first = lambda w: pl.BlockSpec((FFN_ROW_TILE, w), lambda i: (jnp.minimum(i, steps_a - 1), 0))
    second = lambda w: pl.BlockSpec((FFN_ROW_TILE, w), lambda i: (jnp.maximum(i - steps_a, 0), 0))
    return first, second


def _on_group(steps_a, run, refs_a, refs_b):
    i = pl.program_id(0)

    @pl.when(i < steps_a)
    def _():
        run(*refs_a)

    @pl.when(i >= steps_a)
    def _():
        run(*refs_b)


def _ffn1_proj_body(steps_a, n_later, xa_ref, xb_ref, gpre_ref, gpost_ref, wfi_ref, wfo_ref, gmix_ref, wp_ref,
                    *refs):
    later_f32, out_refs, later_bf16 = refs[:n_later], refs[n_later:len(refs) - n_later], refs[len(refs) - n_later:]
    _cast_chunks(later_f32, later_bf16)

    def run(x_ref, x1_ref, *proj_refs):
        x1s = _ffn(_row_parts(x_ref, FFN_PARTS), gpre_ref[...], gpost_ref[...], wfi_ref, wfo_ref)
        rows = x_ref.shape[0] // FFN_PARTS
        ps = [_mm(_rms(x1, gmix_ref[...]).astype(BF16), wp_ref[...]) for x1 in x1s]
        for i, (x1, p) in enumerate(zip(x1s, ps)):
            part = slice(i * rows, (i + 1) * rows)
            x1_ref[part, :] = x1
            col, cols = 0, {}
            for name, width, _ in PROJ_SPLIT[:N_PROJ_MATMUL]:
                cols[name] = p[:, col:col + width]
                col += width
            cols["q"] = cols["q"] * Q_SCALE
            cols["k2"], cols["v2"] = _twice_per_head_lanes(cols["k"]), _twice_per_head_lanes(cols["v"])
            for ref, (name, _, dtype) in zip(proj_refs, PROJ_SPLIT):
                ref[part, :] = cols[name].astype(dtype)

    n_out = len(out_refs) // 2
    _on_group(steps_a, run, (xa_ref,) + out_refs[:n_out], (xb_ref,) + out_refs[n_out:])


def _row_chunk_spec(w, steps, cols=None):
    rows = w.shape[0]
    n = next(n for n in range(min(steps, rows // BF16_ROWS), 0, -1)
             if rows % n == 0 and (rows // n) % BF16_ROWS == 0)
    return pl.BlockSpec((rows // n, cols or w.shape[1]), lambda i: (jnp.minimum(i, n - 1), 0))


def _cast_chunks(f32_refs, bf16_refs):
    for src, dst in zip(f32_refs, bf16_refs):
        dst[...] = src[...].astype(BF16)


def _ffn1_proj(xa, xb, gpre, gpost, wfi, wfo, gmix, wp, later_weights):
    steps_a, steps_b = xa.shape[0] // FFN_ROW_TILE, xb.shape[0] // FFN_ROW_TILE
    first, second = _two_group_specs(steps_a)
    widths = [D_MODEL] + [w for _, w, _ in PROJ_SPLIT]
    dtypes = [F32] + [d for _, _, d in PROJ_SPLIT]
    chunk_specs = [_row_chunk_spec(w, steps_a + steps_b) for w in later_weights]
    outs = pl.pallas_call(
        functools.partial(_ffn1_proj_body, steps_a, len(later_weights)),
        grid=(steps_a + steps_b,),
        in_specs=[first(D_MODEL), second(D_MODEL), _resident((1, D_MODEL)), _resident((1, D_MODEL)),
                  _resident(wfi.shape), _resident(wfo.shape), _resident((1, D_MODEL)),
                  _resident(wp.shape)] + chunk_specs,
        out_specs=[first(w) for w in widths] + [second(w) for w in widths] + chunk_specs,
        out_shape=[jax.ShapeDtypeStruct((x.shape[0], w), d) for x in (xa, xb) for w, d in zip(widths, dtypes)]
        + [jax.ShapeDtypeStruct(w.shape, BF16) for w in later_weights],
        compiler_params=_params("arbitrary"),
        name="ffn1_proj",
    )(xa, xb, gpre, gpost, wfi, wfo, gmix, wp, *later_weights)
    n = len(widths)
    return outs[:n], outs[n:2 * n], outs[2 * n:]


def _mem_kv_body(m_ref, g_ref, w_ref, k_ref, v_ref):
    kv = _mm(_rms(m_ref[...], g_ref[...]).astype(BF16), w_ref[...])
    n = m_ref.shape[0]
    for h in range(MEM_HEADS):
        k_ref[pl.ds(h, n, stride=MEM_HEADS), :] = kv[:, h * MEM_HEAD_DIM:(h + 1) * MEM_HEAD_DIM]
        v_ref[pl.ds(h, n, stride=MEM_HEADS), :] = kv[:, MEM_W + h * MEM_HEAD_DIM:MEM_W + (h + 1) * MEM_HEAD_DIM]


def _mem_kv(mem, g, w):
    n = mem.shape[0]
    out = pl.BlockSpec((ROW_TILE * MEM_HEADS, MEM_HEAD_DIM), lambda i: (i, 0))
    return pl.pallas_call(
        _mem_kv_body,
        grid=(n // ROW_TILE,),
        in_specs=[pl.BlockSpec((ROW_TILE, D_MODEL), lambda i: (i, 0)), _resident((1, D_MODEL)), _resident(w.shape)],
        out_specs=[out, out],
        out_shape=[jax.ShapeDtypeStruct((n * MEM_HEADS, MEM_HEAD_DIM), F32)] * 2,
        compiler_params=_params("parallel"),
        name="mem_kv",
    )(mem, g, w)


def _qk(q, k):
    return lax.dot_general(q, k, (((1,), (1,)), ((), ())), preferred_element_type=F32)


def _softmax_pv(scores, values, sinks=None):
    probs = []
    for i, s in enumerate(scores):
        m = jnp.max(s, axis=-1, keepdims=True)
        if sinks is not None:
            m = jnp.maximum(m, sinks[i])
        e = jnp.exp(s - m)
        den = jnp.sum(e, axis=-1, keepdims=True)
        if sinks is not None:
            den = den + jnp.exp(sinks[i] - m)
        probs.append((e * (1.0 / den)).astype(BF16))
    return [_mm(p, v) for p, v in zip(probs, values)]


def _gqa_queries(q):
    nq = q.shape[0]
    low = lax.broadcasted_iota(jnp.int32, (nq, LANES), 1) < HEAD_DIM
    zero = jnp.zeros((nq, LANES), BF16)
    stacks = []
    for g in range(N_KV_HEADS):
        rows = []
        for r in range(KV_REP):
            h = g * KV_REP + r
            q2 = q[:, (h // 2) * LANES:(h // 2 + 1) * LANES]
            rows.append(jnp.where(low, q2, zero) if h % 2 == 0 else jnp.where(low, zero, q2))
        stacks.append(jnp.concatenate(rows, axis=0))
    return stacks


def _gqa_outputs(outs, nq):
    low = lax.broadcasted_iota(jnp.int32, (nq, LANES), 1) < HEAD_DIM
    pairs = []
    for o in outs:
        for r in range(0, KV_REP, 2):
            pairs.append(jnp.where(low, o[r * nq:(r + 1) * nq], o[(r + 1) * nq:(r + 2) * nq]))
    return jnp.concatenate(pairs, axis=1)


def _mem_scores(qm, k_head):
    return [_qk(qm[:, h * MEM_HEAD_DIM:(h + 1) * MEM_HEAD_DIM], k_head(h).astype(BF16)) * (MEM_HEAD_DIM ** -0.5)
            for h in range(MEM_HEADS)]


def _mem_values(v_head):
    return [v_head(h).astype(BF16) for h in range(MEM_HEADS)]


def _row_heads(ref, i):
    return lambda h: ref[i, pl.ds(h, MEM_LEN, stride=MEM_HEADS), :]


def _kv_heads(x):
    return [x[:, g * LANES:(g + 1) * LANES] for g in range(N_KV_HEADS)]


def _merge_pre(x1, y, gmix_ref, wg_ref, wglu_ref):
    h = _rms(x1, gmix_ref[...]).astype(BF16)
    logits = [_mm(h, wg_ref[:, PROJ_COLS + j * D_MODEL:PROJ_COLS + (j + 1) * D_MODEL]) for j in range(3)]
    y = y.astype(BF16)
    return logits, _mm(y, wglu_ref[:, :D_MODEL]), _mm(y, wglu_ref[:, D_MODEL:])


def _mix_parts(parts, x1_ref, y_ref, gmix_ref, wg_ref, wglu_ref, wab_ref, wmb_ref, wo_ref, gpost_ref, x2_ref):
    x1s, ys = _row_parts(x1_ref, len(parts)), _row_parts(y_ref, len(parts))
    pres = [_merge_pre(x1, y, gmix_ref, wg_ref, wglu_ref) for x1, y in zip(x1s, ys)]
    branches = [assemble(_softmax_pv(scores, values, sinks), _softmax_pv(mem_scores, mem_values))
                for scores, values, sinks, mem_scores, mem_values, assemble in parts]
    projected = [(_mm(attn.astype(BF16), wab_ref[...]), _mm(memo.astype(BF16), wmb_ref[...]))
                 for attn, memo in branches]
    sig = _sigmoid
    merged = [sig(lg[0]) * pa + sig(lg[1]) * (ya * sig(yb)) + sig(lg[2]) * pm
              for (lg, ya, yb), (pa, pm) in zip(pres, projected)]
    outs = [_mm(m.astype(BF16), wo_ref[...]) for m in merged]
    rows = x1_ref.shape[0] // len(parts)
    for i, (x1, o) in enumerate(zip(x1s, outs)):
        x2_ref[i * rows:(i + 1) * rows, :] = x1 + _rms(o, gpost_ref[...])


def _mixer_prompt_body(q_ref, k_ref, kh_ref, v_ref, vh_ref, bias_ref, sink_ref, qm_ref, mk_ref, mv_ref,
                       x1_ref, y_ref, *merge_and_out_refs):
    i = pl.program_id(1)
    tq = q_ref.shape[1]
    kf = jnp.concatenate([kh_ref[0], k_ref[0]], axis=0)
    vf = jnp.concatenate([vh_ref[0], v_ref[0]], axis=0)
    band = WINDOW + CHUNK
    rows = tq // MIXER_PARTS
    parts = []
    for p in range(MIXER_PARTS):
        scores, values, sinks = [], [], []
        for lo in range(p * rows, (p + 1) * rows, CHUNK):
            valid = None
            if lo < WINDOW:
                key_pos = lax.broadcasted_iota(jnp.int32, (1, band), 1) + (i * tq + lo - WINDOW)
                valid = key_pos >= 0
            kb, vb = _kv_heads(kf[lo:lo + band]), _kv_heads(vf[lo:lo + band])
            for g, qs in enumerate(_gqa_queries(q_ref[0, lo:lo + CHUNK, :])):
                s = _qk(qs, kb[g]) + bias_ref[g]
                scores.append(s if valid is None else jnp.where(valid, s, NEG_INF))
                values.append(vb[g])
                sinks.append(sink_ref[g])
        mem_scores = _mem_scores(qm_ref[0, p * rows:(p + 1) * rows, :], _row_heads(mk_ref, 0))

        def assemble(outs, mem):
            attn = jnp.concatenate([_gqa_outputs(outs[c:c + N_KV_HEADS], CHUNK)
                                    for c in range(0, len(outs), N_KV_HEADS)], axis=0)
            return attn, jnp.concatenate(mem, axis=1)

        parts.append((scores, values, sinks, mem_scores, _mem_values(_row_heads(mv_ref, 0)), assemble))
    _mix_parts(parts, x1_ref, y_ref, *merge_and_out_refs)


def _merge_weight_specs(weights):
    return [_resident(w.shape) for w in weights]


def _mixer_prompt(q, k2, v2, bias, sink, qm, mk, mv, x1, y, merge_w):
    b, l, _ = q.shape
    tq = MIXER_TILE
    hb = tq // WINDOW
    nt = l // tq
    cur = lambda w: pl.BlockSpec((1, tq, w), lambda bi, i: (bi, i, 0))
    row = lambda w: pl.BlockSpec((tq, w), lambda bi, i: (bi * nt + i, 0))
    halo = pl.BlockSpec((1, WINDOW, KV2_W), lambda bi, i: (bi, jnp.maximum(i * hb - 1, 0), 0))
    mem = pl.BlockSpec((1, MEM_LEN * MEM_HEADS, MEM_HEAD_DIM), lambda bi, i: (bi, 0, 0))
    return pl.pallas_call(
        _mixer_prompt_body,
        grid=(b, nt),
        in_specs=[cur(ATTN_W), cur(KV2_W), halo, cur(KV2_W), halo, _resident(bias.shape),
                  _resident(sink.shape), cur(MEM_W), mem, mem, row(D_MODEL), row(SSM_W)]
        + _merge_weight_specs(merge_w),
        out_specs=row(D_MODEL),
        out_shape=jax.ShapeDtypeStruct((b * l, D_MODEL), F32),
        compiler_params=_params("parallel", "parallel"),
        name="mixer_prompt",
    )(q, k2, k2, v2, v2, bias, sink, qm, mk, mv, x1, y, *merge_w)


def _mixer_sample_body(q_ref, k_ref, v_ref, bias_ref, sink_ref, qm_ref, mk_ref, mv_ref,
                       x1_ref, y_ref, *merge_and_out_refs):
    nb, nq = q_ref.shape[0], q_ref.shape[1]
    per_part = nb // MIXER_PARTS
    parts = []
    for p in range(MIXER_PARTS):
        scores, values, sinks, mem_scores, mem_values = [], [], [], [], []
        for b in range(p * per_part, (p + 1) * per_part):
            kb, vb = _kv_heads(k_ref[b]), _kv_heads(v_ref[b])
            for g, qs in enumerate(_gqa_queries(q_ref[b])):
                scores.append(_qk(qs, kb[g]) + bias_ref[g])
                values.append(vb[g])
                sinks.append(sink_ref[g])
            mem_scores += _mem_scores(qm_ref[b], _row_heads(mk_ref, b))
            mem_values += _mem_values(_row_heads(mv_ref, b))

        def assemble(outs, mem):
            attn = jnp.concatenate([_gqa_outputs(outs[j:j + N_KV_HEADS], nq)
                                    for j in range(0, len(outs), N_KV_HEADS)], axis=0)
            memo = jnp.concatenate([jnp.concatenate(mem[j:j + MEM_HEADS], axis=1)
                                    for j in range(0, len(mem), MEM_HEADS)], axis=0)
            return attn, memo

        parts.append((scores, values, sinks, mem_scores, mem_values, assemble))
    _mix_parts(parts, x1_ref, y_ref, *merge_and_out_refs)


def _mixer_sample(q, kk2, vv2, bias, sink, qm, mk, mv, x1, y, merge_w):
    b, s, _ = q.shape
    nb = SAMPLE_MIXER_TILE // s
    nk = kk2.shape[1]
    blk = lambda *shape: pl.BlockSpec((nb,) + shape, lambda i: (i,) + (0,) * len(shape))
    row = lambda w: pl.BlockSpec((nb * s, w), lambda i: (i, 0))
    return pl.pallas_call(
        _mixer_sample_body,
        grid=(b // nb,),
        in_specs=[blk(s, ATTN_W), blk(nk, KV2_W), blk(nk, KV2_W),
                  _resident(bias.shape), _resident(sink.shape),
                  blk(s, MEM_W), blk(*mk.shape[1:]), blk(*mv.shape[1:]), row(D_MODEL), row(SSM_W)]
        + _merge_weight_specs(merge_w),
        out_specs=row(D_MODEL),
        out_shape=jax.ShapeDtypeStruct((b * s, D_MODEL), F32),
        compiler_params=_params("parallel"),
        name="mixer_sample",
    )(q, kk2, vv2, bias, sink, qm, mk, mv, x1, y, *merge_w)


def _ssm_weights_body(n_first, lam_re_row, lam_im_row, lam_re_col, lam_im_col, log_dt, b_re_t, b_im_t,
                      c_re, c_im, c_re_lanes, c_im_lanes, sel_ref, *refs):
    first_f32, (wconv_ref, wst_ref, wot_ref, a_tab_ref), first_bf16 = (
        refs[:n_first], refs[n_first:n_first + 4], refs[n_first + 4:])
    _cast_chunks(first_f32, first_bf16)
    _ssm_block_weights(lam_re_row, lam_im_row, lam_re_col, lam_im_col, log_dt, b_re_t, b_im_t,
                       c_re, c_im, c_re_lanes, c_im_lanes, sel_ref, wconv_ref, wst_ref, wot_ref, a_tab_ref)


def _ssm_block_weights(lam_re_row, lam_im_row, lam_re_col, lam_im_col, log_dt, b_re_t, b_im_t,
                       c_re, c_im, c_re_lanes, c_im_lanes, sel_ref,
                       wconv_ref, wst_ref, wot_ref, a_tab_ref):
    wst_ref[0] = jnp.zeros(wst_ref.shape[1:], BF16)
    wot_ref[0] = jnp.zeros(wot_ref.shape[1:], BF16)
    krows = []
    for gl in range(SSM_GB):
        krows.append(_ssm_group_weights(
            gl, *(r[gl] for r in (lam_re_row, lam_im_row, lam_re_col, lam_im_col, log_dt, b_re_t, b_im_t,
                                  c_re, c_im, c_re_lanes, c_im_lanes)),
            wst_ref, wot_ref, a_tab_ref))
    kstack = jnp.concatenate(krows, axis=0).astype(BF16)
    for d in range(SSM_NT):
        for ti in range(SSM_TPT):
            blk = _mm(kstack, sel_ref[d * SSM_TPT + ti])
            r0 = (SSM_NT - 1 - d) * MXU_TILE + ti * LANES
            for gl in range(SSM_GB):
                piece = blk[gl * SSM_GROUP:(gl + 1) * SSM_GROUP]
                piece = (piece if gl == 0 else pltpu.roll(piece, gl * SSM_GROUP, 1)).astype(BF16)
                r = r0 + gl * SSM_GROUP
                wconv_ref[0, r:r + SSM_GROUP, :MXU_TILE] = piece
                if r0 >= MXU_TILE:
                    wconv_ref[0, r - MXU_TILE:r - MXU_TILE + SSM_GROUP, MXU_TILE:] = piece
    wconv_ref[0, (SSM_NT - 1) * MXU_TILE:, MXU_TILE:] = jnp.zeros((MXU_TILE, MXU_TILE), BF16)


def _ssm_group_weights(gl, lam_re_row, lam_im_row, lam_re_col, lam_im_col, log_dt, b_re_t, b_im_t,
                       c_re, c_im, c_re_lanes, c_im_lanes,
                       wst_ref, wot_ref, a_tab_ref):
    dt = jnp.exp(log_dt)
    rows = lambda t: slice(t * LANES + gl * SSM_GROUP, t * LANES + (gl + 1) * SSM_GROUP)
    lanes = slice(gl * SSM_S2, (gl + 1) * SSM_S2)

    def zoh_coef(lr, li):
        mag = jnp.exp(lr * dt)
        a_re, a_im = mag * jnp.cos(li * dt), mag * jnp.sin(li * dt)
        den = lr * lr + li * li
        return ((a_re - 1.0) * lr + a_im * li) / den, (a_im * lr - (a_re - 1.0) * li) / den

    def a_power(lr, li, n):
        mag = jnp.exp(lr * dt * n)
        return mag * jnp.cos(li * dt * n), mag * jnp.sin(li * dt * n)

    lr, li = lam_re_row, lam_im_row
    cr, ci = zoh_coef(lr, li)
    n_rows = jnp.minimum(lax.broadcasted_iota(jnp.int32, (SSM_T + SCAN_ROWS, 1), 0), SSM_T).astype(F32)
    pw_r, pw_i = a_power(lr, li, n_rows)

    def over_channels(pw, exps):
        return jnp.concatenate([jnp.broadcast_to(pw[e:e + 1], (SSM_GROUP, SSM_STATE)) for e in exps], axis=0)

    back = [SSM_T - 1 - t for t in range(SSM_T)]
    pr, pi = over_channels(pw_r, back), over_channels(pw_i, back)
    zr, zi = pr * cr - pi * ci, pr * ci + pi * cr
    over_positions = lambda x: jnp.concatenate([x] * SSM_T, axis=0)
    br, bi = over_positions(b_re_t), over_positions(b_im_t)
    wstate = jnp.concatenate([zr * br - zi * bi, zr * bi + zi * br], axis=1).astype(BF16)

    fwd = [t + 1 for t in range(SSM_T)]
    qr, qi = over_channels(pw_r, fwd), over_channels(pw_i, fwd)
    ccr, cci = over_positions(c_re), over_positions(c_im)
    wout_t = jnp.concatenate([ccr * qr - cci * qi, -(ccr * qi + cci * qr)], axis=1).astype(BF16)
    for t in range(SSM_T):
        wst_ref[0, rows(t), lanes] = wstate[t * SSM_GROUP:(t + 1) * SSM_GROUP]
        wot_ref[0, rows(t), lanes] = wout_t[t * SSM_GROUP:(t + 1) * SSM_GROUP]

    lrc, lic = lam_re_col, lam_im_col
    hi = lax.Precision.HIGHEST
    n_lanes = jnp.minimum(lax.broadcasted_iota(jnp.int32, (1, LANES), 1), SSM_T).astype(F32)
    pc_r, pc_i = a_power(lrc, lic, n_lanes)
    spread = (lax.broadcasted_iota(jnp.int32, (LANES, SSM_K), 0)
              == lax.broadcasted_iota(jnp.int32, (LANES, SSM_K), 1) // SSM_GROUP).astype(F32)
    gr = jnp.dot(pc_r, spread, precision=hi, preferred_element_type=F32)
    gi = jnp.dot(pc_i, spread, precision=hi, preferred_element_type=F32)
    clr, cli = c_re_lanes, c_im_lanes
    g_re, g_im = clr * gr - cli * gi, clr * gi + cli * gr
    btr, bti = b_re_t, b_im_t
    bbr, bbi = cr * btr - ci * bti, cr * bti + ci * btr
    krow = (jnp.dot(bbr, g_re, precision=hi, preferred_element_type=F32)
            - jnp.dot(bbi, g_im, precision=hi, preferred_element_type=F32))

    idx = lax.broadcasted_iota(jnp.int32, (N_SCAN_TABLES * SCAN_ROWS, SSM_STATE), 0)
    tab, r = idx // SCAN_ROWS, idx % SCAN_ROWS
    stride = jnp.where(tab < 2, 1, jnp.where(tab < 4, 2, 4))
    n = jnp.where(tab < 6, stride, jnp.where(tab < 8, r, SCAN_ROWS))
    keep = jnp.logical_or(tab >= 6, r >= stride)
    positions = jnp.where(tab < 10, n * SCAN_T, SSM_T)
    er, ei = a_power(lr, li, positions.astype(F32))
    er, ei = jnp.where(keep, er, 0.0), jnp.where(keep, ei, 0.0)
    odd = tab % 2 == 1
    tabs = jnp.concatenate([jnp.where(odd, -ei, er), jnp.where(odd, ei, er)], axis=1)
    a_tab_ref[:, :, lanes] = tabs.reshape(N_SCAN_TABLES, SCAN_ROWS, SSM_S2)
    return krow


def _lag_selectors():
    sel = np.zeros((SSM_NT, SSM_TPT, SSM_K, MXU_TILE), np.float32)
    ch = np.arange(SSM_GROUP)
    for d in range(SSM_NT):
        for ti in range(SSM_TPT):
            for to in range(SSM_TPT):
                lag = SSM_TPT * d + to - ti
                if lag >= 0:
                    sel[d, ti, lag * SSM_GROUP + ch, to * LANES + ch] = 1.0
    return jnp.asarray(sel.reshape(SSM_NT * SSM_TPT, SSM_K, MXU_TILE), BF16)


def _ssm_weights(lam_re, lam_im, log_dt, b_re, b_im, c_re, c_im, first_weights):
    g, p, c = b_re.shape
    row3 = lambda x: x.reshape(g, 1, p)
    col3 = lambda x: x.reshape(g, p, 1)
    t3 = lambda x: jnp.transpose(x, (0, 2, 1))
    c_lanes = lambda x: jnp.tile(t3(x), (1, 1, SSM_T))
    args = (row3(lam_re), row3(lam_im), col3(lam_re), col3(lam_im), log_dt.reshape(g, 1, 1),
            t3(b_re), t3(b_im), c_re, c_im, c_lanes(c_re), c_lanes(c_im))
    sel = _lag_selectors()
    spec = lambda x: pl.BlockSpec((SSM_GB,) + x.shape[1:], lambda i: (i, 0, 0))
    per_gb = lambda *shape: pl.BlockSpec((1,) + shape, lambda i: (i, 0, 0))
    chunk_specs = [_row_chunk_spec(w, SSM_NGB, cols) for w, cols in first_weights]
    outs = pl.pallas_call(
        functools.partial(_ssm_weights_body, len(first_weights)),
        grid=(SSM_NGB,),
        in_specs=[spec(a) for a in args] + [_resident(sel.shape)] + chunk_specs,
        out_specs=[per_gb(SSM_NT * MXU_TILE, 2 * MXU_TILE), per_gb(SSM_XW, SSM_SW), per_gb(SSM_XW, SSM_SW),
                   pl.BlockSpec((N_SCAN_TABLES, SCAN_ROWS, SSM_SW), lambda i: (0, 0, i))] + chunk_specs,
        out_shape=[jax.ShapeDtypeStruct((SSM_NGB, SSM_NT * MXU_TILE, 2 * MXU_TILE), BF16),
                   jax.ShapeDtypeStruct((SSM_NGB, SSM_XW, SSM_SW), BF16),
                   jax.ShapeDtypeStruct((SSM_NGB, SSM_XW, SSM_SW), BF16),
                   jax.ShapeDtypeStruct((N_SCAN_TABLES, SCAN_ROWS, SSM_WIDTH), F32)]
        + [jax.ShapeDtypeStruct((w.shape[0], cols or w.shape[1]), BF16) for w, cols in first_weights],
        compiler_params=_params("arbitrary"),
        name="ssm_weights",
    )(*args, sel, *[w for w, _ in first_weights])
    return outs[:4], outs[4:]


def _swap_halves(s):
    ax = s.ndim - 1
    return jnp.concatenate([pltpu.roll(s[..., l:l + SSM_S2], SSM_STATE, ax)
                            for l in range(0, s.shape[ax], SSM_S2)], axis=ax)


def _ssm_body(row_sets, slab_rows, u_ref, wconv_ref, wst_ref, wot_ref, d_ref, a_tab_ref, s0_ref,
              y_ref, s_ref, v_scr, s_scr):
    def piece(ref_set, first, n, stride, t):
        return ref_set, pl.ds(first + t, n, stride=stride)

    @pl.when(pl.program_id(1) == 0)
    def _():
        s_scr[...] = s0_ref[0]

    x = jnp.concatenate(
        [jnp.concatenate([u_ref[piece(*rs, t)] for t in range(SSM_T)], axis=1) for rs in row_sets],
        axis=0).astype(BF16)
    lane_blocks = range(SSM_SW // LANES)

    def v_read(rows):
        return jnp.concatenate([v_scr[j, rows, :] for j in lane_blocks], axis=1)

    def v_write(rows, value):
        for j in lane_blocks:
            v_scr[j, rows, :] = value[:, j * LANES:(j + 1) * LANES]

    n_rows = v_scr.shape[1]
    all_rows = slice(0, n_rows)
    v_write(all_rows, _mm(x, wst_ref[0]))
    conv = []
    for jo in range(0, SSM_NT, 2):
        pair = _mm(x[:, :(jo + 2) * MXU_TILE], wconv_ref[0, (SSM_NT - 2 - jo) * MXU_TILE:, :])
        conv += [pair[:, MXU_TILE:], pair[:, :MXU_TILE]]

    a_mul, a_swap = a_tab_ref[10, 0:1, :], a_tab_ref[11, 0:1, :]
    if slab_rows:
        s = s_scr[...]
        for c in range(n_rows // slab_rows):
            rows = slice(c * slab_rows, (c + 1) * slab_rows)
            inc = v_read(rows)
            v_write(rows, s)
            s = a_mul * s + a_swap * _swap_halves(s) + inc
        s_scr[...] = s
    else:
        set_starts = np.cumsum([0] + [rs[2] for rs in row_sets])
        halves = [(int(r0), rs[2] // SCAN_PAIR) for r0, rs in zip(set_starts, row_sets)]
        first = jnp.concatenate([v_read(pl.ds(r0, n, stride=SCAN_PAIR)) for r0, n in halves], axis=0)
        second = jnp.concatenate([v_read(pl.ds(r0 + 1, n, stride=SCAN_PAIR)) for r0, n in halves], axis=0)
        merged = a_mul * first + a_swap * _swap_halves(first) + second
        n_blocks = merged.shape[0] // SCAN_ROWS
        xs = merged.reshape(n_blocks, SCAN_ROWS, SSM_SW)
        for k in range(3):
            sh = pltpu.roll(xs, 1 << k, 1)
            xs = xs + a_tab_ref[2 * k] * sh + a_tab_ref[2 * k + 1] * _swap_halves(sh)
        carries, blk = [], 0
        for j, (_, n) in enumerate(halves):
            carry = s_scr[j * SCAN_ROWS:(j + 1) * SCAN_ROWS, :]
            for _ in range(n // SCAN_ROWS):
                carries.append(carry)
                last = jnp.broadcast_to(xs[blk, SCAN_ROWS - 1:, :], carry.shape)
                carry = last + a_tab_ref[8] * carry + a_tab_ref[9] * _swap_halves(carry)
                blk += 1
            s_scr[j * SCAN_ROWS:(j + 1) * SCAN_ROWS, :] = carry
        carries = jnp.stack(carries, axis=0)
        not_first = lax.broadcasted_iota(jnp.int32, (1, SCAN_ROWS, 1), 1) >= 1
        enter = (jnp.where(not_first, pltpu.roll(xs, 1, 1), 0.0)
                 + a_tab_ref[6] * carries + a_tab_ref[7] * _swap_halves(carries)).reshape(merged.shape)
        enter_second = a_mul * enter + a_swap * _swap_halves(enter) + first
        h0 = 0
        for r0, n in halves:
            v_write(pl.ds(r0, n, stride=SCAN_PAIR), enter[h0:h0 + n])
            v_write(pl.ds(r0 + 1, n, stride=SCAN_PAIR), enter_second[h0:h0 + n])
            h0 += n
    s_ref[0] = s_scr[...]

    enter = v_read(all_rows).astype(BF16)
    d = d_ref[...]
    carried = _qk(enter, wot_ref[0])
    for jo in range(SSM_NT):
        yt = conv[jo] + carried[:, jo * MXU_TILE:(jo + 1) * MXU_TILE]
        r0 = 0
        for rs in row_sets:
            for tl in range(SSM_TPT):
                idx = piece(*rs, jo * SSM_TPT + tl)
                y_ref[idx] = yt[r0:r0 + rs[2], tl * LANES:(tl + 1) * LANES] + u_ref[idx] * d
            r0 += rs[2]


def _ssm(u, row_sets, slab_rows, block_rows, wconv, wst, wo, d_lanes, a_tab, s0):
    ns, r, _ = u.shape
    chunk_rows = sum(rs[2] for rs in row_sets)
    carry_rows = s0.shape[1]
    blk = pl.BlockSpec((ns, block_rows, LANES), lambda gb, i: (0, i, gb))
    per_gb = lambda x: pl.BlockSpec((1,) + x.shape[1:], lambda gb, i: (gb, 0, 0))
    return pl.pallas_call(
        functools.partial(_ssm_body, row_sets, slab_rows),
        grid=(SSM_NGB, r // block_rows),
        in_specs=[blk, per_gb(wconv), per_gb(wst), per_gb(wo),
                  pl.BlockSpec((1, LANES), lambda gb, i: (0, gb)),
                  pl.BlockSpec((N_SCAN_TABLES, SCAN_ROWS, SSM_SW), lambda gb, i: (0, 0, gb)),
                  per_gb(s0)],
        out_specs=[blk, per_gb(s0)],
        out_shape=[jax.ShapeDtypeStruct(u.shape, F32), jax.ShapeDtypeStruct(s0.shape, F32)],
        scratch_shapes=[pltpu.VMEM((SSM_SW // LANES, chunk_rows, LANES), F32),
                        pltpu.VMEM((carry_rows, SSM_SW), F32)],
        compiler_params=_params("parallel", "arbitrary"),
        name="ssm_scan",
    )(u, wconv, wst, wo, d_lanes, a_tab, s0)


def _ssm_branch(u, s0, ssm_w):
    b, l, _ = u.shape
    nc = l // SSM_T
    by_gb = lambda s: s.reshape(s.shape[0], SSM_NGB, SSM_SW).transpose(1, 0, 2)
    if b % SCAN_ROWS == 0:
        row_sets = tuple((0, c * SSM_T, b, l) for c in range(nc))
        y, s_last = _ssm(u.reshape(1, b * l, SSM_W), row_sets, b, b * l, *ssm_w, by_gb(s0))
    else:
        cps = min(nc, SSM_CHUNKS_PER_STEP)
        row_sets = tuple((j, 0, cps, SSM_T) for j in range(b))
        y, s_last = _ssm(u, row_sets, 0, cps * SSM_T, *ssm_w, by_gb(jnp.repeat(s0, SCAN_ROWS, axis=0)))
        s_last = s_last[:, ::SCAN_ROWS]
    s_last = s_last.transpose(1, 0, 2).reshape(b, SSM_GROUPS, 2, SSM_STATE)
    return y.reshape(b, l, SSM_W), s_last[:, :, 0], s_last[:, :, 1]


def _ffn2_body(steps_a, xa_ref, xb_ref, gpre_ref, gpost_ref, wfi_ref, wfo_ref, oa_ref, ob_ref):
    def run(x_ref, o_ref):
        outs = _ffn(_row_parts(x_ref, FFN_PARTS), gpre_ref[...], gpost_ref[...], wfi_ref, wfo_ref)
        rows = x_ref.shape[0] // FFN_PARTS
        for i, o in enumerate(outs):
            o_ref[i * rows:(i + 1) * rows, :] = o

    _on_group(steps_a, run, (xa_ref, oa_ref), (xb_ref, ob_ref))


def _ffn2(xa, xb, gpre, gpost, wfi, wfo):
    steps_a, steps_b = xa.shape[0] // FFN_ROW_TILE, xb.shape[0] // FFN_ROW_TILE
    first, second = _two_group_specs(steps_a)
    vec = _resident((1, D_MODEL))
    return pl.pallas_call(
        functools.partial(_ffn2_body, steps_a),
        grid=(steps_a + steps_b,),
        in_specs=[first(D_MODEL), second(D_MODEL), vec, vec, _resident(wfi.shape), _resident(wfo.shape)],
        out_specs=[first(D_MODEL), second(D_MODEL)],
        out_shape=[jax.ShapeDtypeStruct(xa.shape, F32), jax.ShapeDtypeStruct(xb.shape, F32)],
        compiler_params=_params("arbitrary"),
        name="ffn2",
    )(xa, xb, gpre, gpost, wfi, wfo)


def _t5_bucket(rel):
    half = N_BUCKETS // 2
    max_exact = half // 2
    ret = (rel > 0).astype(np.int32) * half
    n = np.abs(rel)
    large = max_exact + (np.log(np.maximum(n, 1) / max_exact) / math.log(MAX_DISTANCE / max_exact)
                         * (half - max_exact)).astype(np.int32)
    large = np.minimum(large, half - 1)
    return ret + np.where(n < max_exact, n, large)


def _band_bias(rel_table, n_q, n_back, n_k):
    i = np.arange(n_q)[:, None]
    j = np.arange(n_k)[None, :]
    bucket = _t5_bucket((j - n_back) - i).reshape(-1)
    onehot = np.zeros((N_BUCKETS, bucket.size), np.float32)
    onehot[bucket, np.arange(bucket.size)] = 1.0
    b = jnp.dot(rel_table.astype(F32).T, jnp.asarray(onehot), precision=lax.Precision.HIGHEST)
    return b.reshape(N_KV_HEADS, KV_REP * n_q, n_k)


def _sink_rows(sink, n_q):
    return jnp.repeat(sink.astype(F32).reshape(N_KV_HEADS, KV_REP), n_q, axis=1)[:, :, None]


def _twice_per_head(x, axis):
    shape = x.shape
    x = x.reshape(shape[:axis] + (N_KV_HEADS, 1, HEAD_DIM) + shape[axis + 1:])
    x = jnp.concatenate([x, x], axis=axis + 1)
    return x.reshape(shape[:axis] + (KV2_W,) + shape[axis + 1:])


def kernel(x_prompt, x_sample, cache_swa_k, cache_swa_v, cache_mem_k, cache_mem_v, state_ssm_re, state_ssm_im, mem_prompt, rel_bias_table, ff1_pre_g, ff1_post_g, w_ff1_in, w_ff1_out, mix_pre_g, mix_post_g, w_in, mem_norm_g, w_mem_kv, attn_sink, ssm_lambda_re, ssm_lambda_im, ssm_log_dt, ssm_b_re, ssm_b_im, ssm_c_re, ssm_c_im, ssm_d, w_ssm_glu, w_attn_br, w_mem_br, w_out, ff2_pre_g, ff2_post_g, w_ff2_in, w_ff2_out):
    assert ff1_pre_g.shape[0] == 1, "single-layer step"
    bp, lp, _ = x_prompt.shape
    bs, ls, _ = x_sample.shape
    vec = lambda g: g[0].reshape(1, D_MODEL).astype(F32)

    (wconv, wst, wot, a_tab), (wfi1, wfo1, wp) = _ssm_weights(
        ssm_lambda_re[0], ssm_lambda_im[0], ssm_log_dt[0], ssm_b_re[0], ssm_b_im[0], ssm_c_re[0], ssm_c_im[0],
        [(w_ff1_in[0], None), (w_ff1_out[0], None), (w_in[0], PROJ_COLS)])
    ssm_w = (wconv, wst, wot, ssm_d[0].astype(F32).reshape(1, SSM_W), a_tab)

    later = (w_ff2_in, w_ff2_out, w_in, w_ssm_glu, w_attn_br, w_mem_br, w_out, w_mem_kv)
    proj_p, proj_s, (wfi2, wfo2, w_in16, wglu, wab, wmb, wo, wmkv) = _ffn1_proj(
        x_prompt.reshape(bp * lp, D_MODEL), x_sample.reshape(bs * ls, D_MODEL), vec(ff1_pre_g), vec(ff1_post_g),
        wfi1, wfo1, vec(mix_pre_g), wp, [w[0] for w in later])

    mk_p, mv_p = _mem_kv(mem_prompt.reshape(bp * MEM_LEN, D_MODEL), vec(mem_norm_g), wmkv)

    def mix(x_shape, proj, mixer, s0, mem_k, mem_v):
        b, l, _ = x_shape
        r3 = lambda t: t.reshape(b, l, t.shape[-1])
        x1, q, k, v, u, qm, k2, v2 = proj
        y_ssm, s_re, s_im = _ssm_branch(r3(u), s0, ssm_w)
        x2 = mixer(r3(q), r3(k2), r3(v2), r3(qm), mem_k, mem_v, x1, y_ssm.reshape(b * l, SSM_W))
        return x2, r3(k), r3(v), s_re, s_im

    merge_w = (vec(mix_pre_g), w_in16, wglu, wab, wmb, wo, vec(mix_post_g))

    def mixer_prompt(q, k2, v2, qm, mem_k, mem_v, x1, y_ssm):
        bias = _band_bias(rel_bias_table, CHUNK, WINDOW, WINDOW + CHUNK)
        return _mixer_prompt(q, k2, v2, bias, _sink_rows(attn_sink[0], CHUNK), qm, mem_k, mem_v, x1, y_ssm, merge_w)

    def mixer_sample(q, k2, v2, qm, mem_k, mem_v, x1, y_ssm):
        n_back = cache_swa_k.shape[2]
        cache2 = lambda c: _twice_per_head(c[0].reshape(bs, n_back, KV_W).astype(BF16), 2)
        kk = jnp.concatenate([cache2(cache_swa_k), k2], axis=1)
        vv = jnp.concatenate([cache2(cache_swa_v), v2], axis=1)
        bias = _band_bias(rel_bias_table, ls, n_back, n_back + ls)
        return _mixer_sample(q, kk, vv, bias, _sink_rows(attn_sink[0], ls), qm, mem_k, mem_v, x1, y_ssm, merge_w)

    x2p, pk, pv, pre, pim = mix(x_prompt.shape, proj_p, mixer_prompt, jnp.zeros((bp, SSM_WIDTH), F32),
                                mk_p.reshape(bp, MEM_LEN * MEM_HEADS, MEM_HEAD_DIM),
                                mv_p.reshape(bp, MEM_LEN * MEM_HEADS, MEM_HEAD_DIM))
    s0 = jnp.stack([state_ssm_re[0], state_ssm_im[0]], axis=2).reshape(bs, SSM_WIDTH).astype(F32)
    x2s, sk, sv, sre, sim = mix(x_sample.shape, proj_s, mixer_sample, s0,
                                cache_mem_k[0].reshape(bs, MEM_LEN * MEM_HEADS, MEM_HEAD_DIM),
                                cache_mem_v[0].reshape(bs, MEM_LEN * MEM_HEADS, MEM_HEAD_DIM))
    yp, ys = _ffn2(x2p, x2s, vec(ff2_pre_g), vec(ff2_post_g), wfi2, wfo2)
    yp, ys = yp.reshape(x_prompt.shape), ys.reshape(x_sample.shape)

    n_keep = min(WINDOW, lp)
    heads = lambda t: t.reshape(t.shape[0], t.shape[1], N_KV_HEADS, HEAD_DIM)[None]
    mem_heads = lambda t: t.reshape(bp, MEM_LEN, MEM_HEADS, MEM_HEAD_DIM)[None]
    return (yp, ys, heads(pk[:, -n_keep:]), heads(pv[:, -n_keep:]), mem_heads(mk_p), mem_heads(mv_p),
            pre[None], pim[None], heads(sk), heads(sv), sre[None], sim[None])
```

```python
import functools
import math

import numpy as np
import jax
import jax.numpy as jnp
from jax import lax
from jax.experimental import pallas as pl
from jax.experimental.pallas import tpu as pltpu

D_MODEL = 1024
CHUNK = 64
WINDOW = 128
HEAD_DIM = 64
MIX_W = D_MODEL // 2
N_HEADS = MIX_W // HEAD_DIM
N_KV_HEADS = N_HEADS // 4
KV_REP = N_HEADS // N_KV_HEADS
ATTN_W = N_HEADS * HEAD_DIM
KV_W = N_KV_HEADS * HEAD_DIM
SSM_GROUP = 16
SSM_W = MIX_W
SSM_GROUPS = SSM_W // SSM_GROUP
SSM_STATE = 64
MEM_LEN = 256
MEM_HEADS = 4
MEM_HEAD_DIM = MIX_W // MEM_HEADS
MEM_W = MEM_HEADS * MEM_HEAD_DIM
D_FF = 128 * ((8 * D_MODEL // 3 + 127) // 128)
N_BUCKETS = 32
MAX_DISTANCE = 128
RMS_EPS = 1e-6
NEG_INF = -1e30

LANES = 128
BF16_ROWS = 16
MXU_TILE = 256
ROW_TILE = 256
MIXER_TILE = 512
SAMPLE_MIXER_TILE = 256
MIXER_PARTS = 2
FFN_ROW_TILE = 512
FFN_PARTS = 2
FFN_CHUNK = 2 * MXU_TILE
VMEM_LIMIT = 60 * 1024 * 1024

KV2_W = N_KV_HEADS * LANES

SSM_T = 8
SCAN_RUN = 4
SCAN_T = SSM_T * SCAN_RUN
SSM_K = SSM_T * SSM_GROUP
SSM_S2 = 2 * SSM_STATE
SSM_WIDTH = SSM_GROUPS * SSM_S2
SSM_GB = LANES // SSM_GROUP
SSM_NGB = SSM_GROUPS // SSM_GB
SSM_XW = SSM_T * LANES
SSM_SW = SSM_GB * SSM_S2
SSM_TPT = MXU_TILE // LANES
SSM_NT = SSM_T // SSM_TPT
SCAN_ROWS = 8
N_SCAN_TABLES = 12
SSM_CHUNKS_PER_STEP = 512

F32 = jnp.float32
BF16 = jnp.bfloat16


def _params(*sem):
    return pltpu.CompilerParams(dimension_semantics=sem, vmem_limit_bytes=VMEM_LIMIT)


def _resident(shape):
    zeros = (0,) * len(shape)
    return pl.BlockSpec(shape, lambda *_: zeros, pipeline_mode=pl.Buffered(1))


def _rms(x, g):
    return x * lax.rsqrt(jnp.mean(x * x, axis=-1, keepdims=True) + RMS_EPS) * g


def _mm(a, b):
    return jnp.dot(a, b, preferred_element_type=F32)


def _sigmoid(x):
    return 0.5 * jnp.tanh(0.5 * x) + 0.5


def _row_parts(ref, n_parts):
    rows = ref.shape[0] // n_parts
    return [ref[i * rows:(i + 1) * rows, :] for i in range(n_parts)]


def _ffn(xs, gpre, gpost, w_in_ref, w_out_ref):
    hs = [_rms(x, gpre).astype(BF16) for x in xs]
    acts = [[] for _ in xs]
    for c0 in range(0, D_FF, FFN_CHUNK):
        c1 = min(c0 + FFN_CHUNK, D_FF)
        for h, a in zip(hs, acts):
            g = _mm(h, w_in_ref[:, c0:c1])
            u = _mm(h, w_in_ref[:, D_FF + c0:D_FF + c1])
            a.append((g * jax.nn.sigmoid(g) * u).astype(BF16))
    outs = [_mm(jnp.concatenate(a, axis=1), w_out_ref[...]) for a in acts]
    return [x + 0.5 * _rms(o, gpost) for x, o in zip(xs, outs)]


PROJ_SPLIT = (("q", ATTN_W, BF16), ("k", KV_W, F32), ("v", KV_W, F32), ("u", SSM_W, F32),
              ("qm", MEM_W, BF16), ("k2", KV2_W, BF16), ("v2", KV2_W, BF16))
N_PROJ_MATMUL = 5
PROJ_COLS = sum(w for _, w, _ in PROJ_SPLIT[:N_PROJ_MATMUL])
Q_SCALE = HEAD_DIM ** -0.5
assert math.frexp(Q_SCALE)[0] == 0.5, "power of two: scaling q before the bf16 rounding and the dot is exact"


def _twice_per_head_lanes(x):
    assert N_KV_HEADS == 2 and KV_W == LANES
    swapped = pltpu.roll(x, HEAD_DIM, 1)
    low = lax.broadcasted_iota(jnp.int32, x.shape, 1) < HEAD_DIM
    return jnp.concatenate([jnp.where(low, x, swapped), jnp.where(low, swapped, x)], axis=1)


def _two_group_specs(steps_a):
    first = lambda w: pl.BlockSpec((FFN_ROW_TILE, w), lambda i: (jnp.minimum(i, steps_a - 1), 0))
    second = lambda w: pl.BlockSpec((FFN_ROW_TILE, w), lambda i: (jnp.maximum(i - steps_a, 0), 0))
    return first, second


def _on_group(steps_a, run, refs_a, refs_b):
    i = pl.program_id(0)

    @pl.when(i < steps_a)
    def _():
        run(*refs_a)

    @pl.when(i >= steps_a)
    def _():
        run(*refs_b)


def _ffn1_proj_body(steps_a, n_later, xa_ref, xb_ref, gpre_ref, gpost_ref, wfi_ref, wfo_ref, gmix_ref, wp_ref,
                    *refs):
    later_f32, out_refs, later_bf16 = refs[:n_later], refs[n_later:len(refs) - n_later], refs[len(refs) - n_later:]
    _cast_chunks(later_f32, later_bf16)

    def run(x_ref, x1_ref, *proj_refs):
        x1s = _ffn(_row_parts(x_ref, FFN_PARTS), gpre_ref[...], gpost_ref[...], wfi_ref, wfo_ref)
        rows = x_ref.shape[0] // FFN_PARTS
        ps = [_mm(_rms(x1, gmix_ref[...]).astype(BF16), wp_ref[...]) for x1 in x1s]
        for i, (x1, p) in enumerate(zip(x1s, ps)):
            part = slice(i * rows, (i + 1) * rows)
            x1_ref[part, :] = x1
            col, cols = 0, {}
            for name, width, _ in PROJ_SPLIT[:N_PROJ_MATMUL]:
                cols[name] = p[:, col:col + width]
                col += width
            cols["q"] = cols["q"] * Q_SCALE
            cols["k2"], cols["v2"] = _twice_per_head_lanes(cols["k"]), _twice_per_head_lanes(cols["v"])
            for ref, (name, _, dtype) in zip(proj_refs, PROJ_SPLIT):
                ref[part, :] = cols[name].astype(dtype)

    n_out = len(out_refs) // 2
    _on_group(steps_a, run, (xa_ref,) + out_refs[:n_out], (xb_ref,) + out_refs[n_out:])


def _row_chunk_spec(w, steps, cols=None):
    rows = w.shape[0]
    n = next(n for n in range(min(steps, rows // BF16_ROWS), 0, -1)
             if rows % n == 0 and (rows // n) % BF16_ROWS == 0)
    return pl.BlockSpec((rows // n, cols or w.shape[1]), lambda i: (jnp.minimum(i, n - 1), 0))


def _cast_chunks(f32_refs, bf16_refs):
    for src, dst in zip(f32_refs, bf16_refs):
        dst[...] = src[...].astype(BF16)


def _ffn1_proj(xa, xb, gpre, gpost, wfi, wfo, gmix, wp, later_weights):
    steps_a, steps_b = xa.shape[0] // FFN_ROW_TILE, xb.shape[0] // FFN_ROW_TILE
    first, second = _two_group_specs(steps_a)
    widths = [D_MODEL] + [w for _, w, _ in PROJ_SPLIT]
    dtypes = [F32] + [d for _, _, d in PROJ_SPLIT]
    chunk_specs = [_row_chunk_spec(w, steps_a + steps_b) for w in later_weights]
    outs = pl.pallas_call(
        functools.partial(_ffn1_proj_body, steps_a, len(later_weights)),
        grid=(steps_a + steps_b,),
        in_specs=[first(D_MODEL), second(D_MODEL), _resident((1, D_MODEL)), _resident((1, D_MODEL)),
                  _resident(wfi.shape), _resident(wfo.shape), _resident((1, D_MODEL)),
                  _resident(wp.shape)] + chunk_specs,
        out_specs=[first(w) for w in widths] + [second(w) for w in widths] + chunk_specs,
        out_shape=[jax.ShapeDtypeStruct((x.shape[0], w), d) for x in (xa, xb) for w, d in zip(widths, dtypes)]
        + [jax.ShapeDtypeStruct(w.shape, BF16) for w in later_weights],
        compiler_params=_params("arbitrary"),
        name="ffn1_proj",
    )(xa, xb, gpre, gpost, wfi, wfo, gmix, wp, *later_weights)
    n = len(widths)
    return outs[:n], outs[n:2 * n], outs[2 * n:]


def _mem_kv_body(m_ref, g_ref, w_ref, k_ref, v_ref):
    kv = _mm(_rms(m_ref[...], g_ref[...]).astype(BF16), w_ref[...])
    n = m_ref.shape[0]
    for h in range(MEM_HEADS):
        k_ref[pl.ds(h, n, stride=MEM_HEADS), :] = kv[:, h * MEM_HEAD_DIM:(h + 1) * MEM_HEAD_DIM]
        v_ref[pl.ds(h, n, stride=MEM_HEADS), :] = kv[:, MEM_W + h * MEM_HEAD_DIM:MEM_W + (h + 1) * MEM_HEAD_DIM]


def _mem_kv(mem, g, w):
    n = mem.shape[0]
    out = pl.BlockSpec((ROW_TILE * MEM_HEADS, MEM_HEAD_DIM), lambda i: (i, 0))
    return pl.pallas_call(
        _mem_kv_body,
        grid=(n // ROW_TILE,),
        in_specs=[pl.BlockSpec((ROW_TILE, D_MODEL), lambda i: (i, 0)), _resident((1, D_MODEL)), _resident(w.shape)],
        out_specs=[out, out],
        out_shape=[jax.ShapeDtypeStruct((n * MEM_HEADS, MEM_HEAD_DIM), F32)] * 2,
        compiler_params=_params("parallel"),
        name="mem_kv",
    )(mem, g, w)


def _qk(q, k):
    return lax.dot_general(q, k, (((1,), (1,)), ((), ())), preferred_element_type=F32)


def _softmax_pv(scores, values, sinks=None):
    probs = []
    for i, s in enumerate(scores):
        m = jnp.max(s, axis=-1, keepdims=True)
        if sinks is not None:
            m = jnp.maximum(m, sinks[i])
        e = jnp.exp(s - m)
        den = jnp.sum(e, axis=-1, keepdims=True)
        if sinks is not None:
            den = den + jnp.exp(sinks[i] - m)
        probs.append((e * (1.0 / den)).astype(BF16))
    return [_mm(p, v) for p, v in zip(probs, values)]


def _gqa_queries(q):
    nq = q.shape[0]
    low = lax.broadcasted_iota(jnp.int32, (nq, LANES), 1) < HEAD_DIM
    zero = jnp.zeros((nq, LANES), BF16)
    stacks = []
    for g in range(N_KV_HEADS):
        rows = []
        for r in range(KV_REP):
            h = g * KV_REP + r
            q2 = q[:, (h // 2) * LANES:(h // 2 + 1) * LANES]
            rows.append(jnp.where(low, q2, zero) if h % 2 == 0 else jnp.where(low, zero, q2))
        stacks.append(jnp.concatenate(rows, axis=0))
    return stacks


def _gqa_outputs(outs, nq):
    low = lax.broadcasted_iota(jnp.int32, (nq, LANES), 1) < HEAD_DIM
    pairs = []
    for o in outs:
        for r in range(0, KV_REP, 2):
            pairs.append(jnp.where(low, o[r * nq:(r + 1) * nq], o[(r + 1) * nq:(r + 2) * nq]))
    return jnp.concatenate(pairs, axis=1)


def _mem_scores(qm, k_head):
    return [_qk(qm[:, h * MEM_HEAD_DIM:(h + 1) * MEM_HEAD_DIM], k_head(h).astype(BF16)) * (MEM_HEAD_DIM ** -0.5)
            for h in range(MEM_HEADS)]


def _mem_values(v_head):
    return [v_head(h).astype(BF16) for h in range(MEM_HEADS)]


def _row_heads(ref, i):
    return lambda h: ref[i, pl.ds(h, MEM_LEN, stride=MEM_HEADS), :]


def _kv_heads(x):
    return [x[:, g * LANES:(g + 1) * LANES] for g in range(N_KV_HEADS)]


def _merge_pre(x1, y, gmix_ref, wg_ref, wglu_ref):
    h = _rms(x1, gmix_ref[...]).astype(BF16)
    logits = [_mm(h, wg_ref[:, PROJ_COLS + j * D_MODEL:PROJ_COLS + (j + 1) * D_MODEL]) for j in range(3)]
    y = y.astype(BF16)
    return logits, _mm(y, wglu_ref[:, :D_MODEL]), _mm(y, wglu_ref[:, D_MODEL:])


def _mix_parts(parts, x1_ref, y_ref, gmix_ref, wg_ref, wglu_ref, wab_ref, wmb_ref, wo_ref, gpost_ref, x2_ref):
    x1s, ys = _row_parts(x1_ref, len(parts)), _row_parts(y_ref, len(parts))
    pres = [_merge_pre(x1, y, gmix_ref, wg_ref, wglu_ref) for x1, y in zip(x1s, ys)]
    branches = [assemble(_softmax_pv(scores, values, sinks), _softmax_pv(mem_scores, mem_values))
                for scores, values, sinks, mem_scores, mem_values, assemble in parts]
    projected = [(_mm(attn.astype(BF16), wab_ref[...]), _mm(memo.astype(BF16), wmb_ref[...]))
                 for attn, memo in branches]
    sig = _sigmoid
    merged = [sig(lg[0]) * pa + sig(lg[1]) * (ya * sig(yb)) + sig(lg[2]) * pm
              for (lg, ya, yb), (pa, pm) in zip(pres, projected)]
    outs = [_mm(m.astype(BF16), wo_ref[...]) for m in merged]
    rows = x1_ref.shape[0] // len(parts)
    for i, (x1, o) in enumerate(zip(x1s, outs)):
        x2_ref[i * rows:(i + 1) * rows, :] = x1 + _rms(o, gpost_ref[...])


def _mixer_prompt_body(q_ref, k_ref, kh_ref, v_ref, vh_ref, bias_ref, sink_ref, qm_ref, mk_ref, mv_ref,
                       x1_ref, y_ref, *merge_and_out_refs):
    i = pl.program_id(1)
    tq = q_ref.shape[1]
    kf = jnp.concatenate([kh_ref[0], k_ref[0]], axis=0)
    vf = jnp.concatenate([vh_ref[0], v_ref[0]], axis=0)
    band = WINDOW + CHUNK
    rows = tq // MIXER_PARTS
    parts = []
    for p in range(MIXER_PARTS):
        scores, values, sinks = [], [], []
        for lo in range(p * rows, (p + 1) * rows, CHUNK):
            valid = None
            if lo < WINDOW:
                key_pos = lax.broadcasted_iota(jnp.int32, (1, band), 1) + (i * tq + lo - WINDOW)
                valid = key_pos >= 0
            kb, vb = _kv_heads(kf[lo:lo + band]), _kv_heads(vf[lo:lo + band])
            for g, qs in enumerate(_gqa_queries(q_ref[0, lo:lo + CHUNK, :])):
                s = _qk(qs, kb[g]) + bias_ref[g]
                scores.append(s if valid is None else jnp.where(valid, s, NEG_INF))
                values.append(vb[g])
                sinks.append(sink_ref[g])
        mem_scores = _mem_scores(qm_ref[0, p * rows:(p + 1) * rows, :], _row_heads(mk_ref, 0))

        def assemble(outs, mem):
            attn = jnp.concatenate([_gqa_outputs(outs[c:c + N_KV_HEADS], CHUNK)
                                    for c in range(0, len(outs), N_KV_HEADS)], axis=0)
            return attn, jnp.concatenate(mem, axis=1)

        parts.append((scores, values, sinks, mem_scores, _mem_values(_row_heads(mv_ref, 0)), assemble))
    _mix_parts(parts, x1_ref, y_ref, *merge_and_out_refs)


def _merge_weight_specs(weights):
    return [_resident(w.shape) for w in weights]


def _mixer_prompt(q, k2, v2, bias, sink, qm, mk, mv, x1, y, merge_w):
    b, l, _ = q.shape
    tq = MIXER_TILE
    hb = tq // WINDOW
    nt = l // tq
    cur = lambda w: pl.BlockSpec((1, tq, w), lambda bi, i: (bi, i, 0))
    row = lambda w: pl.BlockSpec((tq, w), lambda bi, i: (bi * nt + i, 0))
    halo = pl.BlockSpec((1, WINDOW, KV2_W), lambda bi, i: (bi, jnp.maximum(i * hb - 1, 0), 0))
    mem = pl.BlockSpec((1, MEM_LEN * MEM_HEADS, MEM_HEAD_DIM), lambda bi, i: (bi, 0, 0))
    return pl.pallas_call(
        _mixer_prompt_body,
        grid=(b, nt),
        in_specs=[cur(ATTN_W), cur(KV2_W), halo, cur(KV2_W), halo, _resident(bias.shape),
                  _resident(sink.shape), cur(MEM_W), mem, mem, row(D_MODEL), row(SSM_W)]
        + _merge_weight_specs(merge_w),
        out_specs=row(D_MODEL),
        out_shape=jax.ShapeDtypeStruct((b * l, D_MODEL), F32),
        compiler_params=_params("parallel", "parallel"),
        name="mixer_prompt",
    )(q, k2, k2, v2, v2, bias, sink, qm, mk, mv, x1, y, *merge_w)


def _mixer_sample_body(q_ref, k_ref, v_ref, bias_ref, sink_ref, qm_ref, mk_ref, mv_ref,
                       x1_ref, y_ref, *merge_and_out_refs):
    nb, nq = q_ref.shape[0], q_ref.shape[1]
    per_part = nb // MIXER_PARTS
    parts = []
    for p in range(MIXER_PARTS):
        scores, values, sinks, mem_scores, mem_values = [], [], [], [], []
        for b in range(p * per_part, (p + 1) * per_part):
            kb, vb = _kv_heads(k_ref[b]), _kv_heads(v_ref[b])
            for g, qs in enumerate(_gqa_queries(q_ref[b])):
                scores.append(_qk(qs, kb[g]) + bias_ref[g])
                values.append(vb[g])
                sinks.append(sink_ref[g])
            mem_scores += _mem_scores(qm_ref[b], _row_heads(mk_ref, b))
            mem_values += _mem_values(_row_heads(mv_ref, b))

        def assemble(outs, mem):
            attn = jnp.concatenate([_gqa_outputs(outs[j:j + N_KV_HEADS], nq)
                                    for j in range(0, len(outs), N_KV_HEADS)], axis=0)
            memo = jnp.concatenate([jnp.concatenate(mem[j:j + MEM_HEADS], axis=1)
                                    for j in range(0, len(mem), MEM_HEADS)], axis=0)
            return attn, memo

        parts.append((scores, values, sinks, mem_scores, mem_values, assemble))
    _mix_parts(parts, x1_ref, y_ref, *merge_and_out_refs)


def _mixer_sample(q, kk2, vv2, bias, sink, qm, mk, mv, x1, y, merge_w):
    b, s, _ = q.shape
    nb = SAMPLE_MIXER_TILE // s
    nk = kk2.shape[1]
    blk = lambda *shape: pl.BlockSpec((nb,) + shape, lambda i: (i,) + (0,) * len(shape))
    row = lambda w: pl.BlockSpec((nb * s, w), lambda i: (i, 0))
    return pl.pallas_call(
        _mixer_sample_body,
        grid=(b // nb,),
        in_specs=[blk(s, ATTN_W), blk(nk, KV2_W), blk(nk, KV2_W),
                  _resident(bias.shape), _resident(sink.shape),
                  blk(s, MEM_W), blk(*mk.shape[1:]), blk(*mv.shape[1:]), row(D_MODEL), row(SSM_W)]
        + _merge_weight_specs(merge_w),
        out_specs=row(D_MODEL),
        out_shape=jax.ShapeDtypeStruct((b * s, D_MODEL), F32),
        compiler_params=_params("parallel"),
        name="mixer_sample",
    )(q, kk2, vv2, bias, sink, qm, mk, mv, x1, y, *merge_w)


def _ssm_weights_body(n_first, lam_re_row, lam_im_row, lam_re_col, lam_im_col, log_dt, b_re_t, b_im_t,
                      c_re, c_im, c_re_lanes, c_im_lanes, sel_ref, *refs):
    first_f32, (wconv_ref, wst_ref, wot_ref, a_tab_ref), first_bf16 = (
        refs[:n_first], refs[n_first:n_first + 4], refs[n_first + 4:])
    _cast_chunks(first_f32, first_bf16)
    _ssm_block_weights(lam_re_row, lam_im_row, lam_re_col, lam_im_col, log_dt, b_re_t, b_im_t,
                       c_re, c_im, c_re_lanes, c_im_lanes, sel_ref, wconv_ref, wst_ref, wot_ref, a_tab_ref)


def _ssm_block_weights(lam_re_row, lam_im_row, lam_re_col, lam_im_col, log_dt, b_re_t, b_im_t,
                       c_re, c_im, c_re_lanes, c_im_lanes, sel_ref,
                       wconv_ref, wst_ref, wot_ref, a_tab_ref):
    wst_ref[0] = jnp.zeros(wst_ref.shape[1:], BF16)
    wot_ref[0] = jnp.zeros(wot_ref.shape[1:], BF16)
    krows = []
    for gl in range(SSM_GB):
        krows.append(_ssm_group_weights(
            gl, *(r[gl] for r in (lam_re_row, lam_im_row, lam_re_col, lam_im_col, log_dt, b_re_t, b_im_t,
                                  c_re, c_im, c_re_lanes, c_im_lanes)),
            wst_ref, wot_ref, a_tab_ref))
    kstack = jnp.concatenate(krows, axis=0).astype(BF16)
    for d in range(SSM_NT):
        for ti in range(SSM_TPT):
            blk = _mm(kstack, sel_ref[d * SSM_TPT + ti])
            r0 = (SSM_NT - 1 - d) * MXU_TILE + ti * LANES
            for gl in range(SSM_GB):
                piece = blk[gl * SSM_GROUP:(gl + 1) * SSM_GROUP]
                piece = (piece if gl == 0 else pltpu.roll(piece, gl * SSM_GROUP, 1)).astype(BF16)
                r = r0 + gl * SSM_GROUP
                wconv_ref[0, r:r + SSM_GROUP, :MXU_TILE] = piece
                if r0 >= MXU_TILE:
                    wconv_ref[0, r - MXU_TILE:r - MXU_TILE + SSM_GROUP, MXU_TILE:] = piece
    wconv_ref[0, (SSM_NT - 1) * MXU_TILE:, MXU_TILE:] = jnp.zeros((MXU_TILE, MXU_TILE), BF16)


def _ssm_group_weights(gl, lam_re_row, lam_im_row, lam_re_col, lam_im_col, log_dt, b_re_t, b_im_t,
                       c_re, c_im, c_re_lanes, c_im_lanes,
                       wst_ref, wot_ref, a_tab_ref):
    dt = jnp.exp(log_dt)
    rows = lambda t: slice(t * LANES + gl * SSM_GROUP, t * LANES + (gl + 1) * SSM_GROUP)
    lanes = slice(gl * SSM_S2, (gl + 1) * SSM_S2)

    def zoh_coef(lr, li):
        mag = jnp.exp(lr * dt)
        a_re, a_im = mag * jnp.cos(li * dt), mag * jnp.sin(li * dt)
        den = lr * lr + li * li
        return ((a_re - 1.0) * lr + a_im * li) / den, (a_im * lr - (a_re - 1.0) * li) / den

    def a_power(lr, li, n):
        mag = jnp.exp(lr * dt * n)
        return mag * jnp.cos(li * dt * n), mag * jnp.sin(li * dt * n)

    lr, li = lam_re_row, lam_im_row
    cr, ci = zoh_coef(lr, li)
    n_rows = jnp.minimum(lax.broadcasted_iota(jnp.int32, (SSM_T + SCAN_ROWS, 1), 0), SSM_T).astype(F32)
    pw_r, pw_i = a_power(lr, li, n_rows)

    def over_channels(pw, exps):
        return jnp.concatenate([jnp.broadcast_to(pw[e:e + 1], (SSM_GROUP, SSM_STATE)) for e in exps], axis=0)

    back = [SSM_T - 1 - t for t in range(SSM_T)]
    pr, pi = over_channels(pw_r, back), over_channels(pw_i, back)
    zr, zi = pr * cr - pi * ci, pr * ci + pi * cr
    over_positions = lambda x: jnp.concatenate([x] * SSM_T, axis=0)
    br, bi = over_positions(b_re_t), over_positions(b_im_t)
    wstate = jnp.concatenate([zr * br - zi * bi, zr * bi + zi * br], axis=1).astype(BF16)

    fwd = [t + 1 for t in range(SSM_T)]
    qr, qi = over_channels(pw_r, fwd), over_channels(pw_i, fwd)
    ccr, cci = over_positions(c_re), over_positions(c_im)
    wout_t = jnp.concatenate([ccr * qr - cci * qi, -(ccr * qi + cci * qr)], axis=1).astype(BF16)
    for t in range(SSM_T):
        wst_ref[0, rows(t), lanes] = wstate[t * SSM_GROUP:(t + 1) * SSM_GROUP]
        wot_ref[0, rows(t), lanes] = wout_t[t * SSM_GROUP:(t + 1) * SSM_GROUP]

    lrc, lic = lam_re_col, lam_im_col
    hi = lax.Precision.HIGHEST
    n_lanes = jnp.minimum(lax.broadcasted_iota(jnp.int32, (1, LANES), 1), SSM_T).astype(F32)
    pc_r, pc_i = a_power(lrc, lic, n_lanes)
    spread = (lax.broadcasted_iota(jnp.int32, (LANES, SSM_K), 0)
              == lax.broadcasted_iota(jnp.int32, (LANES, SSM_K), 1) // SSM_GROUP).astype(F32)
    gr = jnp.dot(pc_r, spread, precision=hi, preferred_element_type=F32)
    gi = jnp.dot(pc_i, spread, precision=hi, preferred_element_type=F32)
    clr, cli = c_re_lanes, c_im_lanes
    g_re, g_im = clr * gr - cli * gi, clr * gi + cli * gr
    btr, bti = b_re_t, b_im_t
    bbr, bbi = cr * btr - ci * bti, cr * bti + ci * btr
    krow = (jnp.dot(bbr, g_re, precision=hi, preferred_element_type=F32)
            - jnp.dot(bbi, g_im, precision=hi, preferred_element_type=F32))

    idx = lax.broadcasted_iota(jnp.int32, (N_SCAN_TABLES * SCAN_ROWS, SSM_STATE), 0)
    tab, r = idx // SCAN_ROWS, idx % SCAN_ROWS
    stride = jnp.where(tab < 2, 1, jnp.where(tab < 4, 2, 4))
    n = jnp.where(tab < 6, stride, jnp.where(tab < 8, r, SCAN_ROWS))
    keep = jnp.logical_or(tab >= 6, r >= stride)
    positions = jnp.where(tab < 10, n * SCAN_T, SSM_T)
    er, ei = a_power(lr, li, positions.astype(F32))
    er, ei = jnp.where(keep, er, 0.0), jnp.where(keep, ei, 0.0)
    odd = tab % 2 == 1
    tabs = jnp.concatenate([jnp.where(odd, -ei, er), jnp.where(odd, ei, er)], axis=1)
    a_tab_ref[:, :, lanes] = tabs.reshape(N_SCAN_TABLES, SCAN_ROWS, SSM_S2)
    return krow


def _lag_selectors():
    sel = np.zeros((SSM_NT, SSM_TPT, SSM_K, MXU_TILE), np.float32)
    ch = np.arange(SSM_GROUP)
    for d in range(SSM_NT):
        for ti in range(SSM_TPT):
            for to in range(SSM_TPT):
                lag = SSM_TPT * d + to - ti
                if lag >= 0:
                    sel[d, ti, lag * SSM_GROUP + ch, to * LANES + ch] = 1.0
    return jnp.asarray(sel.reshape(SSM_NT * SSM_TPT, SSM_K, MXU_TILE), BF16)


def _ssm_weights(lam_re, lam_im, log_dt, b_re, b_im, c_re, c_im, first_weights):
    g, p, c = b_re.shape
    row3 = lambda x: x.reshape(g, 1, p)
    col3 = lambda x: x.reshape(g, p, 1)
    t3 = lambda x: jnp.transpose(x, (0, 2, 1))
    c_lanes = lambda x: jnp.tile(t3(x), (1, 1, SSM_T))
    args = (row3(lam_re), row3(lam_im), col3(lam_re), col3(lam_im), log_dt.reshape(g, 1, 1),
            t3(b_re), t3(b_im), c_re, c_im, c_lanes(c_re), c_lanes(c_im))
    sel = _lag_selectors()
    spec = lambda x: pl.BlockSpec((SSM_GB,) + x.shape[1:], lambda i: (i, 0, 0))
    per_gb = lambda *shape: pl.BlockSpec((1,) + shape, lambda i: (i, 0, 0))
    chunk_specs = [_row_chunk_spec(w, SSM_NGB, cols) for w, cols in first_weights]
    outs = pl.pallas_call(
        functools.partial(_ssm_weights_body, len(first_weights)),
        grid=(SSM_NGB,),
        in_specs=[spec(a) for a in args] + [_resident(sel.shape)] + chunk_specs,
        out_specs=[per_gb(SSM_NT * MXU_TILE, 2 * MXU_TILE), per_gb(SSM_XW, SSM_SW), per_gb(SSM_XW, SSM_SW),
                   pl.BlockSpec((N_SCAN_TABLES, SCAN_ROWS, SSM_SW), lambda i: (0, 0, i))] + chunk_specs,
        out_shape=[jax.ShapeDtypeStruct((SSM_NGB, SSM_NT * MXU_TILE, 2 * MXU_TILE), BF16),
                   jax.ShapeDtypeStruct((SSM_NGB, SSM_XW, SSM_SW), BF16),
                   jax.ShapeDtypeStruct((SSM_NGB, SSM_XW, SSM_SW), BF16),
                   jax.ShapeDtypeStruct((N_SCAN_TABLES, SCAN_ROWS, SSM_WIDTH), F32)]
        + [jax.ShapeDtypeStruct((w.shape[0], cols or w.shape[1]), BF16) for w, cols in first_weights],
        compiler_params=_params("arbitrary"),
        name="ssm_weights",
    )(*args, sel, *[w for w, _ in first_weights])
    return outs[:4], outs[4:]


def _swap_halves(s):
    ax = s.ndim - 1
    return jnp.concatenate([pltpu.roll(s[..., l:l + SSM_S2], SSM_STATE, ax)
                            for l in range(0, s.shape[ax], SSM_S2)], axis=ax)


def _ssm_body(row_sets, slab_rows, u_ref, wconv_ref, wst_ref, wot_ref, d_ref, a_tab_ref, s0_ref,
              y_ref, s_ref, v_scr, s_scr):
    def piece(ref_set, first, n, stride, t):
        return ref_set, pl.ds(first + t, n, stride=stride)

    @pl.when(pl.program_id(1) == 0)
    def _():
        s_scr[...] = s0_ref[0]

    x = jnp.concatenate(
        [jnp.concatenate([u_ref[piece(*rs, t)] for t in range(SSM_T)], axis=1) for rs in row_sets],
        axis=0).astype(BF16)
    lane_blocks = range(SSM_SW // LANES)

    def v_read(rows):
        return jnp.concatenate([v_scr[j, rows, :] for j in lane_blocks], axis=1)

    def v_write(rows, value):
        for j in lane_blocks:
            v_scr[j, rows, :] = value[:, j * LANES:(j + 1) * LANES]

    n_rows = v_scr.shape[1]
    all_rows = slice(0, n_rows)
    v_write(all_rows, _mm(x, wst_ref[0]))
    conv = []
    for jo in range(0, SSM_NT, 2):
        pair = _mm(x[:, :(jo + 2) * MXU_TILE], wconv_ref[0, (SSM_NT - 2 - jo) * MXU_TILE:, :])
        conv += [pair[:, MXU_TILE:], pair[:, :MXU_TILE]]

    a_mul, a_swap = a_tab_ref[10, 0:1, :], a_tab_ref[11, 0:1, :]
    if slab_rows:
        s = s_scr[...]
        for c in range(n_rows // slab_rows):
            rows = slice(c * slab_rows, (c + 1) * slab_rows)
            inc = v_read(rows)
            v_write(rows, s)
            s = a_mul * s + a_swap * _swap_halves(s) + inc
        s_scr[...] = s
    else:
        step = lambda s, inc: a_mul * s + a_swap * _swap_halves(s) + inc
        set_starts = np.cumsum([0] + [rs[2] for rs in row_sets])
        runs = [(int(r0), rs[2] // SCAN_RUN) for r0, rs in zip(set_starts, row_sets)]
        incs = [jnp.concatenate([v_read(pl.ds(r0 + k, n, stride=SCAN_RUN)) for r0, n in runs], axis=0)
                for k in range(SCAN_RUN)]
        merged = incs[0]
        for k in range(1, SCAN_RUN):
            merged = step(merged, incs[k])
        n_blocks = merged.shape[0] // SCAN_ROWS
        xs = merged.reshape(n_blocks, SCAN_ROWS, SSM_SW)
        for k in range(3):
            sh = pltpu.roll(xs, 1 << k, 1)
            xs = xs + a_tab_ref[2 * k] * sh + a_tab_ref[2 * k + 1] * _swap_halves(sh)
        carries, blk = [], 0
        for j, (_, n) in enumerate(runs):
            carry = s_scr[j * SCAN_ROWS:(j + 1) * SCAN_ROWS, :]
            for _ in range(n // SCAN_ROWS):
                carries.append(carry)
                last = jnp.broadcast_to(xs[blk, SCAN_ROWS - 1:, :], carry.shape)
                carry = last + a_tab_ref[8] * carry + a_tab_ref[9] * _swap_halves(carry)
                blk += 1
            s_scr[j * SCAN_ROWS:(j + 1) * SCAN_ROWS, :] = carry
        carries = jnp.stack(carries, axis=0)
        not_first = lax.broadcasted_iota(jnp.int32, (1, SCAN_ROWS, 1), 1) >= 1
        enter = (jnp.where(not_first, pltpu.roll(xs, 1, 1), 0.0)
                 + a_tab_ref[6] * carries + a_tab_ref[7] * _swap_halves(carries)).reshape(merged.shape)
        for k in range(SCAN_RUN):
            h0 = 0
            for r0, n in runs:
                v_write(pl.ds(r0 + k, n, stride=SCAN_RUN), enter[h0:h0 + n])
                h0 += n
            enter = step(enter, incs[k])
    s_ref[0] = s_scr[...]

    enter = v_read(all_rows).astype(BF16)
    d = d_ref[...]
    carried = _qk(enter, wot_ref[0])
    for jo in range(SSM_NT):
        yt = conv[jo] + carried[:, jo * MXU_TILE:(jo + 1) * MXU_TILE]
        r0 = 0
        for rs in row_sets:
            for tl in range(SSM_TPT):
                idx = piece(*rs, jo * SSM_TPT + tl)
                y_ref[idx] = yt[r0:r0 + rs[2], tl * LANES:(tl + 1) * LANES] + u_ref[idx] * d
            r0 += rs[2]


def _ssm(u, row_sets, slab_rows, block_rows, wconv, wst, wo, d_lanes, a_tab, s0):
    ns, r, _ = u.shape
    chunk_rows = sum(rs[2] for rs in row_sets)
    carry_rows = s0.shape[1]
    blk = pl.BlockSpec((ns, block_rows, LANES), lambda gb, i: (0, i, gb))
    per_gb = lambda x: pl.BlockSpec((1,) + x.shape[1:], lambda gb, i: (gb, 0, 0))
    return pl.pallas_call(
        functools.partial(_ssm_body, row_sets, slab_rows),
        grid=(SSM_NGB, r // block_rows),
        in_specs=[blk, per_gb(wconv), per_gb(wst), per_gb(wo),
                  pl.BlockSpec((1, LANES), lambda gb, i: (0, gb)),
                  pl.BlockSpec((N_SCAN_TABLES, SCAN_ROWS, SSM_SW), lambda gb, i: (0, 0, gb)),
                  per_gb(s0)],
        out_specs=[blk, per_gb(s0)],
        out_shape=[jax.ShapeDtypeStruct(u.shape, F32), jax.ShapeDtypeStruct(s0.shape, F32)],
        scratch_shapes=[pltpu.VMEM((SSM_SW // LANES, chunk_rows, LANES), F32),
                        pltpu.VMEM((carry_rows, SSM_SW), F32)],
        compiler_params=_params("parallel", "arbitrary"),
        name="ssm_scan",
    )(u, wconv, wst, wo, d_lanes, a_tab, s0)


def _ssm_branch(u, s0, ssm_w):
    b, l, _ = u.shape
    nc = l // SSM_T
    by_gb = lambda s: s.reshape(s.shape[0], SSM_NGB, SSM_SW).transpose(1, 0, 2)
    if b % SCAN_ROWS == 0:
        row_sets = tuple((0, c * SSM_T, b, l) for c in range(nc))
        y, s_last = _ssm(u.reshape(1, b * l, SSM_W), row_sets, b, b * l, *ssm_w, by_gb(s0))
    else:
        cps = min(nc, SSM_CHUNKS_PER_STEP)
        row_sets = tuple((j, 0, cps, SSM_T) for j in range(b))
        y, s_last = _ssm(u, row_sets, 0, cps * SSM_T, *ssm_w, by_gb(jnp.repeat(s0, SCAN_ROWS, axis=0)))
        s_last = s_last[:, ::SCAN_ROWS]
    s_last = s_last.transpose(1, 0, 2).reshape(b, SSM_GROUPS, 2, SSM_STATE)
    return y.reshape(b, l, SSM_W), s_last[:, :, 0], s_last[:, :, 1]


def _ffn2_body(steps_a, xa_ref, xb_ref, gpre_ref, gpost_ref, wfi_ref, wfo_ref, oa_ref, ob_ref):
    def run(x_ref, o_ref):
        outs = _ffn(_row_parts(x_ref, FFN_PARTS), gpre_ref[...], gpost_ref[...], wfi_ref, wfo_ref)
        rows = x_ref.shape[0] // FFN_PARTS
        for i, o in enumerate(outs):
            o_ref[i * rows:(i + 1) * rows, :] = o

    _on_group(steps_a, run, (xa_ref, oa_ref), (xb_ref, ob_ref))


def _ffn2(xa, xb, gpre, gpost, wfi, wfo):
    steps_a, steps_b = xa.shape[0] // FFN_ROW_TILE, xb.shape[0] // FFN_ROW_TILE
    first, second = _two_group_specs(steps_a)
    vec = _resident((1, D_MODEL))
    return pl.pallas_call(
        functools.partial(_ffn2_body, steps_a),
        grid=(steps_a + steps_b,),
        in_specs=[first(D_MODEL), second(D_MODEL), vec, vec, _resident(wfi.shape), _resident(wfo.shape)],
        out_specs=[first(D_MODEL), second(D_MODEL)],
        out_shape=[jax.ShapeDtypeStruct(xa.shape, F32), jax.ShapeDtypeStruct(xb.shape, F32)],
        compiler_params=_params("arbitrary"),
        name="ffn2",
    )(xa, xb, gpre, gpost, wfi, wfo)


def _t5_bucket(rel):
    half = N_BUCKETS // 2
    max_exact = half // 2
    ret = (rel > 0).astype(np.int32) * half
    n = np.abs(rel)
    large = max_exact + (np.log(np.maximum(n, 1) / max_exact) / math.log(MAX_DISTANCE / max_exact)
                         * (half - max_exact)).astype(np.int32)
    large = np.minimum(large, half - 1)
    return ret + np.where(n < max_exact, n, large)


def _band_bias(rel_table, n_q, n_back, n_k):
    i = np.arange(n_q)[:, None]
    j = np.arange(n_k)[None, :]
    bucket = _t5_bucket((j - n_back) - i).reshape(-1)
    onehot = np.zeros((N_BUCKETS, bucket.size), np.float32)
    onehot[bucket, np.arange(bucket.size)] = 1.0
    b = jnp.dot(rel_table.astype(F32).T, jnp.asarray(onehot), precision=lax.Precision.HIGHEST)
    return b.reshape(N_KV_HEADS, KV_REP * n_q, n_k)


def _sink_rows(sink, n_q):
    return jnp.repeat(sink.astype(F32).reshape(N_KV_HEADS, KV_REP), n_q, axis=1)[:, :, None]


def _twice_per_head(x, axis):
    shape = x.shape
    x = x.reshape(shape[:axis] + (N_KV_HEADS, 1, HEAD_DIM) + shape[axis + 1:])
    x = jnp.concatenate([x, x], axis=axis + 1)
    return x.reshape(shape[:axis] + (KV2_W,) + shape[axis + 1:])


def kernel(x_prompt, x_sample, cache_swa_k, cache_swa_v, cache_mem_k, cache_mem_v, state_ssm_re, state_ssm_im, mem_prompt, rel_bias_table, ff1_pre_g, ff1_post_g, w_ff1_in, w_ff1_out, mix_pre_g, mix_post_g, w_in, mem_norm_g, w_mem_kv, attn_sink, ssm_lambda_re, ssm_lambda_im, ssm_log_dt, ssm_b_re, ssm_b_im, ssm_c_re, ssm_c_im, ssm_d, w_ssm_glu, w_attn_br, w_mem_br, w_out, ff2_pre_g, ff2_post_g, w_ff2_in, w_ff2_out):
    assert ff1_pre_g.shape[0] == 1, "single-layer step"
    bp, lp, _ = x_prompt.shape
    bs, ls, _ = x_sample.shape
    vec = lambda g: g[0].reshape(1, D_MODEL).astype(F32)

    (wconv, wst, wot, a_tab), (wfi1, wfo1, wp) = _ssm_weights(
        ssm_lambda_re[0], ssm_lambda_im[0], ssm_log_dt[0], ssm_b_re[0], ssm_b_im[0], ssm_c_re[0], ssm_c_im[0],
        [(w_ff1_in[0], None), (w_ff1_out[0], None), (w_in[0], PROJ_COLS)])
    ssm_w = (wconv, wst, wot, ssm_d[0].astype(F32).reshape(1, SSM_W), a_tab)

    later = (w_ff2_in, w_ff2_out, w_in, w_ssm_glu, w_attn_br, w_mem_br, w_out, w_mem_kv)
    proj_p, proj_s, (wfi2, wfo2, w_in16, wglu, wab, wmb, wo, wmkv) = _ffn1_proj(
        x_prompt.reshape(bp * lp, D_MODEL), x_sample.reshape(bs * ls, D_MODEL), vec(ff1_pre_g), vec(ff1_post_g),
        wfi1, wfo1, vec(mix_pre_g), wp, [w[0] for w in later])

    mk_p, mv_p = _mem_kv(mem_prompt.reshape(bp * MEM_LEN, D_MODEL), vec(mem_norm_g), wmkv)

    def mix(x_shape, proj, mixer, s0, mem_k, mem_v):
        b, l, _ = x_shape
        r3 = lambda t: t.reshape(b, l, t.shape[-1])
        x1, q, k, v, u, qm, k2, v2 = proj
        y_ssm, s_re, s_im = _ssm_branch(r3(u), s0, ssm_w)
        x2 = mixer(r3(q), r3(k2), r3(v2), r3(qm), mem_k, mem_v, x1, y_ssm.reshape(b * l, SSM_W))
        return x2, r3(k), r3(v), s_re, s_im

    merge_w = (vec(mix_pre_g), w_in16, wglu, wab, wmb, wo, vec(mix_post_g))

    def mixer_prompt(q, k2, v2, qm, mem_k, mem_v, x1, y_ssm):
        bias = _band_bias(rel_bias_table, CHUNK, WINDOW, WINDOW + CHUNK)
        return _mixer_prompt(q, k2, v2, bias, _sink_rows(attn_sink[0], CHUNK), qm, mem_k, mem_v, x1, y_ssm, merge_w)

    def mixer_sample(q, k2, v2, qm, mem_k, mem_v, x1, y_ssm):
        n_back = cache_swa_k.shape[2]
        cache2 = lambda c: _twice_per_head(c[0].reshape(bs, n_back, KV_W).astype(BF16), 2)
        kk = jnp.concatenate([cache2(cache_swa_k), k2], axis=1)
        vv = jnp.concatenate([cache2(cache_swa_v), v2], axis=1)
        bias = _band_bias(rel_bias_table, ls, n_back, n_back + ls)
        return _mixer_sample(q, kk, vv, bias, _sink_rows(attn_sink[0], ls), qm, mem_k, mem_v, x1, y_ssm, merge_w)

    x2p, pk, pv, pre, pim = mix(x_prompt.shape, proj_p, mixer_prompt, jnp.zeros((bp, SSM_WIDTH), F32),
                                mk_p.reshape(bp, MEM_LEN * MEM_HEADS, MEM_HEAD_DIM),
                                mv_p.reshape(bp, MEM_LEN * MEM_HEADS, MEM_HEAD_DIM))
    s0 = jnp.stack([state_ssm_re[0], state_ssm_im[0]], axis=2).reshape(bs, SSM_WIDTH).astype(F32)
    x2s, sk, sv, sre, sim = mix(x_sample.shape, proj_s, mixer_sample, s0,
                                cache_mem_k[0].reshape(bs, MEM_LEN * MEM_HEADS, MEM_HEAD_DIM),
                                cache_mem_v[0].reshape(bs, MEM_LEN * MEM_HEADS, MEM_HEAD_DIM))
    yp, ys = _ffn2(x2p, x2s, vec(ff2_pre_g), vec(ff2_post_g), wfi2, wfo2)
    yp, ys = yp.reshape(x_prompt.shape), ys.reshape(x_sample.shape)

    n_keep = min(WINDOW, lp)
    heads = lambda t: t.reshape(t.shape[0], t.shape[1], N_KV_HEADS, HEAD_DIM)[None]
    mem_heads = lambda t: t.reshape(bp, MEM_LEN, MEM_HEADS, MEM_HEAD_DIM)[None]
    return (yp, ys, heads(pk[:, -n_keep:]), heads(pv[:, -n_keep:]), mem_heads(mk_p), mem_heads(mv_p),
            pre[None], pim[None], heads(sk), heads(sv), sre[None], sim[None])
```

```python
import functools
import math

import numpy as np
import jax
import jax.numpy as jnp
from jax import lax
from jax.experimental import pallas as pl
from jax.experimental.pallas import tpu as pltpu

D_MODEL = 1024
CHUNK = 64
WINDOW = 128
HEAD_DIM = 64
MIX_W = D_MODEL // 2
N_HEADS = MIX_W // HEAD_DIM
N_KV_HEADS = N_HEADS // 4
KV_REP = N_HEADS // N_KV_HEADS
ATTN_W = N_HEADS * HEAD_DIM
KV_W = N_KV_HEADS * HEAD_DIM
SSM_GROUP = 16
SSM_W = MIX_W
SSM_GROUPS = SSM_W // SSM_GROUP
SSM_STATE = 64
MEM_LEN = 256
MEM_HEADS = 4
MEM_HEAD_DIM = MIX_W // MEM_HEADS
MEM_W = MEM_HEADS * MEM_HEAD_DIM
D_FF = 128 * ((8 * D_MODEL // 3 + 127) // 128)
N_BUCKETS = 32
MAX_DISTANCE = 128
RMS_EPS = 1e-6
NEG_INF = -1e30

LANES = 128
BF16_ROWS = 16
MXU_TILE = 256
ROW_TILE = 256
MIXER_TILE = 512
SAMPLE_MIXER_TILE = 256
MIXER_PARTS = 2
FFN_ROW_TILE = 512
FFN_PARTS = 2
FFN_CHUNK = 2 * MXU_TILE
VMEM_LIMIT = 60 * 1024 * 1024

KV2_W = N_KV_HEADS * LANES

SSM_T = 8
SCAN_RUN = 4
SCAN_T = SSM_T * SCAN_RUN
SSM_K = SSM_T * SSM_GROUP
SSM_S2 = 2 * SSM_STATE
SSM_WIDTH = SSM_GROUPS * SSM_S2
SSM_GB = LANES // SSM_GROUP
SSM_NGB = SSM_GROUPS // SSM_GB
SSM_XW = SSM_T * LANES
SSM_SW = SSM_GB * SSM_S2
SSM_TPT = MXU_TILE // LANES
SSM_NT = SSM_T // SSM_TPT
SCAN_ROWS = 8
N_SCAN_TABLES = 12
SSM_CHUNKS_PER_STEP = 512

F32 = jnp.float32
BF16 = jnp.bfloat16


def _params(*sem):
    return pltpu.CompilerParams(dimension_semantics=sem, vmem_limit_bytes=VMEM_LIMIT)


def _resident(shape):
    zeros = (0,) * len(shape)
    return pl.BlockSpec(shape, lambda *_: zeros, pipeline_mode=pl.Buffered(1))


def _rms(x, g):
    return x * lax.rsqrt(jnp.mean(x * x, axis=-1, keepdims=True) + RMS_EPS) * g


def _mm(a, b):
    return jnp.dot(a, b, preferred_element_type=F32)


def _sigmoid(x):
    return 0.5 * jnp.tanh(0.5 * x) + 0.5


def _row_parts(ref, n_parts):
    rows = ref.shape[0] // n_parts
    return [ref[i * rows:(i + 1) * rows, :] for i in range(n_parts)]


def _ffn(xs, gpre, gpost, w_in_ref, w_out_ref):
    hs = [_rms(x, gpre).astype(BF16) for x in xs]
    acts = [[] for _ in xs]
    for c0 in range(0, D_FF, FFN_CHUNK):
        c1 = min(c0 + FFN_CHUNK, D_FF)
        for h, a in zip(hs, acts):
            g = _mm(h, w_in_ref[:, c0:c1])
            u = _mm(h, w_in_ref[:, D_FF + c0:D_FF + c1])
            a.append((g * jax.nn.sigmoid(g) * u).astype(BF16))
    outs = [_mm(jnp.concatenate(a, axis=1), w_out_ref[...]) for a in acts]
    return [x + 0.5 * _rms(o, gpost) for x, o in zip(xs, outs)]


PROJ_SPLIT = (("q", ATTN_W, BF16), ("k", KV_W, F32), ("v", KV_W, F32), ("u", SSM_W, F32),
              ("qm", MEM_W, BF16), ("k2", KV2_W, BF16), ("v2", KV2_W, BF16))
N_PROJ_MATMUL = 5
PROJ_COLS = sum(w for _, w, _ in PROJ_SPLIT[:N_PROJ_MATMUL])
Q_SCALE = HEAD_DIM ** -0.5
MEM_SCALE = MEM_HEAD_DIM ** -0.5
assert math.frexp(Q_SCALE)[0] == 0.5, "power of two: scaling q before the bf16 rounding and the dot is exact"


def _twice_per_head_lanes(x):
    assert N_KV_HEADS == 2 and KV_W == LANES
    swapped = pltpu.roll(x, HEAD_DIM, 1)
    low = lax.broadcasted_iota(jnp.int32, x.shape, 1) < HEAD_DIM
    return jnp.concatenate([jnp.where(low, x, swapped), jnp.where(low, swapped, x)], axis=1)


def _two_group_specs(steps_a):
    first = lambda w: pl.BlockSpec((FFN_ROW_TILE, w), lambda i: (jnp.minimum(i, steps_a - 1), 0))
    second = lambda w: pl.BlockSpec((FFN_ROW_TILE, w), lambda i: (jnp.maximum(i - steps_a, 0), 0))
    return first, second


def _on_group(steps_a, run, refs_a, refs_b):
    i = pl.program_id(0)

    @pl.when(i < steps_a)
    def _():
        run(*refs_a)

    @pl.when(i >= steps_a)
    def _():
        run(*refs_b)


def _ffn1_proj_body(steps_a, n_later, xa_ref, xb_ref, gpre_ref, gpost_ref, wfi_ref, wfo_ref, gmix_ref, wp_ref,
                    *refs):
    later_f32, out_refs, later_bf16 = refs[:n_later], refs[n_later:len(refs) - n_later], refs[len(refs) - n_later:]
    _cast_chunks(later_f32, later_bf16)

    def run(x_ref, x1_ref, *proj_refs):
        x1s = _ffn(_row_parts(x_ref, FFN_PARTS), gpre_ref[...], gpost_ref[...], wfi_ref, wfo_ref)
        rows = x_ref.shape[0] // FFN_PARTS
        ps = [_mm(_rms(x1, gmix_ref[...]).astype(BF16), wp_ref[...]) for x1 in x1s]
        for i, (x1, p) in enumerate(zip(x1s, ps)):
            part = slice(i * rows, (i + 1) * rows)
            x1_ref[part, :] = x1
            col, cols = 0, {}
            for name, width, _ in PROJ_SPLIT[:N_PROJ_MATMUL]:
                cols[name] = p[:, col:col + width]
                col += width
            cols["q"] = cols["q"] * Q_SCALE
            cols["k2"], cols["v2"] = _twice_per_head_lanes(cols["k"]), _twice_per_head_lanes(cols["v"])
            for ref, (name, _, dtype) in zip(proj_refs, PROJ_SPLIT):
                ref[part, :] = cols[name].astype(dtype)

    n_out = len(out_refs) // 2
    _on_group(steps_a, run, (xa_ref,) + out_refs[:n_out], (xb_ref,) + out_refs[n_out:])


def _row_chunk_spec(w, steps, cols=None):
    rows = w.shape[0]
    n = next(n for n in range(min(steps, rows // BF16_ROWS), 0, -1)
             if rows % n == 0 and (rows // n) % BF16_ROWS == 0)
    return pl.BlockSpec((rows // n, cols or w.shape[1]), lambda i: (jnp.minimum(i, n - 1), 0))


def _cast_chunks(f32_refs, bf16_refs):
    for src, dst in zip(f32_refs, bf16_refs):
        dst[...] = src[...].astype(BF16)


def _ffn1_proj(xa, xb, gpre, gpost, wfi, wfo, gmix, wp, later_weights):
    steps_a, steps_b = xa.shape[0] // FFN_ROW_TILE, xb.shape[0] // FFN_ROW_TILE
    first, second = _two_group_specs(steps_a)
    widths = [D_MODEL] + [w for _, w, _ in PROJ_SPLIT]
    dtypes = [F32] + [d for _, _, d in PROJ_SPLIT]
    chunk_specs = [_row_chunk_spec(w, steps_a + steps_b) for w in later_weights]
    outs = pl.pallas_call(
        functools.partial(_ffn1_proj_body, steps_a, len(later_weights)),
        grid=(steps_a + steps_b,),
        in_specs=[first(D_MODEL), second(D_MODEL), _resident((1, D_MODEL)), _resident((1, D_MODEL)),
                  _resident(wfi.shape), _resident(wfo.shape), _resident((1, D_MODEL)),
                  _resident(wp.shape)] + chunk_specs,
        out_specs=[first(w) for w in widths] + [second(w) for w in widths] + chunk_specs,
        out_shape=[jax.ShapeDtypeStruct((x.shape[0], w), d) for x in (xa, xb) for w, d in zip(widths, dtypes)]
        + [jax.ShapeDtypeStruct(w.shape, BF16) for w in later_weights],
        compiler_params=_params("arbitrary"),
        name="ffn1_proj",
    )(xa, xb, gpre, gpost, wfi, wfo, gmix, wp, *later_weights)
    n = len(widths)
    return outs[:n], outs[n:2 * n], outs[2 * n:]


def _mem_kv_body(m_ref, g_ref, w_ref, k_ref, v_ref):
    kv = _mm(_rms(m_ref[...], g_ref[...]).astype(BF16), w_ref[...])
    n = m_ref.shape[0]
    for h in range(MEM_HEADS):
        k_ref[pl.ds(h, n, stride=MEM_HEADS), :] = kv[:, h * MEM_HEAD_DIM:(h + 1) * MEM_HEAD_DIM]
        v_ref[pl.ds(h, n, stride=MEM_HEADS), :] = kv[:, MEM_W + h * MEM_HEAD_DIM:MEM_W + (h + 1) * MEM_HEAD_DIM]


def _mem_kv(mem, g, w):
    n = mem.shape[0]
    out = pl.BlockSpec((ROW_TILE * MEM_HEADS, MEM_HEAD_DIM), lambda i: (i, 0))
    return pl.pallas_call(
        _mem_kv_body,
        grid=(n // ROW_TILE,),
        in_specs=[pl.BlockSpec((ROW_TILE, D_MODEL), lambda i: (i, 0)), _resident((1, D_MODEL)), _resident(w.shape)],
        out_specs=[out, out],
        out_shape=[jax.ShapeDtypeStruct((n * MEM_HEADS, MEM_HEAD_DIM), F32)] * 2,
        compiler_params=_params("parallel"),
        name="mem_kv",
    )(mem, g, w)


def _qk(q, k):
    return lax.dot_general(q, k, (((1,), (1,)), ((), ())), preferred_element_type=F32)


def _softmax_pv(scores, values, sinks=None, scale=None):
    probs = []
    for i, s in enumerate(scores):
        m = jnp.max(s, axis=-1, keepdims=True)
        if sinks is not None:
            m = jnp.maximum(m, sinks[i])
        e = jnp.exp(s - m) if scale is None else jnp.exp2((s - m) * (scale * math.log2(math.e)))
        den = jnp.sum(e, axis=-1, keepdims=True)
        if sinks is not None:
            den = den + jnp.exp(sinks[i] - m)
        probs.append((e * (1.0 / den)).astype(BF16))
    return [_mm(p, v) for p, v in zip(probs, values)]


def _gqa_queries(q):
    nq = q.shape[0]
    low = lax.broadcasted_iota(jnp.int32, (nq, LANES), 1) < HEAD_DIM
    zero = jnp.zeros((nq, LANES), BF16)
    stacks = []
    for g in range(N_KV_HEADS):
        rows = []
        for r in range(KV_REP):
            h = g * KV_REP + r
            q2 = q[:, (h // 2) * LANES:(h // 2 + 1) * LANES]
            rows.append(jnp.where(low, q2, zero) if h % 2 == 0 else jnp.where(low, zero, q2))
        stacks.append(jnp.concatenate(rows, axis=0))
    return stacks


def _gqa_outputs(outs, nq):
    low = lax.broadcasted_iota(jnp.int32, (nq, LANES), 1) < HEAD_DIM
    pairs = []
    for o in outs:
        for r in range(0, KV_REP, 2):
            pairs.append(jnp.where(low, o[r * nq:(r + 1) * nq], o[(r + 1) * nq:(r + 2) * nq]))
    return jnp.concatenate(pairs, axis=1)


def _mem_scores(qm, k_head):
    return [_qk(qm[:, h * MEM_HEAD_DIM:(h + 1) * MEM_HEAD_DIM], k_head(h).astype(BF16)) for h in range(MEM_HEADS)]


def _mem_values(v_head):
    return [v_head(h).astype(BF16) for h in range(MEM_HEADS)]


def _row_heads(ref, i):
    return lambda h: ref[i, pl.ds(h, MEM_LEN, stride=MEM_HEADS), :]


def _kv_heads(x):
    return [x[:, g * LANES:(g + 1) * LANES] for g in range(N_KV_HEADS)]


def _merge_pre(x1, y, gmix_ref, wg_ref, wglu_ref):
    h = _rms(x1, gmix_ref[...]).astype(BF16)
    logits = [_mm(h, wg_ref[:, PROJ_COLS + j * D_MODEL:PROJ_COLS + (j + 1) * D_MODEL]) for j in range(3)]
    y = y.astype(BF16)
    return logits, _mm(y, wglu_ref[:, :D_MODEL]), _mm(y, wglu_ref[:, D_MODEL:])


def _mix_parts(parts, x1_ref, y_ref, gmix_ref, wg_ref, wglu_ref, wab_ref, wmb_ref, wo_ref, gpost_ref, x2_ref):
    x1s, ys = _row_parts(x1_ref, len(parts)), _row_parts(y_ref, len(parts))
    pres = [_merge_pre(x1, y, gmix_ref, wg_ref, wglu_ref) for x1, y in zip(x1s, ys)]
    branches = [assemble(_softmax_pv(scores, values, sinks), _softmax_pv(mem_scores, mem_values, scale=MEM_SCALE))
                for scores, values, sinks, mem_scores, mem_values, assemble in parts]
    projected = [(_mm(attn.astype(BF16), wab_ref[...]), _mm(memo.astype(BF16), wmb_ref[...]))
                 for attn, memo in branches]
    sig = _sigmoid
    merged = [sig(lg[0]) * pa + sig(lg[1]) * (ya * sig(yb)) + sig(lg[2]) * pm
              for (lg, ya, yb), (pa, pm) in zip(pres, projected)]
    outs = [_mm(m.astype(BF16), wo_ref[...]) for m in merged]
    rows = x1_ref.shape[0] // len(parts)
    for i, (x1, o) in enumerate(zip(x1s, outs)):
        x2_ref[i * rows:(i + 1) * rows, :] = x1 + _rms(o, gpost_ref[...])


def _mixer_prompt_body(q_ref, k_ref, kh_ref, v_ref, vh_ref, bias_ref, sink_ref, qm_ref, mk_ref, mv_ref,
                       x1_ref, y_ref, *merge_and_out_refs):
    i = pl.program_id(1)
    tq = q_ref.shape[1]
    kf = jnp.concatenate([kh_ref[0], k_ref[0]], axis=0)
    vf = jnp.concatenate([vh_ref[0], v_ref[0]], axis=0)
    band = WINDOW + CHUNK
    rows = tq // MIXER_PARTS
    parts = []
    for p in range(MIXER_PARTS):
        scores, values, sinks = [], [], []
        for lo in range(p * rows, (p + 1) * rows, CHUNK):
            valid = None
            if lo < WINDOW:
                key_pos = lax.broadcasted_iota(jnp.int32, (1, band), 1) + (i * tq + lo - WINDOW)
                valid = key_pos >= 0
            kb, vb = _kv_heads(kf[lo:lo + band]), _kv_heads(vf[lo:lo + band])
            for g, qs in enumerate(_gqa_queries(q_ref[0, lo:lo + CHUNK, :])):
                s = _qk(qs, kb[g]) + bias_ref[g]
                scores.append(s if valid is None else jnp.where(valid, s, NEG_INF))
                values.append(vb[g])
                sinks.append(sink_ref[g])
        mem_scores = _mem_scores(qm_ref[0, p * rows:(p + 1) * rows, :], _row_heads(mk_ref, 0))

        def assemble(outs, mem):
            attn = jnp.concatenate([_gqa_outputs(outs[c:c + N_KV_HEADS], CHUNK)
                                    for c in range(0, len(outs), N_KV_HEADS)], axis=0)
            return attn, jnp.concatenate(mem, axis=1)

        parts.append((scores, values, sinks, mem_scores, _mem_values(_row_heads(mv_ref, 0)), assemble))
    _mix_parts(parts, x1_ref, y_ref, *merge_and_out_refs)


def _merge_weight_specs(weights):
    return [_resident(w.shape) for w in weights]


def _mixer_prompt(q, k2, v2, bias, sink, qm, mk, mv, x1, y, merge_w):
    b, l, _ = q.shape
    tq = MIXER_TILE
    hb = tq // WINDOW
    nt = l // tq
    cur = lambda w: pl.BlockSpec((1, tq, w), lambda bi, i: (bi, i, 0))
    row = lambda w: pl.BlockSpec((tq, w), lambda bi, i: (bi * nt + i, 0))
    halo = pl.BlockSpec((1, WINDOW, KV2_W), lambda bi, i: (bi, jnp.maximum(i * hb - 1, 0), 0))
    mem = pl.BlockSpec((1, MEM_LEN * MEM_HEADS, MEM_HEAD_DIM), lambda bi, i: (bi, 0, 0))
    return pl.pallas_call(
        _mixer_prompt_body,
        grid=(b, nt),
        in_specs=[cur(ATTN_W), cur(KV2_W), halo, cur(KV2_W), halo, _resident(bias.shape),
                  _resident(sink.shape), cur(MEM_W), mem, mem, row(D_MODEL), row(SSM_W)]
        + _merge_weight_specs(merge_w),
        out_specs=row(D_MODEL),
        out_shape=jax.ShapeDtypeStruct((b * l, D_MODEL), F32),
        compiler_params=_params("parallel", "parallel"),
        name="mixer_prompt",
    )(q, k2, k2, v2, v2, bias, sink, qm, mk, mv, x1, y, *merge_w)


def _mixer_sample_body(q_ref, k_ref, v_ref, bias_ref, sink_ref, qm_ref, mk_ref, mv_ref,
                       x1_ref, y_ref, *merge_and_out_refs):
    nb, nq = q_ref.shape[0], q_ref.shape[1]
    per_part = nb // MIXER_PARTS
    parts = []
    for p in range(MIXER_PARTS):
        scores, values, sinks, mem_scores, mem_values = [], [], [], [], []
        for b in range(p * per_part, (p + 1) * per_part):
            kb, vb = _kv_heads(k_ref[b]), _kv_heads(v_ref[b])
            for g, qs in enumerate(_gqa_queries(q_ref[b])):
                scores.append(_qk(qs, kb[g]) + bias_ref[g])
                values.append(vb[g])
                sinks.append(sink_ref[g])
            mem_scores += _mem_scores(qm_ref[b], _row_heads(mk_ref, b))
            mem_values += _mem_values(_row_heads(mv_ref, b))

        def assemble(outs, mem):
            attn = jnp.concatenate([_gqa_outputs(outs[j:j + N_KV_HEADS], nq)
                                    for j in range(0, len(outs), N_KV_HEADS)], axis=0)
            memo = jnp.concatenate([jnp.concatenate(mem[j:j + MEM_HEADS], axis=1)
                                    for j in range(0, len(mem), MEM_HEADS)], axis=0)
            return attn, memo

        parts.append((scores, values, sinks, mem_scores, mem_values, assemble))
    _mix_parts(parts, x1_ref, y_ref, *merge_and_out_refs)


def _mixer_sample(q, kk2, vv2, bias, sink, qm, mk, mv, x1, y, merge_w):
    b, s, _ = q.shape
    nb = SAMPLE_MIXER_TILE // s
    nk = kk2.shape[1]
    blk = lambda *shape: pl.BlockSpec((nb,) + shape, lambda i: (i,) + (0,) * len(shape))
    row = lambda w: pl.BlockSpec((nb * s, w), lambda i: (i, 0))
    return pl.pallas_call(
        _mixer_sample_body,
        grid=(b // nb,),
        in_specs=[blk(s, ATTN_W), blk(nk, KV2_W), blk(nk, KV2_W),
                  _resident(bias.shape), _resident(sink.shape),
                  blk(s, MEM_W), blk(*mk.shape[1:]), blk(*mv.shape[1:]), row(D_MODEL), row(SSM_W)]
        + _merge_weight_specs(merge_w),
        out_specs=row(D_MODEL),
        out_shape=jax.ShapeDtypeStruct((b * s, D_MODEL), F32),
        compiler_params=_params("parallel"),
        name="mixer_sample",
    )(q, kk2, vv2, bias, sink, qm, mk, mv, x1, y, *merge_w)


def _ssm_weights_body(n_first, lam_re_row, lam_im_row, lam_re_col, lam_im_col, log_dt, b_re_t, b_im_t,
                      c_re, c_im, c_re_lanes, c_im_lanes, sel_ref, *refs):
    first_f32, (wconv_ref, wst_ref, wot_ref, a_tab_ref), first_bf16 = (
        refs[:n_first], refs[n_first:n_first + 4], refs[n_first + 4:])
    _cast_chunks(first_f32, first_bf16)
    _ssm_block_weights(lam_re_row, lam_im_row, lam_re_col, lam_im_col, log_dt, b_re_t, b_im_t,
                       c_re, c_im, c_re_lanes, c_im_lanes, sel_ref, wconv_ref, wst_ref, wot_ref, a_tab_ref)


def _ssm_block_weights(lam_re_row, lam_im_row, lam_re_col, lam_im_col, log_dt, b_re_t, b_im_t,
                       c_re, c_im, c_re_lanes, c_im_lanes, sel_ref,
                       wconv_ref, wst_ref, wot_ref, a_tab_ref):
    wst_ref[0] = jnp.zeros(wst_ref.shape[1:], BF16)
    wot_ref[0] = jnp.zeros(wot_ref.shape[1:], BF16)
    krows = []
    for gl in range(SSM_GB):
        krows.append(_ssm_group_weights(
            gl, *(r[gl] for r in (lam_re_row, lam_im_row, lam_re_col, lam_im_col, log_dt, b_re_t, b_im_t,
                                  c_re, c_im, c_re_lanes, c_im_lanes)),
            wst_ref, wot_ref, a_tab_ref))
    kstack = jnp.concatenate(krows, axis=0).astype(BF16)
    for d in range(SSM_NT):
        for ti in range(SSM_TPT):
            blk = _mm(kstack, sel_ref[d * SSM_TPT + ti])
            r0 = (SSM_NT - 1 - d) * MXU_TILE + ti * LANES
            for gl in range(SSM_GB):
                piece = blk[gl * SSM_GROUP:(gl + 1) * SSM_GROUP]
                piece = (piece if gl == 0 else pltpu.roll(piece, gl * SSM_GROUP, 1)).astype(BF16)
                r = r0 + gl * SSM_GROUP
                wconv_ref[0, r:r + SSM_GROUP, :MXU_TILE] = piece
                if r0 >= MXU_TILE:
                    wconv_ref[0, r - MXU_TILE:r - MXU_TILE + SSM_GROUP, MXU_TILE:] = piece
    wconv_ref[0, (SSM_NT - 1) * MXU_TILE:, MXU_TILE:] = jnp.zeros((MXU_TILE, MXU_TILE), BF16)


def _ssm_group_weights(gl, lam_re_row, lam_im_row, lam_re_col, lam_im_col, log_dt, b_re_t, b_im_t,
                       c_re, c_im, c_re_lanes, c_im_lanes,
                       wst_ref, wot_ref, a_tab_ref):
    dt = jnp.exp(log_dt)
    rows = lambda t: slice(t * LANES + gl * SSM_GROUP, t * LANES + (gl + 1) * SSM_GROUP)
    lanes = slice(gl * SSM_S2, (gl + 1) * SSM_S2)

    def zoh_coef(lr, li):
        mag = jnp.exp(lr * dt)
        a_re, a_im = mag * jnp.cos(li * dt), mag * jnp.sin(li * dt)
        den = lr * lr + li * li
        return ((a_re - 1.0) * lr + a_im * li) / den, (a_im * lr - (a_re - 1.0) * li) / den

    def a_power(lr, li, n):
        mag = jnp.exp(lr * dt * n)
        return mag * jnp.cos(li * dt * n), mag * jnp.sin(li * dt * n)

    lr, li = lam_re_row, lam_im_row
    cr, ci = zoh_coef(lr, li)
    n_rows = jnp.minimum(lax.broadcasted_iota(jnp.int32, (SSM_T + SCAN_ROWS, 1), 0), SSM_T).astype(F32)
    pw_r, pw_i = a_power(lr, li, n_rows)

    def over_channels(pw, exps):
        return jnp.concatenate([jnp.broadcast_to(pw[e:e + 1], (SSM_GROUP, SSM_STATE)) for e in exps], axis=0)

    back = [SSM_T - 1 - t for t in range(SSM_T)]
    pr, pi = over_channels(pw_r, back), over_channels(pw_i, back)
    zr, zi = pr * cr - pi * ci, pr * ci + pi * cr
    over_positions = lambda x: jnp.concatenate([x] * SSM_T, axis=0)
    br, bi = over_positions(b_re_t), over_positions(b_im_t)
    wstate = jnp.concatenate([zr * br - zi * bi, zr * bi + zi * br], axis=1).astype(BF16)

    fwd = [t + 1 for t in range(SSM_T)]
    qr, qi = over_channels(pw_r, fwd), over_channels(pw_i, fwd)
    ccr, cci = over_positions(c_re), over_positions(c_im)
    wout_t = jnp.concatenate([ccr * qr - cci * qi, -(ccr * qi + cci * qr)], axis=1).astype(BF16)
    for t in range(SSM_T):
        wst_ref[0, rows(t), lanes] = wstate[t * SSM_GROUP:(t + 1) * SSM_GROUP]
        wot_ref[0, rows(t), lanes] = wout_t[t * SSM_GROUP:(t + 1) * SSM_GROUP]

    lrc, lic = lam_re_col, lam_im_col
    hi = lax.Precision.HIGHEST
    n_lanes = jnp.minimum(lax.broadcasted_iota(jnp.int32, (1, LANES), 1), SSM_T).astype(F32)
    pc_r, pc_i = a_power(lrc, lic, n_lanes)
    spread = (lax.broadcasted_iota(jnp.int32, (LANES, SSM_K), 0)
              == lax.broadcasted_iota(jnp.int32, (LANES, SSM_K), 1) // SSM_GROUP).astype(F32)
    gr = jnp.dot(pc_r, spread, precision=hi, preferred_element_type=F32)
    gi = jnp.dot(pc_i, spread, precision=hi, preferred_element_type=F32)
    clr, cli = c_re_lanes, c_im_lanes
    g_re, g_im = clr * gr - cli * gi, clr * gi + cli * gr
    btr, bti = b_re_t, b_im_t
    bbr, bbi = cr * btr - ci * bti, cr * bti + ci * btr
    krow = (jnp.dot(bbr, g_re, precision=hi, preferred_element_type=F32)
            - jnp.dot(bbi, g_im, precision=hi, preferred_element_type=F32))

    idx = lax.broadcasted_iota(jnp.int32, (N_SCAN_TABLES * SCAN_ROWS, SSM_STATE), 0)
    tab, r = idx // SCAN_ROWS, idx % SCAN_ROWS
    stride = jnp.where(tab < 2, 1, jnp.where(tab < 4, 2, 4))
    n = jnp.where(tab < 6, stride, jnp.where(tab < 8, r, SCAN_ROWS))
    keep = jnp.logical_or(tab >= 6, r >= stride)
    positions = jnp.where(tab < 10, n * SCAN_T, SSM_T)
    er, ei = a_power(lr, li, positions.astype(F32))
    er, ei = jnp.where(keep, er, 0.0), jnp.where(keep, ei, 0.0)
    odd = tab % 2 == 1
    tabs = jnp.concatenate([jnp.where(odd, -ei, er), jnp.where(odd, ei, er)], axis=1)
    a_tab_ref[:, :, lanes] = tabs.reshape(N_SCAN_TABLES, SCAN_ROWS, SSM_S2)
    return krow


def _lag_selectors():
    sel = np.zeros((SSM_NT, SSM_TPT, SSM_K, MXU_TILE), np.float32)
    ch = np.arange(SSM_GROUP)
    for d in range(SSM_NT):
        for ti in range(SSM_TPT):
            for to in range(SSM_TPT):
                lag = SSM_TPT * d + to - ti
                if lag >= 0:
                    sel[d, ti, lag * SSM_GROUP + ch, to * LANES + ch] = 1.0
    return jnp.asarray(sel.reshape(SSM_NT * SSM_TPT, SSM_K, MXU_TILE), BF16)


def _ssm_weights(lam_re, lam_im, log_dt, b_re, b_im, c_re, c_im, first_weights):
    g, p, c = b_re.shape
    row3 = lambda x: x.reshape(g, 1, p)
    col3 = lambda x: x.reshape(g, p, 1)
    t3 = lambda x: jnp.transpose(x, (0, 2, 1))
    c_lanes = lambda x: jnp.tile(t3(x), (1, 1, SSM_T))
    args = (row3(lam_re), row3(lam_im), col3(lam_re), col3(lam_im), log_dt.reshape(g, 1, 1),
            t3(b_re), t3(b_im), c_re, c_im, c_lanes(c_re), c_lanes(c_im))
    sel = _lag_selectors()
    spec = lambda x: pl.BlockSpec((SSM_GB,) + x.shape[1:], lambda i: (i, 0, 0))
    per_gb = lambda *shape: pl.BlockSpec((1,) + shape, lambda i: (i, 0, 0))
    chunk_specs = [_row_chunk_spec(w, SSM_NGB, cols) for w, cols in first_weights]
    outs = pl.pallas_call(
        functools.partial(_ssm_weights_body, len(first_weights)),
        grid=(SSM_NGB,),
        in_specs=[spec(a) for a in args] + [_resident(sel.shape)] + chunk_specs,
        out_specs=[per_gb(SSM_NT * MXU_TILE, 2 * MXU_TILE), per_gb(SSM_XW, SSM_SW), per_gb(SSM_XW, SSM_SW),
                   pl.BlockSpec((N_SCAN_TABLES, SCAN_ROWS, SSM_SW), lambda i: (0, 0, i))] + chunk_specs,
        out_shape=[jax.ShapeDtypeStruct((SSM_NGB, SSM_NT * MXU_TILE, 2 * MXU_TILE), BF16),
                   jax.ShapeDtypeStruct((SSM_NGB, SSM_XW, SSM_SW), BF16),
                   jax.ShapeDtypeStruct((SSM_NGB, SSM_XW, SSM_SW), BF16),
                   jax.ShapeDtypeStruct((N_SCAN_TABLES, SCAN_ROWS, SSM_WIDTH), F32)]
        + [jax.ShapeDtypeStruct((w.shape[0], cols or w.shape[1]), BF16) for w, cols in first_weights],
        compiler_params=_params("arbitrary"),
        name="ssm_weights",
    )(*args, sel, *[w for w, _ in first_weights])
    return outs[:4], outs[4:]


def _swap_halves(s):
    ax = s.ndim - 1
    return jnp.concatenate([pltpu.roll(s[..., l:l + SSM_S2], SSM_STATE, ax)
                            for l in range(0, s.shape[ax], SSM_S2)], axis=ax)


def _ssm_body(row_sets, slab_rows, u_ref, wconv_ref, wst_ref, wot_ref, d_ref, a_tab_ref, s0_ref,
              y_ref, s_ref, v_scr, s_scr):
    def piece(ref_set, first, n, stride, t):
        return ref_set, pl.ds(first + t, n, stride=stride)

    @pl.when(pl.program_id(1) == 0)
    def _():
        s_scr[...] = s0_ref[0]

    x = jnp.concatenate(
        [jnp.concatenate([u_ref[piece(*rs, t)] for t in range(SSM_T)], axis=1) for rs in row_sets],
        axis=0).astype(BF16)
    lane_blocks = range(SSM_SW // LANES)

    def v_read(rows):
        return jnp.concatenate([v_scr[j, rows, :] for j in lane_blocks], axis=1)

    def v_write(rows, value):
        for j in lane_blocks:
            v_scr[j, rows, :] = value[:, j * LANES:(j + 1) * LANES]

    n_rows = v_scr.shape[1]
    all_rows = slice(0, n_rows)
    v_write(all_rows, _mm(x, wst_ref[0]))
    conv = []
    for jo in range(0, SSM_NT, 2):
        pair = _mm(x[:, :(jo + 2) * MXU_TILE], wconv_ref[0, (SSM_NT - 2 - jo) * MXU_TILE:, :])
        conv += [pair[:, MXU_TILE:], pair[:, :MXU_TILE]]

    a_mul, a_swap = a_tab_ref[10, 0:1, :], a_tab_ref[11, 0:1, :]
    if slab_rows:
        s = s_scr[...]
        for c in range(n_rows // slab_rows):
            rows = slice(c * slab_rows, (c + 1) * slab_rows)
            inc = v_read(rows)
            v_write(rows, s)
            s = a_mul * s + a_swap * _swap_halves(s) + inc
        s_scr[...] = s
    else:
        step = lambda s, inc: a_mul * s + a_swap * _swap_halves(s) + inc
        set_starts = np.cumsum([0] + [rs[2] for rs in row_sets])
        runs = [(int(r0), rs[2] // SCAN_RUN) for r0, rs in zip(set_starts, row_sets)]
        incs = [jnp.concatenate([v_read(pl.ds(r0 + k, n, stride=SCAN_RUN)) for r0, n in runs], axis=0)
                for k in range(SCAN_RUN)]
        merged = incs[0]
        for k in range(1, SCAN_RUN):
            merged = step(merged, incs[k])
        n_blocks = merged.shape[0] // SCAN_ROWS
        xs = merged.reshape(n_blocks, SCAN_ROWS, SSM_SW)
        for k in range(3):
            sh = pltpu.roll(xs, 1 << k, 1)
            xs = xs + a_tab_ref[2 * k] * sh + a_tab_ref[2 * k + 1] * _swap_halves(sh)
        carries, blk = [], 0
        for j, (_, n) in enumerate(runs):
            carry = s_scr[j * SCAN_ROWS:(j + 1) * SCAN_ROWS, :]
            for _ in range(n // SCAN_ROWS):
                carries.append(carry)
                last = jnp.broadcast_to(xs[blk, SCAN_ROWS - 1:, :], carry.shape)
                carry = last + a_tab_ref[8] * carry + a_tab_ref[9] * _swap_halves(carry)
                blk += 1
            s_scr[j * SCAN_ROWS:(j + 1) * SCAN_ROWS, :] = carry
        carries = jnp.stack(carries, axis=0)
        not_first = lax.broadcasted_iota(jnp.int32, (1, SCAN_ROWS, 1), 1) >= 1
        enter = (jnp.where(not_first, pltpu.roll(xs, 1, 1), 0.0)
                 + a_tab_ref[6] * carries + a_tab_ref[7] * _swap_halves(carries)).reshape(merged.shape)
        for k in range(SCAN_RUN):
            h0 = 0
            for r0, n in runs:
                v_write(pl.ds(r0 + k, n, stride=SCAN_RUN), enter[h0:h0 + n])
                h0 += n
            enter = step(enter, incs[k])
    s_ref[0] = s_scr[...]

    enter = v_read(all_rows).astype(BF16)
    d = d_ref[...]
    carried = _qk(enter, wot_ref[0])
    for jo in range(SSM_NT):
        yt = conv[jo] + carried[:, jo * MXU_TILE:(jo + 1) * MXU_TILE]
        r0 = 0
        for rs in row_sets:
            for tl in range(SSM_TPT):
                idx = piece(*rs, jo * SSM_TPT + tl)
                y_ref[idx] = yt[r0:r0 + rs[2], tl * LANES:(tl + 1) * LANES] + u_ref[idx] * d
            r0 += rs[2]


def _ssm(u, row_sets, slab_rows, block_rows, wconv, wst, wo, d_lanes, a_tab, s0):
    ns, r, _ = u.shape
    chunk_rows = sum(rs[2] for rs in row_sets)
    carry_rows = s0.shape[1]
    blk = pl.BlockSpec((ns, block_rows, LANES), lambda gb, i: (0, i, gb))
    per_gb = lambda x: pl.BlockSpec((1,) + x.shape[1:], lambda gb, i: (gb, 0, 0))
    return pl.pallas_call(
        functools.partial(_ssm_body, row_sets, slab_rows),
        grid=(SSM_NGB, r // block_rows),
        in_specs=[blk, per_gb(wconv), per_gb(wst), per_gb(wo),
                  pl.BlockSpec((1, LANES), lambda gb, i: (0, gb)),
                  pl.BlockSpec((N_SCAN_TABLES, SCAN_ROWS, SSM_SW), lambda gb, i: (0, 0, gb)),
                  per_gb(s0)],
        out_specs=[blk, per_gb(s0)],
        out_shape=[jax.ShapeDtypeStruct(u.shape, F32), jax.ShapeDtypeStruct(s0.shape, F32)],
        scratch_shapes=[pltpu.VMEM((SSM_SW // LANES, chunk_rows, LANES), F32),
                        pltpu.VMEM((carry_rows, SSM_SW), F32)],
        compiler_params=_params("parallel", "arbitrary"),
        name="ssm_scan",
    )(u, wconv, wst, wo, d_lanes, a_tab, s0)


def _ssm_branch(u, s0, ssm_w):
    b, l, _ = u.shape
    nc = l // SSM_T
    by_gb = lambda s: s.reshape(s.shape[0], SSM_NGB, SSM_SW).transpose(1, 0, 2)
    if b % SCAN_ROWS == 0:
        row_sets = tuple((0, c * SSM_T, b, l) for c in range(nc))
        y, s_last = _ssm(u.reshape(1, b * l, SSM_W), row_sets, b, b * l, *ssm_w, by_gb(s0))
    else:
        cps = min(nc, SSM_CHUNKS_PER_STEP)
        row_sets = tuple((j, 0, cps, SSM_T) for j in range(b))
        y, s_last = _ssm(u, row_sets, 0, cps * SSM_T, *ssm_w, by_gb(jnp.repeat(s0, SCAN_ROWS, axis=0)))
        s_last = s_last[:, ::SCAN_ROWS]
    s_last = s_last.transpose(1, 0, 2).reshape(b, SSM_GROUPS, 2, SSM_STATE)
    return y.reshape(b, l, SSM_W), s_last[:, :, 0], s_last[:, :, 1]


def _ffn2_body(steps_a, xa_ref, xb_ref, gpre_ref, gpost_ref, wfi_ref, wfo_ref, oa_ref, ob_ref):
    def run(x_ref, o_ref):
        outs = _ffn(_row_parts(x_ref, FFN_PARTS), gpre_ref[...], gpost_ref[...], wfi_ref, wfo_ref)
        rows = x_ref.shape[0] // FFN_PARTS
        for i, o in enumerate(outs):
            o_ref[i * rows:(i + 1) * rows, :] = o

    _on_group(steps_a, run, (xa_ref, oa_ref), (xb_ref, ob_ref))


def _ffn2(xa, xb, gpre, gpost, wfi, wfo):
    steps_a, steps_b = xa.shape[0] // FFN_ROW_TILE, xb.shape[0] // FFN_ROW_TILE
    first, second = _two_group_specs(steps_a)
    vec = _resident((1, D_MODEL))
    return pl.pallas_call(
        functools.partial(_ffn2_body, steps_a),
        grid=(steps_a + steps_b,),
        in_specs=[first(D_MODEL), second(D_MODEL), vec, vec, _resident(wfi.shape), _resident(wfo.shape)],
        out_specs=[first(D_MODEL), second(D_MODEL)],
        out_shape=[jax.ShapeDtypeStruct(xa.shape, F32), jax.ShapeDtypeStruct(xb.shape, F32)],
        compiler_params=_params("arbitrary"),
        name="ffn2",
    )(xa, xb, gpre, gpost, wfi, wfo)


def _t5_bucket(rel):
    half = N_BUCKETS // 2
    max_exact = half // 2
    ret = (rel > 0).astype(np.int32) * half
    n = np.abs(rel)
    large = max_exact + (np.log(np.maximum(n, 1) / max_exact) / math.log(MAX_DISTANCE / max_exact)
                         * (half - max_exact)).astype(np.int32)
    large = np.minimum(large, half - 1)
    return ret + np.where(n < max_exact, n, large)


def _band_bias(rel_table, n_q, n_back, n_k):
    i = np.arange(n_q)[:, None]
    j = np.arange(n_k)[None, :]
    bucket = _t5_bucket((j - n_back) - i).reshape(-1)
    onehot = np.zeros((N_BUCKETS, bucket.size), np.float32)
    onehot[bucket, np.arange(bucket.size)] = 1.0
    b = jnp.dot(rel_table.astype(F32).T, jnp.asarray(onehot), precision=lax.Precision.HIGHEST)
    return b.reshape(N_KV_HEADS, KV_REP * n_q, n_k)


def _sink_rows(sink, n_q):
    return jnp.repeat(sink.astype(F32).reshape(N_KV_HEADS, KV_REP), n_q, axis=1)[:, :, None]


def _twice_per_head(x, axis):
    shape = x.shape
    x = x.reshape(shape[:axis] + (N_KV_HEADS, 1, HEAD_DIM) + shape[axis + 1:])
    x = jnp.concatenate([x, x], axis=axis + 1)
    return x.reshape(shape[:axis] + (KV2_W,) + shape[axis + 1:])


def kernel(x_prompt, x_sample, cache_swa_k, cache_swa_v, cache_mem_k, cache_mem_v, state_ssm_re, state_ssm_im, mem_prompt, rel_bias_table, ff1_pre_g, ff1_post_g, w_ff1_in, w_ff1_out, mix_pre_g, mix_post_g, w_in, mem_norm_g, w_mem_kv, attn_sink, ssm_lambda_re, ssm_lambda_im, ssm_log_dt, ssm_b_re, ssm_b_im, ssm_c_re, ssm_c_im, ssm_d, w_ssm_glu, w_attn_br, w_mem_br, w_out, ff2_pre_g, ff2_post_g, w_ff2_in, w_ff2_out):
    assert ff1_pre_g.shape[0] == 1, "single-layer step"
    bp, lp, _ = x_prompt.shape
    bs, ls, _ = x_sample.shape
    vec = lambda g: g[0].reshape(1, D_MODEL).astype(F32)

    (wconv, wst, wot, a_tab), (wfi1, wfo1, wp) = _ssm_weights(
        ssm_lambda_re[0], ssm_lambda_im[0], ssm_log_dt[0], ssm_b_re[0], ssm_b_im[0], ssm_c_re[0], ssm_c_im[0],
        [(w_ff1_in[0], None), (w_ff1_out[0], None), (w_in[0], PROJ_COLS)])
    ssm_w = (wconv, wst, wot, ssm_d[0].astype(F32).reshape(1, SSM_W), a_tab)

    later = (w_ff2_in, w_ff2_out, w_in, w_ssm_glu, w_attn_br, w_mem_br, w_out, w_mem_kv)
    proj_p, proj_s, (wfi2, wfo2, w_in16, wglu, wab, wmb, wo, wmkv) = _ffn1_proj(
        x_prompt.reshape(bp * lp, D_MODEL), x_sample.reshape(bs * ls, D_MODEL), vec(ff1_pre_g), vec(ff1_post_g),
        wfi1, wfo1, vec(mix_pre_g), wp, [w[0] for w in later])

    mk_p, mv_p = _mem_kv(mem_prompt.reshape(bp * MEM_LEN, D_MODEL), vec(mem_norm_g), wmkv)

    def mix(x_shape, proj, mixer, s0, mem_k, mem_v):
        b, l, _ = x_shape
        r3 = lambda t: t.reshape(b, l, t.shape[-1])
        x1, q, k, v, u, qm, k2, v2 = proj
        y_ssm, s_re, s_im = _ssm_branch(r3(u), s0, ssm_w)
        x2 = mixer(r3(q), r3(k2), r3(v2), r3(qm), mem_k, mem_v, x1, y_ssm.reshape(b * l, SSM_W))
        return x2, r3(k), r3(v), s_re, s_im

    merge_w = (vec(mix_pre_g), w_in16, wglu, wab, wmb, wo, vec(mix_post_g))

    def mixer_prompt(q, k2, v2, qm, mem_k, mem_v, x1, y_ssm):
        bias = _band_bias(rel_bias_table, CHUNK, WINDOW, WINDOW + CHUNK)
        return _mixer_prompt(q, k2, v2, bias, _sink_rows(attn_sink[0], CHUNK), qm, mem_k, mem_v, x1, y_ssm, merge_w)

    def mixer_sample(q, k2, v2, qm, mem_k, mem_v, x1, y_ssm):
        n_back = cache_swa_k.shape[2]
        cache2 = lambda c: _twice_per_head(c[0].reshape(bs, n_back, KV_W).astype(BF16), 2)
        kk = jnp.concatenate([cache2(cache_swa_k), k2], axis=1)
        vv = jnp.concatenate([cache2(cache_swa_v), v2], axis=1)
        bias = _band_bias(rel_bias_table, ls, n_back, n_back + ls)
        return _mixer_sample(q, kk, vv, bias, _sink_rows(attn_sink[0], ls), qm, mem_k, mem_v, x1, y_ssm, merge_w)

    x2p, pk, pv, pre, pim = mix(x_prompt.shape, proj_p, mixer_prompt, jnp.zeros((bp, SSM_WIDTH), F32),
                                mk_p.reshape(bp, MEM_LEN * MEM_HEADS, MEM_HEAD_DIM),
                                mv_p.reshape(bp, MEM_LEN * MEM_HEADS, MEM_HEAD_DIM))
    s0 = jnp.stack([state_ssm_re[0], state_ssm_im[0]], axis=2).reshape(bs, SSM_WIDTH).astype(F32)
    x2s, sk, sv, sre, sim = mix(x_sample.shape, proj_s, mixer_sample, s0,
                                cache_mem_k[0].reshape(bs, MEM_LEN * MEM_HEADS, MEM_HEAD_DIM),
                                cache_mem_v[0].reshape(bs, MEM_LEN * MEM_HEADS, MEM_HEAD_DIM))
    yp, ys = _ffn2(x2p, x2s, vec(ff2_pre_g), vec(ff2_post_g), wfi2, wfo2)
    yp, ys = yp.reshape(x_prompt.shape), ys.reshape(x_sample.shape)

    n_keep = min(WINDOW, lp)
    heads = lambda t: t.reshape(t.shape[0], t.shape[1], N_KV_HEADS, HEAD_DIM)[None]
    mem_heads = lambda t: t.reshape(bp, MEM_LEN, MEM_HEADS, MEM_HEAD_DIM)[None]
    return (yp, ys, heads(pk[:, -n_keep:]), heads(pv[:, -n_keep:]), mem_heads(mk_p), mem_heads(mv_p),
            pre[None], pim[None], heads(sk), heads(sv), sre[None], sim[None])
```
